```python
import math
import numpy as np
import jax
import jax.numpy as jnp
from jax import lax

D_MODEL = 1024
BATCH = 8
SEQ = 2048
DEPTH = 4

GRID_W = 64
CTX_LEN = 256
Q_BLOCK = 128
ROPE_THETA = 10000.0
NORM_EPS = 1e-6
NEG_INF = -1e30

MLA_HEADS = 4
MLA_NOPE = 64
MLA_ROPE = 32
MLA_V = 64
MLA_Q_RANK = 256
MLA_KV_RANK = 128
DIFF_HEADS = 4
DIFF_D = 32
NA_HEADS = 4
NA_D = 64
NA_ROWS = 8
NA_COLS = 16
NA_QCOLS = 16
NA_BAND = 2 * NA_COLS
GQA_HEADS = 4
GQA_KV_HEADS = 2
GQA_D = 64
D_FF = 2816
CONV_W = 3

MIX_SPLITS = (
    MLA_Q_RANK, MLA_KV_RANK, MLA_ROPE,
    DIFF_HEADS * 2 * DIFF_D, DIFF_HEADS * 2 * DIFF_D, DIFF_HEADS * 2 * DIFF_D,
    NA_HEADS * NA_D, NA_HEADS * NA_D, NA_HEADS * NA_D,
    GQA_HEADS * GQA_D, GQA_KV_HEADS * GQA_D, GQA_KV_HEADS * GQA_D,
)
IN_COLS = sum(MIX_SPLITS)
MIX_WIDTH = MLA_HEADS * MLA_V + DIFF_HEADS * 2 * DIFF_D + NA_HEADS * NA_D + GQA_HEADS * GQA_D

kernel_name = 'hybrid_parallel_head_dit_block'


def rmsnorm(x, g):
    xf = x.astype(jnp.float32)
    y = xf * lax.rsqrt(jnp.mean(xf * xf, axis=-1, keepdims=True) + NORM_EPS)
    return (y * g.astype(jnp.float32)).astype(x.dtype)


def modulate(x, g, shift, scale):
    return rmsnorm(x, g) * (1.0 + scale) + shift


def split_cols(p, sizes):
    idx = np.cumsum(np.array(sizes))[:-1].tolist()
    return jnp.split(p, idx, axis=-1)


def to_heads(t, h):
    b, n, _ = t.shape
    return t.reshape(b, n, h, -1).transpose(0, 2, 1, 3)


def from_heads(t):
    b, h, n, d = t.shape
    return t.transpose(0, 2, 1, 3).reshape(b, n, h * d)


def axial_rope_tables(n_tok, rot_dim):
    n_freq = rot_dim // 4
    inv = jnp.power(ROPE_THETA, -jnp.arange(n_freq, dtype=jnp.float32) / n_freq)
    t = jnp.arange(n_tok)
    row = (t // GRID_W).astype(jnp.float32)
    col = (t % GRID_W).astype(jnp.float32)
    ar = row[:, None] * inv
    ac = col[:, None] * inv
    ang = jnp.concatenate([ar, ar, ac, ac], axis=-1)
    return jnp.cos(ang), jnp.sin(ang)


def apply_axial_rope(x, cos, sin):
    a1, a2, b1, b2 = jnp.split(x, 4, axis=-1)
    rot = jnp.concatenate([-a2, a1, -b2, b1], axis=-1)
    return x * cos.astype(x.dtype) + rot * sin.astype(x.dtype)


def attend(q, k, v, scale):
    s = jnp.matmul(q, jnp.swapaxes(k, -1, -2)).astype(jnp.float32) * scale
    p = jax.nn.softmax(s, axis=-1)
    return jnp.matmul(p.astype(v.dtype), v)


def sweep_query_blocks(fn, *qs):
    n_tok = qs[0].shape[-2]
    nb = n_tok // Q_BLOCK

    def to_blocks(q):
        q = q.reshape(q.shape[:-2] + (nb, Q_BLOCK, q.shape[-1]))
        return jnp.moveaxis(q, -3, 0)

    out = lax.map(lambda blk: fn(*blk), tuple(to_blocks(q) for q in qs))
    out = jnp.moveaxis(out, 0, -3)
    return out.reshape(out.shape[:-3] + (n_tok, out.shape[-1]))


def mla_mixer(p_lat, p_ctx, q_a_g, w_uq, kv_a_g, w_ukv, q_g, k_g, cos, sin, with_ctx):
    scale = (MLA_NOPE + MLA_ROPE) ** -0.5

    def q_of(cq):
        q = to_heads(rmsnorm(cq, q_a_g) @ w_uq, MLA_HEADS)
        return rmsnorm(q, q_g)

    def kv_of(ckv, kr):
        kv = to_heads(rmsnorm(ckv, kv_a_g) @ w_ukv, MLA_HEADS)
        k_nope, v = kv[..., :MLA_NOPE], kv[..., MLA_NOPE:]
        kr = jnp.broadcast_to(kr[:, None], k_nope.shape[:-1] + (MLA_ROPE,))
        return rmsnorm(jnp.concatenate([k_nope, kr], axis=-1), k_g), v

    def rope_tail(t):
        return jnp.concatenate([t[..., :MLA_NOPE], apply_axial_rope(t[..., MLA_NOPE:], cos, sin)], axis=-1)

    cq_l, ckv_l, kr_l = p_lat
    cq_c, ckv_c, kr_c = p_ctx
    q_l = rope_tail(q_of(cq_l))
    k_l, v_l = kv_of(ckv_l, kr_l)
    k_l = rope_tail(k_l)
    k_c, v_c = kv_of(ckv_c, kr_c)
    k_all = jnp.concatenate([k_l, k_c], axis=2)
    v_all = jnp.concatenate([v_l, v_c], axis=2)
    o_l = sweep_query_blocks(lambda qb: attend(qb, k_all, v_all, scale), q_l)
    o_c = from_heads(attend(q_of(cq_c), k_c, v_c, scale)) if with_ctx else None
    return from_heads(o_l), o_c


def diff_mixer(p_lat, p_ctx, q_g, k_g, lq1, lk1, lq2, lk2, subln_g, lambda_init, cos, sin, with_ctx):
    scale = DIFF_D ** -0.5
    lq1f, lk1f = lq1.astype(jnp.float32), lk1.astype(jnp.float32)
    lq2f, lk2f = lq2.astype(jnp.float32), lk2.astype(jnp.float32)
    lam = jnp.exp(jnp.sum(lq1f * lk1f)) - jnp.exp(jnp.sum(lq2f * lk2f)) + lambda_init

    def pair(t, g):
        b, n, _ = t.shape
        t = rmsnorm(t.reshape(b, n, DIFF_HEADS, 2, DIFF_D), g).transpose(3, 0, 2, 1, 4)
        return t[0], t[1]

    def diff_attend(q1, q2, k1, k2, v):
        s1 = jnp.matmul(q1, jnp.swapaxes(k1, -1, -2)).astype(jnp.float32) * scale
        s2 = jnp.matmul(q2, jnp.swapaxes(k2, -1, -2)).astype(jnp.float32) * scale
        w = jax.nn.softmax(s1, axis=-1) - lam * jax.nn.softmax(s2, axis=-1)
        return jnp.matmul(w.astype(v.dtype), v)

    def finish(o):
        return from_heads(rmsnorm(o, subln_g) * (1.0 - lambda_init))

    q_l, k_l, v_l = p_lat
    q_c, k_c, v_c = p_ctx
    q1l, q2l = pair(q_l, q_g)
    k1l, k2l = pair(k_l, k_g)
    q1l, q2l = apply_axial_rope(q1l, cos, sin), apply_axial_rope(q2l, cos, sin)
    k1l, k2l = apply_axial_rope(k1l, cos, sin), apply_axial_rope(k2l, cos, sin)
    k1c, k2c = pair(k_c, k_g)
    vl = to_heads(v_l, DIFF_HEADS)
    vc = to_heads(v_c, DIFF_HEADS)
    k1a = jnp.concatenate([k1l, k1c], axis=2)
    k2a = jnp.concatenate([k2l, k2c], axis=2)
    va = jnp.concatenate([vl, vc], axis=2)
    o_l = sweep_query_blocks(lambda a, b: diff_attend(a, b, k1a, k2a, va), q1l, q2l)
    if with_ctx:
        q1c, q2c = pair(q_c, q_g)
        o_c = finish(diff_attend(q1c, q2c, k1c, k2c, vc))
    else:
        o_c = None
    return finish(o_l), o_c


def na_mixer(p_lat, p_ctx, q_g, k_g, rpb, with_ctx):
    scale = NA_D ** -0.5
    q_l, k_l, v_l = [to_heads(t, NA_HEADS) for t in p_lat]
    q_c, k_c, v_c = [to_heads(t, NA_HEADS) for t in p_ctx]
    q_l, k_l = rmsnorm(q_l, q_g), rmsnorm(k_l, k_g)
    q_c, k_c = rmsnorm(q_c, q_g), rmsnorm(k_c, k_g)
    b, h, n_tok, d = q_l.shape
    rows = n_tok // GRID_W
    wr = min(NA_ROWS, rows)
    ncb = GRID_W // NA_QCOLS
    n_loc = wr * NA_BAND

    r = jnp.arange(rows)
    row_start = jnp.clip(r - wr // 2, 0, rows - wr)
    row_idx = row_start[:, None] + jnp.arange(wr)
    qcol = jnp.arange(ncb)[:, None] * NA_QCOLS + jnp.arange(NA_QCOLS)
    band_start = jnp.clip(jnp.arange(ncb) * NA_QCOLS - NA_COLS // 2, 0, GRID_W - NA_BAND)
    col_idx = band_start[:, None] + jnp.arange(NA_BAND)
    win_start = jnp.clip(qcol - NA_COLS // 2, 0, GRID_W - NA_COLS)
    valid = (col_idx[:, None, :] >= win_start[..., None]) & (col_idx[:, None, :] < win_start[..., None] + NA_COLS)
    valid = jnp.broadcast_to(valid[:, :, None, :], (ncb, NA_QCOLS, wr, NA_BAND)).reshape(ncb, NA_QCOLS, n_loc)
    row_off = row_idx - r[:, None] + (NA_ROWS - 1)
    col_off = jnp.clip(col_idx[:, None, :] - qcol[..., None] + (NA_COLS - 1), 0, 2 * NA_COLS - 2)
    bias = rpb[:, row_off[:, None, None, :, None], col_off[None, :, :, None, :]]
    bias = bias.reshape(h, rows, ncb, NA_QCOLS, n_loc).astype(jnp.float32)

    def band(t):
        g = t.reshape(b, h, rows, GRID_W, d)[:, :, row_idx[:, None, :, None], col_idx[None, :, None, :]]
        return g.reshape(b, h, rows, ncb, n_loc, d)

    kb, vb = band(k_l), band(v_l)
    qb = q_l.reshape(b, h, rows, ncb, NA_QCOLS, d)
    s_loc = jnp.einsum('bhrjqd,bhrjkd->bhrjqk', qb, kb).astype(jnp.float32) * scale + bias[None]
    s_loc = jnp.where(valid, s_loc, NEG_INF)
    s_ctx = jnp.einsum('bhrjqd,bhcd->bhrjqc', qb, k_c).astype(jnp.float32) * scale
    p = jax.nn.softmax(jnp.concatenate([s_loc, s_ctx], axis=-1), axis=-1).astype(v_l.dtype)
    o = (jnp.einsum('bhrjqk,bhrjkd->bhrjqd', p[..., :n_loc], vb)
         + jnp.einsum('bhrjqc,bhcd->bhrjqd', p[..., n_loc:], v_c))
    o_l = from_heads(o.reshape(b, h, n_tok, d))
    o_c = from_heads(attend(q_c, k_c, v_c, scale)) if with_ctx else None
    return o_l, o_c


def gqa_mixer(p_lat, p_ctx, q_g, k_g, cos, sin, with_ctx):
    scale = GQA_D ** -0.5
    rep = GQA_HEADS // GQA_KV_HEADS

    def q_of(t):
        b, n, _ = t.shape
        q = rmsnorm(t.reshape(b, n, GQA_KV_HEADS, rep, GQA_D), q_g)
        return q.transpose(0, 2, 3, 1, 4)

    def k_of(t):
        return rmsnorm(to_heads(t, GQA_KV_HEADS), k_g)

    q_l, k_l, v_l = p_lat
    q_c, k_c, v_c = p_ctx
    ql = apply_axial_rope(q_of(q_l), cos, sin)
    kl = apply_axial_rope(k_of(k_l), cos, sin)[:, :, None]
    kc = k_of(k_c)[:, :, None]
    vl = to_heads(v_l, GQA_KV_HEADS)[:, :, None]
    vc = to_heads(v_c, GQA_KV_HEADS)[:, :, None]
    k_all = jnp.concatenate([kl, kc], axis=-2)
    v_all = jnp.concatenate([vl, vc], axis=-2)
    o = sweep_query_blocks(lambda qb: attend(qb, k_all, v_all, scale), ql)
    b, _, _, n_tok, d = o.shape
    o_l = from_heads(o.reshape(b, GQA_HEADS, n_tok, d))
    if with_ctx:
        oc = attend(q_of(q_c), kc, vc, scale)
        o_c = from_heads(oc.reshape(oc.shape[0], GQA_HEADS, oc.shape[3], d))
    else:
        o_c = None
    return o_l, o_c


def conv_ffn(h, w_up, conv_w, conv_b, w_down):
    u = h @ w_up
    n_tok = u.shape[1]
    pad = CONV_W // 2
    up = jnp.pad(u, ((0, 0), (pad, pad), (0, 0)))
    u = sum(up[:, i:i + n_tok] * conv_w[i] for i in range(CONV_W)) + conv_b
    a, g = jnp.split(u, 2, axis=-1)
    return (jax.nn.silu(g) * a) @ w_down


def setup_inputs(seed: int = 0) -> dict:
    key = jax.random.key(seed)
    ks = iter(jax.random.split(key, 40))
    L, D = DEPTH, D_MODEL

    def nrm(shape, s):
        return jax.random.normal(next(ks), shape, jnp.float32) * s

    def gain(shape):
        return 1.0 + nrm(shape, 0.05)

    return {
        'x': nrm((BATCH, SEQ, D), 1.0),
        'c': nrm((BATCH, D), 1.0),
        'ctx': nrm((BATCH, CTX_LEN, D), 1.0),
        'c_ctx': nrm((D,), 1.0),
        'w_mod': nrm((L, D, 6 * D), 0.5 * D ** -0.5),
        'b_mod': nrm((L, 6 * D), 0.01),
        'g_mix': gain((L, D)),
        'w_in': nrm((L, D, IN_COLS), D ** -0.5),
        'w_out': nrm((L, MIX_WIDTH, D), MIX_WIDTH ** -0.5),
        'mla_q_a_g': gain((L, MLA_Q_RANK)),
        'mla_w_uq': nrm((L, MLA_Q_RANK, MLA_HEADS * (MLA_NOPE + MLA_ROPE)), MLA_Q_RANK ** -0.5),
        'mla_kv_a_g': gain((L, MLA_KV_RANK)),
        'mla_w_ukv': nrm((L, MLA_KV_RANK, MLA_HEADS * (MLA_NOPE + MLA_V)), MLA_KV_RANK ** -0.5),
        'mla_q_g': gain((L, MLA_NOPE + MLA_ROPE)),
        'mla_k_g': gain((L, MLA_NOPE + MLA_ROPE)),
        'diff_q_g': gain((L, DIFF_D)),
        'diff_k_g': gain((L, DIFF_D)),
        'diff_lq1': nrm((L, DIFF_D), 0.1),
        'diff_lk1': nrm((L, DIFF_D), 0.1),
        'diff_lq2': nrm((L, DIFF_D), 0.1),
        'diff_lk2': nrm((L, DIFF_D), 0.1),
        'diff_subln_g': gain((L, 2 * DIFF_D)),
        'na_q_g': gain((L, NA_D)),
        'na_k_g': gain((L, NA_D)),
        'na_rpb': nrm((L, NA_HEADS, 2 * NA_ROWS - 1, 2 * NA_COLS - 1), 0.1),
        'gqa_q_g': gain((L, GQA_D)),
        'gqa_k_g': gain((L, GQA_D)),
        'g_ffn': gain((L, D)),
        'w_up': nrm((L, D, 2 * D_FF), D ** -0.5),
        'conv_w': nrm((L, CONV_W, 2 * D_FF), CONV_W ** -0.5),
        'conv_b': nrm((L, 2 * D_FF), 0.01),
        'w_down': nrm((L, D_FF, D), D_FF ** -0.5),
    }


def reference(x, c, ctx, c_ctx, w_mod, b_mod, g_mix, w_in, w_out,
              mla_q_a_g, mla_w_uq, mla_kv_a_g, mla_w_ukv, mla_q_g, mla_k_g,
              diff_q_g, diff_k_g, diff_lq1, diff_lk1, diff_lq2, diff_lk2, diff_subln_g,
              na_q_g, na_k_g, na_rpb, gqa_q_g, gqa_k_g,
              g_ffn, w_up, conv_w, conv_b, w_down):
    n_tok = x.shape[1]
    cos_a, sin_a = axial_rope_tables(n_tok, MLA_ROPE)
    cos_b, sin_b = axial_rope_tables(n_tok, DIFF_D)
    cos_d, sin_d = axial_rope_tables(n_tok, GQA_D)
    sc_lat = jax.nn.silu(c)
    sc_ctx = jax.nn.silu(c_ctx)
    for l in range(DEPTH):
        with_ctx = l < DEPTH - 1
        lambda_init = 0.8 - 0.6 * math.exp(-0.3 * l)
        m_lat = (sc_lat @ w_mod[l] + b_mod[l])[:, None, :]
        m_ctx = sc_ctx @ w_mod[l] + b_mod[l]
        sh1, s1, g1, sh2, s2, g2 = jnp.split(m_lat, 6, axis=-1)
        csh1, cs1, cg1, csh2, cs2, cg2 = jnp.split(m_ctx, 6, axis=-1)

        p_lat = split_cols(modulate(x, g_mix[l], sh1, s1) @ w_in[l], MIX_SPLITS)
        p_ctx = split_cols(modulate(ctx, g_mix[l], csh1, cs1) @ w_in[l], MIX_SPLITS)

        oa = mla_mixer(p_lat[0:3], p_ctx[0:3], mla_q_a_g[l], mla_w_uq[l], mla_kv_a_g[l], mla_w_ukv[l],
                       mla_q_g[l], mla_k_g[l], cos_a, sin_a, with_ctx)
        ob = diff_mixer(p_lat[3:6], p_ctx[3:6], diff_q_g[l], diff_k_g[l], diff_lq1[l], diff_lk1[l],
                        diff_lq2[l], diff_lk2[l], diff_subln_g[l], lambda_init, cos_b, sin_b, with_ctx)
        oc = na_mixer(p_lat[6:9], p_ctx[6:9], na_q_g[l], na_k_g[l], na_rpb[l], with_ctx)
        od = gqa_mixer(p_lat[9:12], p_ctx[9:12], gqa_q_g[l], gqa_k_g[l], cos_d, sin_d, with_ctx)

        mix_lat = jnp.concatenate([oa[0], ob[0], oc[0], od[0]], axis=-1) @ w_out[l]
        x = x + g1 * mix_lat
        x = x + g2 * conv_ffn(modulate(x, g_ffn[l], sh2, s2), w_up[l], conv_w[l], conv_b[l], w_down[l])
        if with_ctx:
            mix_ctx = jnp.concatenate([oa[1], ob[1], oc[1], od[1]], axis=-1) @ w_out[l]
            ctx = ctx + cg1 * mix_ctx
            ctx = ctx + cg2 * conv_ffn(modulate(ctx, g_ffn[l], csh2, cs2), w_up[l], conv_w[l], conv_b[l], w_down[l])
    return x
```

```python
import functools
import math

import numpy as np
import jax
import jax.numpy as jnp
from jax import lax
from jax.experimental import pallas as pl
from jax.experimental.pallas import tpu as pltpu

F32 = jnp.float32
BF16 = jnp.bfloat16

D = 1024
SEQ = 2048
GRID_W = 64
CTX = 256
T = SEQ + CTX
RB = 256
NRB = T // RB
NLAT = SEQ // RB
EPS = 1e-6
NEG_INF = -1e30
ROPE_THETA = 10000.0

MLA_H, MLA_NOPE, MLA_ROPE, MLA_V, MLA_QR, MLA_KVR = 4, 64, 32, 64, 256, 128
MLA_D = MLA_NOPE + MLA_ROPE
DIFF_H, DIFF_D = 4, 32
NA_H, NA_D, NA_ROWS, NA_COLS = 4, 64, 8, 16
GQA_H, GQA_KV, GQA_D = 4, 2, 64
D_FF = 2816
FF_CHUNK = 256
N_CHUNK = D_FF // FF_CHUNK
HALO = 16

R_CQ, R_CKV, R_KR, R_DQ, R_DK, R_NQ, R_NK, R_GQ, R_GK = 0, 256, 384, 416, 672, 928, 1184, 1440, 1696
QK_ROWS = 1824
V_COLS = 640

G_QA, G_KVA, G_MQ, G_MK, G_DQ, G_DK, G_NQ, G_NK, G_GQ, G_GK = 0, 256, 384, 480, 576, 608, 640, 704, 768, 832
G_ROWS = 896

VMEM_LIMIT = 56 * 1024 * 1024


def _params(sem):
    return pltpu.CompilerParams(dimension_semantics=sem, vmem_limit_bytes=VMEM_LIMIT)


def _dot(a, b):
    return jnp.dot(a, b, preferred_element_type=F32)


def _dot_nt(a, b):
    return lax.dot_general(a, b, (((1,), (1,)), ((), ())), preferred_element_type=F32)


def _mod_kernel(c_ref, w_ref, b_ref, o_ref):
    c = c_ref[...]
    a = (c * jax.nn.sigmoid(c)).astype(BF16)
    o_ref[0] = _dot(a, w_ref[0].astype(BF16)) + b_ref[0]


def _modulation(cc, w_mod, b_mod):
    n_layer = w_mod.shape[0]
    rows = cc.shape[0]
    return pl.pallas_call(
        _mod_kernel,
        grid=(n_layer, 6),
        in_specs=[
            pl.BlockSpec((rows, D), lambda l, j: (0, 0)),
            pl.BlockSpec((1, D, D), lambda l, j: (l, 0, j)),
            pl.BlockSpec((1, 1, D), lambda l, j: (l, 0, j)),
        ],
        out_specs=pl.BlockSpec((1, rows, D), lambda l, j: (l, 0, j)),
        out_shape=jax.ShapeDtypeStruct((n_layer, rows, 6 * D), F32),
        compiler_params=_params(("parallel", "parallel")),
        name="modulation",
    )(cc, w_mod, b_mod.reshape(n_layer, 1, 6 * D))


def _rms_rows(x, g, n):
    ss = jnp.sum(x * x, axis=0, keepdims=True) * (1.0 / n)
    return x * lax.rsqrt(ss + EPS) * g


def _rope_rows(x, cos, sin_signed, w):
    rot = jnp.concatenate([x[w:2 * w], x[0:w], x[3 * w:4 * w], x[2 * w:3 * w]], axis=0)
    return x * cos + rot * sin_signed


def _inproj_kernel(x_ref, mod_ref, gmix_ref, wqk_ref, wv_ref, wuq_ref, wukn_ref, wuv_ref, gcol_ref,
                   cs32_ref, cs64_ref,
                   qmla_ref, kmla_ref, vmla_ref, qdiff_ref, kdiff_ref, vdiff_ref,
                   qna_ref, kna_ref, vna_ref, qgqa_ref, kgqa_ref, vgqa_ref):
    x = x_ref[0]
    mod = mod_ref[0]
    shift, scale = mod[:, 0:D], mod[:, D:2 * D]
    ms = jnp.mean(x * x, axis=-1, keepdims=True)
    h = x * lax.rsqrt(ms + EPS) * gmix_ref[...]
    hb = (h * (1.0 + scale) + shift).astype(BF16)

    pt = _dot_nt(wqk_ref[...], hb)
    pv = _dot(hb, wv_ref[...])
    vdiff_ref[0] = pv[:, 0:256].astype(BF16)
    vna_ref[0] = pv[:, 256:512].astype(BF16)
    vgqa_ref[0] = pv[:, 512:640].astype(BF16)

    def gain(off, n):
        return gcol_ref[off:off + n, :]

    cos32, sin32 = cs32_ref[0:32, :], cs32_ref[32:64, :]
    cos64, sin64 = cs64_ref[0:64, :], cs64_ref[64:128, :]

    cq = _rms_rows(pt[R_CQ:R_CQ + MLA_QR], gain(G_QA, MLA_QR), MLA_QR).astype(BF16)
    qt = _dot(wuq_ref[...], cq)
    ckv = _rms_rows(pt[R_CKV:R_CKV + MLA_KVR], gain(G_KVA, MLA_KVR), MLA_KVR).astype(BF16)
    knt = _dot(wukn_ref[...], ckv)
    vt = _dot(wuv_ref[...], ckv)
    vmla_ref[0] = vt.T.astype(BF16)
    kr = pt[R_KR:R_KR + MLA_ROPE]
    kr_ss = jnp.sum(kr * kr, axis=0, keepdims=True)
    g_mq, g_mk = gain(G_MQ, MLA_D), gain(G_MK, MLA_D)
    zpad = jnp.zeros((128 - MLA_D, RB), F32)
    q_parts = []
    for hd in range(MLA_H):
        qh = _rms_rows(qt[hd * MLA_D:(hd + 1) * MLA_D], g_mq, MLA_D)
        q_rope = _rope_rows(qh[MLA_NOPE:], cos32, sin32, MLA_ROPE // 4)
        q_parts += [qh[:MLA_NOPE], q_rope, zpad]
        kn = knt[hd * MLA_NOPE:(hd + 1) * MLA_NOPE]
        ss = (jnp.sum(kn * kn, axis=0, keepdims=True) + kr_ss) * (1.0 / MLA_D)
        r = lax.rsqrt(ss + EPS)
        k_rope = _rope_rows(kr * r * g_mk[MLA_NOPE:], cos32, sin32, MLA_ROPE // 4)
        kmla_ref[0, hd * 128:(hd + 1) * 128, :] = jnp.concatenate(
            [kn * r * g_mk[:MLA_NOPE], k_rope, zpad], axis=0).astype(BF16)
    q_all = jnp.concatenate(q_parts, axis=0) * (MLA_D ** -0.5)
    qmla_ref[0] = q_all.T.astype(BF16)

    g_dq, g_dk = gain(G_DQ, DIFF_D), gain(G_DK, DIFF_D)
    q_parts, k_parts = [], []
    for gi in range(2 * DIFF_H):
        qg = _rms_rows(pt[R_DQ + gi * DIFF_D:R_DQ + (gi + 1) * DIFF_D], g_dq, DIFF_D)
        q_parts.append(_rope_rows(qg, cos32, sin32, DIFF_D // 4))
        kg = _rms_rows(pt[R_DK + gi * DIFF_D:R_DK + (gi + 1) * DIFF_D], g_dk, DIFF_D)
        k_parts.append(_rope_rows(kg, cos32, sin32, DIFF_D // 4))
    qdiff_ref[0] = (jnp.concatenate(q_parts, axis=0) * (DIFF_D ** -0.5)).T.astype(BF16)
    kdiff_ref[0] = jnp.concatenate(k_parts, axis=0).astype(BF16)

    g_nq, g_nk = gain(G_NQ, NA_D), gain(G_NK, NA_D)
    q_parts, k_parts = [], []
    for hd in range(NA_H):
        q_parts.append(_rms_rows(pt[R_NQ + hd * NA_D:R_NQ + (hd + 1) * NA_D], g_nq, NA_D))
        k_parts.append(_rms_rows(pt[R_NK + hd * NA_D:R_NK + (hd + 1) * NA_D], g_nk, NA_D))
    qna_ref[0] = (jnp.concatenate(q_parts, axis=0) * (NA_D ** -0.5)).T.astype(BF16)
    kna_ref[0] = jnp.concatenate(k_parts, axis=0).astype(BF16)

    g_gq, g_gk = gain(G_GQ, GQA_D), gain(G_GK, GQA_D)
    q_parts, k_parts = [], []
    for hd in range(GQA_H):
        qg = _rms_rows(pt[R_GQ + hd * GQA_D:R_GQ + (hd + 1) * GQA_D], g_gq, GQA_D)
        q_parts.append(_rope_rows(qg, cos64, sin64, GQA_D // 4))
    for hd in range(GQA_KV):
        kg = _rms_rows(pt[R_GK + hd * GQA_D:R_GK + (hd + 1) * GQA_D], g_gk, GQA_D)
        k_parts.append(_rope_rows(kg, cos64, sin64, GQA_D // 4))
    qgqa_ref[0] = (jnp.concatenate(q_parts, axis=0) * (GQA_D ** -0.5)).T.astype(BF16)
    kgqa_ref[0] = jnp.concatenate(k_parts, axis=0).astype(BF16)


def _inproj(xs, mods, gmix, wqk, wv, wuq, wukn, wuv, gcol, cs32, cs64):
    n_b = xs.shape[0]

    def full(a):
        return pl.BlockSpec(a.shape, lambda r, b, _n=a.ndim: (0,) * _n)

    def tok(width):
        return pl.BlockSpec((1, RB, width), lambda r, b: (b, r, 0))

    def chan(rows):
        return pl.BlockSpec((1, rows, RB), lambda r, b: (b, 0, r))

    def tshape(width):
        return jax.ShapeDtypeStruct((n_b, T, width), BF16)

    def cshape(rows):
        return jax.ShapeDtypeStruct((n_b, rows, T), BF16)

    return pl.pallas_call(
        _inproj_kernel,
        grid=(NRB, n_b),
        in_specs=[
            pl.BlockSpec((1, RB, D), lambda r, b: (b, r, 0)),
            pl.BlockSpec((1, 1, 6 * D), lambda r, b: (jnp.where(r < NLAT, b, n_b), 0, 0)),
            full(gmix), full(wqk), full(wv), full(wuq), full(wukn), full(wuv), full(gcol),
            pl.BlockSpec((64, RB), lambda r, b: (0, r)),
            pl.BlockSpec((128, RB), lambda r, b: (0, r)),
        ],
        out_specs=[tok(512), chan(512), tok(256), tok(256), chan(256), tok(256),
                   tok(256), chan(256), tok(256), tok(256), chan(128), tok(128)],
        out_shape=[tshape(512), cshape(512), tshape(256), tshape(256), cshape(256), tshape(256),
                   tshape(256), cshape(256), tshape(256), tshape(256), cshape(128), tshape(128)],
        compiler_params=_params(("parallel", "parallel")),
        name="inproj_prep",
    )(xs, mods, gmix, wqk, wv, wuq, wukn, wuv, gcol, cs32, cs64)


def _exp_sum(s):
    m = jnp.max(s, axis=-1, keepdims=True)
    e = jnp.exp(s - m)
    return e, jnp.sum(e, axis=-1, keepdims=True)


def _lane_id(shape):
    return lax.broadcasted_iota(jnp.int32, shape, 1)


def _mla_kernel(q_ref, k_ref, v_ref, o_ref, *, k_lo_ctx):
    lane = _lane_id((RB, 128))

    def run(k_lo):
        outs = []
        for pair in range(2):
            halves = []
            for sub in range(2):
                hd = 2 * pair + sub
                s = _dot(q_ref[0, :, hd * 128:(hd + 1) * 128], k_ref[0, hd * 128:(hd + 1) * 128, k_lo:])
                e, l = _exp_sum(s)
                o = _dot(e.astype(BF16), v_ref[0, k_lo:, pair * 128:(pair + 1) * 128])
                halves.append(o * (1.0 / l))
            outs.append(jnp.where(lane < 64, halves[0], halves[1]))
        o_ref[0] = jnp.concatenate(outs, axis=1).astype(o_ref.dtype)

    rb = pl.program_id(1)

    @pl.when(rb < NLAT)
    def _():
        run(0)

    @pl.when(rb >= NLAT)
    def _():
        run(k_lo_ctx)


def _gqa_kernel(q_ref, k_ref, v_ref, o_ref, *, k_lo_ctx):
    lane = _lane_id((RB, 128))

    def run(k_lo):
        kt = k_ref[0, :, k_lo:]
        vv = v_ref[0, k_lo:, :]
        res = {}
        for rep in range(2):
            qs = q_ref[0, :, rep * 128:(rep + 1) * 128]
            for grp in range(2):
                qm = jnp.where((lane >= 64) == (grp == 1), qs, jnp.zeros_like(qs))
                e, l = _exp_sum(_dot(qm, kt))
                o = _dot(e.astype(BF16), vv) * (1.0 / l)
                if grp != rep:
                    o = pltpu.roll(o, 64, axis=1)
                res[(grp, rep)] = o
        outs = [jnp.where(lane < 64, res[(grp, 0)], res[(grp, 1)]) for grp in range(2)]
        o_ref[0] = jnp.concatenate(outs, axis=1).astype(o_ref.dtype)

    rb = pl.program_id(1)

    @pl.when(rb < NLAT)
    def _():
        run(0)

    @pl.when(rb >= NLAT)
    def _():
        run(k_lo_ctx)


def _diff_kernel(q_ref, k_ref, v_ref, lam_ref, g_ref, o_ref, *, k_lo_ctx, lambda_init):
    lane = _lane_id((RB, 128))
    lq1, lk1, lq2, lk2 = lam_ref[0:1, :], lam_ref[1:2, :], lam_ref[2:3, :], lam_ref[3:4, :]
    lam = (jnp.exp(jnp.sum(lq1 * lk1, axis=-1, keepdims=True))
           - jnp.exp(jnp.sum(lq2 * lk2, axis=-1, keepdims=True)) + lambda_init)
    gsub = g_ref[...]

    def run(k_lo):
        outs = []
        for pair in range(2):
            qs = q_ref[0, :, pair * 128:(pair + 1) * 128]
            kt = k_ref[0, pair * 128:(pair + 1) * 128, k_lo:]
            vv = v_ref[0, k_lo:, pair * 128:(pair + 1) * 128]
            halves = []
            for sub in range(2):
                grp = lane // DIFF_D
                q1 = jnp.where(grp == 2 * sub, qs, jnp.zeros_like(qs))
                q2 = jnp.where(grp == 2 * sub + 1, qs, jnp.zeros_like(qs))
                e1, l1 = _exp_sum(_dot(q1, kt))
                e2, l2 = _exp_sum(_dot(q2, kt))
                w = e1 * (1.0 / l1) - e2 * (lam / l2)
                halves.append(_dot(w.astype(BF16), vv))
            o = jnp.where(lane < 64, halves[0], halves[1])
            o2 = o * o
            ss0 = jnp.sum(jnp.where(lane < 64, o2, 0.0), axis=-1, keepdims=True)
            ss1 = jnp.sum(jnp.where(lane < 64, 0.0, o2), axis=-1, keepdims=True)
            ss = jnp.where(lane < 64, ss0, ss1) * (1.0 / (2 * DIFF_D))
            outs.append(o * lax.rsqrt(ss + EPS) * gsub * (1.0 - lambda_init))
        o_ref[0] = jnp.concatenate(outs, axis=1).astype(o_ref.dtype)

    rb = pl.program_id(1)

    @pl.when(rb < NLAT)
    def _():
        run(0)

    @pl.when(rb >= NLAT)
    def _():
        run(k_lo_ctx)


NA_WIN = 1024


def _na_kernel(q_ref, k_ref, v_ref, bias_ref, o_ref):
    lane = _lane_id((RB, 128))
    rb = pl.program_id(1)

    def finish(outs):
        o_ref[0] = jnp.concatenate(outs, axis=1).astype(o_ref.dtype)

    @pl.when(rb < NLAT)
    def _():
        off = pl.multiple_of(jnp.clip((rb - 1) * RB, 0, SEQ - NA_WIN), RB)
        outs = []
        for pair in range(2):
            qs = q_ref[0, :, pair * 128:(pair + 1) * 128]
            k_loc = k_ref[0, pair * 128:(pair + 1) * 128, pl.ds(off, NA_WIN)]
            k_ctx = k_ref[0, pair * 128:(pair + 1) * 128, SEQ:]
            v_loc = v_ref[0, pl.ds(off, NA_WIN), pair * 128:(pair + 1) * 128]
            v_ctx = v_ref[0, SEQ:, pair * 128:(pair + 1) * 128]
            halves = []
            for sub in range(2):
                qm = jnp.where((lane >= 64) == (sub == 1), qs, jnp.zeros_like(qs))
                s_loc = _dot(qm, k_loc) + bias_ref[2 * pair + sub, 0]
                s_ctx = _dot(qm, k_ctx)
                m = jnp.maximum(jnp.max(s_loc, axis=-1, keepdims=True), jnp.max(s_ctx, axis=-1, keepdims=True))
                e_loc, e_ctx = jnp.exp(s_loc - m), jnp.exp(s_ctx - m)
                l = jnp.sum(e_loc, axis=-1, keepdims=True) + jnp.sum(e_ctx, axis=-1, keepdims=True)
                o = _dot(e_loc.astype(BF16), v_loc) + _dot(e_ctx.astype(BF16), v_ctx)
                halves.append(o * (1.0 / l))
            outs.append(jnp.where(lane < 64, halves[0], halves[1]))
        finish(outs)

    @pl.when(rb >= NLAT)
    def _():
        outs = []
        for pair in range(2):
            qs = q_ref[0, :, pair * 128:(pair + 1) * 128]
            k_ctx = k_ref[0, pair * 128:(pair + 1) * 128, SEQ:]
            v_ctx = v_ref[0, SEQ:, pair * 128:(pair + 1) * 128]
            halves = []
            for sub in range(2):
                qm = jnp.where((lane >= 64) == (sub == 1), qs, jnp.zeros_like(qs))
                e, l = _exp_sum(_dot(qm, k_ctx))
                halves.append(_dot(e.astype(BF16), v_ctx) * (1.0 / l))
            outs.append(jnp.where(lane < 64, halves[0], halves[1]))
        finish(outs)


def _attention(body, name, q, k, v, extra, n_rb, out_width=256):
    n_b = q.shape[0]
    in_specs = [
        pl.BlockSpec((1, RB, q.shape[2]), lambda b, r: (b, r, 0)),
        pl.BlockSpec((1, k.shape[1], T), lambda b, r: (b, 0, 0)),
        pl.BlockSpec((1, T, v.shape[2]), lambda b, r: (b, 0, 0)),
    ] + [spec for _, spec in extra]
    return pl.pallas_call(
        body,
        grid=(n_b, n_rb),
        in_specs=in_specs,
        out_specs=pl.BlockSpec((1, RB, out_width), lambda b, r: (b, r, 0)),
        out_shape=jax.ShapeDtypeStruct((n_b, n_rb * RB, out_width), BF16),
        compiler_params=_params(("parallel", "parallel")),
        name=name,
    )(q, k, v, *[a for a, _ in extra])


def _outproj_kernel(x_ref, mod_ref, ma_ref, mb_ref, mc_ref, md_ref, w_ref, o_ref):
    g1 = mod_ref[0][:, 2 * D:3 * D]
    acc = _dot(ma_ref[0], w_ref[0:256, :])
    acc += _dot(mb_ref[0], w_ref[256:512, :])
    acc += _dot(mc_ref[0], w_ref[512:768, :])
    acc += _dot(md_ref[0], w_ref[768:1024, :])
    o_ref[0] = x_ref[0] + g1 * acc


def _outproj(xs, mods, mixes, w_out, n_rb):
    n_b = xs.shape[0]
    mix_spec = pl.BlockSpec((1, RB, 256), lambda b, r: (b, r, 0))
    return pl.pallas_call(
        _outproj_kernel,
        grid=(n_b, n_rb),
        in_specs=[
            pl.BlockSpec((1, RB, D), lambda b, r: (b, r, 0)),
            pl.BlockSpec((1, 1, 6 * D), lambda b, r: (jnp.where(r < NLAT, b, n_b), 0, 0)),
            mix_spec, mix_spec, mix_spec, mix_spec,
            pl.BlockSpec((D, D), lambda b, r: (0, 0)),
        ],
        out_specs=pl.BlockSpec((1, RB, D), lambda b, r: (b, r, 0)),
        out_shape=jax.ShapeDtypeStruct((n_b, n_rb * RB, D), F32),
        compiler_params=_params(("parallel", "parallel")),
        name="outproj",
    )(xs, mods, *mixes, w_out)


def _ffn_kernel(x_ref, hb_ref, ha_ref, modb_ref, modc_ref, g_ref, wa_ref, wg_ref, cwa_ref, cwg_ref,
                cba_ref, cbg_ref, wd_ref, o_ref, h_scr, ua_scr, ug_scr, *, tm, rows_per_sample):
    rb = pl.program_id(1)
    j = pl.program_id(2)
    row0 = rb * tm

    def per_row(rows, col):
        lat = modb_ref[0][:, col * D:(col + 1) * D]
        if rows_per_sample == SEQ:
            return lat
        return jnp.where(rows >= SEQ, modc_ref[0][:, col * D:(col + 1) * D], lat)

    def modulated(x, rows):
        ms = jnp.mean(x * x, axis=-1, keepdims=True)
        y = x * lax.rsqrt(ms + EPS) * g_ref[...]
        return (y * (1.0 + per_row(rows, 4)) + per_row(rows, 3)).astype(BF16)

    @pl.when(j == 0)
    def _():
        halo_rows = lax.broadcasted_iota(jnp.int32, (HALO, 1), 0)
        h_scr[0:HALO, :] = modulated(hb_ref[0], row0 - HALO + halo_rows)
        h_scr[HALO + tm:, :] = modulated(ha_ref[0], row0 + tm + halo_rows)
        main_rows = row0 + lax.broadcasted_iota(jnp.int32, (tm, 1), 0)
        h_scr[HALO:HALO + tm, :] = modulated(x_ref[0], main_rows)
        o_ref[0] = jnp.zeros((tm, D), F32)

    hb = h_scr[...]
    ua_scr[...] = _dot(hb, wa_ref[...])
    ug_scr[...] = _dot(hb, wg_ref[...])

    rows = row0 + lax.broadcasted_iota(jnp.int32, (tm, 1), 0)
    has_prev = (rows != 0) & (rows != SEQ)
    has_next = (rows != SEQ - 1) & (rows != rows_per_sample - 1)

    def conv(u_scr, cw_ref, cb_ref):
        prev = jnp.where(has_prev, u_scr[HALO - 1:HALO - 1 + tm, :], 0.0)
        nxt = jnp.where(has_next, u_scr[HALO + 1:HALO + 1 + tm, :], 0.0)
        return (prev * cw_ref[0:1, :] + u_scr[HALO:HALO + tm, :] * cw_ref[1:2, :]
                + nxt * cw_ref[2:3, :] + cb_ref[...])

    a = conv(ua_scr, cwa_ref, cba_ref)
    g = conv(ug_scr, cwg_ref, cbg_ref)
    act = (g * jax.nn.sigmoid(g) * a).astype(BF16)
    o_ref[0] += _dot(act, wd_ref[...])

    @pl.when(j == N_CHUNK - 1)
    def _():
        o_ref[0] = x_ref[0] + per_row(rows, 5) * o_ref[0]


def _ffn(x1, mods, g_ffn, w_up, conv_w, conv_b, w_down, rows_per_sample, tm):
    n_b = x1.shape[0]
    n_rb = rows_per_sample // tm
    halo_per_block = tm // HALO
    n_halo = x1.shape[1] // HALO
    body = functools.partial(_ffn_kernel, tm=tm, rows_per_sample=rows_per_sample)
    return pl.pallas_call(
        body,
        grid=(n_b, n_rb, N_CHUNK),
        in_specs=[
            pl.BlockSpec((1, tm, D), lambda b, r, j: (b, r, 0)),
            pl.BlockSpec((1, HALO, D), lambda b, r, j: (b, jnp.maximum(r * halo_per_block - 1, 0), 0)),
            pl.BlockSpec((1, HALO, D), lambda b, r, j: (b, jnp.minimum((r + 1) * halo_per_block, n_halo - 1), 0)),
            pl.BlockSpec((1, 1, 6 * D), lambda b, r, j: (b, 0, 0)),
            pl.BlockSpec((1, 1, 6 * D), lambda b, r, j: (n_b, 0, 0)),
            pl.BlockSpec((1, D), lambda b, r, j: (0, 0)),
            pl.BlockSpec((D, FF_CHUNK), lambda b, r, j: (0, j)),
            pl.BlockSpec((D, FF_CHUNK), lambda b, r, j: (0, N_CHUNK + j)),
            pl.BlockSpec((3, FF_CHUNK), lambda b, r, j: (0, j)),
            pl.BlockSpec((3, FF_CHUNK), lambda b, r, j: (0, N_CHUNK + j)),
            pl.BlockSpec((1, FF_CHUNK), lambda b, r, j: (0, j)),
            pl.BlockSpec((1, FF_CHUNK), lambda b, r, j: (0, N_CHUNK + j)),
            pl.BlockSpec((FF_CHUNK, D), lambda b, r, j: (j, 0)),
        ],
        out_specs=pl.BlockSpec((1, tm, D), lambda b, r, j: (b, r, 0)),
        out_shape=jax.ShapeDtypeStruct((n_b, rows_per_sample, D), F32),
        scratch_shapes=[
            pltpu.VMEM((tm + 2 * HALO, D), BF16),
            pltpu.VMEM((tm + 2 * HALO, FF_CHUNK), F32),
            pltpu.VMEM((tm + 2 * HALO, FF_CHUNK), F32),
        ],
        compiler_params=_params(("parallel", "parallel", "arbitrary")),
        name="conv_ffn",
    )(x1, x1, x1, mods, mods, g_ffn, w_up, w_up, conv_w, conv_w, conv_b, conv_b, w_down)


def _rope_table(rot_dim):
    n_freq = rot_dim // 4
    inv = jnp.power(ROPE_THETA, -jnp.arange(n_freq, dtype=F32) / n_freq)
    t = jnp.arange(SEQ)
    row = (t // GRID_W).astype(F32)
    col = (t % GRID_W).astype(F32)
    ar, ac = row[:, None] * inv, col[:, None] * inv
    ang = jnp.concatenate([ar, ar, ac, ac], axis=-1)
    sign = jnp.concatenate([-jnp.ones(n_freq), jnp.ones(n_freq), -jnp.ones(n_freq), jnp.ones(n_freq)]).astype(F32)
    cos = jnp.concatenate([jnp.cos(ang), jnp.ones((CTX, rot_dim), F32)], axis=0)
    sin = jnp.concatenate([jnp.sin(ang) * sign, jnp.zeros((CTX, rot_dim), F32)], axis=0)
    return jnp.concatenate([cos.T, sin.T], axis=0)


def _na_bias_index():
    rows = SEQ // GRID_W
    ro = np.zeros((4, RB, NA_WIN), np.int32)
    co = np.zeros((4, RB, NA_WIN), np.int32)
    ok = np.zeros((4, RB, NA_WIN), bool)
    qi = np.arange(RB)
    ki = np.arange(NA_WIN)
    for idx, rb in enumerate((0, 1, NLAT - 2, NLAT - 1)):
        r = 4 * rb + qi // GRID_W
        c = qi % GRID_W
        kr = np.clip(4 * rb - 4, 0, rows - 16) + ki // GRID_W
        kc = ki % GRID_W
        row_start = np.clip(r - NA_ROWS // 2, 0, rows - NA_ROWS)
        win_start = np.clip(c - NA_COLS // 2, 0, GRID_W - NA_COLS)
        v_row = (kr[None, :] >= row_start[:, None]) & (kr[None, :] < row_start[:, None] + NA_ROWS)
        v_col = (kc[None, :] >= win_start[:, None]) & (kc[None, :] < win_start[:, None] + NA_COLS)
        ok[idx] = v_row & v_col
        ro[idx] = np.clip(kr[None, :] - r[:, None] + NA_ROWS - 1, 0, 2 * NA_ROWS - 2)
        co[idx] = np.clip(kc[None, :] - c[:, None] + NA_COLS - 1, 0, 2 * NA_COLS - 2)
    return ro, co, ok


def _col(v):
    return v.reshape(-1, 1).astype(F32)


def kernel(x, c, ctx, c_ctx, w_mod, b_mod, g_mix, w_in, w_out, mla_q_a_g, mla_w_uq, mla_kv_a_g, mla_w_ukv,
           mla_q_g, mla_k_g, diff_q_g, diff_k_g, diff_lq1, diff_lk1, diff_lq2, diff_lk2, diff_subln_g,
           na_q_g, na_k_g, na_rpb, gqa_q_g, gqa_k_g, g_ffn, w_up, conv_w, conv_b, w_down):
    n_b = x.shape[0]
    n_layer = w_mod.shape[0]
    assert x.shape[1:] == (SEQ, D) and ctx.shape[1:] == (CTX, D)

    xs = jnp.concatenate([x, ctx], axis=1)
    mod_rows = -(-(n_b + 1) // 8) * 8
    cc = jnp.concatenate([c, c_ctx[None], jnp.zeros((mod_rows - n_b - 1, D), F32)], axis=0)
    mods_all = _modulation(cc, w_mod, b_mod).reshape(n_layer, mod_rows, 1, 6 * D)

    cs32, cs64 = _rope_table(32), _rope_table(64)
    ro, co, ok = _na_bias_index()
    k_lo_ctx = SEQ

    s = np.cumsum([0, 256, 128, 32, 256, 256, 256, 256, 256, 256, 256, 128, 128])
    gq_perm = np.concatenate([np.arange(GQA_D) + (2 * g + r) * GQA_D for r in range(2) for g in range(2)])

    for l in range(n_layer):
        with_ctx = l < n_layer - 1
        lambda_init = 0.8 - 0.6 * math.exp(-0.3 * l)
        mods = mods_all[l]
        w = w_in[l]
        gq_cols = w[:, s[9]:s[10]][:, gq_perm]
        wqk = jnp.concatenate([w[:, s[0]:s[3]], w[:, s[3]:s[5]], w[:, s[6]:s[8]], gq_cols, w[:, s[10]:s[11]]],
                              axis=1).T.astype(BF16)
        wv = jnp.concatenate([w[:, s[5]:s[6]], w[:, s[8]:s[9]], w[:, s[11]:s[12]]], axis=1).astype(BF16)
        wuq = mla_w_uq[l].T.astype(BF16)
        wukv = mla_w_ukv[l].reshape(MLA_KVR, MLA_H, MLA_NOPE + MLA_V)
        wukn = wukv[:, :, :MLA_NOPE].reshape(MLA_KVR, MLA_H * MLA_NOPE).T.astype(BF16)
        wuv = wukv[:, :, MLA_NOPE:].reshape(MLA_KVR, MLA_H * MLA_V).T.astype(BF16)
        gcol = jnp.concatenate([_col(mla_q_a_g[l]), _col(mla_kv_a_g[l]), _col(mla_q_g[l]), _col(mla_k_g[l]),
                                _col(diff_q_g[l]), _col(diff_k_g[l]), _col(na_q_g[l]), _col(na_k_g[l]),
                                _col(gqa_q_g[l]), _col(gqa_k_g[l])], axis=0)
        (q_mla, k_mla, v_mla, q_diff, k_diff, v_diff, q_na, k_na, v_na, q_gqa, k_gqa, v_gqa) = _inproj(
            xs, mods, g_mix[l].reshape(1, D), wqk, wv, wuq, wukn, wuv, gcol, cs32, cs64)

        n_rb = NRB if with_ctx else NLAT
        mix_a = _attention(functools.partial(_mla_kernel, k_lo_ctx=k_lo_ctx), "attn_mla", q_mla, k_mla, v_mla, [], n_rb)
        lam_rows = jnp.stack([diff_lq1[l], diff_lk1[l], diff_lq2[l], diff_lk2[l]]).astype(F32)
        lam_rows = jnp.concatenate([lam_rows, jnp.zeros((4, DIFF_D), F32)], axis=0)
        gsub = jnp.tile(diff_subln_g[l].astype(F32), 2).reshape(1, 128)
        mix_b = _attention(
            functools.partial(_diff_kernel, k_lo_ctx=k_lo_ctx, lambda_init=lambda_init), "attn_diff",
            q_diff, k_diff, v_diff,
            [(lam_rows, pl.BlockSpec((8, DIFF_D), lambda b, r: (0, 0))),
             (gsub, pl.BlockSpec((1, 128), lambda b, r: (0, 0)))], n_rb)
        bias = jnp.where(ok[None], na_rpb[l][:, ro, co], NEG_INF).astype(F32)
        bias_spec = pl.BlockSpec(
            (NA_H, 1, RB, NA_WIN),
            lambda b, r: (0, jnp.where(r == 0, 0, jnp.where(r < NLAT - 2, 1, jnp.where(r == NLAT - 2, 2, 3))), 0, 0))
        mix_c = _attention(_na_kernel, "attn_na", q_na, k_na, v_na, [(bias, bias_spec)], n_rb)
        mix_d = _attention(functools.partial(_gqa_kernel, k_lo_ctx=k_lo_ctx), "attn_gqa", q_gqa, k_gqa, v_gqa, [], n_rb)

        x1 = _outproj(xs, mods, (mix_a, mix_b, mix_c, mix_d), w_out[l].astype(BF16), n_rb)
        rows_per_sample = T if with_ctx else SEQ
        xs = _ffn(x1, mods, g_ffn[l].reshape(1, D), w_up[l].astype(BF16), conv_w[l], conv_b[l].reshape(1, 2 * D_FF),
                  w_down[l].astype(BF16), rows_per_sample, rows_per_sample // 2)
    return xs
```

```python
import functools
import math

import numpy as np
import jax
import jax.numpy as jnp
from jax import lax
from jax.experimental import pallas as pl
from jax.experimental.pallas import tpu as pltpu

F32 = jnp.float32
BF16 = jnp.bfloat16

D = 1024
SEQ = 2048
GRID_W = 64
CTX = 256
T = SEQ + CTX
RB = 256
NRB = T // RB
NLAT = SEQ // RB
EPS = 1e-6
NEG_INF = -1e30
ROPE_THETA = 10000.0

MLA_H, MLA_NOPE, MLA_ROPE, MLA_V, MLA_QR, MLA_KVR = 4, 64, 32, 64, 256, 128
MLA_D = MLA_NOPE + MLA_ROPE
DIFF_H, DIFF_D = 4, 32
NA_H, NA_D, NA_ROWS, NA_COLS = 4, 64, 8, 16
GQA_H, GQA_KV, GQA_D = 4, 2, 64
D_FF = 2816
FF_CHUNK = 256
N_CHUNK = D_FF // FF_CHUNK
HALO = 16

R_CQ, R_CKV, R_KR, R_DQ, R_DK, R_NQ, R_NK, R_GQ, R_GK = 0, 256, 384, 416, 672, 928, 1184, 1440, 1696
QK_ROWS = 1824
V_COLS = 640

G_QA, G_KVA, G_MQ, G_MK, G_DQ, G_DK, G_NQ, G_NK, G_GQ, G_GK = 0, 256, 384, 480, 576, 608, 640, 704, 768, 832
G_ROWS = 896

VMEM_LIMIT = 56 * 1024 * 1024


def _params(sem):
    return pltpu.CompilerParams(dimension_semantics=sem, vmem_limit_bytes=VMEM_LIMIT)


def _dot(a, b):
    return jnp.dot(a, b, preferred_element_type=F32)


def _dot_nt(a, b):
    return lax.dot_general(a, b, (((1,), (1,)), ((), ())), preferred_element_type=F32)


def _mod_kernel(c_ref, w_ref, b_ref, o_ref):
    c = c_ref[...]
    a = (c * jax.nn.sigmoid(c)).astype(BF16)
    o_ref[0] = _dot(a, w_ref[0].astype(BF16)) + b_ref[0]


def _modulation(cc, w_mod, b_mod):
    n_layer = w_mod.shape[0]
    rows = cc.shape[0]
    return pl.pallas_call(
        _mod_kernel,
        grid=(n_layer, 6),
        in_specs=[
            pl.BlockSpec((rows, D), lambda l, j: (0, 0)),
            pl.BlockSpec((1, D, D), lambda l, j: (l, 0, j)),
            pl.BlockSpec((1, 1, D), lambda l, j: (l, 0, j)),
        ],
        out_specs=pl.BlockSpec((1, rows, D), lambda l, j: (l, 0, j)),
        out_shape=jax.ShapeDtypeStruct((n_layer, rows, 6 * D), F32),
        compiler_params=_params(("parallel", "parallel")),
        name="modulation",
    )(cc, w_mod, b_mod.reshape(n_layer, 1, 6 * D))


def _rms_rows(x, g, n):
    ss = jnp.sum(x * x, axis=0, keepdims=True) * (1.0 / n)
    return x * lax.rsqrt(ss + EPS) * g


def _rope_rows(x, cos, sin_signed, w):
    rot = jnp.concatenate([x[w:2 * w], x[0:w], x[3 * w:4 * w], x[2 * w:3 * w]], axis=0)
    return x * cos + rot * sin_signed


def _inproj_kernel(x_ref, mod_ref, gmix_ref, wqk_ref, wv_ref, wuq_ref, wukn_ref, wuv_ref, gcol_ref,
                   cs32_ref, cs64_ref,
                   qmla_ref, kmla_ref, vmla_ref, qdiff_ref, kdiff_ref, vdiff_ref,
                   qna_ref, kna_ref, vna_ref, qgqa_ref, kgqa_ref, vgqa_ref):
    x = x_ref[0]
    mod = mod_ref[0]
    shift, scale = mod[:, 0:D], mod[:, D:2 * D]
    ms = jnp.mean(x * x, axis=-1, keepdims=True)
    h = x * lax.rsqrt(ms + EPS) * gmix_ref[...]
    hb = (h * (1.0 + scale) + shift).astype(BF16)

    pt = _dot_nt(wqk_ref[...], hb)
    pv = _dot(hb, wv_ref[...])
    vdiff_ref[0] = pv[:, 0:256].astype(BF16)
    vna_ref[0] = pv[:, 256:512].astype(BF16)
    vgqa_ref[0] = pv[:, 512:640].astype(BF16)

    def gain(off, n):
        return gcol_ref[off:off + n, :]

    cos32, sin32 = cs32_ref[0:32, :], cs32_ref[32:64, :]
    cos64, sin64 = cs64_ref[0:64, :], cs64_ref[64:128, :]

    cq = _rms_rows(pt[R_CQ:R_CQ + MLA_QR], gain(G_QA, MLA_QR), MLA_QR).astype(BF16)
    qt = _dot(wuq_ref[...], cq)
    ckv = _rms_rows(pt[R_CKV:R_CKV + MLA_KVR], gain(G_KVA, MLA_KVR), MLA_KVR).astype(BF16)
    knt = _dot(wukn_ref[...], ckv)
    vt = _dot(wuv_ref[...], ckv)
    vmla_ref[0] = vt.T.astype(BF16)
    kr = pt[R_KR:R_KR + MLA_ROPE]
    kr_ss = jnp.sum(kr * kr, axis=0, keepdims=True)
    g_mq, g_mk = gain(G_MQ, MLA_D), gain(G_MK, MLA_D)
    zpad = jnp.zeros((128 - MLA_D, RB), F32)
    q_parts = []
    for hd in range(MLA_H):
        qh = _rms_rows(qt[hd * MLA_D:(hd + 1) * MLA_D], g_mq, MLA_D)
        q_rope = _rope_rows(qh[MLA_NOPE:], cos32, sin32, MLA_ROPE // 4)
        q_parts += [qh[:MLA_NOPE], q_rope, zpad]
        kn = knt[hd * MLA_NOPE:(hd + 1) * MLA_NOPE]
        ss = (jnp.sum(kn * kn, axis=0, keepdims=True) + kr_ss) * (1.0 / MLA_D)
        r = lax.rsqrt(ss + EPS)
        k_rope = _rope_rows(kr * r * g_mk[MLA_NOPE:], cos32, sin32, MLA_ROPE // 4)
        kmla_ref[0, hd * 128:(hd + 1) * 128, :] = jnp.concatenate(
            [kn * r * g_mk[:MLA_NOPE], k_rope, zpad], axis=0).astype(BF16)
    q_all = jnp.concatenate(q_parts, axis=0) * (MLA_D ** -0.5)
    qmla_ref[0] = q_all.T.astype(BF16)

    g_dq, g_dk = gain(G_DQ, DIFF_D), gain(G_DK, DIFF_D)
    q_parts, k_parts = [], []
    for gi in range(2 * DIFF_H):
        qg = _rms_rows(pt[R_DQ + gi * DIFF_D:R_DQ + (gi + 1) * DIFF_D], g_dq, DIFF_D)
        q_parts.append(_rope_rows(qg, cos32, sin32, DIFF_D // 4))
        kg = _rms_rows(pt[R_DK + gi * DIFF_D:R_DK + (gi + 1) * DIFF_D], g_dk, DIFF_D)
        k_parts.append(_rope_rows(kg, cos32, sin32, DIFF_D // 4))
    qdiff_ref[0] = (jnp.concatenate(q_parts, axis=0) * (DIFF_D ** -0.5)).T.astype(BF16)
    kdiff_ref[0] = jnp.concatenate(k_parts, axis=0).astype(BF16)

    g_nq, g_nk = gain(G_NQ, NA_D), gain(G_NK, NA_D)
    q_parts, k_parts = [], []
    for hd in range(NA_H):
        q_parts.append(_rms_rows(pt[R_NQ + hd * NA_D:R_NQ + (hd + 1) * NA_D], g_nq, NA_D))
        k_parts.append(_rms_rows(pt[R_NK + hd * NA_D:R_NK + (hd + 1) * NA_D], g_nk, NA_D))
    qna_ref[0] = (jnp.concatenate(q_parts, axis=0) * (NA_D ** -0.5)).T.astype(BF16)
    kna_ref[0] = jnp.concatenate(k_parts, axis=0).astype(BF16)

    g_gq, g_gk = gain(G_GQ, GQA_D), gain(G_GK, GQA_D)
    q_parts, k_parts = [], []
    for hd in range(GQA_H):
        qg = _rms_rows(pt[R_GQ + hd * GQA_D:R_GQ + (hd + 1) * GQA_D], g_gq, GQA_D)
        q_parts.append(_rope_rows(qg, cos64, sin64, GQA_D // 4))
    for hd in range(GQA_KV):
        kg = _rms_rows(pt[R_GK + hd * GQA_D:R_GK + (hd + 1) * GQA_D], g_gk, GQA_D)
        k_parts.append(_rope_rows(kg, cos64, sin64, GQA_D // 4))
    qgqa_ref[0] = (jnp.concatenate(q_parts, axis=0) * (GQA_D ** -0.5)).T.astype(BF16)
    kgqa_ref[0] = jnp.concatenate(k_parts, axis=0).astype(BF16)


def _inproj(xs, mods, gmix, wqk, wv, wuq, wukn, wuv, gcol, cs32, cs64):
    n_b = xs.shape[0]

    def full(a):
        return pl.BlockSpec(a.shape, lambda r, b, _n=a.ndim: (0,) * _n)

    def tok(width):
        return pl.BlockSpec((1, RB, width), lambda r, b: (b, r, 0))

    def chan(rows):
        return pl.BlockSpec((1, rows, RB), lambda r, b: (b, 0, r))

    def tshape(width):
        return jax.ShapeDtypeStruct((n_b, T, width), BF16)

    def cshape(rows):
        return jax.ShapeDtypeStruct((n_b, rows, T), BF16)

    return pl.pallas_call(
        _inproj_kernel,
        grid=(NRB, n_b),
        in_specs=[
            pl.BlockSpec((1, RB, D), lambda r, b: (b, r, 0)),
            pl.BlockSpec((1, 1, 6 * D), lambda r, b: (jnp.where(r < NLAT, b, n_b), 0, 0)),
            full(gmix), full(wqk), full(wv), full(wuq), full(wukn), full(wuv), full(gcol),
            pl.BlockSpec((64, RB), lambda r, b: (0, r)),
            pl.BlockSpec((128, RB), lambda r, b: (0, r)),
        ],
        out_specs=[tok(512), chan(512), tok(256), tok(256), chan(256), tok(256),
                   tok(256), chan(256), tok(256), tok(256), chan(128), tok(128)],
        out_shape=[tshape(512), cshape(512), tshape(256), tshape(256), cshape(256), tshape(256),
                   tshape(256), cshape(256), tshape(256), tshape(256), cshape(128), tshape(128)],
        compiler_params=_params(("parallel", "parallel")),
        name="inproj_prep",
    )(xs, mods, gmix, wqk, wv, wuq, wukn, wuv, gcol, cs32, cs64)


def _exp_sum(s):
    m = jnp.max(s, axis=-1, keepdims=True)
    e = jnp.exp(s - m)
    return e, jnp.sum(e, axis=-1, keepdims=True)


def _lane_id(shape):
    return lax.broadcasted_iota(jnp.int32, shape, 1)


def _mla_kernel(q_ref, k_ref, v_ref, o_ref, *, k_lo_ctx):
    lane = _lane_id((RB, 128))

    def run(k_lo):
        outs = []
        for pair in range(2):
            halves = []
            for sub in range(2):
                hd = 2 * pair + sub
                s = _dot(q_ref[0, :, hd * 128:(hd + 1) * 128], k_ref[0, hd * 128:(hd + 1) * 128, k_lo:])
                e, l = _exp_sum(s)
                o = _dot(e.astype(BF16), v_ref[0, k_lo:, pair * 128:(pair + 1) * 128])
                halves.append(o * (1.0 / l))
            outs.append(jnp.where(lane < 64, halves[0], halves[1]))
        o_ref[0] = jnp.concatenate(outs, axis=1).astype(o_ref.dtype)

    rb = pl.program_id(1)

    @pl.when(rb < NLAT)
    def _():
        run(0)

    @pl.when(rb >= NLAT)
    def _():
        run(k_lo_ctx)


def _gqa_kernel(q_ref, k_ref, v_ref, o_ref, *, k_lo_ctx):
    lane = _lane_id((RB, 128))

    def run(k_lo):
        kt = k_ref[0, :, k_lo:]
        vv = v_ref[0, k_lo:, :]
        res = {}
        for rep in range(2):
            qs = q_ref[0, :, rep * 128:(rep + 1) * 128]
            for grp in range(2):
                qm = jnp.where((lane >= 64) == (grp == 1), qs, jnp.zeros_like(qs))
                e, l = _exp_sum(_dot(qm, kt))
                o = _dot(e.astype(BF16), vv) * (1.0 / l)
                if grp != rep:
                    o = pltpu.roll(o, 64, axis=1)
                res[(grp, rep)] = o
        outs = [jnp.where(lane < 64, res[(grp, 0)], res[(grp, 1)]) for grp in range(2)]
        o_ref[0] = jnp.concatenate(outs, axis=1).astype(o_ref.dtype)

    rb = pl.program_id(1)

    @pl.when(rb < NLAT)
    def _():
        run(0)

    @pl.when(rb >= NLAT)
    def _():
        run(k_lo_ctx)


def _diff_kernel(q_ref, k_ref, v_ref, lam_ref, g_ref, o_ref, *, k_lo_ctx, lambda_init):
    lane = _lane_id((RB, 128))
    lq1, lk1, lq2, lk2 = lam_ref[0:1, :], lam_ref[1:2, :], lam_ref[2:3, :], lam_ref[3:4, :]
    lam = (jnp.exp(jnp.sum(lq1 * lk1, axis=-1, keepdims=True))
           - jnp.exp(jnp.sum(lq2 * lk2, axis=-1, keepdims=True)) + lambda_init)
    gsub = g_ref[...]

    def run(k_lo):
        outs = []
        for pair in range(2):
            qs = q_ref[0, :, pair * 128:(pair + 1) * 128]
            kt = k_ref[0, pair * 128:(pair + 1) * 128, k_lo:]
            vv = v_ref[0, k_lo:, pair * 128:(pair + 1) * 128]
            halves = []
            for sub in range(2):
                grp = lane // DIFF_D
                q1 = jnp.where(grp == 2 * sub, qs, jnp.zeros_like(qs))
                q2 = jnp.where(grp == 2 * sub + 1, qs, jnp.zeros_like(qs))
                e1, l1 = _exp_sum(_dot(q1, kt))
                e2, l2 = _exp_sum(_dot(q2, kt))
                w = e1 * (1.0 / l1) - e2 * (lam / l2)
                halves.append(_dot(w.astype(BF16), vv))
            o = jnp.where(lane < 64, halves[0], halves[1])
            o2 = o * o
            ss0 = jnp.sum(jnp.where(lane < 64, o2, 0.0), axis=-1, keepdims=True)
            ss1 = jnp.sum(jnp.where(lane < 64, 0.0, o2), axis=-1, keepdims=True)
            ss = jnp.where(lane < 64, ss0, ss1) * (1.0 / (2 * DIFF_D))
            outs.append(o * lax.rsqrt(ss + EPS) * gsub * (1.0 - lambda_init))
        o_ref[0] = jnp.concatenate(outs, axis=1).astype(o_ref.dtype)

    rb = pl.program_id(1)

    @pl.when(rb < NLAT)
    def _():
        run(0)

    @pl.when(rb >= NLAT)
    def _():
        run(k_lo_ctx)


NA_WIN = 1024


def _na_kernel(q_ref, k_ref, v_ref, bias_ref, o_ref):
    lane = _lane_id((RB, 128))
    rb = pl.program_id(1)

    def finish(outs):
        o_ref[0] = jnp.concatenate(outs, axis=1).astype(o_ref.dtype)

    @pl.when(rb < NLAT)
    def _():
        off = pl.multiple_of(jnp.clip((rb - 1) * RB, 0, SEQ - NA_WIN), RB)
        outs = []
        for pair in range(2):
            qs = q_ref[0, :, pair * 128:(pair + 1) * 128]
            k_loc = k_ref[0, pair * 128:(pair + 1) * 128, pl.ds(off, NA_WIN)]
            k_ctx = k_ref[0, pair * 128:(pair + 1) * 128, SEQ:]
            v_loc = v_ref[0, pl.ds(off, NA_WIN), pair * 128:(pair + 1) * 128]
            v_ctx = v_ref[0, SEQ:, pair * 128:(pair + 1) * 128]
            halves = []
            for sub in range(2):
                qm = jnp.where((lane >= 64) == (sub == 1), qs, jnp.zeros_like(qs))
                s_loc = _dot(qm, k_loc) + bias_ref[2 * pair + sub, 0]
                s_ctx = _dot(qm, k_ctx)
                m = jnp.maximum(jnp.max(s_loc, axis=-1, keepdims=True), jnp.max(s_ctx, axis=-1, keepdims=True))
                e_loc, e_ctx = jnp.exp(s_loc - m), jnp.exp(s_ctx - m)
                l = jnp.sum(e_loc, axis=-1, keepdims=True) + jnp.sum(e_ctx, axis=-1, keepdims=True)
                o = _dot(e_loc.astype(BF16), v_loc) + _dot(e_ctx.astype(BF16), v_ctx)
                halves.append(o * (1.0 / l))
            outs.append(jnp.where(lane < 64, halves[0], halves[1]))
        finish(outs)

    @pl.when(rb >= NLAT)
    def _():
        outs = []
        for pair in range(2):
            qs = q_ref[0, :, pair * 128:(pair + 1) * 128]
            k_ctx = k_ref[0, pair * 128:(pair + 1) * 128, SEQ:]
            v_ctx = v_ref[0, SEQ:, pair * 128:(pair + 1) * 128]
            halves = []
            for sub in range(2):
                qm = jnp.where((lane >= 64) == (sub == 1), qs, jnp.zeros_like(qs))
                e, l = _exp_sum(_dot(qm, k_ctx))
                halves.append(_dot(e.astype(BF16), v_ctx) * (1.0 / l))
            outs.append(jnp.where(lane < 64, halves[0], halves[1]))
        finish(outs)


def _attention(body, name, q, k, v, extra, n_rb, out_width=256):
    n_b = q.shape[0]
    in_specs = [
        pl.BlockSpec((1, RB, q.shape[2]), lambda b, r: (b, r, 0)),
        pl.BlockSpec((1, k.shape[1], T), lambda b, r: (b, 0, 0)),
        pl.BlockSpec((1, T, v.shape[2]), lambda b, r: (b, 0, 0)),
    ] + [spec for _, spec in extra]
    return pl.pallas_call(
        body,
        grid=(n_b, n_rb),
        in_specs=in_specs,
        out_specs=pl.BlockSpec((1, RB, out_width), lambda b, r: (b, r, 0)),
        out_shape=jax.ShapeDtypeStruct((n_b, n_rb * RB, out_width), BF16),
        compiler_params=_params(("parallel", "parallel")),
        name=name,
    )(q, k, v, *[a for a, _ in extra])


def _outproj_kernel(x_ref, mod_ref, ma_ref, mb_ref, mc_ref, md_ref, w_ref, o_ref):
    g1 = mod_ref[0][:, 2 * D:3 * D]
    acc = _dot(ma_ref[0], w_ref[0:256, :])
    acc += _dot(mb_ref[0], w_ref[256:512, :])
    acc += _dot(mc_ref[0], w_ref[512:768, :])
    acc += _dot(md_ref[0], w_ref[768:1024, :])
    o_ref[0] = x_ref[0] + g1 * acc


def _outproj(xs, mods, mixes, w_out, n_rb):
    n_b = xs.shape[0]
    mix_spec = pl.BlockSpec((1, RB, 256), lambda b, r: (b, r, 0))
    return pl.pallas_call(
        _outproj_kernel,
        grid=(n_b, n_rb),
        in_specs=[
            pl.BlockSpec((1, RB, D), lambda b, r: (b, r, 0)),
            pl.BlockSpec((1, 1, 6 * D), lambda b, r: (jnp.where(r < NLAT, b, n_b), 0, 0)),
            mix_spec, mix_spec, mix_spec, mix_spec,
            pl.BlockSpec((D, D), lambda b, r: (0, 0)),
        ],
        out_specs=pl.BlockSpec((1, RB, D), lambda b, r: (b, r, 0)),
        out_shape=jax.ShapeDtypeStruct((n_b, n_rb * RB, D), F32),
        compiler_params=_params(("parallel", "parallel")),
        name="outproj",
    )(xs, mods, *mixes, w_out)


def _ffn_kernel(x_ref, hb_ref, ha_ref, modb_ref, modc_ref, g_ref, wa_ref, wg_ref, cwa_ref, cwg_ref,
                cba_ref, cbg_ref, wd_ref, o_ref, h_scr, ua_scr, ug_scr, *, tm, rows_per_sample):
    rb = pl.program_id(1)
    j = pl.program_id(2)
    row0 = rb * tm

    def per_row(rows, col):
        lat = modb_ref[0][:, col * D:(col + 1) * D]
        if rows_per_sample == SEQ:
            return lat
        return jnp.where(rows >= SEQ, modc_ref[0][:, col * D:(col + 1) * D], lat)

    def modulated(x, rows):
        ms = jnp.mean(x * x, axis=-1, keepdims=True)
        y = x * lax.rsqrt(ms + EPS) * g_ref[...]
        return (y * (1.0 + per_row(rows, 4)) + per_row(rows, 3)).astype(BF16)

    @pl.when(j == 0)
    def _():
        halo_rows = lax.broadcasted_iota(jnp.int32, (HALO, 1), 0)
        h_scr[0:HALO, :] = modulated(hb_ref[0], row0 - HALO + halo_rows)
        h_scr[HALO + tm:, :] = modulated(ha_ref[0], row0 + tm + halo_rows)
        main_rows = row0 + lax.broadcasted_iota(jnp.int32, (tm, 1), 0)
        h_scr[HALO:HALO + tm, :] = modulated(x_ref[0], main_rows)
        o_ref[0] = jnp.zeros((tm, D), F32)

    hb = h_scr[...]
    ua_scr[...] = _dot(hb, wa_ref[...])
    ug_scr[...] = _dot(hb, wg_ref[...])

    rows = row0 + lax.broadcasted_iota(jnp.int32, (tm, 1), 0)
    has_prev = (rows != 0) & (rows != SEQ)
    has_next = (rows != SEQ - 1) & (rows != rows_per_sample - 1)

    def conv(u_scr, cw_ref, cb_ref):
        prev = jnp.where(has_prev, u_scr[HALO - 1:HALO - 1 + tm, :], 0.0)
        nxt = jnp.where(has_next, u_scr[HALO + 1:HALO + 1 + tm, :], 0.0)
        return (prev * cw_ref[0:1, :] + u_scr[HALO:HALO + tm, :] * cw_ref[1:2, :]
                + nxt * cw_ref[2:3, :] + cb_ref[...])

    a = conv(ua_scr, cwa_ref, cba_ref)
    g = conv(ug_scr, cwg_ref, cbg_ref)
    act = (g * jax.nn.sigmoid(g) * a).astype(BF16)
    o_ref[0] += _dot(act, wd_ref[...])

    @pl.when(j == N_CHUNK - 1)
    def _():
        o_ref[0] = x_ref[0] + per_row(rows, 5) * o_ref[0]


def _ffn(x1, mods, g_ffn, w_up, conv_w, conv_b, w_down, rows_per_sample, tm):
    n_b = x1.shape[0]
    n_rb = rows_per_sample // tm
    halo_per_block = tm // HALO
    n_halo = x1.shape[1] // HALO
    body = functools.partial(_ffn_kernel, tm=tm, rows_per_sample=rows_per_sample)
    return pl.pallas_call(
        body,
        grid=(n_b, n_rb, N_CHUNK),
        in_specs=[
            pl.BlockSpec((1, tm, D), lambda b, r, j: (b, r, 0)),
            pl.BlockSpec((1, HALO, D), lambda b, r, j: (b, jnp.maximum(r * halo_per_block - 1, 0), 0)),
            pl.BlockSpec((1, HALO, D), lambda b, r, j: (b, jnp.minimum((r + 1) * halo_per_block, n_halo - 1), 0)),
            pl.BlockSpec((1, 1, 6 * D), lambda b, r, j: (b, 0, 0)),
            pl.BlockSpec((1, 1, 6 * D), lambda b, r, j: (n_b, 0, 0)),
            pl.BlockSpec((1, D), lambda b, r, j: (0, 0)),
            pl.BlockSpec((D, FF_CHUNK), lambda b, r, j: (0, j)),
            pl.BlockSpec((D, FF_CHUNK), lambda b, r, j: (0, N_CHUNK + j)),
            pl.BlockSpec((3, FF_CHUNK), lambda b, r, j: (0, j)),
            pl.BlockSpec((3, FF_CHUNK), lambda b, r, j: (0, N_CHUNK + j)),
            pl.BlockSpec((1, FF_CHUNK), lambda b, r, j: (0, j)),
            pl.BlockSpec((1, FF_CHUNK), lambda b, r, j: (0, N_CHUNK + j)),
            pl.BlockSpec((FF_CHUNK, D), lambda b, r, j: (j, 0)),
        ],
        out_specs=pl.BlockSpec((1, tm, D), lambda b, r, j: (b, r, 0)),
        out_shape=jax.ShapeDtypeStruct((n_b, rows_per_sample, D), F32),
        scratch_shapes=[
            pltpu.VMEM((tm + 2 * HALO, D), BF16),
            pltpu.VMEM((tm + 2 * HALO, FF_CHUNK), F32),
            pltpu.VMEM((tm + 2 * HALO, FF_CHUNK), F32),
        ],
        compiler_params=_params(("parallel", "parallel", "arbitrary")),
        name="conv_ffn",
    )(x1, x1, x1, mods, mods, g_ffn, w_up, w_up, conv_w, conv_w, conv_b, conv_b, w_down)


def _rope_table(rot_dim):
    n_freq = rot_dim // 4
    inv = jnp.power(ROPE_THETA, -jnp.arange(n_freq, dtype=F32) / n_freq)
    t = jnp.arange(SEQ)
    row = (t // GRID_W).astype(F32)
    col = (t % GRID_W).astype(F32)
    ar, ac = row[:, None] * inv, col[:, None] * inv
    ang = jnp.concatenate([ar, ar, ac, ac], axis=-1)
    sign = jnp.concatenate([-jnp.ones(n_freq), jnp.ones(n_freq), -jnp.ones(n_freq), jnp.ones(n_freq)]).astype(F32)
    cos = jnp.concatenate([jnp.cos(ang), jnp.ones((CTX, rot_dim), F32)], axis=0)
    sin = jnp.concatenate([jnp.sin(ang) * sign, jnp.zeros((CTX, rot_dim), F32)], axis=0)
    return jnp.concatenate([cos.T, sin.T], axis=0)


def _na_bias_tables(rpb):
    n_l, n_h = rpb.shape[:2]
    n_off = 2 * NA_ROWS - 1
    rows = SEQ // GRID_W
    p = jnp.pad(rpb.astype(F32), ((0, 0), (0, 0), (0, 0), (48, 49)))
    sk = jnp.broadcast_to(p[..., None, :], (n_l, n_h, n_off, GRID_W, 128)).reshape(n_l, n_h, n_off, GRID_W * 128)
    sk = sk[..., :GRID_W * 127].reshape(n_l, n_h, n_off, GRID_W, 127)[..., 63:127]
    c = np.arange(GRID_W)
    win_start = np.clip(c - NA_COLS // 2, 0, GRID_W - NA_COLS)
    v_col = (c[None, :] >= win_start[:, None]) & (c[None, :] < win_start[:, None] + NA_COLS)
    tz = jnp.transpose(jnp.where(v_col, sk, NEG_INF), (0, 1, 3, 2, 4))

    def neg(n):
        return jnp.full((n_l, n_h, GRID_W, n, GRID_W), NEG_INF, F32)

    tabs = []
    for rb in (0, 1, NLAT - 2, NLAT - 1):
        win_row = min(max(4 * rb - 4, 0), rows - 16)
        blocks = []
        for ri in range(4):
            r = 4 * rb + ri
            row_start = min(max(r - NA_ROWS // 2, 0), rows - NA_ROWS)
            n_before = row_start - win_row
            d0 = row_start - r + NA_ROWS - 1
            parts = [neg(n_before), tz[:, :, :, d0:d0 + NA_ROWS], neg(16 - NA_ROWS - n_before)]
            blocks.append(jnp.concatenate([q for q in parts if q.shape[3]], axis=3).reshape(n_l, n_h, GRID_W, NA_WIN))
        tabs.append(jnp.concatenate(blocks, axis=2))
    return jnp.stack(tabs, axis=2)


def _col(v):
    return v.reshape(-1, 1).astype(F32)


def kernel(x, c, ctx, c_ctx, w_mod, b_mod, g_mix, w_in, w_out, mla_q_a_g, mla_w_uq, mla_kv_a_g, mla_w_ukv,
           mla_q_g, mla_k_g, diff_q_g, diff_k_g, diff_lq1, diff_lk1, diff_lq2, diff_lk2, diff_subln_g,
           na_q_g, na_k_g, na_rpb, gqa_q_g, gqa_k_g, g_ffn, w_up, conv_w, conv_b, w_down):
    n_b = x.shape[0]
    n_layer = w_mod.shape[0]
    assert x.shape[1:] == (SEQ, D) and ctx.shape[1:] == (CTX, D)

    xs = jnp.concatenate([x, ctx], axis=1)
    mod_rows = -(-(n_b + 1) // 8) * 8
    cc = jnp.concatenate([c, c_ctx[None], jnp.zeros((mod_rows - n_b - 1, D), F32)], axis=0)
    mods_all = _modulation(cc, w_mod, b_mod).reshape(n_layer, mod_rows, 1, 6 * D)

    cs32, cs64 = _rope_table(32), _rope_table(64)
    bias_all = _na_bias_tables(na_rpb)
    k_lo_ctx = SEQ

    s = np.cumsum([0, 256, 128, 32, 256, 256, 256, 256, 256, 256, 256, 128, 128])
    gq_perm = np.concatenate([np.arange(GQA_D) + (2 * g + r) * GQA_D for r in range(2) for g in range(2)])

    for l in range(n_layer):
        with_ctx = l < n_layer - 1
        lambda_init = 0.8 - 0.6 * math.exp(-0.3 * l)
        mods = mods_all[l]
        w = w_in[l]
        gq_cols = w[:, s[9]:s[10]][:, gq_perm]
        wqk = jnp.concatenate([w[:, s[0]:s[3]], w[:, s[3]:s[5]], w[:, s[6]:s[8]], gq_cols, w[:, s[10]:s[11]]],
                              axis=1).T.astype(BF16)
        wv = jnp.concatenate([w[:, s[5]:s[6]], w[:, s[8]:s[9]], w[:, s[11]:s[12]]], axis=1).astype(BF16)
        wuq = mla_w_uq[l].T.astype(BF16)
        wukv = mla_w_ukv[l].reshape(MLA_KVR, MLA_H, MLA_NOPE + MLA_V)
        wukn = wukv[:, :, :MLA_NOPE].reshape(MLA_KVR, MLA_H * MLA_NOPE).T.astype(BF16)
        wuv = wukv[:, :, MLA_NOPE:].reshape(MLA_KVR, MLA_H * MLA_V).T.astype(BF16)
        gcol = jnp.concatenate([_col(mla_q_a_g[l]), _col(mla_kv_a_g[l]), _col(mla_q_g[l]), _col(mla_k_g[l]),
                                _col(diff_q_g[l]), _col(diff_k_g[l]), _col(na_q_g[l]), _col(na_k_g[l]),
                                _col(gqa_q_g[l]), _col(gqa_k_g[l])], axis=0)
        (q_mla, k_mla, v_mla, q_diff, k_diff, v_diff, q_na, k_na, v_na, q_gqa, k_gqa, v_gqa) = _inproj(
            xs, mods, g_mix[l].reshape(1, D), wqk, wv, wuq, wukn, wuv, gcol, cs32, cs64)

        n_rb = NRB if with_ctx else NLAT
        mix_a = _attention(functools.partial(_mla_kernel, k_lo_ctx=k_lo_ctx), "attn_mla", q_mla, k_mla, v_mla, [], n_rb)
        lam_rows = jnp.stack([diff_lq1[l], diff_lk1[l], diff_lq2[l], diff_lk2[l]]).astype(F32)
        lam_rows = jnp.concatenate([lam_rows, jnp.zeros((4, DIFF_D), F32)], axis=0)
        gsub = jnp.tile(diff_subln_g[l].astype(F32), 2).reshape(1, 128)
        mix_b = _attention(
            functools.partial(_diff_kernel, k_lo_ctx=k_lo_ctx, lambda_init=lambda_init), "attn_diff",
            q_diff, k_diff, v_diff,
            [(lam_rows, pl.BlockSpec((8, DIFF_D), lambda b, r: (0, 0))),
             (gsub, pl.BlockSpec((1, 128), lambda b, r: (0, 0)))], n_rb)
        bias = bias_all[l]
        bias_spec = pl.BlockSpec(
            (NA_H, 1, RB, NA_WIN),
            lambda b, r: (0, jnp.where(r == 0, 0, jnp.where(r < NLAT - 2, 1, jnp.where(r == NLAT - 2, 2, 3))), 0, 0))
        mix_c = _attention(_na_kernel, "attn_na", q_na, k_na, v_na, [(bias, bias_spec)], n_rb)
        mix_d = _attention(functools.partial(_gqa_kernel, k_lo_ctx=k_lo_ctx), "attn_gqa", q_gqa, k_gqa, v_gqa, [], n_rb)

        x1 = _outproj(xs, mods, (mix_a, mix_b, mix_c, mix_d), w_out[l].astype(BF16), n_rb)
        rows_per_sample = T if with_ctx else SEQ
        xs = _ffn(x1, mods, g_ffn[l].reshape(1, D), w_up[l].astype(BF16), conv_w[l], conv_b[l].reshape(1, 2 * D_FF),
                  w_down[l].astype(BF16), rows_per_sample, rows_per_sample // 2)
    return xs
```

```python
import functools
import math

import numpy as np
import jax
import jax.numpy as jnp
from jax import lax
from jax.experimental import pallas as pl
from jax.experimental.pallas import tpu as pltpu

F32 = jnp.float32
BF16 = jnp.bfloat16

D = 1024
SEQ = 2048
GRID_W = 64
CTX = 256
T = SEQ + CTX
RB = 256
NRB = T // RB
NLAT = SEQ // RB
EPS = 1e-6
NEG_INF = -1e30
ROPE_THETA = 10000.0
LOG2E = 1.4426950408889634

MLA_H, MLA_NOPE, MLA_ROPE, MLA_V, MLA_QR, MLA_KVR = 4, 64, 32, 64, 256, 128
MLA_D = MLA_NOPE + MLA_ROPE
DIFF_H, DIFF_D = 4, 32
NA_H, NA_D, NA_ROWS, NA_COLS = 4, 64, 8, 16
GQA_H, GQA_KV, GQA_D = 4, 2, 64
D_FF = 2816
FF_CHUNK = 256
N_CHUNK = D_FF // FF_CHUNK
OUT_TM, OUT_TM_LAST = 768, 1024
HALO = 16
FFN_TILES = 4

R_CQ, R_CKV, R_KR, R_DQ, R_DK, R_NQ, R_NK, R_GQ, R_GK = 0, 256, 384, 416, 672, 928, 1184, 1440, 1696
QK_ROWS = 1824
V_COLS = 640

G_QA, G_KVA, G_MQ, G_MK, G_DQ, G_DK, G_NQ, G_NK, G_GQ, G_GK = 0, 256, 384, 480, 576, 608, 640, 704, 768, 832
G_ROWS = 896

VMEM_LIMIT = 56 * 1024 * 1024


def _params(sem):
    return pltpu.CompilerParams(dimension_semantics=sem, vmem_limit_bytes=VMEM_LIMIT)


def _dot(a, b):
    return jnp.dot(a, b, preferred_element_type=F32)


def _dot_nt(a, b):
    return lax.dot_general(a, b, (((1,), (1,)), ((), ())), preferred_element_type=F32)


def _mod_kernel(c_ref, w_ref, b_ref, o_ref):
    c = c_ref[...]
    a = (c * jax.nn.sigmoid(c)).astype(BF16)
    o_ref[0] = _dot(a, w_ref[0].astype(BF16)) + b_ref[0]


def _modulation(cc, w_mod, b_mod):
    n_layer = w_mod.shape[0]
    rows = cc.shape[0]
    return pl.pallas_call(
        _mod_kernel,
        grid=(n_layer, 6),
        in_specs=[
            pl.BlockSpec((rows, D), lambda l, j: (0, 0)),
            pl.BlockSpec((1, D, D), lambda l, j: (l, 0, j)),
            pl.BlockSpec((1, 1, D), lambda l, j: (l, 0, j)),
        ],
        out_specs=pl.BlockSpec((1, rows, D), lambda l, j: (l, 0, j)),
        out_shape=jax.ShapeDtypeStruct((n_layer, rows, 6 * D), F32),
        compiler_params=_params(("parallel", "parallel")),
        name="modulation",
    )(cc, w_mod, b_mod.reshape(n_layer, 1, 6 * D))


def _rms_rows(x, g, n):
    ss = jnp.sum(x * x, axis=0, keepdims=True) * (1.0 / n)
    return x * lax.rsqrt(ss + EPS) * g


def _rope_rows(x, cos, sin_signed, w):
    rot = jnp.concatenate([x[w:2 * w], x[0:w], x[3 * w:4 * w], x[2 * w:3 * w]], axis=0)
    return x * cos + rot * sin_signed


def _inproj_kernel(x_ref, mod_ref, gmix_ref, wqk_ref, wv_ref, wuq_ref, wukn_ref, wuv_ref, gcol_ref,
                   cs32_ref, cs64_ref,
                   qmla_ref, kmla_ref, vmla_ref, qdiff_ref, kdiff_ref, vdiff_ref,
                   qna_ref, kna_ref, vna_ref, qgqa_ref, kgqa_ref, vgqa_ref):
    x = x_ref[0]
    mod = mod_ref[0]
    shift, scale = mod[:, 0:D], mod[:, D:2 * D]
    ms = jnp.mean(x * x, axis=-1, keepdims=True)
    h = x * lax.rsqrt(ms + EPS) * gmix_ref[...]
    hb = (h * (1.0 + scale) + shift).astype(BF16)

    pt = _dot_nt(wqk_ref[...], hb)
    pv = _dot(hb, wv_ref[...])
    vdiff_ref[0] = pv[:, 0:256].astype(BF16)
    vna_ref[0] = pv[:, 256:512].astype(BF16)
    vgqa_ref[0] = pv[:, 512:640].astype(BF16)

    def gain(off, n):
        return gcol_ref[off:off + n, :]

    cos32, sin32 = cs32_ref[0:32, :], cs32_ref[32:64, :]
    cos64, sin64 = cs64_ref[0:64, :], cs64_ref[64:128, :]

    cq = _rms_rows(pt[R_CQ:R_CQ + MLA_QR], gain(G_QA, MLA_QR), MLA_QR).astype(BF16)
    qt = _dot(wuq_ref[...], cq)
    ckv = _rms_rows(pt[R_CKV:R_CKV + MLA_KVR], gain(G_KVA, MLA_KVR), MLA_KVR).astype(BF16)
    knt = _dot(wukn_ref[...], ckv)
    vt = _dot(wuv_ref[...], ckv)
    vmla_ref[0] = vt.T.astype(BF16)
    kr = pt[R_KR:R_KR + MLA_ROPE]
    kr_ss = jnp.sum(kr * kr, axis=0, keepdims=True)
    g_mq, g_mk = gain(G_MQ, MLA_D), gain(G_MK, MLA_D)
    zpad = jnp.zeros((128 - MLA_D, RB), F32)
    q_parts = []
    for hd in range(MLA_H):
        qh = _rms_rows(qt[hd * MLA_D:(hd + 1) * MLA_D], g_mq, MLA_D)
        q_rope = _rope_rows(qh[MLA_NOPE:], cos32, sin32, MLA_ROPE // 4)
        q_parts += [qh[:MLA_NOPE], q_rope, zpad]
        kn = knt[hd * MLA_NOPE:(hd + 1) * MLA_NOPE]
        ss = (jnp.sum(kn * kn, axis=0, keepdims=True) + kr_ss) * (1.0 / MLA_D)
        r = lax.rsqrt(ss + EPS)
        k_rope = _rope_rows(kr * r * g_mk[MLA_NOPE:], cos32, sin32, MLA_ROPE // 4)
        kmla_ref[0, hd * 128:(hd + 1) * 128, :] = jnp.concatenate(
            [kn * r * g_mk[:MLA_NOPE], k_rope, zpad], axis=0).astype(BF16)
    q_all = jnp.concatenate(q_parts, axis=0) * (MLA_D ** -0.5 * LOG2E)
    qmla_ref[0] = q_all.T.astype(BF16)

    g_dq, g_dk = gain(G_DQ, DIFF_D), gain(G_DK, DIFF_D)
    q_parts, k_parts = [], []
    for gi in range(2 * DIFF_H):
        qg = _rms_rows(pt[R_DQ + gi * DIFF_D:R_DQ + (gi + 1) * DIFF_D], g_dq, DIFF_D)
        q_parts.append(_rope_rows(qg, cos32, sin32, DIFF_D // 4))
        kg = _rms_rows(pt[R_DK + gi * DIFF_D:R_DK + (gi + 1) * DIFF_D], g_dk, DIFF_D)
        k_parts.append(_rope_rows(kg, cos32, sin32, DIFF_D // 4))
    qdiff_ref[0] = (jnp.concatenate(q_parts, axis=0) * (DIFF_D ** -0.5 * LOG2E)).T.astype(BF16)
    kdiff_ref[0] = jnp.concatenate(k_parts, axis=0).astype(BF16)

    g_nq, g_nk = gain(G_NQ, NA_D), gain(G_NK, NA_D)
    q_parts, k_parts = [], []
    for hd in range(NA_H):
        q_parts.append(_rms_rows(pt[R_NQ + hd * NA_D:R_NQ + (hd + 1) * NA_D], g_nq, NA_D))
        k_parts.append(_rms_rows(pt[R_NK + hd * NA_D:R_NK + (hd + 1) * NA_D], g_nk, NA_D))
    qna_ref[0] = (jnp.concatenate(q_parts, axis=0) * (NA_D ** -0.5 * LOG2E)).T.astype(BF16)
    kna_ref[0] = jnp.concatenate(k_parts, axis=0).astype(BF16)

    g_gq, g_gk = gain(G_GQ, GQA_D), gain(G_GK, GQA_D)
    q_parts, k_parts = [], []
    for hd in range(GQA_H):
        qg = _rms_rows(pt[R_GQ + hd * GQA_D:R_GQ + (hd + 1) * GQA_D], g_gq, GQA_D)
        q_parts.append(_rope_rows(qg, cos64, sin64, GQA_D // 4))
    for hd in range(GQA_KV):
        kg = _rms_rows(pt[R_GK + hd * GQA_D:R_GK + (hd + 1) * GQA_D], g_gk, GQA_D)
        k_parts.append(_rope_rows(kg, cos64, sin64, GQA_D // 4))
    qgqa_ref[0] = (jnp.concatenate(q_parts, axis=0) * (GQA_D ** -0.5 * LOG2E)).T.astype(BF16)
    kgqa_ref[0] = jnp.concatenate(k_parts, axis=0).astype(BF16)


def _inproj(xs, mods, gmix, wqk, wv, wuq, wukn, wuv, gcol, cs32, cs64):
    n_b = xs.shape[0]

    def full(a):
        return pl.BlockSpec(a.shape, lambda r, b, _n=a.ndim: (0,) * _n)

    def tok(width):
        return pl.BlockSpec((1, RB, width), lambda r, b: (b, r, 0))

    def chan(rows):
        return pl.BlockSpec((1, rows, RB), lambda r, b: (b, 0, r))

    def tshape(width):
        return jax.ShapeDtypeStruct((n_b, T, width), BF16)

    def cshape(rows):
        return jax.ShapeDtypeStruct((n_b, rows, T), BF16)

    return pl.pallas_call(
        _inproj_kernel,
        grid=(NRB, n_b),
        in_specs=[
            pl.BlockSpec((1, RB, D), lambda r, b: (b, r, 0)),
            pl.BlockSpec((1, 1, 6 * D), lambda r, b: (jnp.where(r < NLAT, b, n_b), 0, 0)),
            full(gmix), full(wqk), full(wv), full(wuq), full(wukn), full(wuv), full(gcol),
            pl.BlockSpec((64, RB), lambda r, b: (0, r)),
            pl.BlockSpec((128, RB), lambda r, b: (0, r)),
        ],
        out_specs=[tok(512), chan(512), tok(256), tok(256), chan(256), tok(256),
                   tok(256), chan(256), tok(256), tok(256), chan(128), tok(128)],
        out_shape=[tshape(512), cshape(512), tshape(256), tshape(256), cshape(256), tshape(256),
                   tshape(256), cshape(256), tshape(256), tshape(256), cshape(128), tshape(128)],
        compiler_params=_params(("parallel", "parallel")),
        name="inproj_prep",
    )(xs, mods, gmix, wqk, wv, wuq, wukn, wuv, gcol, cs32, cs64)


def _lane_id(shape):
    return lax.broadcasted_iota(jnp.int32, shape, 1)


KEY_CHUNK = 768
LAT_CHUNKS = tuple((lo, KEY_CHUNK) for lo in range(0, T, KEY_CHUNK))
CTX_CHUNKS = ((SEQ, CTX),)


def _k_slab(k_ref, idx):
    return lambda lo, n: k_ref[0, idx * 128:(idx + 1) * 128, pl.ds(lo, n)]


def _v_slab(v_ref, idx):
    return lambda lo, n: v_ref[0, pl.ds(lo, n), idx * 128:(idx + 1) * 128]


def _attend(units, chunks):
    def scores(i, c):
        q, keys, _, bias = units[i]
        lo, n = chunks[c]
        s = _dot(q, keys(lo, n))
        b = None if bias is None else bias(c)
        return s if b is None else s + b

    cur = [scores(0, c) for c in range(len(chunks))]
    out = []
    for i in range(len(units)):
        m = functools.reduce(jnp.maximum, [jnp.max(s, axis=-1, keepdims=True) for s in cur])
        nxt, acc, l = [], None, None
        for c, (lo, n) in enumerate(chunks):
            if i + 1 < len(units):
                nxt.append(scores(i + 1, c))
            e = jnp.exp2(cur[c] - m)
            lc = jnp.sum(e, axis=-1, keepdims=True)
            oc = _dot(e.astype(BF16), units[i][2](lo, n))
            acc = oc if acc is None else acc + oc
            l = lc if l is None else l + lc
        out.append((acc, l))
        cur = nxt
    return out


def _by_query_kind(run):
    rb = pl.program_id(1)

    @pl.when(rb < NLAT)
    def _():
        run(LAT_CHUNKS)

    @pl.when(rb >= NLAT)
    def _():
        run(CTX_CHUNKS)


def _mla_kernel(q_ref, k_ref, v_ref, o_ref):
    lane = _lane_id((RB, 128))

    def run(chunks):
        units = [(q_ref[0, :, hd * 128:(hd + 1) * 128], _k_slab(k_ref, hd), _v_slab(v_ref, hd // 2), None)
                 for hd in range(MLA_H)]
        o = [acc * (1.0 / l) for acc, l in _attend(units, chunks)]
        outs = [jnp.where(lane < 64, o[0], o[1]), jnp.where(lane < 64, o[2], o[3])]
        o_ref[0] = jnp.concatenate(outs, axis=1).astype(o_ref.dtype)

    _by_query_kind(run)


def _gqa_kernel(q_ref, k_ref, v_ref, o_ref):
    lane = _lane_id((RB, 128))

    def run(chunks):
        order = [(rep, grp) for rep in range(2) for grp in range(2)]
        units = []
        for rep, grp in order:
            qs = q_ref[0, :, rep * 128:(rep + 1) * 128]
            qm = jnp.where((lane >= 64) == (grp == 1), qs, jnp.zeros_like(qs))
            units.append((qm, _k_slab(k_ref, 0), _v_slab(v_ref, 0), None))
        res = {}
        for (rep, grp), (acc, l) in zip(order, _attend(units, chunks)):
            o = acc * (1.0 / l)
            res[(grp, rep)] = o if grp == rep else pltpu.roll(o, 64, axis=1)
        outs = [jnp.where(lane < 64, res[(grp, 0)], res[(grp, 1)]) for grp in range(2)]
        o_ref[0] = jnp.concatenate(outs, axis=1).astype(o_ref.dtype)

    _by_query_kind(run)


def _diff_kernel(q_ref, k_ref, v_ref, lam_ref, g_ref, o_ref, *, lambda_init):
    lane = _lane_id((RB, 128))
    lq1, lk1, lq2, lk2 = lam_ref[0:1, :], lam_ref[1:2, :], lam_ref[2:3, :], lam_ref[3:4, :]
    lam = (jnp.exp(jnp.sum(lq1 * lk1, axis=-1, keepdims=True))
           - jnp.exp(jnp.sum(lq2 * lk2, axis=-1, keepdims=True)) + lambda_init)
    gsub = g_ref[...]
    grp = lane // DIFF_D

    def run(chunks):
        units = []
        for hd in range(DIFF_H):
            pair, sub = divmod(hd, 2)
            qs = q_ref[0, :, pair * 128:(pair + 1) * 128]
            for which in range(2):
                qm = jnp.where(grp == 2 * sub + which, qs, jnp.zeros_like(qs))
                units.append((qm, _k_slab(k_ref, pair), _v_slab(v_ref, pair), None))
        res = _attend(units, chunks)
        outs = []
        for pair in range(2):
            halves = []
            for sub in range(2):
                (a1, l1), (a2, l2) = res[2 * (2 * pair + sub)], res[2 * (2 * pair + sub) + 1]
                halves.append(a1 * (1.0 / l1) - a2 * (lam / l2))
            o = jnp.where(lane < 64, halves[0], halves[1])
            o2 = o * o
            ss0 = jnp.sum(jnp.where(lane < 64, o2, 0.0), axis=-1, keepdims=True)
            ss1 = jnp.sum(jnp.where(lane < 64, 0.0, o2), axis=-1, keepdims=True)
            ss = jnp.where(lane < 64, ss0, ss1) * (1.0 / (2 * DIFF_D))
            outs.append(o * lax.rsqrt(ss + EPS) * gsub * (1.0 - lambda_init))
        o_ref[0] = jnp.concatenate(outs, axis=1).astype(o_ref.dtype)

    _by_query_kind(run)


NA_WIN = 1024
NA_CHUNK = 512


def _na_kernel(q_ref, k_ref, v_ref, bias_ref, o_ref):
    lane = _lane_id((RB, 128))
    rb = pl.program_id(1)

    def run(chunks, biased):
        units = []
        for hd in range(NA_H):
            pair, sub = divmod(hd, 2)
            qs = q_ref[0, :, pair * 128:(pair + 1) * 128]
            qm = jnp.where((lane >= 64) == (sub == 1), qs, jnp.zeros_like(qs))

            def bias(c, hd=hd):
                if c >= NA_WIN // NA_CHUNK:
                    return None
                return bias_ref[hd, 0, :, c * NA_CHUNK:(c + 1) * NA_CHUNK]

            units.append((qm, _k_slab(k_ref, pair), _v_slab(v_ref, pair), bias if biased else None))
        o = [acc * (1.0 / l) for acc, l in _attend(units, chunks)]
        outs = [jnp.where(lane < 64, o[0], o[1]), jnp.where(lane < 64, o[2], o[3])]
        o_ref[0] = jnp.concatenate(outs, axis=1).astype(o_ref.dtype)

    @pl.when(rb < NLAT)
    def _():
        off = pl.multiple_of(jnp.clip((rb - 1) * RB, 0, SEQ - NA_WIN), RB)
        local = [(pl.multiple_of(off + c * NA_CHUNK, RB), NA_CHUNK) for c in range(NA_WIN // NA_CHUNK)]
        run(local + list(CTX_CHUNKS), True)

    @pl.when(rb >= NLAT)
    def _():
        run(CTX_CHUNKS, False)


def _attention(body, name, q, k, v, extra, n_rb, out_width=256):
    n_b = q.shape[0]
    in_specs = [
        pl.BlockSpec((1, RB, q.shape[2]), lambda b, r: (b, r, 0)),
        pl.BlockSpec((1, k.shape[1], T), lambda b, r: (b, 0, 0)),
        pl.BlockSpec((1, T, v.shape[2]), lambda b, r: (b, 0, 0)),
    ] + [spec for _, spec in extra]
    return pl.pallas_call(
        body,
        grid=(n_b, n_rb),
        in_specs=in_specs,
        out_specs=pl.BlockSpec((1, RB, out_width), lambda b, r: (b, r, 0)),
        out_shape=jax.ShapeDtypeStruct((n_b, n_rb * RB, out_width), BF16),
        compiler_params=_params(("parallel", "parallel")),
        name=name,
    )(q, k, v, *[a for a, _ in extra])


def _outproj_kernel(x_ref, modb_ref, modc_ref, ma_ref, mb_ref, mc_ref, md_ref, w_ref, o_ref, *, tm):
    rows = pl.program_id(1) * tm + lax.broadcasted_iota(jnp.int32, (tm, 1), 0)
    g1 = jnp.where(rows >= SEQ, modc_ref[0][:, 2 * D:3 * D], modb_ref[0][:, 2 * D:3 * D])
    acc = _dot(ma_ref[0], w_ref[0:256, :])
    acc += _dot(mb_ref[0], w_ref[256:512, :])
    acc += _dot(mc_ref[0], w_ref[512:768, :])
    acc += _dot(md_ref[0], w_ref[768:1024, :])
    o_ref[0] = x_ref[0] + g1 * acc


def _outproj(xs, mods, mixes, w_out, rows_per_sample, tm):
    n_b = xs.shape[0]
    mix_spec = pl.BlockSpec((1, tm, 256), lambda b, r: (b, r, 0))
    return pl.pallas_call(
        functools.partial(_outproj_kernel, tm=tm),
        grid=(n_b, rows_per_sample // tm),
        in_specs=[
            pl.BlockSpec((1, tm, D), lambda b, r: (b, r, 0)),
            pl.BlockSpec((1, 1, 6 * D), lambda b, r: (b, 0, 0)),
            pl.BlockSpec((1, 1, 6 * D), lambda b, r: (n_b, 0, 0)),
            mix_spec, mix_spec, mix_spec, mix_spec,
            pl.BlockSpec((D, D), lambda b, r: (0, 0)),
        ],
        out_specs=pl.BlockSpec((1, tm, D), lambda b, r: (b, r, 0)),
        out_shape=jax.ShapeDtypeStruct((n_b, rows_per_sample, D), F32),
        compiler_params=_params(("parallel", "parallel")),
        name="outproj",
    )(xs, mods, mods, *mixes, w_out)


def _ffn_kernel(x_ref, hb_ref, ha_ref, modb_ref, modc_ref, g_ref, wa_ref, wg_ref, cwa_ref, cwg_ref,
                cba_ref, cbg_ref, wd_ref, o_ref, h_scr, ua0_scr, ug0_scr, ua1_scr, ug1_scr, act_scr,
                *, tm, rows_per_sample):
    rb = pl.program_id(1)
    j = pl.program_id(2)
    row0 = rb * tm
    split = SEQ % tm if rows_per_sample > SEQ else 0
    band_lo, band_hi = split - HALO, split + HALO

    def per_row(rows, col):
        lat = modb_ref[0][:, col * D:(col + 1) * D]
        if rows_per_sample == SEQ:
            return lat
        return jnp.where(rows >= SEQ, modc_ref[0][:, col * D:(col + 1) * D], lat)

    def modulated(x, rows):
        ms = jnp.mean(x * x, axis=-1, keepdims=True)
        y = x * lax.rsqrt(ms + EPS) * g_ref[...]
        return (y * (1.0 + per_row(rows, 4)) + per_row(rows, 3)).astype(BF16)

    def prologue():
        halo_rows = lax.broadcasted_iota(jnp.int32, (HALO, 1), 0)
        has_before = (row0 != 0) & (row0 != SEQ)
        has_after = (row0 + tm != SEQ) & (row0 + tm != rows_per_sample)
        zero = jnp.zeros((HALO, D), BF16)
        h_scr[0:HALO, :] = jnp.where(has_before, modulated(hb_ref[0], row0 - HALO + halo_rows), zero)
        h_scr[HALO + tm:, :] = jnp.where(has_after, modulated(ha_ref[0], row0 + tm + halo_rows), zero)
        main_rows = row0 + lax.broadcasted_iota(jnp.int32, (tm, 1), 0)
        h_scr[HALO:HALO + tm, :] = modulated(x_ref[0], main_rows)
        o_ref[0] = jnp.zeros((tm, D), F32)

    slots = ((ua0_scr, ug0_scr), (ua1_scr, ug1_scr))

    def tiles(total):
        units = total // HALO
        sizes = [(units // FFN_TILES + (1 if t < units % FFN_TILES else 0)) * HALO for t in range(FFN_TILES)]
        edges = np.cumsum([0] + sizes)
        return [(int(edges[t]), int(edges[t + 1])) for t in range(FFN_TILES)]

    up_tiles, down_tiles = tiles(tm + 2 * HALO), tiles(tm)

    def up(slot, t):
        ua_scr, ug_scr = slots[slot]
        lo, hi = up_tiles[t]
        hb = h_scr[lo:hi, :]
        ua_scr[lo:hi, :] = _dot(hb, wa_ref[...])
        ug_scr[lo:hi, :] = _dot(hb, wg_ref[...])

    def conv(u_scr, cw_ref, cb_ref, lo, hi, masked):
        prev = u_scr[HALO - 1 + lo:HALO - 1 + hi, :]
        nxt = u_scr[HALO + 1 + lo:HALO + 1 + hi, :]
        if masked:
            rows = row0 + lo + lax.broadcasted_iota(jnp.int32, (hi - lo, 1), 0)
            prev = jnp.where(rows != SEQ, prev, 0.0)
            nxt = jnp.where(rows != SEQ - 1, nxt, 0.0)
        return (prev * cw_ref[0:1, :] + u_scr[HALO + lo:HALO + hi, :] * cw_ref[1:2, :]
                + nxt * cw_ref[2:3, :] + cb_ref[...])

    def conv_down(slot, t):
        ua_scr, ug_scr = slots[slot]
        t_lo, t_hi = down_tiles[t]
        cuts = sorted({t_lo, t_hi} | ({c for c in (band_lo, band_hi) if t_lo < c < t_hi} if split else set()))
        for lo, hi in zip(cuts[:-1], cuts[1:]):
            masked = bool(split) and band_lo <= lo < band_hi
            a = conv(ua_scr, cwa_ref, cba_ref, lo, hi, masked)
            g = conv(ug_scr, cwg_ref, cbg_ref, lo, hi, masked)
            act_scr[lo:hi, :] = (g * jax.nn.sigmoid(g) * a).astype(BF16)
        o_ref[0, t_lo:t_hi, :] += _dot(act_scr[t_lo:t_hi, :], wd_ref[...])

    @pl.when(j == 0)
    def _():
        prologue()
        for t in range(FFN_TILES):
            up(0, t)

    for parity in range(2):
        @pl.when((j > 0) & (j < N_CHUNK) & (j % 2 == parity))
        def _():
            for t in range(FFN_TILES):
                up(parity, t)
                conv_down(1 - parity, t)

    @pl.when(j == N_CHUNK)
    def _():
        for t in range(FFN_TILES):
            conv_down((N_CHUNK - 1) % 2, t)
        rows = row0 + lax.broadcasted_iota(jnp.int32, (tm, 1), 0)
        o_ref[0] = x_ref[0] + per_row(rows, 5) * o_ref[0]


def _ffn(x1, mods, g_ffn, w_up, conv_w, conv_b, w_down, rows_per_sample, tm):
    n_b = x1.shape[0]
    n_rb = rows_per_sample // tm
    halo_per_block = tm // HALO
    n_halo = x1.shape[1] // HALO
    body = functools.partial(_ffn_kernel, tm=tm, rows_per_sample=rows_per_sample)

    def up_idx(j):
        return jnp.minimum(j, N_CHUNK - 1)

    def down_idx(j):
        return jnp.maximum(j - 1, 0)

    return pl.pallas_call(
        body,
        grid=(n_b, n_rb, N_CHUNK + 1),
        in_specs=[
            pl.BlockSpec((1, tm, D), lambda b, r, j: (b, r, 0)),
            pl.BlockSpec((1, HALO, D), lambda b, r, j: (b, jnp.maximum(r * halo_per_block - 1, 0), 0)),
            pl.BlockSpec((1, HALO, D), lambda b, r, j: (b, jnp.minimum((r + 1) * halo_per_block, n_halo - 1), 0)),
            pl.BlockSpec((1, 1, 6 * D), lambda b, r, j: (b, 0, 0)),
            pl.BlockSpec((1, 1, 6 * D), lambda b, r, j: (n_b, 0, 0)),
            pl.BlockSpec((1, D), lambda b, r, j: (0, 0)),
            pl.BlockSpec((D, FF_CHUNK), lambda b, r, j: (0, up_idx(j))),
            pl.BlockSpec((D, FF_CHUNK), lambda b, r, j: (0, N_CHUNK + up_idx(j))),
            pl.BlockSpec((3, FF_CHUNK), lambda b, r, j: (0, down_idx(j))),
            pl.BlockSpec((3, FF_CHUNK), lambda b, r, j: (0, N_CHUNK + down_idx(j))),
            pl.BlockSpec((1, FF_CHUNK), lambda b, r, j: (0, down_idx(j))),
            pl.BlockSpec((1, FF_CHUNK), lambda b, r, j: (0, N_CHUNK + down_idx(j))),
            pl.BlockSpec((FF_CHUNK, D), lambda b, r, j: (down_idx(j), 0)),
        ],
        out_specs=pl.BlockSpec((1, tm, D), lambda b, r, j: (b, r, 0)),
        out_shape=jax.ShapeDtypeStruct((n_b, rows_per_sample, D), F32),
        scratch_shapes=[
            pltpu.VMEM((tm + 2 * HALO, D), BF16),
            pltpu.VMEM((tm + 2 * HALO, FF_CHUNK), F32),
            pltpu.VMEM((tm + 2 * HALO, FF_CHUNK), F32),
            pltpu.VMEM((tm + 2 * HALO, FF_CHUNK), F32),
            pltpu.VMEM((tm + 2 * HALO, FF_CHUNK), F32),
            pltpu.VMEM((tm, FF_CHUNK), BF16),
        ],
        compiler_params=_params(("parallel", "parallel", "arbitrary")),
        name="conv_ffn",
    )(x1, x1, x1, mods, mods, g_ffn, w_up, w_up, conv_w, conv_w, conv_b, conv_b, w_down)


def _rope_table(rot_dim):
    n_freq = rot_dim // 4
    inv = jnp.power(ROPE_THETA, -jnp.arange(n_freq, dtype=F32) / n_freq)
    t = jnp.arange(SEQ)
    row = (t // GRID_W).astype(F32)
    col = (t % GRID_W).astype(F32)
    ar, ac = row[:, None] * inv, col[:, None] * inv
    ang = jnp.concatenate([ar, ar, ac, ac], axis=-1)
    sign = jnp.concatenate([-jnp.ones(n_freq), jnp.ones(n_freq), -jnp.ones(n_freq), jnp.ones(n_freq)]).astype(F32)
    cos = jnp.concatenate([jnp.cos(ang), jnp.ones((CTX, rot_dim), F32)], axis=0)
    sin = jnp.concatenate([jnp.sin(ang) * sign, jnp.zeros((CTX, rot_dim), F32)], axis=0)
    return jnp.concatenate([cos.T, sin.T], axis=0)


def _na_bias_tables(rpb):
    n_l, n_h = rpb.shape[:2]
    n_off = 2 * NA_ROWS - 1
    rows = SEQ // GRID_W
    p = jnp.pad(rpb.astype(F32) * LOG2E, ((0, 0), (0, 0), (0, 0), (48, 49)))
    sk = jnp.broadcast_to(p[..., None, :], (n_l, n_h, n_off, GRID_W, 128)).reshape(n_l, n_h, n_off, GRID_W * 128)
    sk = sk[..., :GRID_W * 127].reshape(n_l, n_h, n_off, GRID_W, 127)[..., 63:127]
    c = np.arange(GRID_W)
    win_start = np.clip(c - NA_COLS // 2, 0, GRID_W - NA_COLS)
    v_col = (c[None, :] >= win_start[:, None]) & (c[None, :] < win_start[:, None] + NA_COLS)
    tz = jnp.transpose(jnp.where(v_col, sk, NEG_INF), (0, 1, 3, 2, 4))

    def neg(n):
        return jnp.full((n_l, n_h, GRID_W, n, GRID_W), NEG_INF, F32)

    tabs = []
    for rb in (0, 1, NLAT - 2, NLAT - 1):
        win_row = min(max(4 * rb - 4, 0), rows - 16)
        blocks = []
        for ri in range(4):
            r = 4 * rb + ri
            row_start = min(max(r - NA_ROWS // 2, 0), rows - NA_ROWS)
            n_before = row_start - win_row
            d0 = row_start - r + NA_ROWS - 1
            parts = [neg(n_before), tz[:, :, :, d0:d0 + NA_ROWS], neg(16 - NA_ROWS - n_before)]
            blocks.append(jnp.concatenate([q for q in parts if q.shape[3]], axis=3).reshape(n_l, n_h, GRID_W, NA_WIN))
        tabs.append(jnp.concatenate(blocks, axis=2))
    return jnp.stack(tabs, axis=2)


def _col(v):
    return v.reshape(-1, 1).astype(F32)


def kernel(x, c, ctx, c_ctx, w_mod, b_mod, g_mix, w_in, w_out, mla_q_a_g, mla_w_uq, mla_kv_a_g, mla_w_ukv,
           mla_q_g, mla_k_g, diff_q_g, diff_k_g, diff_lq1, diff_lk1, diff_lq2, diff_lk2, diff_subln_g,
           na_q_g, na_k_g, na_rpb, gqa_q_g, gqa_k_g, g_ffn, w_up, conv_w, conv_b, w_down):
    n_b = x.shape[0]
    n_layer = w_mod.shape[0]
    assert x.shape[1:] == (SEQ, D) and ctx.shape[1:] == (CTX, D)

    xs = jnp.concatenate([x, ctx], axis=1)
    mod_rows = -(-(n_b + 1) // 8) * 8
    cc = jnp.concatenate([c, c_ctx[None], jnp.zeros((mod_rows - n_b - 1, D), F32)], axis=0)
    mods_all = _modulation(cc, w_mod, b_mod).reshape(n_layer, mod_rows, 1, 6 * D)

    cs32, cs64 = _rope_table(32), _rope_table(64)
    bias_all = _na_bias_tables(na_rpb)

    s = np.cumsum([0, 256, 128, 32, 256, 256, 256, 256, 256, 256, 256, 128, 128])
    gq_perm = np.concatenate([np.arange(GQA_D) + (2 * g + r) * GQA_D for r in range(2) for g in range(2)])

    for l in range(n_layer):
        with_ctx = l < n_layer - 1
        lambda_init = 0.8 - 0.6 * math.exp(-0.3 * l)
        mods = mods_all[l]
        w = w_in[l]
        gq_cols = w[:, s[9]:s[10]][:, gq_perm]
        wqk = jnp.concatenate([w[:, s[0]:s[3]], w[:, s[3]:s[5]], w[:, s[6]:s[8]], gq_cols, w[:, s[10]:s[11]]],
                              axis=1).T.astype(BF16)
        wv = jnp.concatenate([w[:, s[5]:s[6]], w[:, s[8]:s[9]], w[:, s[11]:s[12]]], axis=1).astype(BF16)
        wuq = mla_w_uq[l].T.astype(BF16)
        wukv = mla_w_ukv[l].reshape(MLA_KVR, MLA_H, MLA_NOPE + MLA_V)
        wukn = wukv[:, :, :MLA_NOPE].reshape(MLA_KVR, MLA_H * MLA_NOPE).T.astype(BF16)
        wuv = wukv[:, :, MLA_NOPE:].reshape(MLA_KVR, MLA_H * MLA_V).T.astype(BF16)
        gcol = jnp.concatenate([_col(mla_q_a_g[l]), _col(mla_kv_a_g[l]), _col(mla_q_g[l]), _col(mla_k_g[l]),
                                _col(diff_q_g[l]), _col(diff_k_g[l]), _col(na_q_g[l]), _col(na_k_g[l]),
                                _col(gqa_q_g[l]), _col(gqa_k_g[l])], axis=0)
        (q_mla, k_mla, v_mla, q_diff, k_diff, v_diff, q_na, k_na, v_na, q_gqa, k_gqa, v_gqa) = _inproj(
            xs, mods, g_mix[l].reshape(1, D), wqk, wv, wuq, wukn, wuv, gcol, cs32, cs64)

        n_rb = NRB if with_ctx else NLAT
        mix_a = _attention(_mla_kernel, "attn_mla", q_mla, k_mla, v_mla, [], n_rb)
        lam_rows = jnp.stack([diff_lq1[l], diff_lk1[l], diff_lq2[l], diff_lk2[l]]).astype(F32)
        lam_rows = jnp.concatenate([lam_rows, jnp.zeros((4, DIFF_D), F32)], axis=0)
        gsub = jnp.tile(diff_subln_g[l].astype(F32), 2).reshape(1, 128)
        mix_b = _attention(
            functools.partial(_diff_kernel, lambda_init=lambda_init), "attn_diff",
            q_diff, k_diff, v_diff,
            [(lam_rows, pl.BlockSpec((8, DIFF_D), lambda b, r: (0, 0))),
             (gsub, pl.BlockSpec((1, 128), lambda b, r: (0, 0)))], n_rb)
        bias = bias_all[l]
        bias_spec = pl.BlockSpec(
            (NA_H, 1, RB, NA_WIN),
            lambda b, r: (0, jnp.where(r == 0, 0, jnp.where(r < NLAT - 2, 1, jnp.where(r == NLAT - 2, 2, 3))), 0, 0))
        mix_c = _attention(_na_kernel, "attn_na", q_na, k_na, v_na, [(bias, bias_spec)], n_rb)
        mix_d = _attention(_gqa_kernel, "attn_gqa", q_gqa, k_gqa, v_gqa, [], n_rb)

        rows_per_sample = T if with_ctx else SEQ
        x1 = _outproj(xs, mods, (mix_a, mix_b, mix_c, mix_d), w_out[l].astype(BF16), rows_per_sample,
                      OUT_TM if with_ctx else OUT_TM_LAST)
        xs = _ffn(x1, mods, g_ffn[l].reshape(1, D), w_up[l].astype(BF16), conv_w[l], conv_b[l].reshape(1, 2 * D_FF),
                  w_down[l].astype(BF16), rows_per_sample, rows_per_sample // 2)
    return xs
```

```python
import functools
import math

import numpy as np
import jax
import jax.numpy as jnp
from jax import lax
from jax.experimental import pallas as pl
from jax.experimental.pallas import tpu as pltpu

F32 = jnp.float32
BF16 = jnp.bfloat16

D = 1024
SEQ = 2048
GRID_W = 64
CTX = 256
T = SEQ + CTX
RB = 256
NRB = T // RB
NLAT = SEQ // RB
EPS = 1e-6
NEG_INF = -1e30
ROPE_THETA = 10000.0
LOG2E = 1.4426950408889634

MLA_H, MLA_NOPE, MLA_ROPE, MLA_V, MLA_QR, MLA_KVR = 4, 64, 32, 64, 256, 128
MLA_D = MLA_NOPE + MLA_ROPE
DIFF_H, DIFF_D = 4, 32
NA_H, NA_D, NA_ROWS, NA_COLS = 4, 64, 8, 16
GQA_H, GQA_KV, GQA_D = 4, 2, 64
D_FF = 2816
FF_CHUNK = 256
N_CHUNK = D_FF // FF_CHUNK
OUT_TM, OUT_TM_LAST = 768, 1024
HALO = 16
FFN_TILES = 4

R_CQ, R_CKV, R_KR, R_DQ, R_DK, R_NQ, R_NK, R_GQ, R_GK = 0, 256, 384, 416, 672, 928, 1184, 1440, 1696
QK_ROWS = 1824
V_COLS = 640

G_QA, G_KVA, G_MQ, G_MK, G_DQ, G_DK, G_NQ, G_NK, G_GQ, G_GK = 0, 256, 384, 480, 576, 608, 640, 704, 768, 832
G_ROWS = 896

VMEM_LIMIT = 56 * 1024 * 1024


def _params(sem):
    return pltpu.CompilerParams(dimension_semantics=sem, vmem_limit_bytes=VMEM_LIMIT)


def _dot(a, b):
    return jnp.dot(a, b, preferred_element_type=F32)


def _dot_nt(a, b):
    return lax.dot_general(a, b, (((1,), (1,)), ((), ())), preferred_element_type=F32)


def _mod_kernel(c_ref, w_ref, b_ref, o_ref):
    c = c_ref[...]
    a = (c * jax.nn.sigmoid(c)).astype(BF16)
    o_ref[0] = _dot(a, w_ref[0].astype(BF16)) + b_ref[0]


def _modulation(cc, w_mod, b_mod):
    n_layer = w_mod.shape[0]
    rows = cc.shape[0]
    return pl.pallas_call(
        _mod_kernel,
        grid=(n_layer, 6),
        in_specs=[
            pl.BlockSpec((rows, D), lambda l, j: (0, 0)),
            pl.BlockSpec((1, D, D), lambda l, j: (l, 0, j)),
            pl.BlockSpec((1, 1, D), lambda l, j: (l, 0, j)),
        ],
        out_specs=pl.BlockSpec((1, rows, D), lambda l, j: (l, 0, j)),
        out_shape=jax.ShapeDtypeStruct((n_layer, rows, 6 * D), F32),
        compiler_params=_params(("parallel", "parallel")),
        name="modulation",
    )(cc, w_mod, b_mod.reshape(n_layer, 1, 6 * D))


def _rms_rows(x, g, n):
    ss = jnp.sum(x * x, axis=0, keepdims=True) * (1.0 / n)
    return x * lax.rsqrt(ss + EPS) * g


def _rope_rows(x, cos, sin_signed, w):
    rot = jnp.concatenate([x[w:2 * w], x[0:w], x[3 * w:4 * w], x[2 * w:3 * w]], axis=0)
    return x * cos + rot * sin_signed


def _inproj_kernel(x_ref, mod_ref, gmix_ref, wqk_ref, wv_ref, wuq_ref, wukn_ref, wuv_ref, gcol_ref,
                   cs32_ref, cs64_ref,
                   qmla_ref, kmla_ref, vmla_ref, qdiff_ref, kdiff_ref, vdiff_ref,
                   qna_ref, kna_ref, vna_ref, qgqa_ref, kgqa_ref, vgqa_ref):
    x = x_ref[0]
    mod = mod_ref[0]
    shift, scale = mod[:, 0:D], mod[:, D:2 * D]
    ms = jnp.mean(x * x, axis=-1, keepdims=True)
    h = x * lax.rsqrt(ms + EPS) * gmix_ref[...]
    hb = (h * (1.0 + scale) + shift).astype(BF16)

    def project(lo, hi):
        return _dot_nt(wqk_ref[lo:hi, :], hb)

    def gain(off, n):
        return gcol_ref[off:off + n, :]

    cos32, sin32 = cs32_ref[0:32, :], cs32_ref[32:64, :]
    cos64, sin64 = cs64_ref[0:64, :], cs64_ref[64:128, :]

    def values():
        pv = _dot(hb, wv_ref[...])
        vdiff_ref[0] = pv[:, 0:256].astype(BF16)
        vna_ref[0] = pv[:, 256:512].astype(BF16)
        vgqa_ref[0] = pv[:, 512:640].astype(BF16)

    def mla(pt):
        cq = _rms_rows(pt[0:MLA_QR], gain(G_QA, MLA_QR), MLA_QR).astype(BF16)
        qt = _dot(wuq_ref[...], cq)
        ckv = _rms_rows(pt[R_CKV:R_CKV + MLA_KVR], gain(G_KVA, MLA_KVR), MLA_KVR).astype(BF16)
        knt = _dot(wukn_ref[...], ckv)
        vt = _dot(wuv_ref[...], ckv)
        vmla_ref[0] = vt.T.astype(BF16)
        kr = pt[R_KR:R_KR + MLA_ROPE]
        kr_ss = jnp.sum(kr * kr, axis=0, keepdims=True)
        g_mq, g_mk = gain(G_MQ, MLA_D), gain(G_MK, MLA_D)
        zpad = jnp.zeros((128 - MLA_D, RB), F32)
        q_parts = []
        for hd in range(MLA_H):
            qh = _rms_rows(qt[hd * MLA_D:(hd + 1) * MLA_D], g_mq, MLA_D)
            q_rope = _rope_rows(qh[MLA_NOPE:], cos32, sin32, MLA_ROPE // 4)
            q_parts += [qh[:MLA_NOPE], q_rope, zpad]
            kn = knt[hd * MLA_NOPE:(hd + 1) * MLA_NOPE]
            ss = (jnp.sum(kn * kn, axis=0, keepdims=True) + kr_ss) * (1.0 / MLA_D)
            r = lax.rsqrt(ss + EPS)
            k_rope = _rope_rows(kr * r * g_mk[MLA_NOPE:], cos32, sin32, MLA_ROPE // 4)
            kmla_ref[0, hd * 128:(hd + 1) * 128, :] = jnp.concatenate(
                [kn * r * g_mk[:MLA_NOPE], k_rope, zpad], axis=0).astype(BF16)
        q_all = jnp.concatenate(q_parts, axis=0) * (MLA_D ** -0.5 * LOG2E)
        qmla_ref[0] = q_all.T.astype(BF16)

    def diff(pt):
        g_dq, g_dk = gain(G_DQ, DIFF_D), gain(G_DK, DIFF_D)
        q_parts, k_parts = [], []
        for gi in range(2 * DIFF_H):
            qg = _rms_rows(pt[gi * DIFF_D:(gi + 1) * DIFF_D], g_dq, DIFF_D)
            q_parts.append(_rope_rows(qg, cos32, sin32, DIFF_D // 4))
            kg = _rms_rows(pt[256 + gi * DIFF_D:256 + (gi + 1) * DIFF_D], g_dk, DIFF_D)
            k_parts.append(_rope_rows(kg, cos32, sin32, DIFF_D // 4))
        qdiff_ref[0] = (jnp.concatenate(q_parts, axis=0) * (DIFF_D ** -0.5 * LOG2E)).T.astype(BF16)
        kdiff_ref[0] = jnp.concatenate(k_parts, axis=0).astype(BF16)

    def na(pt):
        g_nq, g_nk = gain(G_NQ, NA_D), gain(G_NK, NA_D)
        q_parts, k_parts = [], []
        for hd in range(NA_H):
            q_parts.append(_rms_rows(pt[hd * NA_D:(hd + 1) * NA_D], g_nq, NA_D))
            k_parts.append(_rms_rows(pt[256 + hd * NA_D:256 + (hd + 1) * NA_D], g_nk, NA_D))
        qna_ref[0] = (jnp.concatenate(q_parts, axis=0) * (NA_D ** -0.5 * LOG2E)).T.astype(BF16)
        kna_ref[0] = jnp.concatenate(k_parts, axis=0).astype(BF16)

    def gqa(pt):
        g_gq, g_gk = gain(G_GQ, GQA_D), gain(G_GK, GQA_D)
        q_parts, k_parts = [], []
        for hd in range(GQA_H):
            qg = _rms_rows(pt[hd * GQA_D:(hd + 1) * GQA_D], g_gq, GQA_D)
            q_parts.append(_rope_rows(qg, cos64, sin64, GQA_D // 4))
        for hd in range(GQA_KV):
            kg = _rms_rows(pt[256 + hd * GQA_D:256 + (hd + 1) * GQA_D], g_gk, GQA_D)
            k_parts.append(_rope_rows(kg, cos64, sin64, GQA_D // 4))
        qgqa_ref[0] = (jnp.concatenate(q_parts, axis=0) * (GQA_D ** -0.5 * LOG2E)).T.astype(BF16)
        kgqa_ref[0] = jnp.concatenate(k_parts, axis=0).astype(BF16)

    pt_mla = project(R_CQ, R_DQ)
    pt_diff = project(R_DQ, R_NQ)
    mla(pt_mla)
    pt_na = project(R_NQ, R_GQ)
    diff(pt_diff)
    pt_gqa = project(R_GQ, QK_ROWS)
    na(pt_na)
    values()
    gqa(pt_gqa)


def _inproj(xs, mods, gmix, wqk, wv, wuq, wukn, wuv, gcol, cs32, cs64):
    n_b = xs.shape[0]

    def full(a):
        return pl.BlockSpec(a.shape, lambda r, b, _n=a.ndim: (0,) * _n)

    def tok(width):
        return pl.BlockSpec((1, RB, width), lambda r, b: (b, r, 0))

    def chan(rows):
        return pl.BlockSpec((1, rows, RB), lambda r, b: (b, 0, r))

    def tshape(width):
        return jax.ShapeDtypeStruct((n_b, T, width), BF16)

    def cshape(rows):
        return jax.ShapeDtypeStruct((n_b, rows, T), BF16)

    return pl.pallas_call(
        _inproj_kernel,
        grid=(NRB, n_b),
        in_specs=[
            pl.BlockSpec((1, RB, D), lambda r, b: (b, r, 0)),
            pl.BlockSpec((1, 1, 6 * D), lambda r, b: (jnp.where(r < NLAT, b, n_b), 0, 0)),
            full(gmix), full(wqk), full(wv), full(wuq), full(wukn), full(wuv), full(gcol),
            pl.BlockSpec((64, RB), lambda r, b: (0, r)),
            pl.BlockSpec((128, RB), lambda r, b: (0, r)),
        ],
        out_specs=[tok(512), chan(512), tok(256), tok(256), chan(256), tok(256),
                   tok(256), chan(256), tok(256), tok(256), chan(128), tok(128)],
        out_shape=[tshape(512), cshape(512), tshape(256), tshape(256), cshape(256), tshape(256),
                   tshape(256), cshape(256), tshape(256), tshape(256), cshape(128), tshape(128)],
        compiler_params=_params(("parallel", "parallel")),
        name="inproj_prep",
    )(xs, mods, gmix, wqk, wv, wuq, wukn, wuv, gcol, cs32, cs64)


def _lane_id(shape):
    return lax.broadcasted_iota(jnp.int32, shape, 1)


KEY_CHUNK = 768
LAT_CHUNKS = tuple((lo, KEY_CHUNK) for lo in range(0, T, KEY_CHUNK))
CTX_CHUNKS = ((SEQ, CTX),)


def _k_slab(k_ref, idx):
    return lambda lo, n: k_ref[0, idx * 128:(idx + 1) * 128, pl.ds(lo, n)]


def _v_slab(v_ref, idx):
    return lambda lo, n: v_ref[0, pl.ds(lo, n), idx * 128:(idx + 1) * 128]


def _attend(units, chunks):
    def scores(i, c):
        q, keys, _, bias = units[i]
        lo, n = chunks[c]
        s = _dot(q, keys(lo, n))
        b = None if bias is None else bias(c)
        return s if b is None else s + b

    cur = [scores(0, c) for c in range(len(chunks))]
    out = []
    for i in range(len(units)):
        m = functools.reduce(jnp.maximum, [jnp.max(s, axis=-1, keepdims=True) for s in cur])
        nxt, acc, l = [], None, None
        for c, (lo, n) in enumerate(chunks):
            if i + 1 < len(units):
                nxt.append(scores(i + 1, c))
            e = jnp.exp2(cur[c] - m)
            lc = jnp.sum(e, axis=-1, keepdims=True)
            oc = _dot(e.astype(BF16), units[i][2](lo, n))
            acc = oc if acc is None else acc + oc
            l = lc if l is None else l + lc
        out.append((acc, l))
        cur = nxt
    return out


def _by_query_kind(run):
    rb = pl.program_id(1)

    @pl.when(rb < NLAT)
    def _():
        run(LAT_CHUNKS)

    @pl.when(rb >= NLAT)
    def _():
        run(CTX_CHUNKS)


def _mla_kernel(q_ref, k_ref, v_ref, o_ref):
    lane = _lane_id((RB, 128))

    def run(chunks):
        units = [(q_ref[0, :, hd * 128:(hd + 1) * 128], _k_slab(k_ref, hd), _v_slab(v_ref, hd // 2), None)
                 for hd in range(MLA_H)]
        o = [acc * (1.0 / l) for acc, l in _attend(units, chunks)]
        outs = [jnp.where(lane < 64, o[0], o[1]), jnp.where(lane < 64, o[2], o[3])]
        o_ref[0] = jnp.concatenate(outs, axis=1).astype(o_ref.dtype)

    _by_query_kind(run)


def _gqa_kernel(q_ref, k_ref, v_ref, o_ref):
    lane = _lane_id((RB, 128))

    def run(chunks):
        order = [(rep, grp) for rep in range(2) for grp in range(2)]
        units = []
        for rep, grp in order:
            qs = q_ref[0, :, rep * 128:(rep + 1) * 128]
            qm = jnp.where((lane >= 64) == (grp == 1), qs, jnp.zeros_like(qs))
            units.append((qm, _k_slab(k_ref, 0), _v_slab(v_ref, 0), None))
        res = {}
        for (rep, grp), (acc, l) in zip(order, _attend(units, chunks)):
            o = acc * (1.0 / l)
            res[(grp, rep)] = o if grp == rep else pltpu.roll(o, 64, axis=1)
        outs = [jnp.where(lane < 64, res[(grp, 0)], res[(grp, 1)]) for grp in range(2)]
        o_ref[0] = jnp.concatenate(outs, axis=1).astype(o_ref.dtype)

    _by_query_kind(run)


def _diff_kernel(q_ref, k_ref, v_ref, lam_ref, g_ref, o_ref, *, lambda_init):
    lane = _lane_id((RB, 128))
    lq1, lk1, lq2, lk2 = lam_ref[0:1, :], lam_ref[1:2, :], lam_ref[2:3, :], lam_ref[3:4, :]
    lam = (jnp.exp(jnp.sum(lq1 * lk1, axis=-1, keepdims=True))
           - jnp.exp(jnp.sum(lq2 * lk2, axis=-1, keepdims=True)) + lambda_init)
    gsub = g_ref[...]
    grp = lane // DIFF_D

    def run(chunks):
        units = []
        for hd in range(DIFF_H):
            pair, sub = divmod(hd, 2)
            qs = q_ref[0, :, pair * 128:(pair + 1) * 128]
            for which in range(2):
                qm = jnp.where(grp == 2 * sub + which, qs, jnp.zeros_like(qs))
                units.append((qm, _k_slab(k_ref, pair), _v_slab(v_ref, pair), None))
        res = _attend(units, chunks)
        outs = []
        for pair in range(2):
            halves = []
            for sub in range(2):
                (a1, l1), (a2, l2) = res[2 * (2 * pair + sub)], res[2 * (2 * pair + sub) + 1]
                halves.append(a1 * (1.0 / l1) - a2 * (lam / l2))
            o = jnp.where(lane < 64, halves[0], halves[1])
            o2 = o * o
            ss0 = jnp.sum(jnp.where(lane < 64, o2, 0.0), axis=-1, keepdims=True)
            ss1 = jnp.sum(jnp.where(lane < 64, 0.0, o2), axis=-1, keepdims=True)
            ss = jnp.where(lane < 64, ss0, ss1) * (1.0 / (2 * DIFF_D))
            outs.append(o * lax.rsqrt(ss + EPS) * gsub * (1.0 - lambda_init))
        o_ref[0] = jnp.concatenate(outs, axis=1).astype(o_ref.dtype)

    _by_query_kind(run)


NA_WIN = 1024
NA_CHUNK = 512
NA_TILES = 49
GRID_ROWS = SEQ // GRID_W


def _na_kernel(q_ref, k_ref, v_ref, bias_ref, o_ref):
    lane = _lane_id((RB, 128))
    rb = pl.program_id(1)

    def run(chunks, tile_idx):
        units = []
        for hd in range(NA_H):
            pair, sub = divmod(hd, 2)
            qs = q_ref[0, :, pair * 128:(pair + 1) * 128]
            qm = jnp.where((lane >= 64) == (sub == 1), qs, jnp.zeros_like(qs))

            def bias(c, hd=hd):
                if tile_idx is None or c >= NA_WIN // NA_CHUNK:
                    return None
                return jnp.concatenate(
                    [jnp.concatenate([bias_ref[hd, idx] for idx in row], axis=1) for row in tile_idx[c]], axis=0)

            units.append((qm, _k_slab(k_ref, pair), _v_slab(v_ref, pair), bias))
        o = [acc * (1.0 / l) for acc, l in _attend(units, chunks)]
        outs = [jnp.where(lane < 64, o[0], o[1]), jnp.where(lane < 64, o[2], o[3])]
        o_ref[0] = jnp.concatenate(outs, axis=1).astype(o_ref.dtype)

    @pl.when(rb < NLAT)
    def _():
        win_row = jnp.clip(4 * rb - 4, 0, GRID_ROWS - NA_WIN // GRID_W)
        off = pl.multiple_of(win_row * GRID_W, RB)
        local = [(pl.multiple_of(off + c * NA_CHUNK, RB), NA_CHUNK) for c in range(NA_WIN // NA_CHUNK)]
        tile_idx = []
        for c in range(NA_WIN // NA_CHUNK):
            per_row = []
            for ri in range(RB // GRID_W):
                r = 4 * rb + ri
                row_start = jnp.clip(r - NA_ROWS // 2, 0, GRID_ROWS - NA_ROWS)
                idxs = []
                for p in range(NA_CHUNK // 128):
                    kr = win_row + c * (NA_CHUNK // GRID_W) + 2 * p
                    ok1 = (kr >= row_start) & (kr < row_start + NA_ROWS)
                    ok2 = (kr + 1 >= row_start) & (kr + 1 < row_start + NA_ROWS)
                    e = jnp.clip(kr - r + NA_ROWS, 0, 15)
                    idxs.append(jnp.where(ok1 & ok2, e, jnp.where(ok1, 16 + e, jnp.where(ok2, 32 + e, 48))))
                per_row.append(idxs)
            tile_idx.append(per_row)
        run(local + list(CTX_CHUNKS), tile_idx)

    @pl.when(rb >= NLAT)
    def _():
        run(CTX_CHUNKS, None)


def _attention(body, name, q, k, v, extra, n_rb, out_width=256):
    n_b = q.shape[0]
    in_specs = [
        pl.BlockSpec((1, RB, q.shape[2]), lambda b, r: (b, r, 0)),
        pl.BlockSpec((1, k.shape[1], T), lambda b, r: (b, 0, 0)),
        pl.BlockSpec((1, T, v.shape[2]), lambda b, r: (b, 0, 0)),
    ] + [spec for _, spec in extra]
    return pl.pallas_call(
        body,
        grid=(n_b, n_rb),
        in_specs=in_specs,
        out_specs=pl.BlockSpec((1, RB, out_width), lambda b, r: (b, r, 0)),
        out_shape=jax.ShapeDtypeStruct((n_b, n_rb * RB, out_width), BF16),
        compiler_params=_params(("parallel", "parallel")),
        name=name,
    )(q, k, v, *[a for a, _ in extra])


def _outproj_kernel(x_ref, modb_ref, modc_ref, ma_ref, mb_ref, mc_ref, md_ref, w_ref, o_ref, *, tm):
    rows = pl.program_id(1) * tm + lax.broadcasted_iota(jnp.int32, (tm, 1), 0)
    g1 = jnp.where(rows >= SEQ, modc_ref[0][:, 2 * D:3 * D], modb_ref[0][:, 2 * D:3 * D])
    acc = _dot(ma_ref[0], w_ref[0:256, :])
    acc += _dot(mb_ref[0], w_ref[256:512, :])
    acc += _dot(mc_ref[0], w_ref[512:768, :])
    acc += _dot(md_ref[0], w_ref[768:1024, :])
    o_ref[0] = x_ref[0] + g1 * acc


def _outproj(xs, mods, mixes, w_out, rows_per_sample, tm):
    n_b = xs.shape[0]
    mix_spec = pl.BlockSpec((1, tm, 256), lambda b, r: (b, r, 0))
    return pl.pallas_call(
        functools.partial(_outproj_kernel, tm=tm),
        grid=(n_b, rows_per_sample // tm),
        in_specs=[
            pl.BlockSpec((1, tm, D), lambda b, r: (b, r, 0)),
            pl.BlockSpec((1, 1, 6 * D), lambda b, r: (b, 0, 0)),
            pl.BlockSpec((1, 1, 6 * D), lambda b, r: (n_b, 0, 0)),
            mix_spec, mix_spec, mix_spec, mix_spec,
            pl.BlockSpec((D, D), lambda b, r: (0, 0)),
        ],
        out_specs=pl.BlockSpec((1, tm, D), lambda b, r: (b, r, 0)),
        out_shape=jax.ShapeDtypeStruct((n_b, rows_per_sample, D), F32),
        compiler_params=_params(("parallel", "parallel")),
        name="outproj",
    )(xs, mods, mods, *mixes, w_out)


def _ffn_kernel(x_ref, hb_ref, ha_ref, modb_ref, modc_ref, g_ref, wu_ref, cw_ref, wd_ref, o_ref,
                h_scr, u0_scr, u1_scr, act_scr, *, tm, rows_per_sample):
    rb = pl.program_id(1)
    j = pl.program_id(2)
    row0 = rb * tm
    split = SEQ % tm if rows_per_sample > SEQ else 0
    band_lo, band_hi = split - HALO, split + HALO

    def per_row(rows, col):
        lat = modb_ref[0][:, col * D:(col + 1) * D]
        if rows_per_sample == SEQ:
            return lat
        return jnp.where(rows >= SEQ, modc_ref[0][:, col * D:(col + 1) * D], lat)

    def modulated(x, rows):
        ms = jnp.mean(x * x, axis=-1, keepdims=True)
        y = x * lax.rsqrt(ms + EPS) * g_ref[...]
        return (y * (1.0 + per_row(rows, 4)) + per_row(rows, 3)).astype(BF16)

    def prologue():
        halo_rows = lax.broadcasted_iota(jnp.int32, (HALO, 1), 0)
        has_before = (row0 != 0) & (row0 != SEQ)
        has_after = (row0 + tm != SEQ) & (row0 + tm != rows_per_sample)
        zero = jnp.zeros((HALO, D), BF16)
        h_scr[0:HALO, :] = jnp.where(has_before, modulated(hb_ref[0], row0 - HALO + halo_rows), zero)
        h_scr[HALO + tm:, :] = jnp.where(has_after, modulated(ha_ref[0], row0 + tm + halo_rows), zero)
        main_rows = row0 + lax.broadcasted_iota(jnp.int32, (tm, 1), 0)
        h_scr[HALO:HALO + tm, :] = modulated(x_ref[0], main_rows)
        o_ref[0] = jnp.zeros((tm, D), F32)

    slots = (u0_scr, u1_scr)

    def tiles(total):
        units = total // HALO
        sizes = [(units // FFN_TILES + (1 if t < units % FFN_TILES else 0)) * HALO for t in range(FFN_TILES)]
        edges = np.cumsum([0] + sizes)
        return [(int(edges[t]), int(edges[t + 1])) for t in range(FFN_TILES)]

    up_tiles, down_tiles = tiles(tm + 2 * HALO), tiles(tm)

    def up(slot, t):
        lo, hi = up_tiles[t]
        slots[slot][lo:hi, :] = _dot(h_scr[lo:hi, :], wu_ref[...])

    def conv(u_scr, lo, hi, masked):
        prev = u_scr[HALO - 1 + lo:HALO - 1 + hi, :]
        nxt = u_scr[HALO + 1 + lo:HALO + 1 + hi, :]
        if masked:
            rows = row0 + lo + lax.broadcasted_iota(jnp.int32, (hi - lo, 1), 0)
            prev = jnp.where(rows != SEQ, prev, 0.0)
            nxt = jnp.where(rows != SEQ - 1, nxt, 0.0)
        return (prev * cw_ref[0:1, :] + u_scr[HALO + lo:HALO + hi, :] * cw_ref[1:2, :]
                + nxt * cw_ref[2:3, :] + cw_ref[3:4, :])

    def conv_down(slot, t):
        t_lo, t_hi = down_tiles[t]
        cuts = sorted({t_lo, t_hi} | ({c for c in (band_lo, band_hi) if t_lo < c < t_hi} if split else set()))
        for lo, hi in zip(cuts[:-1], cuts[1:]):
            masked = bool(split) and band_lo <= lo < band_hi
            c = conv(slots[slot], lo, hi, masked)
            a, g = c[:, :FF_CHUNK], c[:, FF_CHUNK:]
            act_scr[lo:hi, :] = (g * jax.nn.sigmoid(g) * a).astype(BF16)
        o_ref[0, t_lo:t_hi, :] += _dot(act_scr[t_lo:t_hi, :], wd_ref[...])

    @pl.when(j == 0)
    def _():
        prologue()
        for t in range(FFN_TILES):
            up(0, t)

    for parity in range(2):
        @pl.when((j > 0) & (j < N_CHUNK) & (j % 2 == parity))
        def _():
            for t in range(FFN_TILES):
                up(parity, t)
                conv_down(1 - parity, t)

    @pl.when(j == N_CHUNK)
    def _():
        for t in range(FFN_TILES):
            conv_down((N_CHUNK - 1) % 2, t)
        rows = row0 + lax.broadcasted_iota(jnp.int32, (tm, 1), 0)
        o_ref[0] = x_ref[0] + per_row(rows, 5) * o_ref[0]


def _chunk_interleave(a):
    lead = a.shape[:-1]
    return a.reshape(lead + (2, N_CHUNK, FF_CHUNK)).swapaxes(-3, -2).reshape(lead + (2 * D_FF,))


def _ffn(x1, mods, g_ffn, w_up, conv_wb, w_down, rows_per_sample, tm):
    n_b = x1.shape[0]
    n_rb = rows_per_sample // tm
    halo_per_block = tm // HALO
    n_halo = x1.shape[1] // HALO
    body = functools.partial(_ffn_kernel, tm=tm, rows_per_sample=rows_per_sample)

    def up_idx(j):
        return jnp.minimum(j, N_CHUNK - 1)

    def down_idx(j):
        return jnp.maximum(j - 1, 0)

    return pl.pallas_call(
        body,
        grid=(n_b, n_rb, N_CHUNK + 1),
        in_specs=[
            pl.BlockSpec((1, tm, D), lambda b, r, j: (b, r, 0)),
            pl.BlockSpec((1, HALO, D), lambda b, r, j: (b, jnp.maximum(r * halo_per_block - 1, 0), 0)),
            pl.BlockSpec((1, HALO, D), lambda b, r, j: (b, jnp.minimum((r + 1) * halo_per_block, n_halo - 1), 0)),
            pl.BlockSpec((1, 1, 6 * D), lambda b, r, j: (b, 0, 0)),
            pl.BlockSpec((1, 1, 6 * D), lambda b, r, j: (n_b, 0, 0)),
            pl.BlockSpec((1, D), lambda b, r, j: (0, 0)),
            pl.BlockSpec((D, 2 * FF_CHUNK), lambda b, r, j: (0, up_idx(j))),
            pl.BlockSpec((4, 2 * FF_CHUNK), lambda b, r, j: (0, down_idx(j))),
            pl.BlockSpec((FF_CHUNK, D), lambda b, r, j: (down_idx(j), 0)),
        ],
        out_specs=pl.BlockSpec((1, tm, D), lambda b, r, j: (b, r, 0)),
        out_shape=jax.ShapeDtypeStruct((n_b, rows_per_sample, D), F32),
        scratch_shapes=[
            pltpu.VMEM((tm + 2 * HALO, D), BF16),
            pltpu.VMEM((tm + 2 * HALO, 2 * FF_CHUNK), F32),
            pltpu.VMEM((tm + 2 * HALO, 2 * FF_CHUNK), F32),
            pltpu.VMEM((tm, FF_CHUNK), BF16),
        ],
        compiler_params=_params(("parallel", "parallel", "arbitrary")),
        name="conv_ffn",
    )(x1, x1, x1, mods, mods, g_ffn, w_up, conv_wb, w_down)


def _rope_table(rot_dim):
    n_freq = rot_dim // 4
    inv = jnp.power(ROPE_THETA, -jnp.arange(n_freq, dtype=F32) / n_freq)
    t = jnp.arange(SEQ)
    row = (t // GRID_W).astype(F32)
    col = (t % GRID_W).astype(F32)
    ar, ac = row[:, None] * inv, col[:, None] * inv
    ang = jnp.concatenate([ar, ar, ac, ac], axis=-1)
    sign = jnp.concatenate([-jnp.ones(n_freq), jnp.ones(n_freq), -jnp.ones(n_freq), jnp.ones(n_freq)]).astype(F32)
    cos = jnp.concatenate([jnp.cos(ang), jnp.ones((CTX, rot_dim), F32)], axis=0)
    sin = jnp.concatenate([jnp.sin(ang) * sign, jnp.zeros((CTX, rot_dim), F32)], axis=0)
    return jnp.concatenate([cos.T, sin.T], axis=0)


def _na_bias_tiles(rpb):
    n_l, n_h = rpb.shape[:2]
    n_off = 2 * NA_ROWS - 1
    p = jnp.pad(rpb.astype(F32) * LOG2E, ((0, 0), (0, 0), (0, 0), (48, 49)))
    sk = jnp.broadcast_to(p[..., None, :], (n_l, n_h, n_off, GRID_W, 128)).reshape(n_l, n_h, n_off, GRID_W * 128)
    sk = sk[..., :GRID_W * 127].reshape(n_l, n_h, n_off, GRID_W, 127)[..., 63:127]
    c = np.arange(GRID_W)
    win_start = np.clip(c - NA_COLS // 2, 0, GRID_W - NA_COLS)
    v_col = (c[None, :] >= win_start[:, None]) & (c[None, :] < win_start[:, None] + NA_COLS)
    tz = jnp.where(v_col, sk, NEG_INF)
    neg1 = jnp.full((n_l, n_h, 1, GRID_W, GRID_W), NEG_INF, F32)
    tzx = jnp.concatenate([neg1, tz, neg1], axis=2)
    first, second = tzx[:, :, 0:16], tzx[:, :, 1:17]
    neg16 = jnp.full_like(first, NEG_INF)
    return jnp.concatenate([
        jnp.concatenate([first, second], axis=-1), jnp.concatenate([first, neg16], axis=-1),
        jnp.concatenate([neg16, second], axis=-1), jnp.concatenate([neg1, neg1], axis=-1)], axis=2)


def _col(v):
    return v.reshape(-1, 1).astype(F32)


def kernel(x, c, ctx, c_ctx, w_mod, b_mod, g_mix, w_in, w_out, mla_q_a_g, mla_w_uq, mla_kv_a_g, mla_w_ukv,
           mla_q_g, mla_k_g, diff_q_g, diff_k_g, diff_lq1, diff_lk1, diff_lq2, diff_lk2, diff_subln_g,
           na_q_g, na_k_g, na_rpb, gqa_q_g, gqa_k_g, g_ffn, w_up, conv_w, conv_b, w_down):
    n_b = x.shape[0]
    n_layer = w_mod.shape[0]
    assert x.shape[1:] == (SEQ, D) and ctx.shape[1:] == (CTX, D)

    xs = jnp.concatenate([x, ctx], axis=1)
    mod_rows = -(-(n_b + 1) // 8) * 8
    cc = jnp.concatenate([c, c_ctx[None], jnp.zeros((mod_rows - n_b - 1, D), F32)], axis=0)
    mods_all = _modulation(cc, w_mod, b_mod).reshape(n_layer, mod_rows, 1, 6 * D)

    cs32, cs64 = _rope_table(32), _rope_table(64)
    bias_all = _na_bias_tiles(na_rpb)

    s = np.cumsum([0, 256, 128, 32, 256, 256, 256, 256, 256, 256, 256, 128, 128])
    gq_perm = np.concatenate([np.arange(GQA_D) + (2 * g + r) * GQA_D for r in range(2) for g in range(2)])

    for l in range(n_layer):
        with_ctx = l < n_layer - 1
        lambda_init = 0.8 - 0.6 * math.exp(-0.3 * l)
        mods = mods_all[l]
        w = w_in[l]
        gq_cols = w[:, s[9]:s[10]][:, gq_perm]
        wqk = jnp.concatenate([w[:, s[0]:s[3]], w[:, s[3]:s[5]], w[:, s[6]:s[8]], gq_cols, w[:, s[10]:s[11]]],
                              axis=1).T.astype(BF16)
        wv = jnp.concatenate([w[:, s[5]:s[6]], w[:, s[8]:s[9]], w[:, s[11]:s[12]]], axis=1).astype(BF16)
        wuq = mla_w_uq[l].T.astype(BF16)
        wukv = mla_w_ukv[l].reshape(MLA_KVR, MLA_H, MLA_NOPE + MLA_V)
        wukn = wukv[:, :, :MLA_NOPE].reshape(MLA_KVR, MLA_H * MLA_NOPE).T.astype(BF16)
        wuv = wukv[:, :, MLA_NOPE:].reshape(MLA_KVR, MLA_H * MLA_V).T.astype(BF16)
        gcol = jnp.concatenate([_col(mla_q_a_g[l]), _col(mla_kv_a_g[l]), _col(mla_q_g[l]), _col(mla_k_g[l]),
                                _col(diff_q_g[l]), _col(diff_k_g[l]), _col(na_q_g[l]), _col(na_k_g[l]),
                                _col(gqa_q_g[l]), _col(gqa_k_g[l])], axis=0)
        (q_mla, k_mla, v_mla, q_diff, k_diff, v_diff, q_na, k_na, v_na, q_gqa, k_gqa, v_gqa) = _inproj(
            xs, mods, g_mix[l].reshape(1, D), wqk, wv, wuq, wukn, wuv, gcol, cs32, cs64)

        n_rb = NRB if with_ctx else NLAT
        mix_a = _attention(_mla_kernel, "attn_mla", q_mla, k_mla, v_mla, [], n_rb)
        lam_rows = jnp.stack([diff_lq1[l], diff_lk1[l], diff_lq2[l], diff_lk2[l]]).astype(F32)
        lam_rows = jnp.concatenate([lam_rows, jnp.zeros((4, DIFF_D), F32)], axis=0)
        gsub = jnp.tile(diff_subln_g[l].astype(F32), 2).reshape(1, 128)
        mix_b = _attention(
            functools.partial(_diff_kernel, lambda_init=lambda_init), "attn_diff",
            q_diff, k_diff, v_diff,
            [(lam_rows, pl.BlockSpec((8, DIFF_D), lambda b, r: (0, 0))),
             (gsub, pl.BlockSpec((1, 128), lambda b, r: (0, 0)))], n_rb)
        bias = bias_all[l]
        bias_spec = pl.BlockSpec((NA_H, NA_TILES, GRID_W, 128), lambda b, r: (0, 0, 0, 0))
        mix_c = _attention(_na_kernel, "attn_na", q_na, k_na, v_na, [(bias, bias_spec)], n_rb)
        mix_d = _attention(_gqa_kernel, "attn_gqa", q_gqa, k_gqa, v_gqa, [], n_rb)

        rows_per_sample = T if with_ctx else SEQ
        x1 = _outproj(xs, mods, (mix_a, mix_b, mix_c, mix_d), w_out[l].astype(BF16), rows_per_sample,
                      OUT_TM if with_ctx else OUT_TM_LAST)
        conv_wb = jnp.concatenate([conv_w[l], conv_b[l].reshape(1, 2 * D_FF)], axis=0).astype(F32)
        xs = _ffn(x1, mods, g_ffn[l].reshape(1, D), _chunk_interleave(w_up[l]).astype(BF16),
                  _chunk_interleave(conv_wb), w_down[l].astype(BF16), rows_per_sample, rows_per_sample // 2)
    return xs
```

```python
import functools
import math

import numpy as np
import jax
import jax.numpy as jnp
from jax import lax
from jax.experimental import pallas as pl
from jax.experimental.pallas import tpu as pltpu

F32 = jnp.float32
BF16 = jnp.bfloat16

D = 1024
SEQ = 2048
GRID_W = 64
CTX = 256
T = SEQ + CTX
RB = 256
NRB = T // RB
NLAT = SEQ // RB
EPS = 1e-6
NEG_INF = -1e30
ROPE_THETA = 10000.0
LOG2E = 1.4426950408889634

MLA_H, MLA_NOPE, MLA_ROPE, MLA_V, MLA_QR, MLA_KVR = 4, 64, 32, 64, 256, 128
MLA_D = MLA_NOPE + MLA_ROPE
DIFF_H, DIFF_D = 4, 32
NA_H, NA_D, NA_ROWS, NA_COLS = 4, 64, 8, 16
GQA_H, GQA_KV, GQA_D = 4, 2, 64
D_FF = 2816
FF_CHUNK = 256
N_CHUNK = D_FF // FF_CHUNK
OUT_TM, OUT_TM_LAST = 768, 1024
HALO = 16
FFN_TILES = 4
CONV_STRIP = 128

R_CQ, R_CKV, R_KR, R_DQ, R_DK, R_NQ, R_NK, R_GQ, R_GK = 0, 256, 384, 416, 672, 928, 1184, 1440, 1696
QK_ROWS = 1824
V_COLS = 640

G_QA, G_KVA, G_MQ, G_MK, G_DQ, G_DK, G_NQ, G_NK, G_GQ, G_GK = 0, 256, 384, 480, 576, 608, 640, 704, 768, 832
G_ROWS = 896

VMEM_LIMIT = 56 * 1024 * 1024


def _params(sem):
    return pltpu.CompilerParams(dimension_semantics=sem, vmem_limit_bytes=VMEM_LIMIT)


def _dot(a, b):
    return jnp.dot(a, b, preferred_element_type=F32)


def _dot_nt(a, b):
    return lax.dot_general(a, b, (((1,), (1,)), ((), ())), preferred_element_type=F32)


def _mod_kernel(c_ref, w_ref, b_ref, o_ref):
    c = c_ref[...]
    a = (c * jax.nn.sigmoid(c)).astype(BF16)
    o_ref[0] = _dot(a, w_ref[0].astype(BF16)) + b_ref[0]


def _modulation(cc, w_mod, b_mod):
    n_layer = w_mod.shape[0]
    rows = cc.shape[0]
    return pl.pallas_call(
        _mod_kernel,
        grid=(n_layer, 6),
        in_specs=[
            pl.BlockSpec((rows, D), lambda l, j: (0, 0)),
            pl.BlockSpec((1, D, D), lambda l, j: (l, 0, j)),
            pl.BlockSpec((1, 1, D), lambda l, j: (l, 0, j)),
        ],
        out_specs=pl.BlockSpec((1, rows, D), lambda l, j: (l, 0, j)),
        out_shape=jax.ShapeDtypeStruct((n_layer, rows, 6 * D), F32),
        compiler_params=_params(("parallel", "parallel")),
        name="modulation",
    )(cc, w_mod, b_mod.reshape(n_layer, 1, 6 * D))


def _rms_rows(x, g, n):
    ss = jnp.sum(x * x, axis=0, keepdims=True) * (1.0 / n)
    return x * lax.rsqrt(ss + EPS) * g


def _rope_rows(x, cos, sin_signed, w):
    rot = jnp.concatenate([x[w:2 * w], x[0:w], x[3 * w:4 * w], x[2 * w:3 * w]], axis=0)
    return x * cos + rot * sin_signed


def _inproj_kernel(x_ref, mod_ref, gmix_ref, wqk_ref, wv_ref, wuq_ref, wukn_ref, wuv_ref, gcol_ref,
                   cs32_ref, cs64_ref,
                   qmla_ref, kmla_ref, vmla_ref, qdiff_ref, kdiff_ref, vdiff_ref,
                   qna_ref, kna_ref, vna_ref, qgqa_ref, kgqa_ref, vgqa_ref):
    x = x_ref[0]
    mod = mod_ref[0]
    shift, scale = mod[:, 0:D], mod[:, D:2 * D]
    ms = jnp.mean(x * x, axis=-1, keepdims=True)
    h = x * lax.rsqrt(ms + EPS) * gmix_ref[...]
    hb = (h * (1.0 + scale) + shift).astype(BF16)

    def project(lo, hi):
        return _dot_nt(wqk_ref[lo:hi, :], hb)

    def gain(off, n):
        return gcol_ref[off:off + n, :]

    cos32, sin32 = cs32_ref[0:32, :], cs32_ref[32:64, :]
    cos64, sin64 = cs64_ref[0:64, :], cs64_ref[64:128, :]

    def values():
        pv = _dot(hb, wv_ref[...])
        vdiff_ref[0] = pv[:, 0:256].astype(BF16)
        vna_ref[0] = pv[:, 256:512].astype(BF16)
        vgqa_ref[0] = pv[:, 512:640].astype(BF16)

    def mla(pt):
        cq = _rms_rows(pt[0:MLA_QR], gain(G_QA, MLA_QR), MLA_QR).astype(BF16)
        qt = _dot(wuq_ref[...], cq)
        ckv = _rms_rows(pt[R_CKV:R_CKV + MLA_KVR], gain(G_KVA, MLA_KVR), MLA_KVR).astype(BF16)
        knt = _dot(wukn_ref[...], ckv)
        vt = _dot(wuv_ref[...], ckv)
        vmla_ref[0] = vt.T.astype(BF16)
        kr = pt[R_KR:R_KR + MLA_ROPE]
        kr_ss = jnp.sum(kr * kr, axis=0, keepdims=True)
        g_mq, g_mk = gain(G_MQ, MLA_D), gain(G_MK, MLA_D)
        zpad = jnp.zeros((128 - MLA_D, RB), F32)
        q_parts = []
        for hd in range(MLA_H):
            qh = _rms_rows(qt[hd * MLA_D:(hd + 1) * MLA_D], g_mq, MLA_D)
            q_rope = _rope_rows(qh[MLA_NOPE:], cos32, sin32, MLA_ROPE // 4)
            q_parts += [qh[:MLA_NOPE], q_rope, zpad]
            kn = knt[hd * MLA_NOPE:(hd + 1) * MLA_NOPE]
            ss = (jnp.sum(kn * kn, axis=0, keepdims=True) + kr_ss) * (1.0 / MLA_D)
            r = lax.rsqrt(ss + EPS)
            k_rope = _rope_rows(kr * r * g_mk[MLA_NOPE:], cos32, sin32, MLA_ROPE // 4)
            kmla_ref[0, hd * 128:(hd + 1) * 128, :] = jnp.concatenate(
                [kn * r * g_mk[:MLA_NOPE], k_rope, zpad], axis=0).astype(BF16)
        q_all = jnp.concatenate(q_parts, axis=0) * (MLA_D ** -0.5 * LOG2E)
        qmla_ref[0] = q_all.T.astype(BF16)

    def diff(pt):
        g_dq, g_dk = gain(G_DQ, DIFF_D), gain(G_DK, DIFF_D)
        q_parts, k_parts = [], []
        for gi in range(2 * DIFF_H):
            qg = _rms_rows(pt[gi * DIFF_D:(gi + 1) * DIFF_D], g_dq, DIFF_D)
            q_parts.append(_rope_rows(qg, cos32, sin32, DIFF_D // 4))
            kg = _rms_rows(pt[256 + gi * DIFF_D:256 + (gi + 1) * DIFF_D], g_dk, DIFF_D)
            k_parts.append(_rope_rows(kg, cos32, sin32, DIFF_D // 4))
        qdiff_ref[0] = (jnp.concatenate(q_parts, axis=0) * (DIFF_D ** -0.5 * LOG2E)).T.astype(BF16)
        kdiff_ref[0] = jnp.concatenate(k_parts, axis=0).astype(BF16)

    def na(pt):
        g_nq, g_nk = gain(G_NQ, NA_D), gain(G_NK, NA_D)
        q_parts, k_parts = [], []
        for hd in range(NA_H):
            q_parts.append(_rms_rows(pt[hd * NA_D:(hd + 1) * NA_D], g_nq, NA_D))
            k_parts.append(_rms_rows(pt[256 + hd * NA_D:256 + (hd + 1) * NA_D], g_nk, NA_D))
        qna_ref[0] = (jnp.concatenate(q_parts, axis=0) * (NA_D ** -0.5 * LOG2E)).T.astype(BF16)
        kna_ref[0] = jnp.concatenate(k_parts, axis=0).astype(BF16)

    def gqa(pt):
        g_gq, g_gk = gain(G_GQ, GQA_D), gain(G_GK, GQA_D)
        q_parts, k_parts = [], []
        for hd in range(GQA_H):
            qg = _rms_rows(pt[hd * GQA_D:(hd + 1) * GQA_D], g_gq, GQA_D)
            q_parts.append(_rope_rows(qg, cos64, sin64, GQA_D // 4))
        for hd in range(GQA_KV):
            kg = _rms_rows(pt[256 + hd * GQA_D:256 + (hd + 1) * GQA_D], g_gk, GQA_D)
            k_parts.append(_rope_rows(kg, cos64, sin64, GQA_D // 4))
        qgqa_ref[0] = (jnp.concatenate(q_parts, axis=0) * (GQA_D ** -0.5 * LOG2E)).T.astype(BF16)
        kgqa_ref[0] = jnp.concatenate(k_parts, axis=0).astype(BF16)

    pt_mla = project(R_CQ, R_DQ)
    pt_diff = project(R_DQ, R_NQ)
    mla(pt_mla)
    pt_na = project(R_NQ, R_GQ)
    diff(pt_diff)
    pt_gqa = project(R_GQ, QK_ROWS)
    na(pt_na)
    values()
    gqa(pt_gqa)


def _inproj(xs, mods, gmix, wqk, wv, wuq, wukn, wuv, gcol, cs32, cs64):
    n_b = xs.shape[0]

    def full(a):
        return pl.BlockSpec(a.shape, lambda r, b, _n=a.ndim: (0,) * _n)

    def tok(width):
        return pl.BlockSpec((1, RB, width), lambda r, b: (b, r, 0))

    def chan(rows):
        return pl.BlockSpec((1, rows, RB), lambda r, b: (b, 0, r))

    def tshape(width):
        return jax.ShapeDtypeStruct((n_b, T, width), BF16)

    def cshape(rows):
        return jax.ShapeDtypeStruct((n_b, rows, T), BF16)

    return pl.pallas_call(
        _inproj_kernel,
        grid=(NRB, n_b),
        in_specs=[
            pl.BlockSpec((1, RB, D), lambda r, b: (b, r, 0)),
            pl.BlockSpec((1, 1, 6 * D), lambda r, b: (jnp.where(r < NLAT, b, n_b), 0, 0)),
            full(gmix), full(wqk), full(wv), full(wuq), full(wukn), full(wuv), full(gcol),
            pl.BlockSpec((64, RB), lambda r, b: (0, r)),
            pl.BlockSpec((128, RB), lambda r, b: (0, r)),
        ],
        out_specs=[tok(512), chan(512), tok(256), tok(256), chan(256), tok(256),
                   tok(256), chan(256), tok(256), tok(256), chan(128), tok(128)],
        out_shape=[tshape(512), cshape(512), tshape(256), tshape(256), cshape(256), tshape(256),
                   tshape(256), cshape(256), tshape(256), tshape(256), cshape(128), tshape(128)],
        compiler_params=_params(("parallel", "parallel")),
        name="inproj_prep",
    )(xs, mods, gmix, wqk, wv, wuq, wukn, wuv, gcol, cs32, cs64)


def _lane_id(shape):
    return lax.broadcasted_iota(jnp.int32, shape, 1)


KEY_CHUNK = 768
LAT_CHUNKS = tuple((lo, KEY_CHUNK) for lo in range(0, T, KEY_CHUNK))
CTX_CHUNKS = ((SEQ, CTX),)


def _k_slab(k_ref, idx):
    return lambda lo, n: k_ref[0, idx * 128:(idx + 1) * 128, pl.ds(lo, n)]


def _v_slab(v_ref, idx):
    return lambda lo, n: v_ref[0, pl.ds(lo, n), idx * 128:(idx + 1) * 128]


def _attend(units, chunks, shift=None):
    def scores(i, c):
        q, keys, _, bias = units[i]
        lo, n = chunks[c]
        s = _dot(q, keys(lo, n))
        b = None if bias is None else bias(c)
        return s if b is None else s + b

    if shift is not None:
        items = [(i, c) for i in range(len(units)) for c in range(len(chunks))]
        acc, l = [None] * len(units), [None] * len(units)
        nxt = scores(*items[0])
        for idx, (i, c) in enumerate(items):
            cur = nxt
            if idx + 1 < len(items):
                nxt = scores(*items[idx + 1])
            e = jnp.exp2(cur - shift)
            lc = jnp.sum(e, axis=-1, keepdims=True)
            oc = _dot(e.astype(BF16), units[i][2](*chunks[c]))
            acc[i] = oc if acc[i] is None else acc[i] + oc
            l[i] = lc if l[i] is None else l[i] + lc
        return list(zip(acc, l))

    cur = [scores(0, c) for c in range(len(chunks))]
    out = []
    for i in range(len(units)):
        m = functools.reduce(jnp.maximum, [jnp.max(s, axis=-1, keepdims=True) for s in cur])
        nxt, acc, l = [], None, None
        for c, (lo, n) in enumerate(chunks):
            if i + 1 < len(units):
                nxt.append(scores(i + 1, c))
            e = jnp.exp2(cur[c] - m)
            lc = jnp.sum(e, axis=-1, keepdims=True)
            oc = _dot(e.astype(BF16), units[i][2](lo, n))
            acc = oc if acc is None else acc + oc
            l = lc if l is None else l + lc
        out.append((acc, l))
        cur = nxt
    return out


def _with_shift(bnd_ref, run):
    @pl.when(bnd_ref[1] > 0.5)
    def _():
        run(bnd_ref[0])

    @pl.when(bnd_ref[1] <= 0.5)
    def _():
        run(None)


def _by_query_kind(bnd_ref, run):
    rb = pl.program_id(1)

    @pl.when(rb < NLAT)
    def _():
        _with_shift(bnd_ref, lambda shift: run(LAT_CHUNKS, shift))

    @pl.when(rb >= NLAT)
    def _():
        _with_shift(bnd_ref, lambda shift: run(CTX_CHUNKS, shift))


def _mla_kernel(q_ref, k_ref, v_ref, bnd_ref, o_ref):
    lane = _lane_id((RB, 128))

    def run(chunks, shift):
        units = [(q_ref[0, :, hd * 128:(hd + 1) * 128], _k_slab(k_ref, hd), _v_slab(v_ref, hd // 2), None)
                 for hd in range(MLA_H)]
        o = [acc * (1.0 / l) for acc, l in _attend(units, chunks, shift)]
        outs = [jnp.where(lane < 64, o[0], o[1]), jnp.where(lane < 64, o[2], o[3])]
        o_ref[0] = jnp.concatenate(outs, axis=1).astype(o_ref.dtype)

    _by_query_kind(bnd_ref, run)


def _gqa_kernel(q_ref, k_ref, v_ref, bnd_ref, o_ref):
    lane = _lane_id((RB, 128))

    def run(chunks, shift):
        order = [(rep, grp) for rep in range(2) for grp in range(2)]
        units = []
        for rep, grp in order:
            qs = q_ref[0, :, rep * 128:(rep + 1) * 128]
            qm = jnp.where((lane >= 64) == (grp == 1), qs, jnp.zeros_like(qs))
            units.append((qm, _k_slab(k_ref, 0), _v_slab(v_ref, 0), None))
        res = {}
        for (rep, grp), (acc, l) in zip(order, _attend(units, chunks, shift)):
            o = acc * (1.0 / l)
            res[(grp, rep)] = o if grp == rep else pltpu.roll(o, 64, axis=1)
        outs = [jnp.where(lane < 64, res[(grp, 0)], res[(grp, 1)]) for grp in range(2)]
        o_ref[0] = jnp.concatenate(outs, axis=1).astype(o_ref.dtype)

    _by_query_kind(bnd_ref, run)


def _diff_kernel(q_ref, k_ref, v_ref, bnd_ref, lam_ref, g_ref, o_ref, *, lambda_init):
    lane = _lane_id((RB, 128))
    lq1, lk1, lq2, lk2 = lam_ref[0:1, :], lam_ref[1:2, :], lam_ref[2:3, :], lam_ref[3:4, :]
    lam = (jnp.exp(jnp.sum(lq1 * lk1, axis=-1, keepdims=True))
           - jnp.exp(jnp.sum(lq2 * lk2, axis=-1, keepdims=True)) + lambda_init)
    gsub = g_ref[...]
    grp = lane // DIFF_D

    def run(chunks, shift):
        units = []
        for hd in range(DIFF_H):
            pair, sub = divmod(hd, 2)
            qs = q_ref[0, :, pair * 128:(pair + 1) * 128]
            for which in range(2):
                qm = jnp.where(grp == 2 * sub + which, qs, jnp.zeros_like(qs))
                units.append((qm, _k_slab(k_ref, pair), _v_slab(v_ref, pair), None))
        res = _attend(units, chunks, shift)
        outs = []
        for pair in range(2):
            halves = []
            for sub in range(2):
                (a1, l1), (a2, l2) = res[2 * (2 * pair + sub)], res[2 * (2 * pair + sub) + 1]
                halves.append(a1 * (1.0 / l1) - a2 * (lam / l2))
            o = jnp.where(lane < 64, halves[0], halves[1])
            o2 = o * o
            ss0 = jnp.sum(jnp.where(lane < 64, o2, 0.0), axis=-1, keepdims=True)
            ss1 = jnp.sum(jnp.where(lane < 64, 0.0, o2), axis=-1, keepdims=True)
            ss = jnp.where(lane < 64, ss0, ss1) * (1.0 / (2 * DIFF_D))
            outs.append(o * lax.rsqrt(ss + EPS) * gsub * (1.0 - lambda_init))
        o_ref[0] = jnp.concatenate(outs, axis=1).astype(o_ref.dtype)

    _by_query_kind(bnd_ref, run)


NA_WIN = 1024
NA_CHUNK = 512
NA_TILES = 49
GRID_ROWS = SEQ // GRID_W


def _na_kernel(q_ref, k_ref, v_ref, bnd_ref, bias_ref, o_ref):
    lane = _lane_id((RB, 128))
    rb = pl.program_id(1)

    def run(chunks, tile_idx, shift):
        units = []
        for hd in range(NA_H):
            pair, sub = divmod(hd, 2)
            qs = q_ref[0, :, pair * 128:(pair + 1) * 128]
            qm = jnp.where((lane >= 64) == (sub == 1), qs, jnp.zeros_like(qs))

            def bias(c, hd=hd):
                if tile_idx is None or c >= NA_WIN // NA_CHUNK:
                    return None
                return jnp.concatenate(
                    [jnp.concatenate([bias_ref[hd, idx] for idx in row], axis=1) for row in tile_idx[c]], axis=0)

            units.append((qm, _k_slab(k_ref, pair), _v_slab(v_ref, pair), bias))
        o = [acc * (1.0 / l) for acc, l in _attend(units, chunks, shift)]
        outs = [jnp.where(lane < 64, o[0], o[1]), jnp.where(lane < 64, o[2], o[3])]
        o_ref[0] = jnp.concatenate(outs, axis=1).astype(o_ref.dtype)

    @pl.when(rb < NLAT)
    def _():
        win_row = jnp.clip(4 * rb - 4, 0, GRID_ROWS - NA_WIN // GRID_W)
        off = pl.multiple_of(win_row * GRID_W, RB)
        local = [(pl.multiple_of(off + c * NA_CHUNK, RB), NA_CHUNK) for c in range(NA_WIN // NA_CHUNK)]
        tile_idx = []
        for c in range(NA_WIN // NA_CHUNK):
            per_row = []
            for ri in range(RB // GRID_W):
                r = 4 * rb + ri
                row_start = jnp.clip(r - NA_ROWS // 2, 0, GRID_ROWS - NA_ROWS)
                idxs = []
                for p in range(NA_CHUNK // 128):
                    kr = win_row + c * (NA_CHUNK // GRID_W) + 2 * p
                    ok1 = (kr >= row_start) & (kr < row_start + NA_ROWS)
                    ok2 = (kr + 1 >= row_start) & (kr + 1 < row_start + NA_ROWS)
                    e = jnp.clip(kr - r + NA_ROWS, 0, 15)
                    idxs.append(jnp.where(ok1 & ok2, e, jnp.where(ok1, 16 + e, jnp.where(ok2, 32 + e, 48))))
                per_row.append(idxs)
            tile_idx.append(per_row)
        _with_shift(bnd_ref, lambda shift: run(local + list(CTX_CHUNKS), tile_idx, shift))

    @pl.when(rb >= NLAT)
    def _():
        _with_shift(bnd_ref, lambda shift: run(CTX_CHUNKS, None, shift))


SHIFT_MAX = 40.0


def _logit_bound(d, q_gain, k_gain, bias_max=0.0):
    bound = 1.01 * math.sqrt(d) * LOG2E * jnp.max(jnp.abs(q_gain)) * jnp.max(jnp.abs(k_gain)) + bias_max
    return jnp.stack([bound, (bound <= SHIFT_MAX).astype(F32)]).astype(F32)


def _attention(body, name, q, k, v, bound, extra, n_rb, out_width=256):
    n_b = q.shape[0]
    extra = [(bound, pl.BlockSpec(memory_space=pltpu.SMEM))] + list(extra)
    in_specs = [
        pl.BlockSpec((1, RB, q.shape[2]), lambda b, r: (b, r, 0)),
        pl.BlockSpec((1, k.shape[1], T), lambda b, r: (b, 0, 0)),
        pl.BlockSpec((1, T, v.shape[2]), lambda b, r: (b, 0, 0)),
    ] + [spec for _, spec in extra]
    return pl.pallas_call(
        body,
        grid=(n_b, n_rb),
        in_specs=in_specs,
        out_specs=pl.BlockSpec((1, RB, out_width), lambda b, r: (b, r, 0)),
        out_shape=jax.ShapeDtypeStruct((n_b, n_rb * RB, out_width), BF16),
        compiler_params=_params(("parallel", "parallel")),
        name=name,
    )(q, k, v, *[a for a, _ in extra])


def _outproj_kernel(x_ref, modb_ref, modc_ref, ma_ref, mb_ref, mc_ref, md_ref, w_ref, o_ref, *, tm):
    rows = pl.program_id(1) * tm + lax.broadcasted_iota(jnp.int32, (tm, 1), 0)
    g1 = jnp.where(rows >= SEQ, modc_ref[0][:, 2 * D:3 * D], modb_ref[0][:, 2 * D:3 * D])
    acc = _dot(ma_ref[0], w_ref[0:256, :])
    acc += _dot(mb_ref[0], w_ref[256:512, :])
    acc += _dot(mc_ref[0], w_ref[512:768, :])
    acc += _dot(md_ref[0], w_ref[768:1024, :])
    o_ref[0] = x_ref[0] + g1 * acc


def _outproj(xs, mods, mixes, w_out, rows_per_sample, tm):
    n_b = xs.shape[0]
    mix_spec = pl.BlockSpec((1, tm, 256), lambda b, r: (b, r, 0))
    return pl.pallas_call(
        functools.partial(_outproj_kernel, tm=tm),
        grid=(n_b, rows_per_sample // tm),
        in_specs=[
            pl.BlockSpec((1, tm, D), lambda b, r: (b, r, 0)),
            pl.BlockSpec((1, 1, 6 * D), lambda b, r: (b, 0, 0)),
            pl.BlockSpec((1, 1, 6 * D), lambda b, r: (n_b, 0, 0)),
            mix_spec, mix_spec, mix_spec, mix_spec,
            pl.BlockSpec((D, D), lambda b, r: (0, 0)),
        ],
        out_specs=pl.BlockSpec((1, tm, D), lambda b, r: (b, r, 0)),
        out_shape=jax.ShapeDtypeStruct((n_b, rows_per_sample, D), F32),
        compiler_params=_params(("parallel", "parallel")),
        name="outproj",
    )(xs, mods, mods, *mixes, w_out)


def _ffn_kernel(x_ref, hb_ref, ha_ref, modb_ref, modc_ref, g_ref, wu_ref, cw_ref, wd_ref, o_ref,
                h_scr, u0_scr, u1_scr, act_scr, *, tm, rows_per_sample):
    rb = pl.program_id(1)
    j = pl.program_id(2)
    row0 = rb * tm
    split = SEQ % tm if rows_per_sample > SEQ else 0
    band_lo, band_hi = split - HALO, split + HALO

    def per_row(rows, col):
        lat = modb_ref[0][:, col * D:(col + 1) * D]
        if rows_per_sample == SEQ:
            return lat
        return jnp.where(rows >= SEQ, modc_ref[0][:, col * D:(col + 1) * D], lat)

    def modulated(x, rows):
        ms = jnp.mean(x * x, axis=-1, keepdims=True)
        y = x * lax.rsqrt(ms + EPS) * g_ref[...]
        return (y * (1.0 + per_row(rows, 4)) + per_row(rows, 3)).astype(BF16)

    def prologue():
        halo_rows = lax.broadcasted_iota(jnp.int32, (HALO, 1), 0)
        has_before = (row0 != 0) & (row0 != SEQ)
        has_after = (row0 + tm != SEQ) & (row0 + tm != rows_per_sample)
        zero = jnp.zeros((HALO, D), BF16)
        h_scr[0:HALO, :] = jnp.where(has_before, modulated(hb_ref[0], row0 - HALO + halo_rows), zero)
        h_scr[HALO + tm:, :] = jnp.where(has_after, modulated(ha_ref[0], row0 + tm + halo_rows), zero)
        main_rows = row0 + lax.broadcasted_iota(jnp.int32, (tm, 1), 0)
        h_scr[HALO:HALO + tm, :] = modulated(x_ref[0], main_rows)
        o_ref[0] = jnp.zeros((tm, D), F32)

    slots = (u0_scr, u1_scr)

    def tiles(total):
        units = total // HALO
        sizes = [(units // FFN_TILES + (1 if t < units % FFN_TILES else 0)) * HALO for t in range(FFN_TILES)]
        edges = np.cumsum([0] + sizes)
        return [(int(edges[t]), int(edges[t + 1])) for t in range(FFN_TILES)]

    up_tiles, down_tiles = tiles(tm + 2 * HALO), tiles(tm)

    def up(slot, t):
        lo, hi = up_tiles[t]
        slots[slot][lo:hi, :] = _dot(h_scr[lo:hi, :], wu_ref[...])

    def conv(u_scr, lo, hi, c_lo, c_hi, masked):
        prev = u_scr[HALO - 1 + lo:HALO - 1 + hi, c_lo:c_hi]
        nxt = u_scr[HALO + 1 + lo:HALO + 1 + hi, c_lo:c_hi]
        if masked:
            rows = row0 + lo + lax.broadcasted_iota(jnp.int32, (hi - lo, 1), 0)
            prev = jnp.where(rows != SEQ, prev, 0.0)
            nxt = jnp.where(rows != SEQ - 1, nxt, 0.0)
        return (prev * cw_ref[0:1, c_lo:c_hi] + u_scr[HALO + lo:HALO + hi, c_lo:c_hi] * cw_ref[1:2, c_lo:c_hi]
                + nxt * cw_ref[2:3, c_lo:c_hi] + cw_ref[3:4, c_lo:c_hi])

    def conv_down(slot, t):
        t_lo, t_hi = down_tiles[t]
        cuts = sorted({t_lo, t_hi} | ({c for c in (band_lo, band_hi) if t_lo < c < t_hi} if split else set()))
        for lo, hi in zip(cuts[:-1], cuts[1:]):
            masked = bool(split) and band_lo <= lo < band_hi
            for c_lo in range(0, FF_CHUNK, CONV_STRIP):
                a = conv(slots[slot], lo, hi, c_lo, c_lo + CONV_STRIP, masked)
                g = conv(slots[slot], lo, hi, FF_CHUNK + c_lo, FF_CHUNK + c_lo + CONV_STRIP, masked)
                act_scr[lo:hi, c_lo:c_lo + CONV_STRIP] = (g * jax.nn.sigmoid(g) * a).astype(BF16)
        o_ref[0, t_lo:t_hi, :] += _dot(act_scr[t_lo:t_hi, :], wd_ref[...])

    @pl.when(j == 0)
    def _():
        prologue()
        for t in range(FFN_TILES):
            up(0, t)

    for parity in range(2):
        @pl.when((j > 0) & (j < N_CHUNK) & (j % 2 == parity))
        def _():
            for t in range(FFN_TILES):
                up(parity, t)
                conv_down(1 - parity, t)

    @pl.when(j == N_CHUNK)
    def _():
        for t in range(FFN_TILES):
            conv_down((N_CHUNK - 1) % 2, t)
        rows = row0 + lax.broadcasted_iota(jnp.int32, (tm, 1), 0)
        o_ref[0] = x_ref[0] + per_row(rows, 5) * o_ref[0]


def _chunk_interleave(a):
    parts = []
    for j in range(N_CHUNK):
        parts += [a[..., j * FF_CHUNK:(j + 1) * FF_CHUNK], a[..., D_FF + j * FF_CHUNK:D_FF + (j + 1) * FF_CHUNK]]
    return jnp.concatenate(parts, axis=-1)


def _ffn(x1, mods, g_ffn, w_up, conv_wb, w_down, rows_per_sample, tm):
    n_b = x1.shape[0]
    n_rb = rows_per_sample // tm
    halo_per_block = tm // HALO
    n_halo = x1.shape[1] // HALO
    body = functools.partial(_ffn_kernel, tm=tm, rows_per_sample=rows_per_sample)

    def up_idx(j):
        return jnp.minimum(j, N_CHUNK - 1)

    def down_idx(j):
        return jnp.maximum(j - 1, 0)

    return pl.pallas_call(
        body,
        grid=(n_b, n_rb, N_CHUNK + 1),
        in_specs=[
            pl.BlockSpec((1, tm, D), lambda b, r, j: (b, r, 0)),
            pl.BlockSpec((1, HALO, D), lambda b, r, j: (b, jnp.maximum(r * halo_per_block - 1, 0), 0)),
            pl.BlockSpec((1, HALO, D), lambda b, r, j: (b, jnp.minimum((r + 1) * halo_per_block, n_halo - 1), 0)),
            pl.BlockSpec((1, 1, 6 * D), lambda b, r, j: (b, 0, 0)),
            pl.BlockSpec((1, 1, 6 * D), lambda b, r, j: (n_b, 0, 0)),
            pl.BlockSpec((1, D), lambda b, r, j: (0, 0)),
            pl.BlockSpec((D, 2 * FF_CHUNK), lambda b, r, j: (0, up_idx(j))),
            pl.BlockSpec((4, 2 * FF_CHUNK), lambda b, r, j: (0, down_idx(j))),
            pl.BlockSpec((FF_CHUNK, D), lambda b, r, j: (down_idx(j), 0)),
        ],
        out_specs=pl.BlockSpec((1, tm, D), lambda b, r, j: (b, r, 0)),
        out_shape=jax.ShapeDtypeStruct((n_b, rows_per_sample, D), F32),
        scratch_shapes=[
            pltpu.VMEM((tm + 2 * HALO, D), BF16),
            pltpu.VMEM((tm + 2 * HALO, 2 * FF_CHUNK), F32),
            pltpu.VMEM((tm + 2 * HALO, 2 * FF_CHUNK), F32),
            pltpu.VMEM((tm, FF_CHUNK), BF16),
        ],
        compiler_params=_params(("parallel", "parallel", "arbitrary")),
        name="conv_ffn",
    )(x1, x1, x1, mods, mods, g_ffn, w_up, conv_wb, w_down)


def _rope_table(rot_dim):
    n_freq = rot_dim // 4
    inv = jnp.power(ROPE_THETA, -jnp.arange(n_freq, dtype=F32) / n_freq)
    t = jnp.arange(SEQ)
    row = (t // GRID_W).astype(F32)
    col = (t % GRID_W).astype(F32)
    ar, ac = row[:, None] * inv, col[:, None] * inv
    ang = jnp.concatenate([ar, ar, ac, ac], axis=-1)
    sign = jnp.concatenate([-jnp.ones(n_freq), jnp.ones(n_freq), -jnp.ones(n_freq), jnp.ones(n_freq)]).astype(F32)
    cos = jnp.concatenate([jnp.cos(ang), jnp.ones((CTX, rot_dim), F32)], axis=0)
    sin = jnp.concatenate([jnp.sin(ang) * sign, jnp.zeros((CTX, rot_dim), F32)], axis=0)
    return jnp.concatenate([cos.T, sin.T], axis=0)


def _na_bias_tiles(rpb):
    n_l, n_h = rpb.shape[:2]
    n_off = 2 * NA_ROWS - 1
    p = jnp.pad(rpb.astype(F32) * LOG2E, ((0, 0), (0, 0), (0, 0), (48, 49)))
    sk = jnp.broadcast_to(p[..., None, :], (n_l, n_h, n_off, GRID_W, 128)).reshape(n_l, n_h, n_off, GRID_W * 128)
    sk = sk[..., :GRID_W * 127].reshape(n_l, n_h, n_off, GRID_W, 127)[..., 63:127]
    c = np.arange(GRID_W)
    win_start = np.clip(c - NA_COLS // 2, 0, GRID_W - NA_COLS)
    v_col = (c[None, :] >= win_start[:, None]) & (c[None, :] < win_start[:, None] + NA_COLS)
    tz = jnp.where(v_col, sk, NEG_INF)
    neg1 = jnp.full((n_l, n_h, 1, GRID_W, GRID_W), NEG_INF, F32)
    tzx = jnp.concatenate([neg1, tz, neg1], axis=2)
    first, second = tzx[:, :, 0:16], tzx[:, :, 1:17]
    neg16 = jnp.full_like(first, NEG_INF)
    return jnp.concatenate([
        jnp.concatenate([first, second], axis=-1), jnp.concatenate([first, neg16], axis=-1),
        jnp.concatenate([neg16, second], axis=-1), jnp.concatenate([neg1, neg1], axis=-1)], axis=2)


def _col(v):
    return v.reshape(-1, 1).astype(F32)


def kernel(x, c, ctx, c_ctx, w_mod, b_mod, g_mix, w_in, w_out, mla_q_a_g, mla_w_uq, mla_kv_a_g, mla_w_ukv,
           mla_q_g, mla_k_g, diff_q_g, diff_k_g, diff_lq1, diff_lk1, diff_lq2, diff_lk2, diff_subln_g,
           na_q_g, na_k_g, na_rpb, gqa_q_g, gqa_k_g, g_ffn, w_up, conv_w, conv_b, w_down):
    n_b = x.shape[0]
    n_layer = w_mod.shape[0]
    assert x.shape[1:] == (SEQ, D) and ctx.shape[1:] == (CTX, D)

    xs = jnp.concatenate([x, ctx], axis=1)
    mod_rows = -(-(n_b + 1) // 8) * 8
    cc = jnp.concatenate([c, c_ctx[None], jnp.zeros((mod_rows - n_b - 1, D), F32)], axis=0)
    mods_all = _modulation(cc, w_mod, b_mod).reshape(n_layer, mod_rows, 1, 6 * D)

    cs32, cs64 = _rope_table(32), _rope_table(64)
    bias_all = _na_bias_tiles(na_rpb)

    s = np.cumsum([0, 256, 128, 32, 256, 256, 256, 256, 256, 256, 256, 128, 128])
    gq_perm = np.concatenate([np.arange(GQA_D) + (2 * g + r) * GQA_D for r in range(2) for g in range(2)])

    for l in range(n_layer):
        with_ctx = l < n_layer - 1
        lambda_init = 0.8 - 0.6 * math.exp(-0.3 * l)
        mods = mods_all[l]
        w = w_in[l]
        gq_cols = w[:, s[9]:s[10]][:, gq_perm]
        wqk = jnp.concatenate([w[:, s[0]:s[3]], w[:, s[3]:s[5]], w[:, s[6]:s[8]], gq_cols, w[:, s[10]:s[11]]],
                              axis=1).T.astype(BF16)
        wv = jnp.concatenate([w[:, s[5]:s[6]], w[:, s[8]:s[9]], w[:, s[11]:s[12]]], axis=1).astype(BF16)
        wuq = mla_w_uq[l].T.astype(BF16)
        wukv = mla_w_ukv[l].reshape(MLA_KVR, MLA_H, MLA_NOPE + MLA_V)
        wukn = wukv[:, :, :MLA_NOPE].reshape(MLA_KVR, MLA_H * MLA_NOPE).T.astype(BF16)
        wuv = wukv[:, :, MLA_NOPE:].reshape(MLA_KVR, MLA_H * MLA_V).T.astype(BF16)
        gcol = jnp.concatenate([_col(mla_q_a_g[l]), _col(mla_kv_a_g[l]), _col(mla_q_g[l]), _col(mla_k_g[l]),
                                _col(diff_q_g[l]), _col(diff_k_g[l]), _col(na_q_g[l]), _col(na_k_g[l]),
                                _col(gqa_q_g[l]), _col(gqa_k_g[l])], axis=0)
        (q_mla, k_mla, v_mla, q_diff, k_diff, v_diff, q_na, k_na, v_na, q_gqa, k_gqa, v_gqa) = _inproj(
            xs, mods, g_mix[l].reshape(1, D), wqk, wv, wuq, wukn, wuv, gcol, cs32, cs64)

        n_rb = NRB if with_ctx else NLAT
        mix_a = _attention(_mla_kernel, "attn_mla", q_mla, k_mla, v_mla,
                           _logit_bound(MLA_D, mla_q_g[l], mla_k_g[l]), [], n_rb)
        lam_rows = jnp.stack([diff_lq1[l], diff_lk1[l], diff_lq2[l], diff_lk2[l]]).astype(F32)
        lam_rows = jnp.concatenate([lam_rows, jnp.zeros((4, DIFF_D), F32)], axis=0)
        gsub = jnp.tile(diff_subln_g[l].astype(F32), 2).reshape(1, 128)
        mix_b = _attention(
            functools.partial(_diff_kernel, lambda_init=lambda_init), "attn_diff",
            q_diff, k_diff, v_diff, _logit_bound(DIFF_D, diff_q_g[l], diff_k_g[l]),
            [(lam_rows, pl.BlockSpec((8, DIFF_D), lambda b, r: (0, 0))),
             (gsub, pl.BlockSpec((1, 128), lambda b, r: (0, 0)))], n_rb)
        bias = bias_all[l]
        bias_spec = pl.BlockSpec((NA_H, NA_TILES, GRID_W, 128), lambda b, r: (0, 0, 0, 0))
        na_bound = _logit_bound(NA_D, na_q_g[l], na_k_g[l], LOG2E * jnp.maximum(jnp.max(na_rpb[l]), 0.0))
        mix_c = _attention(_na_kernel, "attn_na", q_na, k_na, v_na, na_bound, [(bias, bias_spec)], n_rb)
        mix_d = _attention(_gqa_kernel, "attn_gqa", q_gqa, k_gqa, v_gqa,
                           _logit_bound(GQA_D, gqa_q_g[l], gqa_k_g[l]), [], n_rb)

        rows_per_sample = T if with_ctx else SEQ
        x1 = _outproj(xs, mods, (mix_a, mix_b, mix_c, mix_d), w_out[l].astype(BF16), rows_per_sample,
                      OUT_TM if with_ctx else OUT_TM_LAST)
        conv_wb = jnp.concatenate([conv_w[l], conv_b[l].reshape(1, 2 * D_FF)], axis=0).astype(F32)
        xs = _ffn(x1, mods, g_ffn[l].reshape(1, D), _chunk_interleave(w_up[l].astype(BF16)),
                  _chunk_interleave(conv_wb), w_down[l].astype(BF16), rows_per_sample, rows_per_sample // 2)
    return xs
```

```python
import functools
import math

import numpy as np
import jax
import jax.numpy as jnp
from jax import lax
from jax.experimental import pallas as pl
from jax.experimental.pallas import tpu as pltpu

F32 = jnp.float32
BF16 = jnp.bfloat16

D = 1024
SEQ = 2048
GRID_W = 64
CTX = 256
T = SEQ + CTX
RB = 256
NRB = T // RB
NLAT = SEQ // RB
EPS = 1e-6
NEG_INF = -1e30
ROPE_THETA = 10000.0
LOG2E = 1.4426950408889634

MLA_H, MLA_NOPE, MLA_ROPE, MLA_V, MLA_QR, MLA_KVR = 4, 64, 32, 64, 256, 128
MLA_D = MLA_NOPE + MLA_ROPE
DIFF_H, DIFF_D = 4, 32
NA_H, NA_D, NA_ROWS, NA_COLS = 4, 64, 8, 16
GQA_H, GQA_KV, GQA_D = 4, 2, 64
D_FF = 2816
FF_CHUNK = 256
N_CHUNK = D_FF // FF_CHUNK
OUT_TM, OUT_TM_LAST = 768, 1024
HALO = 16
FFN_TILES = 4
CONV_STRIP = 256

R_CQ, R_CKV, R_KR, R_DQ, R_DK, R_NQ, R_NK, R_GQ, R_GK = 0, 256, 384, 416, 672, 928, 1184, 1440, 1696
QK_ROWS = 1824
V_COLS = 640

G_QA, G_KVA, G_MQ, G_MK, G_DQ, G_DK, G_NQ, G_NK, G_GQ, G_GK = 0, 256, 384, 480, 576, 608, 640, 704, 768, 832
G_ROWS = 896

VMEM_LIMIT = 56 * 1024 * 1024


def _params(sem):
    return pltpu.CompilerParams(dimension_semantics=sem, vmem_limit_bytes=VMEM_LIMIT)


def _dot(a, b):
    return jnp.dot(a, b, preferred_element_type=F32)


def _dot_nt(a, b):
    return lax.dot_general(a, b, (((1,), (1,)), ((), ())), preferred_element_type=F32)


def _mod_kernel(c_ref, w_ref, b_ref, o_ref):
    c = c_ref[...]
    a = (c * jax.nn.sigmoid(c)).astype(BF16)
    o_ref[0] = _dot(a, w_ref[0].astype(BF16)) + b_ref[0]


def _modulation(cc, w_mod, b_mod):
    n_layer = w_mod.shape[0]
    rows = cc.shape[0]
    return pl.pallas_call(
        _mod_kernel,
        grid=(n_layer, 6),
        in_specs=[
            pl.BlockSpec((rows, D), lambda l, j: (0, 0)),
            pl.BlockSpec((1, D, D), lambda l, j: (l, 0, j)),
            pl.BlockSpec((1, 1, D), lambda l, j: (l, 0, j)),
        ],
        out_specs=pl.BlockSpec((1, rows, D), lambda l, j: (l, 0, j)),
        out_shape=jax.ShapeDtypeStruct((n_layer, rows, 6 * D), F32),
        compiler_params=_params(("parallel", "parallel")),
        name="modulation",
    )(cc, w_mod, b_mod.reshape(n_layer, 1, 6 * D))


def _rms_rows(x, g, n):
    ss = jnp.sum(x * x, axis=0, keepdims=True) * (1.0 / n)
    return x * lax.rsqrt(ss + EPS) * g


def _rope_rows(x, cos, sin_signed, w):
    rot = jnp.concatenate([x[w:2 * w], x[0:w], x[3 * w:4 * w], x[2 * w:3 * w]], axis=0)
    return x * cos + rot * sin_signed


def _inproj_kernel(x_ref, mod_ref, gmix_ref, wqk_ref, wv_ref, wuq_ref, wukn_ref, wuv_ref, gcol_ref,
                   cs32_ref, cs64_ref,
                   qmla_ref, kmla_ref, vmla_ref, qdiff_ref, kdiff_ref, vdiff_ref,
                   qna_ref, kna_ref, vna_ref, qgqa_ref, kgqa_ref, vgqa_ref):
    x = x_ref[0]
    mod = mod_ref[0]
    shift, scale = mod[:, 0:D], mod[:, D:2 * D]
    ms = jnp.mean(x * x, axis=-1, keepdims=True)
    h = x * lax.rsqrt(ms + EPS) * gmix_ref[...]
    hb = (h * (1.0 + scale) + shift).astype(BF16)

    def project(lo, hi):
        return _dot_nt(wqk_ref[lo:hi, :], hb)

    def gain(off, n):
        return gcol_ref[off:off + n, :]

    cos32, sin32 = cs32_ref[0:32, :], cs32_ref[32:64, :]
    cos64, sin64 = cs64_ref[0:64, :], cs64_ref[64:128, :]

    def values():
        pv = _dot(hb, wv_ref[...])
        vdiff_ref[0] = pv[:, 0:256].astype(BF16)
        vna_ref[0] = pv[:, 256:512].astype(BF16)
        vgqa_ref[0] = pv[:, 512:640].astype(BF16)

    def mla(pt):
        cq = _rms_rows(pt[0:MLA_QR], gain(G_QA, MLA_QR), MLA_QR).astype(BF16)
        qt = _dot(wuq_ref[...], cq)
        ckv = _rms_rows(pt[R_CKV:R_CKV + MLA_KVR], gain(G_KVA, MLA_KVR), MLA_KVR).astype(BF16)
        knt = _dot(wukn_ref[...], ckv)
        vt = _dot(wuv_ref[...], ckv)
        vmla_ref[0] = vt.T.astype(BF16)
        kr = pt[R_KR:R_KR + MLA_ROPE]
        kr_ss = jnp.sum(kr * kr, axis=0, keepdims=True)
        g_mq, g_mk = gain(G_MQ, MLA_D), gain(G_MK, MLA_D)
        zpad = jnp.zeros((128 - MLA_D, RB), F32)
        q_parts = []
        for hd in range(MLA_H):
            qh = _rms_rows(qt[hd * MLA_D:(hd + 1) * MLA_D], g_mq, MLA_D)
            q_rope = _rope_rows(qh[MLA_NOPE:], cos32, sin32, MLA_ROPE // 4)
            q_parts += [qh[:MLA_NOPE], q_rope, zpad]
            kn = knt[hd * MLA_NOPE:(hd + 1) * MLA_NOPE]
            ss = (jnp.sum(kn * kn, axis=0, keepdims=True) + kr_ss) * (1.0 / MLA_D)
            r = lax.rsqrt(ss + EPS)
            k_rope = _rope_rows(kr * r * g_mk[MLA_NOPE:], cos32, sin32, MLA_ROPE // 4)
            kmla_ref[0, hd * 128:(hd + 1) * 128, :] = jnp.concatenate(
                [kn * r * g_mk[:MLA_NOPE], k_rope, zpad], axis=0).astype(BF16)
        q_all = jnp.concatenate(q_parts, axis=0) * (MLA_D ** -0.5 * LOG2E)
        qmla_ref[0] = q_all.T.astype(BF16)

    def diff(pt):
        g_dq, g_dk = gain(G_DQ, DIFF_D), gain(G_DK, DIFF_D)
        q_parts, k_parts = [], []
        for gi in range(2 * DIFF_H):
            qg = _rms_rows(pt[gi * DIFF_D:(gi + 1) * DIFF_D], g_dq, DIFF_D)
            q_parts.append(_rope_rows(qg, cos32, sin32, DIFF_D // 4))
            kg = _rms_rows(pt[256 + gi * DIFF_D:256 + (gi + 1) * DIFF_D], g_dk, DIFF_D)
            k_parts.append(_rope_rows(kg, cos32, sin32, DIFF_D // 4))
        qdiff_ref[0] = (jnp.concatenate(q_parts, axis=0) * (DIFF_D ** -0.5 * LOG2E)).T.astype(BF16)
        kdiff_ref[0] = jnp.concatenate(k_parts, axis=0).astype(BF16)

    def na(pt):
        g_nq, g_nk = gain(G_NQ, NA_D), gain(G_NK, NA_D)
        q_parts, k_parts = [], []
        for hd in range(NA_H):
            q_parts.append(_rms_rows(pt[hd * NA_D:(hd + 1) * NA_D], g_nq, NA_D))
            k_parts.append(_rms_rows(pt[256 + hd * NA_D:256 + (hd + 1) * NA_D], g_nk, NA_D))
        qna_ref[0] = (jnp.concatenate(q_parts, axis=0) * (NA_D ** -0.5 * LOG2E)).T.astype(BF16)
        kna_ref[0] = jnp.concatenate(k_parts, axis=0).astype(BF16)

    def gqa(pt):
        g_gq, g_gk = gain(G_GQ, GQA_D), gain(G_GK, GQA_D)
        q_parts, k_parts = [], []
        for hd in range(GQA_H):
            qg = _rms_rows(pt[hd * GQA_D:(hd + 1) * GQA_D], g_gq, GQA_D)
            q_parts.append(_rope_rows(qg, cos64, sin64, GQA_D // 4))
        for hd in range(GQA_KV):
            kg = _rms_rows(pt[256 + hd * GQA_D:256 + (hd + 1) * GQA_D], g_gk, GQA_D)
            k_parts.append(_rope_rows(kg, cos64, sin64, GQA_D // 4))
        qgqa_ref[0] = (jnp.concatenate(q_parts, axis=0) * (GQA_D ** -0.5 * LOG2E)).T.astype(BF16)
        kgqa_ref[0] = jnp.concatenate(k_parts, axis=0).astype(BF16)

    pt_mla = project(R_CQ, R_DQ)
    pt_diff = project(R_DQ, R_NQ)
    mla(pt_mla)
    pt_na = project(R_NQ, R_GQ)
    diff(pt_diff)
    pt_gqa = project(R_GQ, QK_ROWS)
    na(pt_na)
    values()
    gqa(pt_gqa)


def _inproj(l, xs, mods, gmix, wqk, wv, wuq, wukn, wuv, gcol, cs32, cs64):
    n_b = xs.shape[0]

    def full(a):
        return _layer_spec(a, l)

    def tok(width):
        return pl.BlockSpec((1, RB, width), lambda r, b: (b, r, 0))

    def chan(rows):
        return pl.BlockSpec((1, rows, RB), lambda r, b: (b, 0, r))

    def tshape(width):
        return jax.ShapeDtypeStruct((n_b, T, width), BF16)

    def cshape(rows):
        return jax.ShapeDtypeStruct((n_b, rows, T), BF16)

    return pl.pallas_call(
        _inproj_kernel,
        grid=(NRB, n_b),
        in_specs=[
            pl.BlockSpec((1, RB, D), lambda r, b: (b, r, 0)),
            pl.BlockSpec((None, 1, 1, 6 * D), lambda r, b: (l, jnp.where(r < NLAT, b, n_b), 0, 0)),
            full(gmix), full(wqk), full(wv), full(wuq), full(wukn), full(wuv), full(gcol),
            pl.BlockSpec((64, RB), lambda r, b: (0, r)),
            pl.BlockSpec((128, RB), lambda r, b: (0, r)),
        ],
        out_specs=[tok(512), chan(512), tok(256), tok(256), chan(256), tok(256),
                   tok(256), chan(256), tok(256), tok(256), chan(128), tok(128)],
        out_shape=[tshape(512), cshape(512), tshape(256), tshape(256), cshape(256), tshape(256),
                   tshape(256), cshape(256), tshape(256), tshape(256), cshape(128), tshape(128)],
        compiler_params=_params(("parallel", "parallel")),
        name="inproj_prep",
    )(xs, mods, gmix, wqk, wv, wuq, wukn, wuv, gcol, cs32, cs64)


def _lane_id(shape):
    return lax.broadcasted_iota(jnp.int32, shape, 1)


KEY_CHUNK = 768
LAT_CHUNKS = tuple((lo, KEY_CHUNK) for lo in range(0, T, KEY_CHUNK))
CTX_CHUNKS = ((SEQ, CTX),)


def _k_slab(k_ref, idx):
    return lambda lo, n: k_ref[0, idx * 128:(idx + 1) * 128, pl.ds(lo, n)]


def _v_slab(v_ref, idx):
    return lambda lo, n: v_ref[0, pl.ds(lo, n), idx * 128:(idx + 1) * 128]


def _attend(units, chunks, shift=None):
    def scores(i, c):
        q, keys, _, bias = units[i]
        lo, n = chunks[c]
        s = _dot(q, keys(lo, n))
        b = None if bias is None else bias(c)
        return s if b is None else s + b

    if shift is not None:
        items = [(i, c) for i in range(len(units)) for c in range(len(chunks))]
        acc, l = [None] * len(units), [None] * len(units)
        nxt = scores(*items[0])
        for idx, (i, c) in enumerate(items):
            cur = nxt
            if idx + 1 < len(items):
                nxt = scores(*items[idx + 1])
            e = jnp.exp2(cur - shift)
            lc = jnp.sum(e, axis=-1, keepdims=True)
            oc = _dot(e.astype(BF16), units[i][2](*chunks[c]))
            acc[i] = oc if acc[i] is None else acc[i] + oc
            l[i] = lc if l[i] is None else l[i] + lc
        return list(zip(acc, l))

    cur = [scores(0, c) for c in range(len(chunks))]
    out = []
    for i in range(len(units)):
        m = functools.reduce(jnp.maximum, [jnp.max(s, axis=-1, keepdims=True) for s in cur])
        nxt, acc, l = [], None, None
        for c, (lo, n) in enumerate(chunks):
            if i + 1 < len(units):
                nxt.append(scores(i + 1, c))
            e = jnp.exp2(cur[c] - m)
            lc = jnp.sum(e, axis=-1, keepdims=True)
            oc = _dot(e.astype(BF16), units[i][2](lo, n))
            acc = oc if acc is None else acc + oc
            l = lc if l is None else l + lc
        out.append((acc, l))
        cur = nxt
    return out


def _with_shift(bnd_ref, run):
    @pl.when(bnd_ref[1] > 0.5)
    def _():
        run(bnd_ref[0])

    @pl.when(bnd_ref[1] <= 0.5)
    def _():
        run(None)


def _by_query_kind(bnd_ref, run):
    rb = pl.program_id(1)

    @pl.when(rb < NLAT)
    def _():
        _with_shift(bnd_ref, lambda shift: run(LAT_CHUNKS, shift))

    @pl.when(rb >= NLAT)
    def _():
        _with_shift(bnd_ref, lambda shift: run(CTX_CHUNKS, shift))


def _mla_kernel(q_ref, k_ref, v_ref, bnd_ref, o_ref):
    lane = _lane_id((RB, 128))

    def run(chunks, shift):
        units = [(q_ref[0, :, hd * 128:(hd + 1) * 128], _k_slab(k_ref, hd), _v_slab(v_ref, hd // 2), None)
                 for hd in range(MLA_H)]
        o = [acc * (1.0 / l) for acc, l in _attend(units, chunks, shift)]
        outs = [jnp.where(lane < 64, o[0], o[1]), jnp.where(lane < 64, o[2], o[3])]
        o_ref[0] = jnp.concatenate(outs, axis=1).astype(o_ref.dtype)

    _by_query_kind(bnd_ref, run)


def _gqa_kernel(q_ref, k_ref, v_ref, bnd_ref, o_ref):
    lane = _lane_id((RB, 128))

    def run(chunks, shift):
        order = [(rep, grp) for rep in range(2) for grp in range(2)]
        units = []
        for rep, grp in order:
            qs = q_ref[0, :, rep * 128:(rep + 1) * 128]
            qm = jnp.where((lane >= 64) == (grp == 1), qs, jnp.zeros_like(qs))
            units.append((qm, _k_slab(k_ref, 0), _v_slab(v_ref, 0), None))
        res = {}
        for (rep, grp), (acc, l) in zip(order, _attend(units, chunks, shift)):
            o = acc * (1.0 / l)
            res[(grp, rep)] = o if grp == rep else pltpu.roll(o, 64, axis=1)
        outs = [jnp.where(lane < 64, res[(grp, 0)], res[(grp, 1)]) for grp in range(2)]
        o_ref[0] = jnp.concatenate(outs, axis=1).astype(o_ref.dtype)

    _by_query_kind(bnd_ref, run)


def _diff_kernel(q_ref, k_ref, v_ref, bnd_ref, lam_ref, g_ref, o_ref, *, lambda_init):
    lane = _lane_id((RB, 128))
    lq1, lk1, lq2, lk2 = lam_ref[0:1, :], lam_ref[1:2, :], lam_ref[2:3, :], lam_ref[3:4, :]
    lam = (jnp.exp(jnp.sum(lq1 * lk1, axis=-1, keepdims=True))
           - jnp.exp(jnp.sum(lq2 * lk2, axis=-1, keepdims=True)) + lambda_init)
    gsub = g_ref[...]
    grp = lane // DIFF_D

    def run(chunks, shift):
        units = []
        for hd in range(DIFF_H):
            pair, sub = divmod(hd, 2)
            qs = q_ref[0, :, pair * 128:(pair + 1) * 128]
            for which in range(2):
                qm = jnp.where(grp == 2 * sub + which, qs, jnp.zeros_like(qs))
                units.append((qm, _k_slab(k_ref, pair), _v_slab(v_ref, pair), None))
        res = _attend(units, chunks, shift)
        outs = []
        for pair in range(2):
            halves = []
            for sub in range(2):
                (a1, l1), (a2, l2) = res[2 * (2 * pair + sub)], res[2 * (2 * pair + sub) + 1]
                halves.append(a1 * (1.0 / l1) - a2 * (lam / l2))
            o = jnp.where(lane < 64, halves[0], halves[1])
            o2 = o * o
            ss0 = jnp.sum(jnp.where(lane < 64, o2, 0.0), axis=-1, keepdims=True)
            ss1 = jnp.sum(jnp.where(lane < 64, 0.0, o2), axis=-1, keepdims=True)
            ss = jnp.where(lane < 64, ss0, ss1) * (1.0 / (2 * DIFF_D))
            outs.append(o * lax.rsqrt(ss + EPS) * gsub * (1.0 - lambda_init))
        o_ref[0] = jnp.concatenate(outs, axis=1).astype(o_ref.dtype)

    _by_query_kind(bnd_ref, run)


NA_WIN = 1024
NA_CHUNK = 512
NA_TILES = 49
GRID_ROWS = SEQ // GRID_W


def _na_kernel(q_ref, k_ref, v_ref, bnd_ref, bias_ref, o_ref):
    lane = _lane_id((RB, 128))
    rb = pl.program_id(1)

    def run(chunks, tile_idx, shift):
        units = []
        for hd in range(NA_H):
            pair, sub = divmod(hd, 2)
            qs = q_ref[0, :, pair * 128:(pair + 1) * 128]
            qm = jnp.where((lane >= 64) == (sub == 1), qs, jnp.zeros_like(qs))

            def bias(c, hd=hd):
                if tile_idx is None or c >= NA_WIN // NA_CHUNK:
                    return None
                return jnp.concatenate(
                    [jnp.concatenate([bias_ref[hd, idx] for idx in row], axis=1) for row in tile_idx[c]], axis=0)

            units.append((qm, _k_slab(k_ref, pair), _v_slab(v_ref, pair), bias))
        o = [acc * (1.0 / l) for acc, l in _attend(units, chunks, shift)]
        outs = [jnp.where(lane < 64, o[0], o[1]), jnp.where(lane < 64, o[2], o[3])]
        o_ref[0] = jnp.concatenate(outs, axis=1).astype(o_ref.dtype)

    @pl.when(rb < NLAT)
    def _():
        win_row = jnp.clip(4 * rb - 4, 0, GRID_ROWS - NA_WIN // GRID_W)
        off = pl.multiple_of(win_row * GRID_W, RB)
        local = [(pl.multiple_of(off + c * NA_CHUNK, RB), NA_CHUNK) for c in range(NA_WIN // NA_CHUNK)]
        tile_idx = []
        for c in range(NA_WIN // NA_CHUNK):
            per_row = []
            for ri in range(RB // GRID_W):
                r = 4 * rb + ri
                row_start = jnp.clip(r - NA_ROWS // 2, 0, GRID_ROWS - NA_ROWS)
                idxs = []
                for p in range(NA_CHUNK // 128):
                    kr = win_row + c * (NA_CHUNK // GRID_W) + 2 * p
                    ok1 = (kr >= row_start) & (kr < row_start + NA_ROWS)
                    ok2 = (kr + 1 >= row_start) & (kr + 1 < row_start + NA_ROWS)
                    e = jnp.clip(kr - r + NA_ROWS, 0, 15)
                    idxs.append(jnp.where(ok1 & ok2, e, jnp.where(ok1, 16 + e, jnp.where(ok2, 32 + e, 48))))
                per_row.append(idxs)
            tile_idx.append(per_row)
        _with_shift(bnd_ref, lambda shift: run(local + list(CTX_CHUNKS), tile_idx, shift))

    @pl.when(rb >= NLAT)
    def _():
        _with_shift(bnd_ref, lambda shift: run(CTX_CHUNKS, None, shift))


SHIFT_MAX = 40.0


def _logit_bound(d, q_gain, k_gain, bias_max=0.0):
    bound = (1.01 * math.sqrt(d) * LOG2E * jnp.max(jnp.abs(q_gain), axis=-1) * jnp.max(jnp.abs(k_gain), axis=-1)
             + bias_max)
    return jnp.stack([bound, (bound <= SHIFT_MAX).astype(F32)], axis=-1).astype(F32)


def _layer_spec(a, l):
    return pl.BlockSpec((None,) + a.shape[1:], lambda *_, _n=a.ndim - 1: (l,) + (0,) * _n)


def _attention(body, name, q, k, v, bound, extra, n_rb, out_width=256):
    n_b = q.shape[0]
    extra = [(bound, pl.BlockSpec(memory_space=pltpu.SMEM))] + list(extra)
    in_specs = [
        pl.BlockSpec((1, RB, q.shape[2]), lambda b, r: (b, r, 0)),
        pl.BlockSpec((1, k.shape[1], T), lambda b, r: (b, 0, 0)),
        pl.BlockSpec((1, T, v.shape[2]), lambda b, r: (b, 0, 0)),
    ] + [spec for _, spec in extra]
    return pl.pallas_call(
        body,
        grid=(n_b, n_rb),
        in_specs=in_specs,
        out_specs=pl.BlockSpec((1, RB, out_width), lambda b, r: (b, r, 0)),
        out_shape=jax.ShapeDtypeStruct((n_b, n_rb * RB, out_width), BF16),
        compiler_params=_params(("parallel", "parallel")),
        name=name,
    )(q, k, v, *[a for a, _ in extra])


def _outproj_kernel(x_ref, modb_ref, modc_ref, ma_ref, mb_ref, mc_ref, md_ref, w_ref, o_ref, *, tm):
    rows = pl.program_id(1) * tm + lax.broadcasted_iota(jnp.int32, (tm, 1), 0)
    g1 = jnp.where(rows >= SEQ, modc_ref[0][:, 2 * D:3 * D], modb_ref[0][:, 2 * D:3 * D])
    acc = _dot(ma_ref[0], w_ref[0:256, :])
    acc += _dot(mb_ref[0], w_ref[256:512, :])
    acc += _dot(mc_ref[0], w_ref[512:768, :])
    acc += _dot(md_ref[0], w_ref[768:1024, :])
    o_ref[0] = x_ref[0] + g1 * acc


def _outproj(l, xs, mods, mixes, w_out, rows_per_sample, tm):
    n_b = xs.shape[0]
    mix_spec = pl.BlockSpec((1, tm, 256), lambda b, r: (b, r, 0))
    return pl.pallas_call(
        functools.partial(_outproj_kernel, tm=tm),
        grid=(n_b, rows_per_sample // tm),
        in_specs=[
            pl.BlockSpec((1, tm, D), lambda b, r: (b, r, 0)),
            pl.BlockSpec((None, 1, 1, 6 * D), lambda b, r: (l, b, 0, 0)),
            pl.BlockSpec((None, 1, 1, 6 * D), lambda b, r: (l, n_b, 0, 0)),
            mix_spec, mix_spec, mix_spec, mix_spec,
            _layer_spec(w_out, l),
        ],
        out_specs=pl.BlockSpec((1, tm, D), lambda b, r: (b, r, 0)),
        out_shape=jax.ShapeDtypeStruct((n_b, rows_per_sample, D), F32),
        compiler_params=_params(("parallel", "parallel")),
        name="outproj",
    )(xs, mods, mods, *mixes, w_out)


def _ffn_kernel(x_ref, hb_ref, ha_ref, modb_ref, modc_ref, g_ref, wu_ref, cw_ref, wd_ref, o_ref,
                h_scr, u0_scr, u1_scr, act_scr, *, tm, rows_per_sample):
    rb = pl.program_id(1)
    j = pl.program_id(2)
    row0 = rb * tm
    split = SEQ % tm if rows_per_sample > SEQ else 0
    band_lo, band_hi = split - HALO, split + HALO

    def per_row(rows, col):
        lat = modb_ref[0][:, col * D:(col + 1) * D]
        if rows_per_sample == SEQ:
            return lat
        return jnp.where(rows >= SEQ, modc_ref[0][:, col * D:(col + 1) * D], lat)

    def modulated(x, rows):
        ms = jnp.mean(x * x, axis=-1, keepdims=True)
        y = x * lax.rsqrt(ms + EPS) * g_ref[...]
        return (y * (1.0 + per_row(rows, 4)) + per_row(rows, 3)).astype(BF16)

    def prologue():
        halo_rows = lax.broadcasted_iota(jnp.int32, (HALO, 1), 0)
        has_before = (row0 != 0) & (row0 != SEQ)
        has_after = (row0 + tm != SEQ) & (row0 + tm != rows_per_sample)
        zero = jnp.zeros((HALO, D), BF16)
        h_scr[0:HALO, :] = jnp.where(has_before, modulated(hb_ref[0], row0 - HALO + halo_rows), zero)
        h_scr[HALO + tm:, :] = jnp.where(has_after, modulated(ha_ref[0], row0 + tm + halo_rows), zero)
        main_rows = row0 + lax.broadcasted_iota(jnp.int32, (tm, 1), 0)
        h_scr[HALO:HALO + tm, :] = modulated(x_ref[0], main_rows)
        o_ref[0] = jnp.zeros((tm, D), F32)

    slots = (u0_scr, u1_scr)

    def tiles(total):
        units = total // HALO
        sizes = [(units // FFN_TILES + (1 if t < units % FFN_TILES else 0)) * HALO for t in range(FFN_TILES)]
        edges = np.cumsum([0] + sizes)
        return [(int(edges[t]), int(edges[t + 1])) for t in range(FFN_TILES)]

    up_tiles, down_tiles = tiles(tm + 2 * HALO), tiles(tm)

    def up(slot, t):
        lo, hi = up_tiles[t]
        slots[slot][lo:hi, :] = _dot(h_scr[lo:hi, :], wu_ref[...])

    def conv(u_scr, lo, hi, c_lo, c_hi, masked):
        prev = u_scr[HALO - 1 + lo:HALO - 1 + hi, c_lo:c_hi]
        nxt = u_scr[HALO + 1 + lo:HALO + 1 + hi, c_lo:c_hi]
        if masked:
            rows = row0 + lo + lax.broadcasted_iota(jnp.int32, (hi - lo, 1), 0)
            prev = jnp.where(rows != SEQ, prev, 0.0)
            nxt = jnp.where(rows != SEQ - 1, nxt, 0.0)
        return (prev * cw_ref[0:1, c_lo:c_hi] + u_scr[HALO + lo:HALO + hi, c_lo:c_hi] * cw_ref[1:2, c_lo:c_hi]
                + nxt * cw_ref[2:3, c_lo:c_hi] + cw_ref[3:4, c_lo:c_hi])

    def conv_down(slot, t):
        t_lo, t_hi = down_tiles[t]
        cuts = sorted({t_lo, t_hi} | ({c for c in (band_lo, band_hi) if t_lo < c < t_hi} if split else set()))
        for lo, hi in zip(cuts[:-1], cuts[1:]):
            masked = bool(split) and band_lo <= lo < band_hi
            for c_lo in range(0, FF_CHUNK, CONV_STRIP):
                a = conv(slots[slot], lo, hi, c_lo, c_lo + CONV_STRIP, masked)
                g = conv(slots[slot], lo, hi, FF_CHUNK + c_lo, FF_CHUNK + c_lo + CONV_STRIP, masked)
                act_scr[lo:hi, c_lo:c_lo + CONV_STRIP] = (g * jax.nn.sigmoid(g) * a).astype(BF16)
        o_ref[0, t_lo:t_hi, :] += _dot(act_scr[t_lo:t_hi, :], wd_ref[...])

    @pl.when(j == 0)
    def _():
        prologue()
        for t in range(FFN_TILES):
            up(0, t)

    for parity in range(2):
        @pl.when((j > 0) & (j < N_CHUNK) & (j % 2 == parity))
        def _():
            for t in range(FFN_TILES):
                up(parity, t)
                conv_down(1 - parity, t)

    @pl.when(j == N_CHUNK)
    def _():
        for t in range(FFN_TILES):
            conv_down((N_CHUNK - 1) % 2, t)
        rows = row0 + lax.broadcasted_iota(jnp.int32, (tm, 1), 0)
        o_ref[0] = x_ref[0] + per_row(rows, 5) * o_ref[0]


def _chunk_interleave(a):
    parts = []
    for j in range(N_CHUNK):
        parts += [a[..., j * FF_CHUNK:(j + 1) * FF_CHUNK], a[..., D_FF + j * FF_CHUNK:D_FF + (j + 1) * FF_CHUNK]]
    return jnp.concatenate(parts, axis=-1)


def _ffn(l, x1, mods, g_ffn, w_up, conv_wb, w_down, rows_per_sample, tm):
    n_b = x1.shape[0]
    n_rb = rows_per_sample // tm
    halo_per_block = tm // HALO
    n_halo = x1.shape[1] // HALO
    body = functools.partial(_ffn_kernel, tm=tm, rows_per_sample=rows_per_sample)

    def up_idx(j):
        return jnp.minimum(j, N_CHUNK - 1)

    def down_idx(j):
        return jnp.maximum(j - 1, 0)

    return pl.pallas_call(
        body,
        grid=(n_b, n_rb, N_CHUNK + 1),
        in_specs=[
            pl.BlockSpec((1, tm, D), lambda b, r, j: (b, r, 0)),
            pl.BlockSpec((1, HALO, D), lambda b, r, j: (b, jnp.maximum(r * halo_per_block - 1, 0), 0)),
            pl.BlockSpec((1, HALO, D), lambda b, r, j: (b, jnp.minimum((r + 1) * halo_per_block, n_halo - 1), 0)),
            pl.BlockSpec((None, 1, 1, 6 * D), lambda b, r, j: (l, b, 0, 0)),
            pl.BlockSpec((None, 1, 1, 6 * D), lambda b, r, j: (l, n_b, 0, 0)),
            _layer_spec(g_ffn, l),
            pl.BlockSpec((None, D, 2 * FF_CHUNK), lambda b, r, j: (l, 0, up_idx(j))),
            pl.BlockSpec((None, 4, 2 * FF_CHUNK), lambda b, r, j: (l, 0, down_idx(j))),
            pl.BlockSpec((None, FF_CHUNK, D), lambda b, r, j: (l, down_idx(j), 0)),
        ],
        out_specs=pl.BlockSpec((1, tm, D), lambda b, r, j: (b, r, 0)),
        out_shape=jax.ShapeDtypeStruct((n_b, rows_per_sample, D), F32),
        scratch_shapes=[
            pltpu.VMEM((tm + 2 * HALO, D), BF16),
            pltpu.VMEM((tm + 2 * HALO, 2 * FF_CHUNK), F32),
            pltpu.VMEM((tm + 2 * HALO, 2 * FF_CHUNK), F32),
            pltpu.VMEM((tm, FF_CHUNK), BF16),
        ],
        compiler_params=_params(("parallel", "parallel", "arbitrary")),
        name="conv_ffn",
    )(x1, x1, x1, mods, mods, g_ffn, w_up, conv_wb, w_down)


def _rope_table(rot_dim):
    n_freq = rot_dim // 4
    inv = jnp.power(ROPE_THETA, -jnp.arange(n_freq, dtype=F32) / n_freq)
    t = jnp.arange(SEQ)
    row = (t // GRID_W).astype(F32)
    col = (t % GRID_W).astype(F32)
    ar, ac = row[:, None] * inv, col[:, None] * inv
    ang = jnp.concatenate([ar, ar, ac, ac], axis=-1)
    sign = jnp.concatenate([-jnp.ones(n_freq), jnp.ones(n_freq), -jnp.ones(n_freq), jnp.ones(n_freq)]).astype(F32)
    cos = jnp.concatenate([jnp.cos(ang), jnp.ones((CTX, rot_dim), F32)], axis=0)
    sin = jnp.concatenate([jnp.sin(ang) * sign, jnp.zeros((CTX, rot_dim), F32)], axis=0)
    return jnp.concatenate([cos.T, sin.T], axis=0)


def _na_bias_tiles(rpb):
    n_l, n_h = rpb.shape[:2]
    n_off = 2 * NA_ROWS - 1
    p = jnp.pad(rpb.astype(F32) * LOG2E, ((0, 0), (0, 0), (0, 0), (48, 49)))
    sk = jnp.broadcast_to(p[..., None, :], (n_l, n_h, n_off, GRID_W, 128)).reshape(n_l, n_h, n_off, GRID_W * 128)
    sk = sk[..., :GRID_W * 127].reshape(n_l, n_h, n_off, GRID_W, 127)[..., 63:127]
    c = np.arange(GRID_W)
    win_start = np.clip(c - NA_COLS // 2, 0, GRID_W - NA_COLS)
    v_col = (c[None, :] >= win_start[:, None]) & (c[None, :] < win_start[:, None] + NA_COLS)
    tz = jnp.where(v_col, sk, NEG_INF)
    neg1 = jnp.full((n_l, n_h, 1, GRID_W, GRID_W), NEG_INF, F32)
    tzx = jnp.concatenate([neg1, tz, neg1], axis=2)
    first, second = tzx[:, :, 0:16], tzx[:, :, 1:17]
    neg16 = jnp.full_like(first, NEG_INF)
    return jnp.concatenate([
        jnp.concatenate([first, second], axis=-1), jnp.concatenate([first, neg16], axis=-1),
        jnp.concatenate([neg16, second], axis=-1), jnp.concatenate([neg1, neg1], axis=-1)], axis=2)


def kernel(x, c, ctx, c_ctx, w_mod, b_mod, g_mix, w_in, w_out, mla_q_a_g, mla_w_uq, mla_kv_a_g, mla_w_ukv,
           mla_q_g, mla_k_g, diff_q_g, diff_k_g, diff_lq1, diff_lk1, diff_lq2, diff_lk2, diff_subln_g,
           na_q_g, na_k_g, na_rpb, gqa_q_g, gqa_k_g, g_ffn, w_up, conv_w, conv_b, w_down):
    n_b = x.shape[0]
    n_layer = w_mod.shape[0]
    assert x.shape[1:] == (SEQ, D) and ctx.shape[1:] == (CTX, D)

    xs = jnp.concatenate([x, ctx], axis=1)
    mod_rows = -(-(n_b + 1) // 8) * 8
    cc = jnp.concatenate([c, c_ctx[None], jnp.zeros((mod_rows - n_b - 1, D), F32)], axis=0)
    mods_all = _modulation(cc, w_mod, b_mod).reshape(n_layer, mod_rows, 1, 6 * D)

    cs32, cs64 = _rope_table(32), _rope_table(64)
    bias_all = _na_bias_tiles(na_rpb)

    s = np.cumsum([0, 256, 128, 32, 256, 256, 256, 256, 256, 256, 256, 128, 128])
    gq_perm = np.concatenate([np.arange(GQA_D) + (2 * g + r) * GQA_D for r in range(2) for g in range(2)])

    n_l = n_layer
    gq_cols = w_in[:, :, s[9]:s[10]][:, :, gq_perm]
    wqk = jnp.concatenate([w_in[:, :, s[0]:s[3]], w_in[:, :, s[3]:s[5]], w_in[:, :, s[6]:s[8]], gq_cols,
                           w_in[:, :, s[10]:s[11]]], axis=2).swapaxes(1, 2).astype(BF16)
    wv = jnp.concatenate([w_in[:, :, s[5]:s[6]], w_in[:, :, s[8]:s[9]], w_in[:, :, s[11]:s[12]]],
                         axis=2).astype(BF16)
    wuq = mla_w_uq.swapaxes(1, 2).astype(BF16)
    wukv = mla_w_ukv.reshape(n_l, MLA_KVR, MLA_H, MLA_NOPE + MLA_V)
    wukn = wukv[..., :MLA_NOPE].reshape(n_l, MLA_KVR, MLA_H * MLA_NOPE).swapaxes(1, 2).astype(BF16)
    wuv = wukv[..., MLA_NOPE:].reshape(n_l, MLA_KVR, MLA_H * MLA_V).swapaxes(1, 2).astype(BF16)
    gcol = jnp.concatenate([mla_q_a_g, mla_kv_a_g, mla_q_g, mla_k_g, diff_q_g, diff_k_g, na_q_g, na_k_g,
                            gqa_q_g, gqa_k_g], axis=1).astype(F32)[..., None]
    lam_rows = jnp.stack([diff_lq1, diff_lk1, diff_lq2, diff_lk2], axis=1).astype(F32)
    lam_rows = jnp.concatenate([lam_rows, jnp.zeros_like(lam_rows)], axis=1)
    gsub = jnp.tile(diff_subln_g.astype(F32), (1, 2)).reshape(n_l, 1, 128)
    bounds = {
        "mla": _logit_bound(MLA_D, mla_q_g, mla_k_g),
        "diff": _logit_bound(DIFF_D, diff_q_g, diff_k_g),
        "na": _logit_bound(NA_D, na_q_g, na_k_g, LOG2E * jnp.maximum(jnp.max(na_rpb, axis=(1, 2, 3)), 0.0)),
        "gqa": _logit_bound(GQA_D, gqa_q_g, gqa_k_g),
    }
    w_out_b, w_down_b = w_out.astype(BF16), w_down.astype(BF16)
    w_up_b = _chunk_interleave(w_up.astype(BF16))
    conv_wb = _chunk_interleave(jnp.concatenate([conv_w, conv_b[:, None, :]], axis=1).astype(F32))
    gmix, gffn = g_mix.reshape(n_l, 1, D), g_ffn.reshape(n_l, 1, D)

    for l in range(n_layer):
        with_ctx = l < n_layer - 1
        lambda_init = 0.8 - 0.6 * math.exp(-0.3 * l)
        (q_mla, k_mla, v_mla, q_diff, k_diff, v_diff, q_na, k_na, v_na, q_gqa, k_gqa, v_gqa) = _inproj(
            l, xs, mods_all, gmix, wqk, wv, wuq, wukn, wuv, gcol, cs32, cs64)

        n_rb = NRB if with_ctx else NLAT
        mix_a = _attention(_mla_kernel, "attn_mla", q_mla, k_mla, v_mla, bounds["mla"][l], [], n_rb)
        mix_b = _attention(
            functools.partial(_diff_kernel, lambda_init=lambda_init), "attn_diff",
            q_diff, k_diff, v_diff, bounds["diff"][l],
            [(lam_rows, _layer_spec(lam_rows, l)), (gsub, _layer_spec(gsub, l))], n_rb)
        mix_c = _attention(_na_kernel, "attn_na", q_na, k_na, v_na, bounds["na"][l],
                           [(bias_all, _layer_spec(bias_all, l))], n_rb)
        mix_d = _attention(_gqa_kernel, "attn_gqa", q_gqa, k_gqa, v_gqa, bounds["gqa"][l], [], n_rb)

        rows_per_sample = T if with_ctx else SEQ
        x1 = _outproj(l, xs, mods_all, (mix_a, mix_b, mix_c, mix_d), w_out_b, rows_per_sample,
                      OUT_TM if with_ctx else OUT_TM_LAST)
        xs = _ffn(l, x1, mods_all, gffn, w_up_b, conv_wb, w_down_b, rows_per_sample, rows_per_sample // 2)
    return xs
```

```python
import functools
import math

import numpy as np
import jax
import jax.numpy as jnp
from jax import lax
from jax.experimental import pallas as pl
from jax.experimental.pallas import tpu as pltpu

F32 = jnp.float32
BF16 = jnp.bfloat16

D = 1024
SEQ = 2048
GRID_W = 64
CTX = 256
T = SEQ + CTX
RB = 256
NRB = T // RB
NLAT = SEQ // RB
EPS = 1e-6
NEG_INF = -1e30
ROPE_THETA = 10000.0
LOG2E = 1.4426950408889634

MLA_H, MLA_NOPE, MLA_ROPE, MLA_V, MLA_QR, MLA_KVR = 4, 64, 32, 64, 256, 128
MLA_D = MLA_NOPE + MLA_ROPE
DIFF_H, DIFF_D = 4, 32
NA_H, NA_D, NA_ROWS, NA_COLS = 4, 64, 8, 16
GQA_H, GQA_KV, GQA_D = 4, 2, 64
D_FF = 2816
FF_CHUNK = 256
N_CHUNK = D_FF // FF_CHUNK
OUT_TM, OUT_TM_LAST = 768, 1024
HALO = 16
FFN_TILES = 4
CONV_STRIP = 256

R_CQ, R_CKV, R_KR, R_DQ, R_DK, R_NQ, R_NK, R_GQ, R_GK = 0, 256, 384, 416, 672, 928, 1184, 1440, 1696
QK_ROWS = 1824
V_COLS = 640

G_QA, G_KVA, G_MQ, G_MK, G_DQ, G_DK, G_NQ, G_NK, G_GQ, G_GK = 0, 256, 384, 480, 576, 608, 640, 704, 768, 832
G_ROWS = 896

VMEM_LIMIT = 56 * 1024 * 1024


def _params(sem):
    return pltpu.CompilerParams(dimension_semantics=sem, vmem_limit_bytes=VMEM_LIMIT)


def _dot(a, b):
    return jnp.dot(a, b, preferred_element_type=F32)


def _dot_nt(a, b):
    return lax.dot_general(a, b, (((1,), (1,)), ((), ())), preferred_element_type=F32)


def _mod_kernel(c_ref, w_ref, b_ref, o_ref):
    c = c_ref[...]
    a = (c * jax.nn.sigmoid(c)).astype(BF16)
    o_ref[0] = _dot(a, w_ref[0].astype(BF16)) + b_ref[0]


def _modulation(cc, w_mod, b_mod):
    n_layer = w_mod.shape[0]
    rows = cc.shape[0]
    return pl.pallas_call(
        _mod_kernel,
        grid=(n_layer, 6),
        in_specs=[
            pl.BlockSpec((rows, D), lambda l, j: (0, 0)),
            pl.BlockSpec((1, D, D), lambda l, j: (l, 0, j)),
            pl.BlockSpec((1, 1, D), lambda l, j: (l, 0, j)),
        ],
        out_specs=pl.BlockSpec((1, rows, D), lambda l, j: (l, 0, j)),
        out_shape=jax.ShapeDtypeStruct((n_layer, rows, 6 * D), F32),
        compiler_params=_params(("parallel", "parallel")),
        name="modulation",
    )(cc, w_mod, b_mod.reshape(n_layer, 1, 6 * D))


def _rms_rows(x, g, n):
    ss = jnp.sum(x * x, axis=0, keepdims=True) * (1.0 / n)
    return x * lax.rsqrt(ss + EPS) * g


def _rope_rows(x, cos, sin_signed, w):
    rot = jnp.concatenate([x[w:2 * w], x[0:w], x[3 * w:4 * w], x[2 * w:3 * w]], axis=0)
    return x * cos + rot * sin_signed


def _inproj_kernel(x_ref, mod_ref, gmix_ref, wqk_ref, wv_ref, wuq_ref, wukn_ref, wuv_ref, gcol_ref,
                   cs32_ref, cs64_ref,
                   qmla_ref, kmla_ref, vmla_ref, qdiff_ref, kdiff_ref, vdiff_ref,
                   qna_ref, kna_ref, vna_ref, qgqa_ref, kgqa_ref, vgqa_ref):
    x = x_ref[0]
    mod = mod_ref[0]
    shift, scale = mod[:, 0:D], mod[:, D:2 * D]
    ms = jnp.mean(x * x, axis=-1, keepdims=True)
    h = x * lax.rsqrt(ms + EPS) * gmix_ref[...]
    hb = (h * (1.0 + scale) + shift).astype(BF16)

    def project(lo, hi):
        return _dot_nt(wqk_ref[lo:hi, :], hb)

    def gain(off, n):
        return gcol_ref[off:off + n, :]

    cos32, sin32 = cs32_ref[0:32, :], cs32_ref[32:64, :]
    cos64, sin64 = cs64_ref[0:64, :], cs64_ref[64:128, :]

    def values():
        pv = _dot(hb, wv_ref[...])
        vdiff_ref[0] = pv[:, 0:256].astype(BF16)
        vna_ref[0] = pv[:, 256:512].astype(BF16)
        vgqa_ref[0] = pv[:, 512:640].astype(BF16)

    def mla(pt):
        cq = _rms_rows(pt[0:MLA_QR], gain(G_QA, MLA_QR), MLA_QR).astype(BF16)
        qt = _dot(wuq_ref[...], cq)
        ckv = _rms_rows(pt[R_CKV:R_CKV + MLA_KVR], gain(G_KVA, MLA_KVR), MLA_KVR).astype(BF16)
        knt = _dot(wukn_ref[...], ckv)
        vt = _dot(wuv_ref[...], ckv)
        vmla_ref[0] = vt.T.astype(BF16)
        kr = pt[R_KR:R_KR + MLA_ROPE]
        kr_ss = jnp.sum(kr * kr, axis=0, keepdims=True)
        g_mq, g_mk = gain(G_MQ, MLA_D), gain(G_MK, MLA_D)
        zpad = jnp.zeros((128 - MLA_D, RB), F32)
        q_parts = []
        for hd in range(MLA_H):
            qh = _rms_rows(qt[hd * MLA_D:(hd + 1) * MLA_D], g_mq, MLA_D)
            q_rope = _rope_rows(qh[MLA_NOPE:], cos32, sin32, MLA_ROPE // 4)
            q_parts += [qh[:MLA_NOPE], q_rope, zpad]
            kn = knt[hd * MLA_NOPE:(hd + 1) * MLA_NOPE]
            ss = (jnp.sum(kn * kn, axis=0, keepdims=True) + kr_ss) * (1.0 / MLA_D)
            r = lax.rsqrt(ss + EPS)
            k_rope = _rope_rows(kr * r * g_mk[MLA_NOPE:], cos32, sin32, MLA_ROPE // 4)
            kmla_ref[0, hd * 128:(hd + 1) * 128, :] = jnp.concatenate(
                [kn * r * g_mk[:MLA_NOPE], k_rope, zpad], axis=0).astype(BF16)
        q_all = jnp.concatenate(q_parts, axis=0) * (MLA_D ** -0.5 * LOG2E)
        qmla_ref[0] = q_all.T.astype(BF16)

    def diff(pt):
        g_dq, g_dk = gain(G_DQ, DIFF_D), gain(G_DK, DIFF_D)
        q_parts, k_parts = [], []
        for gi in range(2 * DIFF_H):
            qg = _rms_rows(pt[gi * DIFF_D:(gi + 1) * DIFF_D], g_dq, DIFF_D)
            q_parts.append(_rope_rows(qg, cos32, sin32, DIFF_D // 4))
            kg = _rms_rows(pt[256 + gi * DIFF_D:256 + (gi + 1) * DIFF_D], g_dk, DIFF_D)
            k_parts.append(_rope_rows(kg, cos32, sin32, DIFF_D // 4))
        qdiff_ref[0] = (jnp.concatenate(q_parts, axis=0) * (DIFF_D ** -0.5 * LOG2E)).T.astype(BF16)
        kdiff_ref[0] = jnp.concatenate(k_parts, axis=0).astype(BF16)

    def na(pt):
        g_nq, g_nk = gain(G_NQ, NA_D), gain(G_NK, NA_D)
        q_parts, k_parts = [], []
        for hd in range(NA_H):
            q_parts.append(_rms_rows(pt[hd * NA_D:(hd + 1) * NA_D], g_nq, NA_D))
            k_parts.append(_rms_rows(pt[256 + hd * NA_D:256 + (hd + 1) * NA_D], g_nk, NA_D))
        qna_ref[0] = (jnp.concatenate(q_parts, axis=0) * (NA_D ** -0.5 * LOG2E)).T.astype(BF16)
        kna_ref[0] = jnp.concatenate(k_parts, axis=0).astype(BF16)

    def gqa(pt):
        g_gq, g_gk = gain(G_GQ, GQA_D), gain(G_GK, GQA_D)
        q_parts, k_parts = [], []
        for hd in range(GQA_H):
            qg = _rms_rows(pt[hd * GQA_D:(hd + 1) * GQA_D], g_gq, GQA_D)
            q_parts.append(_rope_rows(qg, cos64, sin64, GQA_D // 4))
        for hd in range(GQA_KV):
            kg = _rms_rows(pt[256 + hd * GQA_D:256 + (hd + 1) * GQA_D], g_gk, GQA_D)
            k_parts.append(_rope_rows(kg, cos64, sin64, GQA_D // 4))
        qgqa_ref[0] = (jnp.concatenate(q_parts, axis=0) * (GQA_D ** -0.5 * LOG2E)).T.astype(BF16)
        kgqa_ref[0] = jnp.concatenate(k_parts, axis=0).astype(BF16)

    pt_mla = project(R_CQ, R_DQ)
    pt_diff = project(R_DQ, R_NQ)
    mla(pt_mla)
    pt_na = project(R_NQ, R_GQ)
    diff(pt_diff)
    pt_gqa = project(R_GQ, QK_ROWS)
    na(pt_na)
    values()
    gqa(pt_gqa)


def _inproj(l, xs, mods, gmix, wqk, wv, wuq, wukn, wuv, gcol, cs32, cs64):
    n_b = xs.shape[0]

    def full(a):
        return _layer_spec(a, l)

    def tok(width):
        return pl.BlockSpec((1, RB, width), lambda r, b: (b, r, 0))

    def chan(rows):
        return pl.BlockSpec((1, rows, RB), lambda r, b: (b, 0, r))

    def tshape(width):
        return jax.ShapeDtypeStruct((n_b, T, width), BF16)

    def cshape(rows):
        return jax.ShapeDtypeStruct((n_b, rows, T), BF16)

    return pl.pallas_call(
        _inproj_kernel,
        grid=(NRB, n_b),
        in_specs=[
            pl.BlockSpec((1, RB, D), lambda r, b: (b, r, 0)),
            pl.BlockSpec((None, 1, 1, 6 * D), lambda r, b: (l, jnp.where(r < NLAT, b, n_b), 0, 0)),
            full(gmix), full(wqk), full(wv), full(wuq), full(wukn), full(wuv), full(gcol),
            pl.BlockSpec((64, RB), lambda r, b: (0, r)),
            pl.BlockSpec((128, RB), lambda r, b: (0, r)),
        ],
        out_specs=[tok(512), chan(512), tok(256), tok(256), chan(256), tok(256),
                   tok(256), chan(256), tok(256), tok(256), chan(128), tok(128)],
        out_shape=[tshape(512), cshape(512), tshape(256), tshape(256), cshape(256), tshape(256),
                   tshape(256), cshape(256), tshape(256), tshape(256), cshape(128), tshape(128)],
        compiler_params=_params(("parallel", "parallel")),
        name="inproj_prep",
    )(xs, mods, gmix, wqk, wv, wuq, wukn, wuv, gcol, cs32, cs64)


def _lane_id(shape):
    return lax.broadcasted_iota(jnp.int32, shape, 1)


KEY_CHUNK = 768
LAT_CHUNKS = tuple((lo, KEY_CHUNK) for lo in range(0, T, KEY_CHUNK))
CTX_CHUNKS = ((SEQ, CTX),)


def _k_slab(k_ref, idx):
    return lambda lo, n: k_ref[0, idx * 128:(idx + 1) * 128, pl.ds(lo, n)]


def _v_slab(v_ref, idx):
    return lambda lo, n: v_ref[0, pl.ds(lo, n), idx * 128:(idx + 1) * 128]


def _attend(units, chunks, shift=None):
    def scores(i, c):
        q, keys, _, bias = units[i]
        lo, n = chunks[c]
        s = _dot(q, keys(lo, n))
        b = None if bias is None else bias(c)
        return s if b is None else s + b

    if shift is not None:
        items = [(i, c) for i in range(len(units)) for c in range(len(chunks))]
        acc, l = [None] * len(units), [None] * len(units)
        nxt = scores(*items[0])
        for idx, (i, c) in enumerate(items):
            cur = nxt
            if idx + 1 < len(items):
                nxt = scores(*items[idx + 1])
            e = jnp.exp2(cur - shift)
            lc = jnp.sum(e, axis=-1, keepdims=True)
            oc = _dot(e.astype(BF16), units[i][2](*chunks[c]))
            acc[i] = oc if acc[i] is None else acc[i] + oc
            l[i] = lc if l[i] is None else l[i] + lc
        return list(zip(acc, l))

    cur = [scores(0, c) for c in range(len(chunks))]
    out = []
    for i in range(len(units)):
        m = functools.reduce(jnp.maximum, [jnp.max(s, axis=-1, keepdims=True) for s in cur])
        nxt, acc, l = [], None, None
        for c, (lo, n) in enumerate(chunks):
            if i + 1 < len(units):
                nxt.append(scores(i + 1, c))
            e = jnp.exp2(cur[c] - m)
            lc = jnp.sum(e, axis=-1, keepdims=True)
            oc = _dot(e.astype(BF16), units[i][2](lo, n))
            acc = oc if acc is None else acc + oc
            l = lc if l is None else l + lc
        out.append((acc, l))
        cur = nxt
    return out


def _with_shift(bnd_ref, run):
    @pl.when(bnd_ref[1] > 0.5)
    def _():
        run(bnd_ref[0])

    @pl.when(bnd_ref[1] <= 0.5)
    def _():
        run(None)


def _query_blocks(n_rb, bnd_ref, run):
    def all_blocks(shift):
        def body(rb, carry):
            run(rb, pl.ds(pl.multiple_of(rb * RB, RB), RB), True, shift)
            return carry

        lax.fori_loop(0, NLAT, body, 0)
        if n_rb > NLAT:
            run(NLAT, pl.ds(SEQ, CTX), False, shift)

    _with_shift(bnd_ref, all_blocks)


def _mla_kernel(q_ref, k_ref, v_ref, bnd_ref, o_ref, *, n_rb):
    lane = _lane_id((RB, 128))

    def run(rb, rows, latent, shift):
        units = [(q_ref[0, rows,hd * 128:(hd + 1) * 128], _k_slab(k_ref, hd), _v_slab(v_ref, hd // 2), None)
                 for hd in range(MLA_H)]
        o = [acc * (1.0 / l) for acc, l in _attend(units, LAT_CHUNKS if latent else CTX_CHUNKS, shift)]
        outs = [jnp.where(lane < 64, o[0], o[1]), jnp.where(lane < 64, o[2], o[3])]
        o_ref[0, rows, :] = jnp.concatenate(outs, axis=1).astype(o_ref.dtype)

    _query_blocks(n_rb, bnd_ref, run)


def _gqa_kernel(q_ref, k_ref, v_ref, bnd_ref, o_ref, *, n_rb):
    lane = _lane_id((RB, 128))

    def run(rb, rows, latent, shift):
        order = [(rep, grp) for rep in range(2) for grp in range(2)]
        units = []
        for rep, grp in order:
            qs = q_ref[0, rows,rep * 128:(rep + 1) * 128]
            qm = jnp.where((lane >= 64) == (grp == 1), qs, jnp.zeros_like(qs))
            units.append((qm, _k_slab(k_ref, 0), _v_slab(v_ref, 0), None))
        res = {}
        for (rep, grp), (acc, l) in zip(order, _attend(units, LAT_CHUNKS if latent else CTX_CHUNKS, shift)):
            o = acc * (1.0 / l)
            res[(grp, rep)] = o if grp == rep else pltpu.roll(o, 64, axis=1)
        outs = [jnp.where(lane < 64, res[(grp, 0)], res[(grp, 1)]) for grp in range(2)]
        o_ref[0, rows, :] = jnp.concatenate(outs, axis=1).astype(o_ref.dtype)

    _query_blocks(n_rb, bnd_ref, run)


def _diff_kernel(q_ref, k_ref, v_ref, bnd_ref, lam_ref, g_ref, o_ref, *, n_rb, lambda_init):
    lane = _lane_id((RB, 128))
    lq1, lk1, lq2, lk2 = lam_ref[0:1, :], lam_ref[1:2, :], lam_ref[2:3, :], lam_ref[3:4, :]
    lam = (jnp.exp(jnp.sum(lq1 * lk1, axis=-1, keepdims=True))
           - jnp.exp(jnp.sum(lq2 * lk2, axis=-1, keepdims=True)) + lambda_init)
    gsub = g_ref[...]
    grp = lane // DIFF_D

    def run(rb, rows, latent, shift):
        units = []
        for hd in range(DIFF_H):
            pair, sub = divmod(hd, 2)
            qs = q_ref[0, rows,pair * 128:(pair + 1) * 128]
            for which in range(2):
                qm = jnp.where(grp == 2 * sub + which, qs, jnp.zeros_like(qs))
                units.append((qm, _k_slab(k_ref, pair), _v_slab(v_ref, pair), None))
        res = _attend(units, LAT_CHUNKS if latent else CTX_CHUNKS, shift)
        outs = []
        for pair in range(2):
            halves = []
            for sub in range(2):
                (a1, l1), (a2, l2) = res[2 * (2 * pair + sub)], res[2 * (2 * pair + sub) + 1]
                halves.append(a1 * (1.0 / l1) - a2 * (lam / l2))
            o = jnp.where(lane < 64, halves[0], halves[1])
            o2 = o * o
            ss0 = jnp.sum(jnp.where(lane < 64, o2, 0.0), axis=-1, keepdims=True)
            ss1 = jnp.sum(jnp.where(lane < 64, 0.0, o2), axis=-1, keepdims=True)
            ss = jnp.where(lane < 64, ss0, ss1) * (1.0 / (2 * DIFF_D))
            outs.append(o * lax.rsqrt(ss + EPS) * gsub * (1.0 - lambda_init))
        o_ref[0, rows, :] = jnp.concatenate(outs, axis=1).astype(o_ref.dtype)

    _query_blocks(n_rb, bnd_ref, run)


NA_WIN = 1024
NA_CHUNK = 512
NA_TILES = 49
GRID_ROWS = SEQ // GRID_W


def _na_kernel(q_ref, k_ref, v_ref, bnd_ref, bias_ref, o_ref, *, n_rb):
    lane = _lane_id((RB, 128))

    def run(rb, rows, latent, shift):
        na_chunks, tile_idx = CTX_CHUNKS, None
        if latent:
            win_row = jnp.clip(4 * rb - 4, 0, GRID_ROWS - NA_WIN // GRID_W)
            off = pl.multiple_of(win_row * GRID_W, RB)
            local = [(pl.multiple_of(off + c * NA_CHUNK, RB), NA_CHUNK) for c in range(NA_WIN // NA_CHUNK)]
            na_chunks = local + list(CTX_CHUNKS)
            tile_idx = []
            for c in range(NA_WIN // NA_CHUNK):
                per_row = []
                for ri in range(RB // GRID_W):
                    r = 4 * rb + ri
                    row_start = jnp.clip(r - NA_ROWS // 2, 0, GRID_ROWS - NA_ROWS)
                    idxs = []
                    for p in range(NA_CHUNK // 128):
                        kr = win_row + c * (NA_CHUNK // GRID_W) + 2 * p
                        ok1 = (kr >= row_start) & (kr < row_start + NA_ROWS)
                        ok2 = (kr + 1 >= row_start) & (kr + 1 < row_start + NA_ROWS)
                        e = jnp.clip(kr - r + NA_ROWS, 0, 15)
                        idxs.append(jnp.where(ok1 & ok2, e, jnp.where(ok1, 16 + e, jnp.where(ok2, 32 + e, 48))))
                    per_row.append(idxs)
                tile_idx.append(per_row)
        units = []
        for hd in range(NA_H):
            pair, sub = divmod(hd, 2)
            qs = q_ref[0, rows,pair * 128:(pair + 1) * 128]
            qm = jnp.where((lane >= 64) == (sub == 1), qs, jnp.zeros_like(qs))

            def bias(c, hd=hd):
                if tile_idx is None or c >= NA_WIN // NA_CHUNK:
                    return None
                return jnp.concatenate(
                    [jnp.concatenate([bias_ref[hd, idx] for idx in row], axis=1) for row in tile_idx[c]], axis=0)

            units.append((qm, _k_slab(k_ref, pair), _v_slab(v_ref, pair), bias))
        o = [acc * (1.0 / l) for acc, l in _attend(units, na_chunks, shift)]
        outs = [jnp.where(lane < 64, o[0], o[1]), jnp.where(lane < 64, o[2], o[3])]
        o_ref[0, rows, :] = jnp.concatenate(outs, axis=1).astype(o_ref.dtype)

    _query_blocks(n_rb, bnd_ref, run)


SHIFT_MAX = 40.0


def _logit_bound(d, q_gain, k_gain, bias_max=0.0):
    bound = (1.01 * math.sqrt(d) * LOG2E * jnp.max(jnp.abs(q_gain), axis=-1) * jnp.max(jnp.abs(k_gain), axis=-1)
             + bias_max)
    return jnp.stack([bound, (bound <= SHIFT_MAX).astype(F32)], axis=-1).astype(F32)


def _layer_spec(a, l):
    return pl.BlockSpec((None,) + a.shape[1:], lambda *_, _n=a.ndim - 1: (l,) + (0,) * _n)


def _attention(body, name, q, k, v, bound, extra, n_rb, out_width=256):
    n_b = q.shape[0]
    extra = [(bound, pl.BlockSpec(memory_space=pltpu.SMEM))] + list(extra)
    in_specs = [
        pl.BlockSpec((1, T, q.shape[2]), lambda b: (b, 0, 0)),
        pl.BlockSpec((1, k.shape[1], T), lambda b: (b, 0, 0)),
        pl.BlockSpec((1, T, v.shape[2]), lambda b: (b, 0, 0)),
    ] + [spec for _, spec in extra]
    return pl.pallas_call(
        functools.partial(body, n_rb=n_rb),
        grid=(n_b,),
        in_specs=in_specs,
        out_specs=pl.BlockSpec((1, n_rb * RB, out_width), lambda b: (b, 0, 0)),
        out_shape=jax.ShapeDtypeStruct((n_b, n_rb * RB, out_width), BF16),
        compiler_params=_params(("parallel",)),
        name=name,
    )(q, k, v, *[a for a, _ in extra])


def _outproj_kernel(x_ref, modb_ref, modc_ref, ma_ref, mb_ref, mc_ref, md_ref, w_ref, o_ref, *, tm):
    rows = pl.program_id(1) * tm + lax.broadcasted_iota(jnp.int32, (tm, 1), 0)
    g1 = jnp.where(rows >= SEQ, modc_ref[0][:, 2 * D:3 * D], modb_ref[0][:, 2 * D:3 * D])
    acc = _dot(ma_ref[0], w_ref[0:256, :])
    acc += _dot(mb_ref[0], w_ref[256:512, :])
    acc += _dot(mc_ref[0], w_ref[512:768, :])
    acc += _dot(md_ref[0], w_ref[768:1024, :])
    o_ref[0] = x_ref[0] + g1 * acc


def _outproj(l, xs, mods, mixes, w_out, rows_per_sample, tm):
    n_b = xs.shape[0]
    mix_spec = pl.BlockSpec((1, tm, 256), lambda b, r: (b, r, 0))
    return pl.pallas_call(
        functools.partial(_outproj_kernel, tm=tm),
        grid=(n_b, rows_per_sample // tm),
        in_specs=[
            pl.BlockSpec((1, tm, D), lambda b, r: (b, r, 0)),
            pl.BlockSpec((None, 1, 1, 6 * D), lambda b, r: (l, b, 0, 0)),
            pl.BlockSpec((None, 1, 1, 6 * D), lambda b, r: (l, n_b, 0, 0)),
            mix_spec, mix_spec, mix_spec, mix_spec,
            _layer_spec(w_out, l),
        ],
        out_specs=pl.BlockSpec((1, tm, D), lambda b, r: (b, r, 0)),
        out_shape=jax.ShapeDtypeStruct((n_b, rows_per_sample, D), F32),
        compiler_params=_params(("parallel", "parallel")),
        name="outproj",
    )(xs, mods, mods, *mixes, w_out)


def _ffn_kernel(x_ref, hb_ref, ha_ref, modb_ref, modc_ref, g_ref, wu_ref, cw_ref, wd_ref, o_ref,
                h_scr, u0_scr, u1_scr, act_scr, *, tm, rows_per_sample):
    rb = pl.program_id(1)
    j = pl.program_id(2)
    row0 = rb * tm
    split = SEQ % tm if rows_per_sample > SEQ else 0
    band_lo, band_hi = split - HALO, split + HALO

    def per_row(rows, col):
        lat = modb_ref[0][:, col * D:(col + 1) * D]
        if rows_per_sample == SEQ:
            return lat
        return jnp.where(rows >= SEQ, modc_ref[0][:, col * D:(col + 1) * D], lat)

    def modulated(x, rows):
        ms = jnp.mean(x * x, axis=-1, keepdims=True)
        y = x * lax.rsqrt(ms + EPS) * g_ref[...]
        return (y * (1.0 + per_row(rows, 4)) + per_row(rows, 3)).astype(BF16)

    def prologue():
        halo_rows = lax.broadcasted_iota(jnp.int32, (HALO, 1), 0)
        has_before = (row0 != 0) & (row0 != SEQ)
        has_after = (row0 + tm != SEQ) & (row0 + tm != rows_per_sample)
        zero = jnp.zeros((HALO, D), BF16)
        h_scr[0:HALO, :] = jnp.where(has_before, modulated(hb_ref[0], row0 - HALO + halo_rows), zero)
        h_scr[HALO + tm:, :] = jnp.where(has_after, modulated(ha_ref[0], row0 + tm + halo_rows), zero)
        main_rows = row0 + lax.broadcasted_iota(jnp.int32, (tm, 1), 0)
        h_scr[HALO:HALO + tm, :] = modulated(x_ref[0], main_rows)
        o_ref[0] = jnp.zeros((tm, D), F32)

    slots = (u0_scr, u1_scr)

    def tiles(total):
        units = total // HALO
        sizes = [(units // FFN_TILES + (1 if t < units % FFN_TILES else 0)) * HALO for t in range(FFN_TILES)]
        edges = np.cumsum([0] + sizes)
        return [(int(edges[t]), int(edges[t + 1])) for t in range(FFN_TILES)]

    up_tiles, down_tiles = tiles(tm + 2 * HALO), tiles(tm)

    def up(slot, t):
        lo, hi = up_tiles[t]
        slots[slot][lo:hi, :] = _dot(h_scr[lo:hi, :], wu_ref[...])

    def conv(u_scr, lo, hi, c_lo, c_hi, masked):
        prev = u_scr[HALO - 1 + lo:HALO - 1 + hi, c_lo:c_hi]
        nxt = u_scr[HALO + 1 + lo:HALO + 1 + hi, c_lo:c_hi]
        if masked:
            rows = row0 + lo + lax.broadcasted_iota(jnp.int32, (hi - lo, 1), 0)
            prev = jnp.where(rows != SEQ, prev, 0.0)
            nxt = jnp.where(rows != SEQ - 1, nxt, 0.0)
        return (prev * cw_ref[0:1, c_lo:c_hi] + u_scr[HALO + lo:HALO + hi, c_lo:c_hi] * cw_ref[1:2, c_lo:c_hi]
                + nxt * cw_ref[2:3, c_lo:c_hi] + cw_ref[3:4, c_lo:c_hi])

    def conv_down(slot, t):
        t_lo, t_hi = down_tiles[t]
        cuts = sorted({t_lo, t_hi} | ({c for c in (band_lo, band_hi) if t_lo < c < t_hi} if split else set()))
        for lo, hi in zip(cuts[:-1], cuts[1:]):
            masked = bool(split) and band_lo <= lo < band_hi
            for c_lo in range(0, FF_CHUNK, CONV_STRIP):
                a = conv(slots[slot], lo, hi, c_lo, c_lo + CONV_STRIP, masked)
                g = conv(slots[slot], lo, hi, FF_CHUNK + c_lo, FF_CHUNK + c_lo + CONV_STRIP, masked)
                act_scr[lo:hi, c_lo:c_lo + CONV_STRIP] = (g * jax.nn.sigmoid(g) * a).astype(BF16)
        o_ref[0, t_lo:t_hi, :] += _dot(act_scr[t_lo:t_hi, :], wd_ref[...])

    @pl.when(j == 0)
    def _():
        prologue()
        for t in range(FFN_TILES):
            up(0, t)

    for parity in range(2):
        @pl.when((j > 0) & (j < N_CHUNK) & (j % 2 == parity))
        def _():
            for t in range(FFN_TILES):
                up(parity, t)
                conv_down(1 - parity, t)

    @pl.when(j == N_CHUNK)
    def _():
        for t in range(FFN_TILES):
            conv_down((N_CHUNK - 1) % 2, t)
        rows = row0 + lax.broadcasted_iota(jnp.int32, (tm, 1), 0)
        o_ref[0] = x_ref[0] + per_row(rows, 5) * o_ref[0]


def _chunk_interleave(a):
    parts = []
    for j in range(N_CHUNK):
        parts += [a[..., j * FF_CHUNK:(j + 1) * FF_CHUNK], a[..., D_FF + j * FF_CHUNK:D_FF + (j + 1) * FF_CHUNK]]
    return jnp.concatenate(parts, axis=-1)


def _ffn(l, x1, mods, g_ffn, w_up, conv_wb, w_down, rows_per_sample, tm):
    n_b = x1.shape[0]
    n_rb = rows_per_sample // tm
    halo_per_block = tm // HALO
    n_halo = x1.shape[1] // HALO
    body = functools.partial(_ffn_kernel, tm=tm, rows_per_sample=rows_per_sample)

    def up_idx(j):
        return jnp.minimum(j, N_CHUNK - 1)

    def down_idx(j):
        return jnp.maximum(j - 1, 0)

    return pl.pallas_call(
        body,
        grid=(n_b, n_rb, N_CHUNK + 1),
        in_specs=[
            pl.BlockSpec((1, tm, D), lambda b, r, j: (b, r, 0)),
            pl.BlockSpec((1, HALO, D), lambda b, r, j: (b, jnp.maximum(r * halo_per_block - 1, 0), 0)),
            pl.BlockSpec((1, HALO, D), lambda b, r, j: (b, jnp.minimum((r + 1) * halo_per_block, n_halo - 1), 0)),
            pl.BlockSpec((None, 1, 1, 6 * D), lambda b, r, j: (l, b, 0, 0)),
            pl.BlockSpec((None, 1, 1, 6 * D), lambda b, r, j: (l, n_b, 0, 0)),
            _layer_spec(g_ffn, l),
            pl.BlockSpec((None, D, 2 * FF_CHUNK), lambda b, r, j: (l, 0, up_idx(j))),
            pl.BlockSpec((None, 4, 2 * FF_CHUNK), lambda b, r, j: (l, 0, down_idx(j))),
            pl.BlockSpec((None, FF_CHUNK, D), lambda b, r, j: (l, down_idx(j), 0)),
        ],
        out_specs=pl.BlockSpec((1, tm, D), lambda b, r, j: (b, r, 0)),
        out_shape=jax.ShapeDtypeStruct((n_b, rows_per_sample, D), F32),
        scratch_shapes=[
            pltpu.VMEM((tm + 2 * HALO, D), BF16),
            pltpu.VMEM((tm + 2 * HALO, 2 * FF_CHUNK), F32),
            pltpu.VMEM((tm + 2 * HALO, 2 * FF_CHUNK), F32),
            pltpu.VMEM((tm, FF_CHUNK), BF16),
        ],
        compiler_params=_params(("parallel", "parallel", "arbitrary")),
        name="conv_ffn",
    )(x1, x1, x1, mods, mods, g_ffn, w_up, conv_wb, w_down)


def _rope_table(rot_dim):
    n_freq = rot_dim // 4
    inv = jnp.power(ROPE_THETA, -jnp.arange(n_freq, dtype=F32) / n_freq)
    t = jnp.arange(SEQ)
    row = (t // GRID_W).astype(F32)
    col = (t % GRID_W).astype(F32)
    ar, ac = row[:, None] * inv, col[:, None] * inv
    ang = jnp.concatenate([ar, ar, ac, ac], axis=-1)
    sign = jnp.concatenate([-jnp.ones(n_freq), jnp.ones(n_freq), -jnp.ones(n_freq), jnp.ones(n_freq)]).astype(F32)
    cos = jnp.concatenate([jnp.cos(ang), jnp.ones((CTX, rot_dim), F32)], axis=0)
    sin = jnp.concatenate([jnp.sin(ang) * sign, jnp.zeros((CTX, rot_dim), F32)], axis=0)
    return jnp.concatenate([cos.T, sin.T], axis=0)


def _na_bias_tiles(rpb):
    n_l, n_h = rpb.shape[:2]
    n_off = 2 * NA_ROWS - 1
    p = jnp.pad(rpb.astype(F32) * LOG2E, ((0, 0), (0, 0), (0, 0), (48, 49)))
    sk = jnp.broadcast_to(p[..., None, :], (n_l, n_h, n_off, GRID_W, 128)).reshape(n_l, n_h, n_off, GRID_W * 128)
    sk = sk[..., :GRID_W * 127].reshape(n_l, n_h, n_off, GRID_W, 127)[..., 63:127]
    c = np.arange(GRID_W)
    win_start = np.clip(c - NA_COLS // 2, 0, GRID_W - NA_COLS)
    v_col = (c[None, :] >= win_start[:, None]) & (c[None, :] < win_start[:, None] + NA_COLS)
    tz = jnp.where(v_col, sk, NEG_INF)
    neg1 = jnp.full((n_l, n_h, 1, GRID_W, GRID_W), NEG_INF, F32)
    tzx = jnp.concatenate([neg1, tz, neg1], axis=2)
    first, second = tzx[:, :, 0:16], tzx[:, :, 1:17]
    neg16 = jnp.full_like(first, NEG_INF)
    return jnp.concatenate([
        jnp.concatenate([first, second], axis=-1), jnp.concatenate([first, neg16], axis=-1),
        jnp.concatenate([neg16, second], axis=-1), jnp.concatenate([neg1, neg1], axis=-1)], axis=2)


def kernel(x, c, ctx, c_ctx, w_mod, b_mod, g_mix, w_in, w_out, mla_q_a_g, mla_w_uq, mla_kv_a_g, mla_w_ukv,
           mla_q_g, mla_k_g, diff_q_g, diff_k_g, diff_lq1, diff_lk1, diff_lq2, diff_lk2, diff_subln_g,
           na_q_g, na_k_g, na_rpb, gqa_q_g, gqa_k_g, g_ffn, w_up, conv_w, conv_b, w_down):
    n_b = x.shape[0]
    n_layer = w_mod.shape[0]
    assert x.shape[1:] == (SEQ, D) and ctx.shape[1:] == (CTX, D)

    xs = jnp.concatenate([x, ctx], axis=1)
    mod_rows = -(-(n_b + 1) // 8) * 8
    cc = jnp.concatenate([c, c_ctx[None], jnp.zeros((mod_rows - n_b - 1, D), F32)], axis=0)
    mods_all = _modulation(cc, w_mod, b_mod).reshape(n_layer, mod_rows, 1, 6 * D)

    cs32, cs64 = _rope_table(32), _rope_table(64)
    bias_all = _na_bias_tiles(na_rpb)

    s = np.cumsum([0, 256, 128, 32, 256, 256, 256, 256, 256, 256, 256, 128, 128])
    gq_perm = np.concatenate([np.arange(GQA_D) + (2 * g + r) * GQA_D for r in range(2) for g in range(2)])

    n_l = n_layer
    gq_cols = w_in[:, :, s[9]:s[10]][:, :, gq_perm]
    wqk = jnp.concatenate([w_in[:, :, s[0]:s[3]], w_in[:, :, s[3]:s[5]], w_in[:, :, s[6]:s[8]], gq_cols,
                           w_in[:, :, s[10]:s[11]]], axis=2).swapaxes(1, 2).astype(BF16)
    wv = jnp.concatenate([w_in[:, :, s[5]:s[6]], w_in[:, :, s[8]:s[9]], w_in[:, :, s[11]:s[12]]],
                         axis=2).astype(BF16)
    wuq = mla_w_uq.swapaxes(1, 2).astype(BF16)
    wukv = mla_w_ukv.reshape(n_l, MLA_KVR, MLA_H, MLA_NOPE + MLA_V)
    wukn = wukv[..., :MLA_NOPE].reshape(n_l, MLA_KVR, MLA_H * MLA_NOPE).swapaxes(1, 2).astype(BF16)
    wuv = wukv[..., MLA_NOPE:].reshape(n_l, MLA_KVR, MLA_H * MLA_V).swapaxes(1, 2).astype(BF16)
    gcol = jnp.concatenate([mla_q_a_g, mla_kv_a_g, mla_q_g, mla_k_g, diff_q_g, diff_k_g, na_q_g, na_k_g,
                            gqa_q_g, gqa_k_g], axis=1).astype(F32)[..., None]
    lam_rows = jnp.stack([diff_lq1, diff_lk1, diff_lq2, diff_lk2], axis=1).astype(F32)
    lam_rows = jnp.concatenate([lam_rows, jnp.zeros_like(lam_rows)], axis=1)
    gsub = jnp.tile(diff_subln_g.astype(F32), (1, 2)).reshape(n_l, 1, 128)
    bounds = {
        "mla": _logit_bound(MLA_D, mla_q_g, mla_k_g),
        "diff": _logit_bound(DIFF_D, diff_q_g, diff_k_g),
        "na": _logit_bound(NA_D, na_q_g, na_k_g, LOG2E * jnp.maximum(jnp.max(na_rpb, axis=(1, 2, 3)), 0.0)),
        "gqa": _logit_bound(GQA_D, gqa_q_g, gqa_k_g),
    }
    w_out_b, w_down_b = w_out.astype(BF16), w_down.astype(BF16)
    w_up_b = _chunk_interleave(w_up.astype(BF16))
    conv_wb = _chunk_interleave(jnp.concatenate([conv_w, conv_b[:, None, :]], axis=1).astype(F32))
    gmix, gffn = g_mix.reshape(n_l, 1, D), g_ffn.reshape(n_l, 1, D)

    for l in range(n_layer):
        with_ctx = l < n_layer - 1
        lambda_init = 0.8 - 0.6 * math.exp(-0.3 * l)
        (q_mla, k_mla, v_mla, q_diff, k_diff, v_diff, q_na, k_na, v_na, q_gqa, k_gqa, v_gqa) = _inproj(
            l, xs, mods_all, gmix, wqk, wv, wuq, wukn, wuv, gcol, cs32, cs64)

        n_rb = NRB if with_ctx else NLAT
        mix_a = _attention(_mla_kernel, "attn_mla", q_mla, k_mla, v_mla, bounds["mla"][l], [], n_rb)
        mix_b = _attention(
            functools.partial(_diff_kernel, lambda_init=lambda_init), "attn_diff",
            q_diff, k_diff, v_diff, bounds["diff"][l],
            [(lam_rows, _layer_spec(lam_rows, l)), (gsub, _layer_spec(gsub, l))], n_rb)
        mix_c = _attention(_na_kernel, "attn_na", q_na, k_na, v_na, bounds["na"][l],
                           [(bias_all, _layer_spec(bias_all, l))], n_rb)
        mix_d = _attention(_gqa_kernel, "attn_gqa", q_gqa, k_gqa, v_gqa, bounds["gqa"][l], [], n_rb)

        rows_per_sample = T if with_ctx else SEQ
        x1 = _outproj(l, xs, mods_all, (mix_a, mix_b, mix_c, mix_d), w_out_b, rows_per_sample,
                      OUT_TM if with_ctx else OUT_TM_LAST)
        xs = _ffn(l, x1, mods_all, gffn, w_up_b, conv_wb, w_down_b, rows_per_sample, rows_per_sample // 2)
    return xs
```

```python
import functools
import math

import numpy as np
import jax
import jax.numpy as jnp
from jax import lax
from jax.experimental import pallas as pl
from jax.experimental.pallas import tpu as pltpu

F32 = jnp.float32
BF16 = jnp.bfloat16

D = 1024
SEQ = 2048
GRID_W = 64
CTX = 256
T = SEQ + CTX
RB = 256
NRB = T // RB
NLAT = SEQ // RB
EPS = 1e-6
NEG_INF = -1e30
ROPE_THETA = 10000.0
LOG2E = 1.4426950408889634

MLA_H, MLA_NOPE, MLA_ROPE, MLA_V, MLA_QR, MLA_KVR = 4, 64, 32, 64, 256, 128
MLA_D = MLA_NOPE + MLA_ROPE
DIFF_H, DIFF_D = 4, 32
NA_H, NA_D, NA_ROWS, NA_COLS = 4, 64, 8, 16
GQA_H, GQA_KV, GQA_D = 4, 2, 64
D_FF = 2816
FF_CHUNK = 256
N_CHUNK = D_FF // FF_CHUNK
OUT_TM, OUT_TM_LAST = 768, 1024
HALO = 16
FFN_TILES = 4
CONV_STRIP = 256

R_CQ, R_CKV, R_KR, R_DQ, R_DK, R_NQ, R_NK, R_GQ, R_GK = 0, 256, 384, 416, 672, 928, 1184, 1440, 1696
QK_ROWS = 1824
V_COLS = 640

G_QA, G_KVA, G_MQ, G_MK, G_DQ, G_DK, G_NQ, G_NK, G_GQ, G_GK = 0, 256, 384, 480, 576, 608, 640, 704, 768, 832
G_ROWS = 896

VMEM_LIMIT = 56 * 1024 * 1024


def _params(sem):
    return pltpu.CompilerParams(dimension_semantics=sem, vmem_limit_bytes=VMEM_LIMIT)


def _dot(a, b):
    return jnp.dot(a, b, preferred_element_type=F32)


def _dot_nt(a, b):
    return lax.dot_general(a, b, (((1,), (1,)), ((), ())), preferred_element_type=F32)


def _mod_kernel(c_ref, w_ref, b_ref, o_ref):
    c = c_ref[...]
    a = (c * jax.nn.sigmoid(c)).astype(BF16)
    o_ref[0] = _dot(a, w_ref[0].astype(BF16)) + b_ref[0]


def _modulation(cc, w_mod, b_mod):
    n_layer = w_mod.shape[0]
    rows = cc.shape[0]
    return pl.pallas_call(
        _mod_kernel,
        grid=(n_layer, 6),
        in_specs=[
            pl.BlockSpec((rows, D), lambda l, j: (0, 0)),
            pl.BlockSpec((1, D, D), lambda l, j: (l, 0, j)),
            pl.BlockSpec((1, 1, D), lambda l, j: (l, 0, j)),
        ],
        out_specs=pl.BlockSpec((1, rows, D), lambda l, j: (l, 0, j)),
        out_shape=jax.ShapeDtypeStruct((n_layer, rows, 6 * D), F32),
        compiler_params=_params(("parallel", "parallel")),
        name="modulation",
    )(cc, w_mod, b_mod.reshape(n_layer, 1, 6 * D))


def _rms_rows(x, g, n):
    ss = jnp.sum(x * x, axis=0, keepdims=True) * (1.0 / n)
    return x * lax.rsqrt(ss + EPS) * g


def _rope_rows(x, cos, sin_signed, w):
    rot = jnp.concatenate([x[w:2 * w], x[0:w], x[3 * w:4 * w], x[2 * w:3 * w]], axis=0)
    return x * cos + rot * sin_signed


def _inproj_kernel(x_ref, mod_ref, gmix_ref, wqk_ref, wv_ref, wuq_ref, wukn_ref, wuv_ref, gcol_ref,
                   cs32_ref, cs64_ref,
                   qmla_ref, kmla_ref, vmla_ref, qdiff_ref, kdiff_ref, vdiff_ref,
                   qna_ref, kna_ref, vna_ref, qgqa_ref, kgqa_ref, vgqa_ref):
    x = x_ref[0]
    mod = mod_ref[0]
    shift, scale = mod[:, 0:D], mod[:, D:2 * D]
    ms = jnp.mean(x * x, axis=-1, keepdims=True)
    h = x * lax.rsqrt(ms + EPS) * gmix_ref[...]
    hb = (h * (1.0 + scale) + shift).astype(BF16)

    def project(lo, hi):
        return _dot_nt(wqk_ref[lo:hi, :], hb)

    def gain(off, n):
        return gcol_ref[off:off + n, :]

    cos32, sin32 = cs32_ref[0:32, :], cs32_ref[32:64, :]
    cos64, sin64 = cs64_ref[0:64, :], cs64_ref[64:128, :]

    def values():
        pv = _dot(hb, wv_ref[...])
        vdiff_ref[0] = pv[:, 0:256].astype(BF16)
        vna_ref[0] = pv[:, 256:512].astype(BF16)
        vgqa_ref[0] = pv[:, 512:640].astype(BF16)

    def mla(pt):
        cq = _rms_rows(pt[0:MLA_QR], gain(G_QA, MLA_QR), MLA_QR).astype(BF16)
        qt = _dot(wuq_ref[...], cq)
        ckv = _rms_rows(pt[R_CKV:R_CKV + MLA_KVR], gain(G_KVA, MLA_KVR), MLA_KVR).astype(BF16)
        knt = _dot(wukn_ref[...], ckv)
        vt = _dot(wuv_ref[...], ckv)
        vmla_ref[0] = vt.T.astype(BF16)
        kr = pt[R_KR:R_KR + MLA_ROPE]
        kr_ss = jnp.sum(kr * kr, axis=0, keepdims=True)
        g_mq, g_mk = gain(G_MQ, MLA_D), gain(G_MK, MLA_D)
        zpad = jnp.zeros((128 - MLA_D, RB), F32)
        q_parts = []
        for hd in range(MLA_H):
            qh = _rms_rows(qt[hd * MLA_D:(hd + 1) * MLA_D], g_mq, MLA_D)
            q_rope = _rope_rows(qh[MLA_NOPE:], cos32, sin32, MLA_ROPE // 4)
            q_parts += [qh[:MLA_NOPE], q_rope, zpad]
            kn = knt[hd * MLA_NOPE:(hd + 1) * MLA_NOPE]
            ss = (jnp.sum(kn * kn, axis=0, keepdims=True) + kr_ss) * (1.0 / MLA_D)
            r = lax.rsqrt(ss + EPS)
            k_rope = _rope_rows(kr * r * g_mk[MLA_NOPE:], cos32, sin32, MLA_ROPE // 4)
            kmla_ref[0, hd * 128:(hd + 1) * 128, :] = jnp.concatenate(
                [kn * r * g_mk[:MLA_NOPE], k_rope, zpad], axis=0).astype(BF16)
        q_all = jnp.concatenate(q_parts, axis=0) * (MLA_D ** -0.5 * LOG2E)
        qmla_ref[0] = q_all.T.astype(BF16)

    def diff(pt):
        g_dq, g_dk = gain(G_DQ, DIFF_D), gain(G_DK, DIFF_D)
        q_parts, k_parts = [], []
        for gi in range(2 * DIFF_H):
            qg = _rms_rows(pt[gi * DIFF_D:(gi + 1) * DIFF_D], g_dq, DIFF_D)
            q_parts.append(_rope_rows(qg, cos32, sin32, DIFF_D // 4))
            kg = _rms_rows(pt[256 + gi * DIFF_D:256 + (gi + 1) * DIFF_D], g_dk, DIFF_D)
            k_parts.append(_rope_rows(kg, cos32, sin32, DIFF_D // 4))
        qdiff_ref[0] = (jnp.concatenate(q_parts, axis=0) * (DIFF_D ** -0.5 * LOG2E)).T.astype(BF16)
        kdiff_ref[0] = jnp.concatenate(k_parts, axis=0).astype(BF16)

    def na(pt):
        g_nq, g_nk = gain(G_NQ, NA_D), gain(G_NK, NA_D)
        q_parts, k_parts = [], []
        for hd in range(NA_H):
            q_parts.append(_rms_rows(pt[hd * NA_D:(hd + 1) * NA_D], g_nq, NA_D))
            k_parts.append(_rms_rows(pt[256 + hd * NA_D:256 + (hd + 1) * NA_D], g_nk, NA_D))
        qna_ref[0] = (jnp.concatenate(q_parts, axis=0) * (NA_D ** -0.5 * LOG2E)).T.astype(BF16)
        kna_ref[0] = jnp.concatenate(k_parts, axis=0).astype(BF16)

    def gqa(pt):
        g_gq, g_gk = gain(G_GQ, GQA_D), gain(G_GK, GQA_D)
        q_parts, k_parts = [], []
        for hd in range(GQA_H):
            qg = _rms_rows(pt[hd * GQA_D:(hd + 1) * GQA_D], g_gq, GQA_D)
            q_parts.append(_rope_rows(qg, cos64, sin64, GQA_D // 4))
        for hd in range(GQA_KV):
            kg = _rms_rows(pt[256 + hd * GQA_D:256 + (hd + 1) * GQA_D], g_gk, GQA_D)
            k_parts.append(_rope_rows(kg, cos64, sin64, GQA_D // 4))
        qgqa_ref[0] = (jnp.concatenate(q_parts, axis=0) * (GQA_D ** -0.5 * LOG2E)).T.astype(BF16)
        kgqa_ref[0] = jnp.concatenate(k_parts, axis=0).astype(BF16)

    pt_mla = project(R_CQ, R_DQ)
    pt_diff = project(R_DQ, R_NQ)
    mla(pt_mla)
    pt_na = project(R_NQ, R_GQ)
    diff(pt_diff)
    pt_gqa = project(R_GQ, QK_ROWS)
    na(pt_na)
    values()
    gqa(pt_gqa)


def _inproj(l, xs, mods, gmix, wqk, wv, wuq, wukn, wuv, gcol, cs32, cs64):
    n_b = xs.shape[0]

    def full(a):
        return _layer_spec(a, l)

    def tok(width):
        return pl.BlockSpec((1, RB, width), lambda r, b: (b, r, 0))

    def chan(rows):
        return pl.BlockSpec((1, rows, RB), lambda r, b: (b, 0, r))

    def tshape(width):
        return jax.ShapeDtypeStruct((n_b, T, width), BF16)

    def cshape(rows):
        return jax.ShapeDtypeStruct((n_b, rows, T), BF16)

    return pl.pallas_call(
        _inproj_kernel,
        grid=(NRB, n_b),
        in_specs=[
            pl.BlockSpec((1, RB, D), lambda r, b: (b, r, 0)),
            pl.BlockSpec((None, 1, 1, 6 * D), lambda r, b: (l, jnp.where(r < NLAT, b, n_b), 0, 0)),
            full(gmix), full(wqk), full(wv), full(wuq), full(wukn), full(wuv), full(gcol),
            pl.BlockSpec((64, RB), lambda r, b: (0, r)),
            pl.BlockSpec((128, RB), lambda r, b: (0, r)),
        ],
        out_specs=[tok(512), chan(512), tok(256), tok(256), chan(256), tok(256),
                   tok(256), chan(256), tok(256), tok(256), chan(128), tok(128)],
        out_shape=[tshape(512), cshape(512), tshape(256), tshape(256), cshape(256), tshape(256),
                   tshape(256), cshape(256), tshape(256), tshape(256), cshape(128), tshape(128)],
        compiler_params=_params(("parallel", "parallel")),
        name="inproj_prep",
    )(xs, mods, gmix, wqk, wv, wuq, wukn, wuv, gcol, cs32, cs64)


def _lane_id(shape):
    return lax.broadcasted_iota(jnp.int32, shape, 1)


KEY_CHUNK = 768
LAT_CHUNKS = tuple((lo, KEY_CHUNK) for lo in range(0, T, KEY_CHUNK))
CTX_CHUNKS = ((SEQ, CTX),)


def _k_slab(k_ref, idx):
    return lambda lo, n: k_ref[0, idx * 128:(idx + 1) * 128, pl.ds(lo, n)]


def _v_slab(v_ref, idx):
    return lambda lo, n: v_ref[0, pl.ds(lo, n), idx * 128:(idx + 1) * 128]


def _attend(units, chunks, shift=None):
    def scores(i, c):
        q, keys, _, bias = units[i]
        lo, n = chunks[c]
        s = _dot(q, keys(lo, n))
        b = None if bias is None else bias(c)
        return s if b is None else s + b

    if shift is not None:
        items = [(i, c) for i in range(len(units)) for c in range(len(chunks))]
        acc, l = [None] * len(units), [None] * len(units)
        nxt = scores(*items[0])
        for idx, (i, c) in enumerate(items):
            cur = nxt
            if idx + 1 < len(items):
                nxt = scores(*items[idx + 1])
            e = jnp.exp2(cur - shift)
            lc = jnp.sum(e, axis=-1, keepdims=True)
            oc = _dot(e.astype(BF16), units[i][2](*chunks[c]))
            acc[i] = oc if acc[i] is None else acc[i] + oc
            l[i] = lc if l[i] is None else l[i] + lc
        return list(zip(acc, l))

    cur = [scores(0, c) for c in range(len(chunks))]
    out = []
    for i in range(len(units)):
        m = functools.reduce(jnp.maximum, [jnp.max(s, axis=-1, keepdims=True) for s in cur])
        nxt, acc, l = [], None, None
        for c, (lo, n) in enumerate(chunks):
            if i + 1 < len(units):
                nxt.append(scores(i + 1, c))
            e = jnp.exp2(cur[c] - m)
            lc = jnp.sum(e, axis=-1, keepdims=True)
            oc = _dot(e.astype(BF16), units[i][2](lo, n))
            acc = oc if acc is None else acc + oc
            l = lc if l is None else l + lc
        out.append((acc, l))
        cur = nxt
    return out


def _with_shift(bnd_ref, run):
    @pl.when(bnd_ref[1] > 0.5)
    def _():
        run(bnd_ref[0])

    @pl.when(bnd_ref[1] <= 0.5)
    def _():
        run(None)


def _query_blocks(n_rb, bnd_ref, run):
    def all_blocks(shift):
        def body(rb, carry):
            run(rb, pl.ds(pl.multiple_of(rb * RB, RB), RB), True, shift)
            return carry

        lax.fori_loop(0, NLAT, body, 0)
        if n_rb > NLAT:
            run(NLAT, pl.ds(SEQ, CTX), False, shift)

    _with_shift(bnd_ref, all_blocks)


def _mla_kernel(q_ref, k_ref, v_ref, bnd_ref, o_ref, *, n_rb):
    lane = _lane_id((RB, 128))

    def run(rb, rows, latent, shift):
        units = [(q_ref[0, rows,hd * 128:(hd + 1) * 128], _k_slab(k_ref, hd), _v_slab(v_ref, hd // 2), None)
                 for hd in range(MLA_H)]
        o = [acc * (1.0 / l) for acc, l in _attend(units, LAT_CHUNKS if latent else CTX_CHUNKS, shift)]
        outs = [jnp.where(lane < 64, o[0], o[1]), jnp.where(lane < 64, o[2], o[3])]
        o_ref[0, rows, :] = jnp.concatenate(outs, axis=1).astype(o_ref.dtype)

    _query_blocks(n_rb, bnd_ref, run)


def _gqa_kernel(q_ref, k_ref, v_ref, bnd_ref, o_ref, *, n_rb):
    lane = _lane_id((RB, 128))

    def run(rb, rows, latent, shift):
        order = [(rep, grp) for rep in range(2) for grp in range(2)]
        units = []
        for rep, grp in order:
            qs = q_ref[0, rows,rep * 128:(rep + 1) * 128]
            qm = jnp.where((lane >= 64) == (grp == 1), qs, jnp.zeros_like(qs))
            units.append((qm, _k_slab(k_ref, 0), _v_slab(v_ref, 0), None))
        res = {}
        for (rep, grp), (acc, l) in zip(order, _attend(units, LAT_CHUNKS if latent else CTX_CHUNKS, shift)):
            o = acc * (1.0 / l)
            res[(grp, rep)] = o if grp == rep else pltpu.roll(o, 64, axis=1)
        outs = [jnp.where(lane < 64, res[(grp, 0)], res[(grp, 1)]) for grp in range(2)]
        o_ref[0, rows, :] = jnp.concatenate(outs, axis=1).astype(o_ref.dtype)

    _query_blocks(n_rb, bnd_ref, run)


def _diff_kernel(q_ref, k_ref, v_ref, bnd_ref, lam_ref, g_ref, o_ref, *, n_rb, lambda_init):
    lane = _lane_id((RB, 128))
    lq1, lk1, lq2, lk2 = lam_ref[0:1, :], lam_ref[1:2, :], lam_ref[2:3, :], lam_ref[3:4, :]
    lam = (jnp.exp(jnp.sum(lq1 * lk1, axis=-1, keepdims=True))
           - jnp.exp(jnp.sum(lq2 * lk2, axis=-1, keepdims=True)) + lambda_init)
    gsub = g_ref[...]
    grp = lane // DIFF_D

    def run(rb, rows, latent, shift):
        units = []
        for hd in range(DIFF_H):
            pair, sub = divmod(hd, 2)
            qs = q_ref[0, rows,pair * 128:(pair + 1) * 128]
            for which in range(2):
                qm = jnp.where(grp == 2 * sub + which, qs, jnp.zeros_like(qs))
                units.append((qm, _k_slab(k_ref, pair), _v_slab(v_ref, pair), None))
        res = _attend(units, LAT_CHUNKS if latent else CTX_CHUNKS, shift)
        outs = []
        for pair in range(2):
            halves = []
            for sub in range(2):
                (a1, l1), (a2, l2) = res[2 * (2 * pair + sub)], res[2 * (2 * pair + sub) + 1]
                halves.append(a1 * (1.0 / l1) - a2 * (lam / l2))
            o = jnp.where(lane < 64, halves[0], halves[1])
            o2 = o * o
            ss0 = jnp.sum(jnp.where(lane < 64, o2, 0.0), axis=-1, keepdims=True)
            ss1 = jnp.sum(jnp.where(lane < 64, 0.0, o2), axis=-1, keepdims=True)
            ss = jnp.where(lane < 64, ss0, ss1) * (1.0 / (2 * DIFF_D))
            outs.append(o * lax.rsqrt(ss + EPS) * gsub * (1.0 - lambda_init))
        o_ref[0, rows, :] = jnp.concatenate(outs, axis=1).astype(o_ref.dtype)

    _query_blocks(n_rb, bnd_ref, run)


NA_WIN = 768
NA_CHUNK = 768
NA_TILES = 49
GRID_ROWS = SEQ // GRID_W


def _na_kernel(q_ref, k_ref, v_ref, bnd_ref, bias_ref, o_ref, *, n_rb):
    lane = _lane_id((RB, 128))

    def run(rb, rows, latent, shift):
        na_chunks, tile_idx = CTX_CHUNKS, None
        if latent:
            win_row = jnp.clip(4 * rb - 4, 0, GRID_ROWS - NA_WIN // GRID_W)
            off = pl.multiple_of(win_row * GRID_W, RB)
            local = [(pl.multiple_of(off + c * NA_CHUNK, RB), NA_CHUNK) for c in range(NA_WIN // NA_CHUNK)]
            na_chunks = local + list(CTX_CHUNKS)
            tile_idx = []
            for c in range(NA_WIN // NA_CHUNK):
                per_row = []
                for ri in range(RB // GRID_W):
                    r = 4 * rb + ri
                    row_start = jnp.clip(r - NA_ROWS // 2, 0, GRID_ROWS - NA_ROWS)
                    idxs = []
                    for p in range(NA_CHUNK // 128):
                        kr = win_row + c * (NA_CHUNK // GRID_W) + 2 * p
                        ok1 = (kr >= row_start) & (kr < row_start + NA_ROWS)
                        ok2 = (kr + 1 >= row_start) & (kr + 1 < row_start + NA_ROWS)
                        e = jnp.clip(kr - r + NA_ROWS, 0, 15)
                        idxs.append(jnp.where(ok1 & ok2, e, jnp.where(ok1, 16 + e, jnp.where(ok2, 32 + e, 48))))
                    per_row.append(idxs)
                tile_idx.append(per_row)
        units = []
        for hd in range(NA_H):
            pair, sub = divmod(hd, 2)
            qs = q_ref[0, rows,pair * 128:(pair + 1) * 128]
            qm = jnp.where((lane >= 64) == (sub == 1), qs, jnp.zeros_like(qs))

            def bias(c, hd=hd):
                if tile_idx is None or c >= NA_WIN // NA_CHUNK:
                    return None
                return jnp.concatenate(
                    [jnp.concatenate([bias_ref[hd, idx] for idx in row], axis=1) for row in tile_idx[c]], axis=0)

            units.append((qm, _k_slab(k_ref, pair), _v_slab(v_ref, pair), bias))
        o = [acc * (1.0 / l) for acc, l in _attend(units, na_chunks, shift)]
        outs = [jnp.where(lane < 64, o[0], o[1]), jnp.where(lane < 64, o[2], o[3])]
        o_ref[0, rows, :] = jnp.concatenate(outs, axis=1).astype(o_ref.dtype)

    _query_blocks(n_rb, bnd_ref, run)


SHIFT_MAX = 40.0


def _logit_bound(d, q_gain, k_gain, bias_max=0.0):
    bound = (1.01 * math.sqrt(d) * LOG2E * jnp.max(jnp.abs(q_gain), axis=-1) * jnp.max(jnp.abs(k_gain), axis=-1)
             + bias_max)
    return jnp.stack([bound, (bound <= SHIFT_MAX).astype(F32)], axis=-1).astype(F32)


def _layer_spec(a, l):
    return pl.BlockSpec((None,) + a.shape[1:], lambda *_, _n=a.ndim - 1: (l,) + (0,) * _n)


def _attention(body, name, q, k, v, bound, extra, n_rb, out_width=256):
    n_b = q.shape[0]
    extra = [(bound, pl.BlockSpec(memory_space=pltpu.SMEM))] + list(extra)
    in_specs = [
        pl.BlockSpec((1, T, q.shape[2]), lambda b: (b, 0, 0)),
        pl.BlockSpec((1, k.shape[1], T), lambda b: (b, 0, 0)),
        pl.BlockSpec((1, T, v.shape[2]), lambda b: (b, 0, 0)),
    ] + [spec for _, spec in extra]
    return pl.pallas_call(
        functools.partial(body, n_rb=n_rb),
        grid=(n_b,),
        in_specs=in_specs,
        out_specs=pl.BlockSpec((1, n_rb * RB, out_width), lambda b: (b, 0, 0)),
        out_shape=jax.ShapeDtypeStruct((n_b, n_rb * RB, out_width), BF16),
        compiler_params=_params(("parallel",)),
        name=name,
    )(q, k, v, *[a for a, _ in extra])


def _outproj_kernel(x_ref, modb_ref, modc_ref, ma_ref, mb_ref, mc_ref, md_ref, w_ref, o_ref, *, tm):
    rows = pl.program_id(1) * tm + lax.broadcasted_iota(jnp.int32, (tm, 1), 0)
    g1 = jnp.where(rows >= SEQ, modc_ref[0][:, 2 * D:3 * D], modb_ref[0][:, 2 * D:3 * D])
    acc = _dot(ma_ref[0], w_ref[0:256, :])
    acc += _dot(mb_ref[0], w_ref[256:512, :])
    acc += _dot(mc_ref[0], w_ref[512:768, :])
    acc += _dot(md_ref[0], w_ref[768:1024, :])
    o_ref[0] = x_ref[0] + g1 * acc


def _outproj(l, xs, mods, mixes, w_out, rows_per_sample, tm):
    n_b = xs.shape[0]
    mix_spec = pl.BlockSpec((1, tm, 256), lambda b, r: (b, r, 0))
    return pl.pallas_call(
        functools.partial(_outproj_kernel, tm=tm),
        grid=(n_b, rows_per_sample // tm),
        in_specs=[
            pl.BlockSpec((1, tm, D), lambda b, r: (b, r, 0)),
            pl.BlockSpec((None, 1, 1, 6 * D), lambda b, r: (l, b, 0, 0)),
            pl.BlockSpec((None, 1, 1, 6 * D), lambda b, r: (l, n_b, 0, 0)),
            mix_spec, mix_spec, mix_spec, mix_spec,
            _layer_spec(w_out, l),
        ],
        out_specs=pl.BlockSpec((1, tm, D), lambda b, r: (b, r, 0)),
        out_shape=jax.ShapeDtypeStruct((n_b, rows_per_sample, D), F32),
        compiler_params=_params(("parallel", "parallel")),
        name="outproj",
    )(xs, mods, mods, *mixes, w_out)


def _ffn_kernel(x_ref, hb_ref, ha_ref, modb_ref, modc_ref, g_ref, wa_ref, wg_ref, cw_ref, wd_ref, o_ref,
                h_scr, u0_scr, u1_scr, act_scr, *, tm, rows_per_sample):
    rb = pl.program_id(1)
    j = pl.program_id(2)
    row0 = rb * tm
    split = SEQ % tm if rows_per_sample > SEQ else 0
    band_lo, band_hi = split - HALO, split + HALO

    def per_row(rows, col):
        lat = modb_ref[0][:, col * D:(col + 1) * D]
        if rows_per_sample == SEQ:
            return lat
        return jnp.where(rows >= SEQ, modc_ref[0][:, col * D:(col + 1) * D], lat)

    def modulated(x, rows):
        ms = jnp.mean(x * x, axis=-1, keepdims=True)
        y = x * lax.rsqrt(ms + EPS) * g_ref[...]
        return (y * (1.0 + per_row(rows, 4)) + per_row(rows, 3)).astype(BF16)

    def prologue():
        halo_rows = lax.broadcasted_iota(jnp.int32, (HALO, 1), 0)
        has_before = (row0 != 0) & (row0 != SEQ)
        has_after = (row0 + tm != SEQ) & (row0 + tm != rows_per_sample)
        zero = jnp.zeros((HALO, D), BF16)
        h_scr[0:HALO, :] = jnp.where(has_before, modulated(hb_ref[0], row0 - HALO + halo_rows), zero)
        h_scr[HALO + tm:, :] = jnp.where(has_after, modulated(ha_ref[0], row0 + tm + halo_rows), zero)
        main_rows = row0 + lax.broadcasted_iota(jnp.int32, (tm, 1), 0)
        h_scr[HALO:HALO + tm, :] = modulated(x_ref[0], main_rows)
        o_ref[0] = jnp.zeros((tm, D), F32)

    slots = (u0_scr, u1_scr)

    def tiles(total):
        units = total // HALO
        sizes = [(units // FFN_TILES + (1 if t < units % FFN_TILES else 0)) * HALO for t in range(FFN_TILES)]
        edges = np.cumsum([0] + sizes)
        return [(int(edges[t]), int(edges[t + 1])) for t in range(FFN_TILES)]

    up_tiles, down_tiles = tiles(tm + 2 * HALO), tiles(tm)

    def up(slot, t):
        lo, hi = up_tiles[t]
        slots[slot][lo:hi, :FF_CHUNK] = _dot(h_scr[lo:hi, :], wa_ref[...])
        slots[slot][lo:hi, FF_CHUNK:] = _dot(h_scr[lo:hi, :], wg_ref[...])

    def conv(u_scr, lo, hi, c_lo, c_hi, masked):
        prev = u_scr[HALO - 1 + lo:HALO - 1 + hi, c_lo:c_hi]
        nxt = u_scr[HALO + 1 + lo:HALO + 1 + hi, c_lo:c_hi]
        if masked:
            rows = row0 + lo + lax.broadcasted_iota(jnp.int32, (hi - lo, 1), 0)
            prev = jnp.where(rows != SEQ, prev, 0.0)
            nxt = jnp.where(rows != SEQ - 1, nxt, 0.0)
        return (prev * cw_ref[0:1, c_lo:c_hi] + u_scr[HALO + lo:HALO + hi, c_lo:c_hi] * cw_ref[1:2, c_lo:c_hi]
                + nxt * cw_ref[2:3, c_lo:c_hi] + cw_ref[3:4, c_lo:c_hi])

    def conv_down(slot, t):
        t_lo, t_hi = down_tiles[t]
        cuts = sorted({t_lo, t_hi} | ({c for c in (band_lo, band_hi) if t_lo < c < t_hi} if split else set()))
        for lo, hi in zip(cuts[:-1], cuts[1:]):
            masked = bool(split) and band_lo <= lo < band_hi
            for c_lo in range(0, FF_CHUNK, CONV_STRIP):
                a = conv(slots[slot], lo, hi, c_lo, c_lo + CONV_STRIP, masked)
                g = conv(slots[slot], lo, hi, FF_CHUNK + c_lo, FF_CHUNK + c_lo + CONV_STRIP, masked)
                act_scr[lo:hi, c_lo:c_lo + CONV_STRIP] = (g * jax.nn.sigmoid(g) * a).astype(BF16)
        o_ref[0, t_lo:t_hi, :] += _dot(act_scr[t_lo:t_hi, :], wd_ref[...])

    @pl.when(j == 0)
    def _():
        prologue()
        for t in range(FFN_TILES):
            up(0, t)

    for parity in range(2):
        @pl.when((j > 0) & (j < N_CHUNK) & (j % 2 == parity))
        def _():
            for t in range(FFN_TILES):
                up(parity, t)
                conv_down(1 - parity, t)

    @pl.when(j == N_CHUNK)
    def _():
        for t in range(FFN_TILES):
            conv_down((N_CHUNK - 1) % 2, t)
        rows = row0 + lax.broadcasted_iota(jnp.int32, (tm, 1), 0)
        o_ref[0] = x_ref[0] + per_row(rows, 5) * o_ref[0]


def _chunk_interleave(a):
    parts = []
    for j in range(N_CHUNK):
        parts += [a[..., j * FF_CHUNK:(j + 1) * FF_CHUNK], a[..., D_FF + j * FF_CHUNK:D_FF + (j + 1) * FF_CHUNK]]
    return jnp.concatenate(parts, axis=-1)


def _ffn(l, x1, mods, g_ffn, w_up, conv_wb, w_down, rows_per_sample, tm):
    n_b = x1.shape[0]
    n_rb = rows_per_sample // tm
    halo_per_block = tm // HALO
    n_halo = x1.shape[1] // HALO
    body = functools.partial(_ffn_kernel, tm=tm, rows_per_sample=rows_per_sample)

    def up_idx(j):
        return jnp.minimum(j, N_CHUNK - 1)

    def down_idx(j):
        return jnp.maximum(j - 1, 0)

    return pl.pallas_call(
        body,
        grid=(n_b, n_rb, N_CHUNK + 1),
        in_specs=[
            pl.BlockSpec((1, tm, D), lambda b, r, j: (b, r, 0)),
            pl.BlockSpec((1, HALO, D), lambda b, r, j: (b, jnp.maximum(r * halo_per_block - 1, 0), 0)),
            pl.BlockSpec((1, HALO, D), lambda b, r, j: (b, jnp.minimum((r + 1) * halo_per_block, n_halo - 1), 0)),
            pl.BlockSpec((None, 1, 1, 6 * D), lambda b, r, j: (l, b, 0, 0)),
            pl.BlockSpec((None, 1, 1, 6 * D), lambda b, r, j: (l, n_b, 0, 0)),
            _layer_spec(g_ffn, l),
            pl.BlockSpec((None, D, FF_CHUNK), lambda b, r, j: (l, 0, up_idx(j))),
            pl.BlockSpec((None, D, FF_CHUNK), lambda b, r, j: (l, 0, N_CHUNK + up_idx(j))),
            pl.BlockSpec((None, 4, 2 * FF_CHUNK), lambda b, r, j: (l, 0, down_idx(j))),
            pl.BlockSpec((None, FF_CHUNK, D), lambda b, r, j: (l, down_idx(j), 0)),
        ],
        out_specs=pl.BlockSpec((1, tm, D), lambda b, r, j: (b, r, 0)),
        out_shape=jax.ShapeDtypeStruct((n_b, rows_per_sample, D), F32),
        scratch_shapes=[
            pltpu.VMEM((tm + 2 * HALO, D), BF16),
            pltpu.VMEM((tm + 2 * HALO, 2 * FF_CHUNK), F32),
            pltpu.VMEM((tm + 2 * HALO, 2 * FF_CHUNK), F32),
            pltpu.VMEM((tm, FF_CHUNK), BF16),
        ],
        compiler_params=_params(("parallel", "parallel", "arbitrary")),
        name="conv_ffn",
    )(x1, x1, x1, mods, mods, g_ffn, w_up, w_up, conv_wb, w_down)


def _rope_table(rot_dim):
    n_freq = rot_dim // 4
    inv = jnp.power(ROPE_THETA, -jnp.arange(n_freq, dtype=F32) / n_freq)
    t = jnp.arange(SEQ)
    row = (t // GRID_W).astype(F32)
    col = (t % GRID_W).astype(F32)
    ar, ac = row[:, None] * inv, col[:, None] * inv
    ang = jnp.concatenate([ar, ar, ac, ac], axis=-1)
    sign = jnp.concatenate([-jnp.ones(n_freq), jnp.ones(n_freq), -jnp.ones(n_freq), jnp.ones(n_freq)]).astype(F32)
    cos = jnp.concatenate([jnp.cos(ang), jnp.ones((CTX, rot_dim), F32)], axis=0)
    sin = jnp.concatenate([jnp.sin(ang) * sign, jnp.zeros((CTX, rot_dim), F32)], axis=0)
    return jnp.concatenate([cos.T, sin.T], axis=0)


def _na_bias_tiles(rpb):
    n_l, n_h = rpb.shape[:2]
    n_off = 2 * NA_ROWS - 1
    p = jnp.pad(rpb.astype(F32) * LOG2E, ((0, 0), (0, 0), (0, 0), (48, 49)))
    sk = jnp.broadcast_to(p[..., None, :], (n_l, n_h, n_off, GRID_W, 128)).reshape(n_l, n_h, n_off, GRID_W * 128)
    sk = sk[..., :GRID_W * 127].reshape(n_l, n_h, n_off, GRID_W, 127)[..., 63:127]
    c = np.arange(GRID_W)
    win_start = np.clip(c - NA_COLS // 2, 0, GRID_W - NA_COLS)
    v_col = (c[None, :] >= win_start[:, None]) & (c[None, :] < win_start[:, None] + NA_COLS)
    tz = jnp.where(v_col, sk, NEG_INF)
    neg1 = jnp.full((n_l, n_h, 1, GRID_W, GRID_W), NEG_INF, F32)
    tzx = jnp.concatenate([neg1, tz, neg1], axis=2)
    first, second = tzx[:, :, 0:16], tzx[:, :, 1:17]
    neg16 = jnp.full_like(first, NEG_INF)
    return jnp.concatenate([
        jnp.concatenate([first, second], axis=-1), jnp.concatenate([first, neg16], axis=-1),
        jnp.concatenate([neg16, second], axis=-1), jnp.concatenate([neg1, neg1], axis=-1)], axis=2)


def kernel(x, c, ctx, c_ctx, w_mod, b_mod, g_mix, w_in, w_out, mla_q_a_g, mla_w_uq, mla_kv_a_g, mla_w_ukv,
           mla_q_g, mla_k_g, diff_q_g, diff_k_g, diff_lq1, diff_lk1, diff_lq2, diff_lk2, diff_subln_g,
           na_q_g, na_k_g, na_rpb, gqa_q_g, gqa_k_g, g_ffn, w_up, conv_w, conv_b, w_down):
    n_b = x.shape[0]
    n_layer = w_mod.shape[0]
    assert x.shape[1:] == (SEQ, D) and ctx.shape[1:] == (CTX, D)

    xs = jnp.concatenate([x, ctx], axis=1)
    mod_rows = -(-(n_b + 1) // 8) * 8
    cc = jnp.concatenate([c, c_ctx[None], jnp.zeros((mod_rows - n_b - 1, D), F32)], axis=0)
    mods_all = _modulation(cc, w_mod, b_mod).reshape(n_layer, mod_rows, 1, 6 * D)

    cs32, cs64 = _rope_table(32), _rope_table(64)
    bias_all = _na_bias_tiles(na_rpb)

    s = np.cumsum([0, 256, 128, 32, 256, 256, 256, 256, 256, 256, 256, 128, 128])
    gq_perm = np.concatenate([np.arange(GQA_D) + (2 * g + r) * GQA_D for r in range(2) for g in range(2)])

    n_l = n_layer
    gq_cols = w_in[:, :, s[9]:s[10]][:, :, gq_perm]
    wqk = jnp.concatenate([w_in[:, :, s[0]:s[3]], w_in[:, :, s[3]:s[5]], w_in[:, :, s[6]:s[8]], gq_cols,
                           w_in[:, :, s[10]:s[11]]], axis=2).swapaxes(1, 2).astype(BF16)
    wv = jnp.concatenate([w_in[:, :, s[5]:s[6]], w_in[:, :, s[8]:s[9]], w_in[:, :, s[11]:s[12]]],
                         axis=2).astype(BF16)
    wuq = mla_w_uq.swapaxes(1, 2).astype(BF16)
    wukv = mla_w_ukv.reshape(n_l, MLA_KVR, MLA_H, MLA_NOPE + MLA_V)
    wukn = wukv[..., :MLA_NOPE].reshape(n_l, MLA_KVR, MLA_H * MLA_NOPE).swapaxes(1, 2).astype(BF16)
    wuv = wukv[..., MLA_NOPE:].reshape(n_l, MLA_KVR, MLA_H * MLA_V).swapaxes(1, 2).astype(BF16)
    gcol = jnp.concatenate([mla_q_a_g, mla_kv_a_g, mla_q_g, mla_k_g, diff_q_g, diff_k_g, na_q_g, na_k_g,
                            gqa_q_g, gqa_k_g], axis=1).astype(F32)[..., None]
    lam_rows = jnp.stack([diff_lq1, diff_lk1, diff_lq2, diff_lk2], axis=1).astype(F32)
    lam_rows = jnp.concatenate([lam_rows, jnp.zeros_like(lam_rows)], axis=1)
    gsub = jnp.tile(diff_subln_g.astype(F32), (1, 2)).reshape(n_l, 1, 128)
    bounds = {
        "mla": _logit_bound(MLA_D, mla_q_g, mla_k_g),
        "diff": _logit_bound(DIFF_D, diff_q_g, diff_k_g),
        "na": _logit_bound(NA_D, na_q_g, na_k_g, LOG2E * jnp.maximum(jnp.max(na_rpb, axis=(1, 2, 3)), 0.0)),
        "gqa": _logit_bound(GQA_D, gqa_q_g, gqa_k_g),
    }
    w_out_b, w_down_b = w_out.astype(BF16), w_down.astype(BF16)
    w_up_b = w_up.astype(BF16)
    conv_wb = _chunk_interleave(jnp.concatenate([conv_w, conv_b[:, None, :]], axis=1).astype(F32))
    gmix, gffn = g_mix.reshape(n_l, 1, D), g_ffn.reshape(n_l, 1, D)

    for l in range(n_layer):
        with_ctx = l < n_layer - 1
        lambda_init = 0.8 - 0.6 * math.exp(-0.3 * l)
        (q_mla, k_mla, v_mla, q_diff, k_diff, v_diff, q_na, k_na, v_na, q_gqa, k_gqa, v_gqa) = _inproj(
            l, xs, mods_all, gmix, wqk, wv, wuq, wukn, wuv, gcol, cs32, cs64)

        n_rb = NRB if with_ctx else NLAT
        mix_a = _attention(_mla_kernel, "attn_mla", q_mla, k_mla, v_mla, bounds["mla"][l], [], n_rb)
        mix_b = _attention(
            functools.partial(_diff_kernel, lambda_init=lambda_init), "attn_diff",
            q_diff, k_diff, v_diff, bounds["diff"][l],
            [(lam_rows, _layer_spec(lam_rows, l)), (gsub, _layer_spec(gsub, l))], n_rb)
        mix_c = _attention(_na_kernel, "attn_na", q_na, k_na, v_na, bounds["na"][l],
                           [(bias_all, _layer_spec(bias_all, l))], n_rb)
        mix_d = _attention(_gqa_kernel, "attn_gqa", q_gqa, k_gqa, v_gqa, bounds["gqa"][l], [], n_rb)

        rows_per_sample = T if with_ctx else SEQ
        x1 = _outproj(l, xs, mods_all, (mix_a, mix_b, mix_c, mix_d), w_out_b, rows_per_sample,
                      OUT_TM if with_ctx else OUT_TM_LAST)
        xs = _ffn(l, x1, mods_all, gffn, w_up_b, conv_wb, w_down_b, rows_per_sample, rows_per_sample // 2)
    return xs
```

```python
import functools
import math

import numpy as np
import jax
import jax.numpy as jnp
from jax import lax
from jax.experimental import pallas as pl
from jax.experimental.pallas import tpu as pltpu

F32 = jnp.float32
BF16 = jnp.bfloat16

D = 1024
SEQ = 2048
GRID_W = 64
CTX = 256
T = SEQ + CTX
RB = 256
NRB = T // RB
NLAT = SEQ // RB
IN_SUB = 3
EPS = 1e-6
NEG_INF = -1e30
ROPE_THETA = 10000.0
LOG2E = 1.4426950408889634

MLA_H, MLA_NOPE, MLA_ROPE, MLA_V, MLA_QR, MLA_KVR = 4, 64, 32, 64, 256, 128
MLA_D = MLA_NOPE + MLA_ROPE
DIFF_H, DIFF_D = 4, 32
NA_H, NA_D, NA_ROWS, NA_COLS = 4, 64, 8, 16
GQA_H, GQA_KV, GQA_D = 4, 2, 64
D_FF = 2816
FF_CHUNK = 256
N_CHUNK = D_FF // FF_CHUNK
OUT_TM, OUT_TM_LAST = 768, 1024
HALO = 16
FFN_TILES = 4
CONV_STRIP = 256

R_CQ, R_CKV, R_KR, R_DQ, R_DK, R_NQ, R_NK, R_GQ, R_GK = 0, 256, 384, 416, 672, 928, 1184, 1440, 1696
QK_ROWS = 1824
V_COLS = 640

G_QA, G_KVA, G_MQ, G_MK, G_DQ, G_DK, G_NQ, G_NK, G_GQ, G_GK = 0, 256, 384, 480, 576, 608, 640, 704, 768, 832
G_ROWS = 896

VMEM_LIMIT = 56 * 1024 * 1024


def _params(sem):
    return pltpu.CompilerParams(dimension_semantics=sem, vmem_limit_bytes=VMEM_LIMIT)


def _dot(a, b):
    return jnp.dot(a, b, preferred_element_type=F32)


def _dot_nt(a, b):
    return lax.dot_general(a, b, (((1,), (1,)), ((), ())), preferred_element_type=F32)


def _mod_kernel(c_ref, w_ref, b_ref, o_ref):
    c = c_ref[...]
    a = (c * jax.nn.sigmoid(c)).astype(BF16)
    o_ref[0] = _dot(a, w_ref[0].astype(BF16)) + b_ref[0]


def _modulation(cc, w_mod, b_mod):
    n_layer = w_mod.shape[0]
    rows = cc.shape[0]
    return pl.pallas_call(
        _mod_kernel,
        grid=(n_layer, 6),
        in_specs=[
            pl.BlockSpec((rows, D), lambda l, j: (0, 0)),
            pl.BlockSpec((1, D, D), lambda l, j: (l, 0, j)),
            pl.BlockSpec((1, 1, D), lambda l, j: (l, 0, j)),
        ],
        out_specs=pl.BlockSpec((1, rows, D), lambda l, j: (l, 0, j)),
        out_shape=jax.ShapeDtypeStruct((n_layer, rows, 6 * D), F32),
        compiler_params=_params(("parallel", "parallel")),
        name="modulation",
    )(cc, w_mod, b_mod.reshape(n_layer, 1, 6 * D))


def _rms_rows(x, g, n):
    ss = jnp.sum(x * x, axis=0, keepdims=True) * (1.0 / n)
    return x * lax.rsqrt(ss + EPS) * g


def _rope_rows(x, cos, sin_signed, w):
    rot = jnp.concatenate([x[w:2 * w], x[0:w], x[3 * w:4 * w], x[2 * w:3 * w]], axis=0)
    return x * cos + rot * sin_signed


def _inproj_kernel(x_ref, modb_ref, modc_ref, gmix_ref, wqk_ref, wv_ref, wuq_ref, wukn_ref, wuv_ref, gcol_ref,
                   cs32_ref, cs64_ref,
                   qmla_ref, kmla_ref, vmla_ref, qdiff_ref, kdiff_ref, vdiff_ref,
                   qna_ref, kna_ref, vna_ref, qgqa_ref, kgqa_ref, vgqa_ref):
    step = pl.program_id(0)

    def modulated(sb):
        r0 = sb * RB
        x = x_ref[0, r0:r0 + RB, :]
        mod = modb_ref[0]
        if sb == IN_SUB - 1:
            mod = jnp.where(step == NRB // IN_SUB - 1, modc_ref[0], mod)
        shift, scale = mod[:, 0:D], mod[:, D:2 * D]
        ms = jnp.mean(x * x, axis=-1, keepdims=True)
        h = x * lax.rsqrt(ms + EPS) * gmix_ref[...]
        return (h * (1.0 + scale) + shift).astype(BF16)

    def gain(off, n):
        return gcol_ref[off:off + n, :]

    def block(sb, hb, after_first_projection):
        rows = slice(sb * RB, (sb + 1) * RB)
        _inproj_block(rows, hb, after_first_projection, gain, wqk_ref, wv_ref, wuq_ref, wukn_ref, wuv_ref,
                      cs32_ref, cs64_ref, qmla_ref, kmla_ref, vmla_ref, qdiff_ref, kdiff_ref, vdiff_ref,
                      qna_ref, kna_ref, vna_ref, qgqa_ref, kgqa_ref, vgqa_ref)

    hbs = [modulated(0)]
    for sb in range(IN_SUB):
        nxt = (lambda sb=sb: hbs.append(modulated(sb + 1))) if sb + 1 < IN_SUB else (lambda: None)
        block(sb, hbs[sb], nxt)


def _inproj_block(rows, hb, after_first_projection, gain, wqk_ref, wv_ref, wuq_ref, wukn_ref, wuv_ref,
                  cs32_ref, cs64_ref, qmla_ref, kmla_ref, vmla_ref, qdiff_ref, kdiff_ref, vdiff_ref,
                  qna_ref, kna_ref, vna_ref, qgqa_ref, kgqa_ref, vgqa_ref):
    def project(lo, hi):
        return _dot_nt(wqk_ref[lo:hi, :], hb)

    cos32, sin32 = cs32_ref[0:32, rows], cs32_ref[32:64, rows]
    cos64, sin64 = cs64_ref[0:64, rows], cs64_ref[64:128, rows]

    def values():
        pv = _dot(hb, wv_ref[...])
        vdiff_ref[0, rows, :] = pv[:, 0:256].astype(BF16)
        vna_ref[0, rows, :] = pv[:, 256:512].astype(BF16)
        vgqa_ref[0, rows, :] = pv[:, 512:640].astype(BF16)

    def mla(pt):
        cq = _rms_rows(pt[0:MLA_QR], gain(G_QA, MLA_QR), MLA_QR).astype(BF16)
        qt = _dot(wuq_ref[...], cq)
        ckv = _rms_rows(pt[R_CKV:R_CKV + MLA_KVR], gain(G_KVA, MLA_KVR), MLA_KVR).astype(BF16)
        knt = _dot(wukn_ref[...], ckv)
        vt = _dot(wuv_ref[...], ckv)
        vmla_ref[0, rows, :] = vt.T.astype(BF16)
        kr = pt[R_KR:R_KR + MLA_ROPE]
        kr_ss = jnp.sum(kr * kr, axis=0, keepdims=True)
        g_mq, g_mk = gain(G_MQ, MLA_D), gain(G_MK, MLA_D)
        zpad = jnp.zeros((128 - MLA_D, RB), F32)
        q_parts = []
        for hd in range(MLA_H):
            qh = _rms_rows(qt[hd * MLA_D:(hd + 1) * MLA_D], g_mq, MLA_D)
            q_rope = _rope_rows(qh[MLA_NOPE:], cos32, sin32, MLA_ROPE // 4)
            q_parts += [qh[:MLA_NOPE], q_rope, zpad]
            kn = knt[hd * MLA_NOPE:(hd + 1) * MLA_NOPE]
            ss = (jnp.sum(kn * kn, axis=0, keepdims=True) + kr_ss) * (1.0 / MLA_D)
            r = lax.rsqrt(ss + EPS)
            k_rope = _rope_rows(kr * r * g_mk[MLA_NOPE:], cos32, sin32, MLA_ROPE // 4)
            kmla_ref[0, hd * 128:(hd + 1) * 128, rows] = jnp.concatenate(
                [kn * r * g_mk[:MLA_NOPE], k_rope, zpad], axis=0).astype(BF16)
        q_all = jnp.concatenate(q_parts, axis=0) * (MLA_D ** -0.5 * LOG2E)
        qmla_ref[0, rows, :] = q_all.T.astype(BF16)

    def diff(pt):
        g_dq, g_dk = gain(G_DQ, DIFF_D), gain(G_DK, DIFF_D)
        q_parts, k_parts = [], []
        for gi in range(2 * DIFF_H):
            qg = _rms_rows(pt[gi * DIFF_D:(gi + 1) * DIFF_D], g_dq, DIFF_D)
            q_parts.append(_rope_rows(qg, cos32, sin32, DIFF_D // 4))
            kg = _rms_rows(pt[256 + gi * DIFF_D:256 + (gi + 1) * DIFF_D], g_dk, DIFF_D)
            k_parts.append(_rope_rows(kg, cos32, sin32, DIFF_D // 4))
        qdiff_ref[0, rows, :] = (jnp.concatenate(q_parts, axis=0) * (DIFF_D ** -0.5 * LOG2E)).T.astype(BF16)
        kdiff_ref[0, :, rows] = jnp.concatenate(k_parts, axis=0).astype(BF16)

    def na(pt):
        g_nq, g_nk = gain(G_NQ, NA_D), gain(G_NK, NA_D)
        q_parts, k_parts = [], []
        for hd in range(NA_H):
            q_parts.append(_rms_rows(pt[hd * NA_D:(hd + 1) * NA_D], g_nq, NA_D))
            k_parts.append(_rms_rows(pt[256 + hd * NA_D:256 + (hd + 1) * NA_D], g_nk, NA_D))
        qna_ref[0, rows, :] = (jnp.concatenate(q_parts, axis=0) * (NA_D ** -0.5 * LOG2E)).T.astype(BF16)
        kna_ref[0, :, rows] = jnp.concatenate(k_parts, axis=0).astype(BF16)

    def gqa(pt):
        g_gq, g_gk = gain(G_GQ, GQA_D), gain(G_GK, GQA_D)
        q_parts, k_parts = [], []
        for hd in range(GQA_H):
            qg = _rms_rows(pt[hd * GQA_D:(hd + 1) * GQA_D], g_gq, GQA_D)
            q_parts.append(_rope_rows(qg, cos64, sin64, GQA_D // 4))
        for hd in range(GQA_KV):
            kg = _rms_rows(pt[256 + hd * GQA_D:256 + (hd + 1) * GQA_D], g_gk, GQA_D)
            k_parts.append(_rope_rows(kg, cos64, sin64, GQA_D // 4))
        qgqa_ref[0, rows, :] = (jnp.concatenate(q_parts, axis=0) * (GQA_D ** -0.5 * LOG2E)).T.astype(BF16)
        kgqa_ref[0, :, rows] = jnp.concatenate(k_parts, axis=0).astype(BF16)

    pt_mla = project(R_CQ, R_DQ)
    after_first_projection()
    pt_diff = project(R_DQ, R_NQ)
    mla(pt_mla)
    pt_na = project(R_NQ, R_GQ)
    diff(pt_diff)
    pt_gqa = project(R_GQ, QK_ROWS)
    na(pt_na)
    values()
    gqa(pt_gqa)


def _inproj(l, xs, mods, gmix, wqk, wv, wuq, wukn, wuv, gcol, cs32, cs64):
    n_b = xs.shape[0]

    def full(a):
        return _layer_spec(a, l)

    rows_per_step = IN_SUB * RB

    def tok(width):
        return pl.BlockSpec((1, rows_per_step, width), lambda r, b: (b, r, 0))

    def chan(rows):
        return pl.BlockSpec((1, rows, rows_per_step), lambda r, b: (b, 0, r))

    def tshape(width):
        return jax.ShapeDtypeStruct((n_b, T, width), BF16)

    def cshape(rows):
        return jax.ShapeDtypeStruct((n_b, rows, T), BF16)

    return pl.pallas_call(
        _inproj_kernel,
        grid=(NRB // IN_SUB, n_b),
        in_specs=[
            pl.BlockSpec((1, rows_per_step, D), lambda r, b: (b, r, 0)),
            pl.BlockSpec((None, 1, 1, 6 * D), lambda r, b: (l, b, 0, 0)),
            pl.BlockSpec((None, 1, 1, 6 * D), lambda r, b: (l, n_b, 0, 0)),
            full(gmix), full(wqk), full(wv), full(wuq), full(wukn), full(wuv), full(gcol),
            pl.BlockSpec((64, rows_per_step), lambda r, b: (0, r)),
            pl.BlockSpec((128, rows_per_step), lambda r, b: (0, r)),
        ],
        out_specs=[tok(512), chan(512), tok(256), tok(256), chan(256), tok(256),
                   tok(256), chan(256), tok(256), tok(256), chan(128), tok(128)],
        out_shape=[tshape(512), cshape(512), tshape(256), tshape(256), cshape(256), tshape(256),
                   tshape(256), cshape(256), tshape(256), tshape(256), cshape(128), tshape(128)],
        compiler_params=_params(("parallel", "parallel")),
        name="inproj_prep",
    )(xs, mods, mods, gmix, wqk, wv, wuq, wukn, wuv, gcol, cs32, cs64)


def _lane_id(shape):
    return lax.broadcasted_iota(jnp.int32, shape, 1)


KEY_CHUNK = 768
LAT_CHUNKS = tuple((lo, KEY_CHUNK) for lo in range(0, T, KEY_CHUNK))
CTX_CHUNKS = ((SEQ, CTX),)


def _k_slab(k_ref, idx):
    return lambda lo, n: k_ref[0, idx * 128:(idx + 1) * 128, pl.ds(lo, n)]


def _v_slab(v_ref, idx):
    return lambda lo, n: v_ref[0, pl.ds(lo, n), idx * 128:(idx + 1) * 128]


def _attend(units, chunks, shift=None):
    def scores(i, c):
        q, keys, _, bias = units[i]
        lo, n = chunks[c]
        s = _dot(q, keys(lo, n))
        b = None if bias is None else bias(c)
        return s if b is None else s + b

    if shift is not None:
        items = [(i, c) for i in range(len(units)) for c in range(len(chunks))]
        acc, l = [None] * len(units), [None] * len(units)
        nxt = scores(*items[0])
        for idx, (i, c) in enumerate(items):
            cur = nxt
            if idx + 1 < len(items):
                nxt = scores(*items[idx + 1])
            e = jnp.exp2(cur - shift)
            lc = jnp.sum(e, axis=-1, keepdims=True)
            oc = _dot(e.astype(BF16), units[i][2](*chunks[c]))
            acc[i] = oc if acc[i] is None else acc[i] + oc
            l[i] = lc if l[i] is None else l[i] + lc
        return list(zip(acc, l))

    cur = [scores(0, c) for c in range(len(chunks))]
    out = []
    for i in range(len(units)):
        m = functools.reduce(jnp.maximum, [jnp.max(s, axis=-1, keepdims=True) for s in cur])
        nxt, acc, l = [], None, None
        for c, (lo, n) in enumerate(chunks):
            if i + 1 < len(units):
                nxt.append(scores(i + 1, c))
            e = jnp.exp2(cur[c] - m)
            lc = jnp.sum(e, axis=-1, keepdims=True)
            oc = _dot(e.astype(BF16), units[i][2](lo, n))
            acc = oc if acc is None else acc + oc
            l = lc if l is None else l + lc
        out.append((acc, l))
        cur = nxt
    return out


def _with_shift(bnd_ref, run):
    @pl.when(bnd_ref[1] > 0.5)
    def _():
        run(bnd_ref[0])

    @pl.when(bnd_ref[1] <= 0.5)
    def _():
        run(None)


def _query_blocks(n_rb, bnd_ref, run):
    def all_blocks(shift):
        def body(rb, carry):
            run(rb, pl.ds(pl.multiple_of(rb * RB, RB), RB), True, shift)
            return carry

        lax.fori_loop(0, NLAT, body, 0)
        if n_rb > NLAT:
            run(NLAT, pl.ds(SEQ, CTX), False, shift)

    _with_shift(bnd_ref, all_blocks)


def _mla_kernel(q_ref, k_ref, v_ref, bnd_ref, o_ref, *, n_rb):
    lane = _lane_id((RB, 128))

    def run(rb, rows, latent, shift):
        units = [(q_ref[0, rows,hd * 128:(hd + 1) * 128], _k_slab(k_ref, hd), _v_slab(v_ref, hd // 2), None)
                 for hd in range(MLA_H)]
        o = [acc * (1.0 / l) for acc, l in _attend(units, LAT_CHUNKS if latent else CTX_CHUNKS, shift)]
        outs = [jnp.where(lane < 64, o[0], o[1]), jnp.where(lane < 64, o[2], o[3])]
        o_ref[0, rows, :] = jnp.concatenate(outs, axis=1).astype(o_ref.dtype)

    _query_blocks(n_rb, bnd_ref, run)


def _gqa_kernel(q_ref, k_ref, v_ref, bnd_ref, o_ref, *, n_rb):
    lane = _lane_id((RB, 128))

    def run(rb, rows, latent, shift):
        order = [(rep, grp) for rep in range(2) for grp in range(2)]
        units = []
        for rep, grp in order:
            qs = q_ref[0, rows,rep * 128:(rep + 1) * 128]
            qm = jnp.where((lane >= 64) == (grp == 1), qs, jnp.zeros_like(qs))
            units.append((qm, _k_slab(k_ref, 0), _v_slab(v_ref, 0), None))
        res = {}
        for (rep, grp), (acc, l) in zip(order, _attend(units, LAT_CHUNKS if latent else CTX_CHUNKS, shift)):
            o = acc * (1.0 / l)
            res[(grp, rep)] = o if grp == rep else pltpu.roll(o, 64, axis=1)
        outs = [jnp.where(lane < 64, res[(grp, 0)], res[(grp, 1)]) for grp in range(2)]
        o_ref[0, rows, :] = jnp.concatenate(outs, axis=1).astype(o_ref.dtype)

    _query_blocks(n_rb, bnd_ref, run)


def _diff_kernel(q_ref, k_ref, v_ref, bnd_ref, lam_ref, g_ref, o_ref, *, n_rb, lambda_init):
    lane = _lane_id((RB, 128))
    lq1, lk1, lq2, lk2 = lam_ref[0:1, :], lam_ref[1:2, :], lam_ref[2:3, :], lam_ref[3:4, :]
    lam = (jnp.exp(jnp.sum(lq1 * lk1, axis=-1, keepdims=True))
           - jnp.exp(jnp.sum(lq2 * lk2, axis=-1, keepdims=True)) + lambda_init)
    gsub = g_ref[...]
    grp = lane // DIFF_D

    def run(rb, rows, latent, shift):
        units = []
        for hd in range(DIFF_H):
            pair, sub = divmod(hd, 2)
            qs = q_ref[0, rows,pair * 128:(pair + 1) * 128]
            for which in range(2):
                qm = jnp.where(grp == 2 * sub + which, qs, jnp.zeros_like(qs))
                units.append((qm, _k_slab(k_ref, pair), _v_slab(v_ref, pair), None))
        res = _attend(units, LAT_CHUNKS if latent else CTX_CHUNKS, shift)
        outs = []
        for pair in range(2):
            halves = []
            for sub in range(2):
                (a1, l1), (a2, l2) = res[2 * (2 * pair + sub)], res[2 * (2 * pair + sub) + 1]
                halves.append(a1 * (1.0 / l1) - a2 * (lam / l2))
            o = jnp.where(lane < 64, halves[0], halves[1])
            o2 = o * o
            ss0 = jnp.sum(jnp.where(lane < 64, o2, 0.0), axis=-1, keepdims=True)
            ss1 = jnp.sum(jnp.where(lane < 64, 0.0, o2), axis=-1, keepdims=True)
            ss = jnp.where(lane < 64, ss0, ss1) * (1.0 / (2 * DIFF_D))
            outs.append(o * lax.rsqrt(ss + EPS) * gsub * (1.0 - lambda_init))
        o_ref[0, rows, :] = jnp.concatenate(outs, axis=1).astype(o_ref.dtype)

    _query_blocks(n_rb, bnd_ref, run)


NA_WIN = 768
NA_CHUNK = 768
NA_TILES = 49
GRID_ROWS = SEQ // GRID_W


def _na_kernel(q_ref, k_ref, v_ref, bnd_ref, bias_ref, o_ref, *, n_rb):
    lane = _lane_id((RB, 128))

    def run(rb, rows, latent, shift):
        na_chunks, tile_idx = CTX_CHUNKS, None
        if latent:
            win_row = jnp.clip(4 * rb - 4, 0, GRID_ROWS - NA_WIN // GRID_W)
            off = pl.multiple_of(win_row * GRID_W, RB)
            local = [(pl.multiple_of(off + c * NA_CHUNK, RB), NA_CHUNK) for c in range(NA_WIN // NA_CHUNK)]
            na_chunks = local + list(CTX_CHUNKS)
            tile_idx = []
            for c in range(NA_WIN // NA_CHUNK):
                per_row = []
                for ri in range(RB // GRID_W):
                    r = 4 * rb + ri
                    row_start = jnp.clip(r - NA_ROWS // 2, 0, GRID_ROWS - NA_ROWS)
                    idxs = []
                    for p in range(NA_CHUNK // 128):
                        kr = win_row + c * (NA_CHUNK // GRID_W) + 2 * p
                        ok1 = (kr >= row_start) & (kr < row_start + NA_ROWS)
                        ok2 = (kr + 1 >= row_start) & (kr + 1 < row_start + NA_ROWS)
                        e = jnp.clip(kr - r + NA_ROWS, 0, 15)
                        idxs.append(jnp.where(ok1 & ok2, e, jnp.where(ok1, 16 + e, jnp.where(ok2, 32 + e, 48))))
                    per_row.append(idxs)
                tile_idx.append(per_row)
        units = []
        for hd in range(NA_H):
            pair, sub = divmod(hd, 2)
            qs = q_ref[0, rows,pair * 128:(pair + 1) * 128]
            qm = jnp.where((lane >= 64) == (sub == 1), qs, jnp.zeros_like(qs))

            def bias(c, hd=hd):
                if tile_idx is None or c >= NA_WIN // NA_CHUNK:
                    return None
                return jnp.concatenate(
                    [jnp.concatenate([bias_ref[hd, idx] for idx in row], axis=1) for row in tile_idx[c]], axis=0)

            units.append((qm, _k_slab(k_ref, pair), _v_slab(v_ref, pair), bias))
        o = [acc * (1.0 / l) for acc, l in _attend(units, na_chunks, shift)]
        outs = [jnp.where(lane < 64, o[0], o[1]), jnp.where(lane < 64, o[2], o[3])]
        o_ref[0, rows, :] = jnp.concatenate(outs, axis=1).astype(o_ref.dtype)

    _query_blocks(n_rb, bnd_ref, run)


SHIFT_MAX = 40.0


def _logit_bound(d, q_gain, k_gain, bias_max=0.0):
    bound = (1.01 * math.sqrt(d) * LOG2E * jnp.max(jnp.abs(q_gain), axis=-1) * jnp.max(jnp.abs(k_gain), axis=-1)
             + bias_max)
    return jnp.stack([bound, (bound <= SHIFT_MAX).astype(F32)], axis=-1).astype(F32)


def _layer_spec(a, l):
    return pl.BlockSpec((None,) + a.shape[1:], lambda *_, _n=a.ndim - 1: (l,) + (0,) * _n)


def _attention(body, name, q, k, v, bound, extra, n_rb, out_width=256):
    n_b = q.shape[0]
    extra = [(bound, pl.BlockSpec(memory_space=pltpu.SMEM))] + list(extra)
    in_specs = [
        pl.BlockSpec((1, T, q.shape[2]), lambda b: (b, 0, 0)),
        pl.BlockSpec((1, k.shape[1], T), lambda b: (b, 0, 0)),
        pl.BlockSpec((1, T, v.shape[2]), lambda b: (b, 0, 0)),
    ] + [spec for _, spec in extra]
    return pl.pallas_call(
        functools.partial(body, n_rb=n_rb),
        grid=(n_b,),
        in_specs=in_specs,
        out_specs=pl.BlockSpec((1, n_rb * RB, out_width), lambda b: (b, 0, 0)),
        out_shape=jax.ShapeDtypeStruct((n_b, n_rb * RB, out_width), BF16),
        compiler_params=_params(("parallel",)),
        name=name,
    )(q, k, v, *[a for a, _ in extra])


def _outproj_kernel(x_ref, modb_ref, modc_ref, ma_ref, mb_ref, mc_ref, md_ref, w_ref, o_ref, *, tm):
    rows = pl.program_id(1) * tm + lax.broadcasted_iota(jnp.int32, (tm, 1), 0)
    g1 = jnp.where(rows >= SEQ, modc_ref[0][:, 2 * D:3 * D], modb_ref[0][:, 2 * D:3 * D])
    acc = _dot(ma_ref[0], w_ref[0:256, :])
    acc += _dot(mb_ref[0], w_ref[256:512, :])
    acc += _dot(mc_ref[0], w_ref[512:768, :])
    acc += _dot(md_ref[0], w_ref[768:1024, :])
    o_ref[0] = x_ref[0] + g1 * acc


def _outproj(l, xs, mods, mixes, w_out, rows_per_sample, tm):
    n_b = xs.shape[0]
    mix_spec = pl.BlockSpec((1, tm, 256), lambda b, r: (b, r, 0))
    return pl.pallas_call(
        functools.partial(_outproj_kernel, tm=tm),
        grid=(n_b, rows_per_sample // tm),
        in_specs=[
            pl.BlockSpec((1, tm, D), lambda b, r: (b, r, 0)),
            pl.BlockSpec((None, 1, 1, 6 * D), lambda b, r: (l, b, 0, 0)),
            pl.BlockSpec((None, 1, 1, 6 * D), lambda b, r: (l, n_b, 0, 0)),
            mix_spec, mix_spec, mix_spec, mix_spec,
            _layer_spec(w_out, l),
        ],
        out_specs=pl.BlockSpec((1, tm, D), lambda b, r: (b, r, 0)),
        out_shape=jax.ShapeDtypeStruct((n_b, rows_per_sample, D), F32),
        compiler_params=_params(("parallel", "parallel")),
        name="outproj",
    )(xs, mods, mods, *mixes, w_out)


def _ffn_kernel(x_ref, hb_ref, ha_ref, modb_ref, modc_ref, g_ref, wa_ref, wg_ref, cw_ref, wd_ref, o_ref,
                h_scr, u0_scr, u1_scr, act_scr, *, tm, rows_per_sample):
    rb = pl.program_id(1)
    j = pl.program_id(2)
    row0 = rb * tm
    split = SEQ % tm if rows_per_sample > SEQ else 0
    band_lo, band_hi = split - HALO, split + HALO

    def per_row(rows, col):
        lat = modb_ref[0][:, col * D:(col + 1) * D]
        if rows_per_sample == SEQ:
            return lat
        return jnp.where(rows >= SEQ, modc_ref[0][:, col * D:(col + 1) * D], lat)

    def modulated(x, rows):
        ms = jnp.mean(x * x, axis=-1, keepdims=True)
        y = x * lax.rsqrt(ms + EPS) * g_ref[...]
        return (y * (1.0 + per_row(rows, 4)) + per_row(rows, 3)).astype(BF16)

    def prologue():
        halo_rows = lax.broadcasted_iota(jnp.int32, (HALO, 1), 0)
        has_before = (row0 != 0) & (row0 != SEQ)
        has_after = (row0 + tm != SEQ) & (row0 + tm != rows_per_sample)
        zero = jnp.zeros((HALO, D), BF16)
        h_scr[0:HALO, :] = jnp.where(has_before, modulated(hb_ref[0], row0 - HALO + halo_rows), zero)
        h_scr[HALO + tm:, :] = jnp.where(has_after, modulated(ha_ref[0], row0 + tm + halo_rows), zero)
        main_rows = row0 + lax.broadcasted_iota(jnp.int32, (tm, 1), 0)
        h_scr[HALO:HALO + tm, :] = modulated(x_ref[0], main_rows)
        o_ref[0] = jnp.zeros((tm, D), F32)

    slots = (u0_scr, u1_scr)

    def tiles(total):
        units = total // HALO
        sizes = [(units // FFN_TILES + (1 if t < units % FFN_TILES else 0)) * HALO for t in range(FFN_TILES)]
        edges = np.cumsum([0] + sizes)
        return [(int(edges[t]), int(edges[t + 1])) for t in range(FFN_TILES)]

    up_tiles, down_tiles = tiles(tm + 2 * HALO), tiles(tm)

    def up(slot, t):
        lo, hi = up_tiles[t]
        slots[slot][lo:hi, :FF_CHUNK] = _dot(h_scr[lo:hi, :], wa_ref[...])
        slots[slot][lo:hi, FF_CHUNK:] = _dot(h_scr[lo:hi, :], wg_ref[...])

    def conv(u_scr, lo, hi, c_lo, c_hi, masked):
        prev = u_scr[HALO - 1 + lo:HALO - 1 + hi, c_lo:c_hi]
        nxt = u_scr[HALO + 1 + lo:HALO + 1 + hi, c_lo:c_hi]
        if masked:
            rows = row0 + lo + lax.broadcasted_iota(jnp.int32, (hi - lo, 1), 0)
            prev = jnp.where(rows != SEQ, prev, 0.0)
            nxt = jnp.where(rows != SEQ - 1, nxt, 0.0)
        return (prev * cw_ref[0:1, c_lo:c_hi] + u_scr[HALO + lo:HALO + hi, c_lo:c_hi] * cw_ref[1:2, c_lo:c_hi]
                + nxt * cw_ref[2:3, c_lo:c_hi] + cw_ref[3:4, c_lo:c_hi])

    def conv_down(slot, t):
        t_lo, t_hi = down_tiles[t]
        cuts = sorted({t_lo, t_hi} | ({c for c in (band_lo, band_hi) if t_lo < c < t_hi} if split else set()))
        for lo, hi in zip(cuts[:-1], cuts[1:]):
            masked = bool(split) and band_lo <= lo < band_hi
            for c_lo in range(0, FF_CHUNK, CONV_STRIP):
                a = conv(slots[slot], lo, hi, c_lo, c_lo + CONV_STRIP, masked)
                g = conv(slots[slot], lo, hi, FF_CHUNK + c_lo, FF_CHUNK + c_lo + CONV_STRIP, masked)
                act_scr[lo:hi, c_lo:c_lo + CONV_STRIP] = (g * jax.nn.sigmoid(g) * a).astype(BF16)
        o_ref[0, t_lo:t_hi, :] += _dot(act_scr[t_lo:t_hi, :], wd_ref[...])

    @pl.when(j == 0)
    def _():
        prologue()
        for t in range(FFN_TILES):
            up(0, t)

    for parity in range(2):
        @pl.when((j > 0) & (j < N_CHUNK) & (j % 2 == parity))
        def _():
            for t in range(FFN_TILES):
                up(parity, t)
                conv_down(1 - parity, t)

    @pl.when(j == N_CHUNK)
    def _():
        for t in range(FFN_TILES):
            conv_down((N_CHUNK - 1) % 2, t)
        rows = row0 + lax.broadcasted_iota(jnp.int32, (tm, 1), 0)
        o_ref[0] = x_ref[0] + per_row(rows, 5) * o_ref[0]


def _chunk_interleave(a):
    parts = []
    for j in range(N_CHUNK):
        parts += [a[..., j * FF_CHUNK:(j + 1) * FF_CHUNK], a[..., D_FF + j * FF_CHUNK:D_FF + (j + 1) * FF_CHUNK]]
    return jnp.concatenate(parts, axis=-1)


def _ffn(l, x1, mods, g_ffn, w_up, conv_wb, w_down, rows_per_sample, tm):
    n_b = x1.shape[0]
    n_rb = rows_per_sample // tm
    halo_per_block = tm // HALO
    n_halo = x1.shape[1] // HALO
    body = functools.partial(_ffn_kernel, tm=tm, rows_per_sample=rows_per_sample)

    def up_idx(j):
        return jnp.minimum(j, N_CHUNK - 1)

    def down_idx(j):
        return jnp.maximum(j - 1, 0)

    return pl.pallas_call(
        body,
        grid=(n_b, n_rb, N_CHUNK + 1),
        in_specs=[
            pl.BlockSpec((1, tm, D), lambda b, r, j: (b, r, 0)),
            pl.BlockSpec((1, HALO, D), lambda b, r, j: (b, jnp.maximum(r * halo_per_block - 1, 0), 0)),
            pl.BlockSpec((1, HALO, D), lambda b, r, j: (b, jnp.minimum((r + 1) * halo_per_block, n_halo - 1), 0)),
            pl.BlockSpec((None, 1, 1, 6 * D), lambda b, r, j: (l, b, 0, 0)),
            pl.BlockSpec((None, 1, 1, 6 * D), lambda b, r, j: (l, n_b, 0, 0)),
            _layer_spec(g_ffn, l),
            pl.BlockSpec((None, D, FF_CHUNK), lambda b, r, j: (l, 0, up_idx(j))),
            pl.BlockSpec((None, D, FF_CHUNK), lambda b, r, j: (l, 0, N_CHUNK + up_idx(j))),
            pl.BlockSpec((None, 4, 2 * FF_CHUNK), lambda b, r, j: (l, 0, down_idx(j))),
            pl.BlockSpec((None, FF_CHUNK, D), lambda b, r, j: (l, down_idx(j), 0)),
        ],
        out_specs=pl.BlockSpec((1, tm, D), lambda b, r, j: (b, r, 0)),
        out_shape=jax.ShapeDtypeStruct((n_b, rows_per_sample, D), F32),
        scratch_shapes=[
            pltpu.VMEM((tm + 2 * HALO, D), BF16),
            pltpu.VMEM((tm + 2 * HALO, 2 * FF_CHUNK), F32),
            pltpu.VMEM((tm + 2 * HALO, 2 * FF_CHUNK), F32),
            pltpu.VMEM((tm, FF_CHUNK), BF16),
        ],
        compiler_params=_params(("parallel", "parallel", "arbitrary")),
        name="conv_ffn",
    )(x1, x1, x1, mods, mods, g_ffn, w_up, w_up, conv_wb, w_down)


def _rope_table(rot_dim):
    n_freq = rot_dim // 4
    inv = jnp.power(ROPE_THETA, -jnp.arange(n_freq, dtype=F32) / n_freq)
    t = jnp.arange(SEQ)
    row = (t // GRID_W).astype(F32)
    col = (t % GRID_W).astype(F32)
    ar, ac = row[:, None] * inv, col[:, None] * inv
    ang = jnp.concatenate([ar, ar, ac, ac], axis=-1)
    sign = jnp.concatenate([-jnp.ones(n_freq), jnp.ones(n_freq), -jnp.ones(n_freq), jnp.ones(n_freq)]).astype(F32)
    cos = jnp.concatenate([jnp.cos(ang), jnp.ones((CTX, rot_dim), F32)], axis=0)
    sin = jnp.concatenate([jnp.sin(ang) * sign, jnp.zeros((CTX, rot_dim), F32)], axis=0)
    return jnp.concatenate([cos.T, sin.T], axis=0)


def _na_bias_tiles(rpb):
    n_l, n_h = rpb.shape[:2]
    n_off = 2 * NA_ROWS - 1
    p = jnp.pad(rpb.astype(F32) * LOG2E, ((0, 0), (0, 0), (0, 0), (48, 49)))
    sk = jnp.broadcast_to(p[..., None, :], (n_l, n_h, n_off, GRID_W, 128)).reshape(n_l, n_h, n_off, GRID_W * 128)
    sk = sk[..., :GRID_W * 127].reshape(n_l, n_h, n_off, GRID_W, 127)[..., 63:127]
    c = np.arange(GRID_W)
    win_start = np.clip(c - NA_COLS // 2, 0, GRID_W - NA_COLS)
    v_col = (c[None, :] >= win_start[:, None]) & (c[None, :] < win_start[:, None] + NA_COLS)
    tz = jnp.where(v_col, sk, NEG_INF)
    neg1 = jnp.full((n_l, n_h, 1, GRID_W, GRID_W), NEG_INF, F32)
    tzx = jnp.concatenate([neg1, tz, neg1], axis=2)
    first, second = tzx[:, :, 0:16], tzx[:, :, 1:17]
    neg16 = jnp.full_like(first, NEG_INF)
    return jnp.concatenate([
        jnp.concatenate([first, second], axis=-1), jnp.concatenate([first, neg16], axis=-1),
        jnp.concatenate([neg16, second], axis=-1), jnp.concatenate([neg1, neg1], axis=-1)], axis=2)


def kernel(x, c, ctx, c_ctx, w_mod, b_mod, g_mix, w_in, w_out, mla_q_a_g, mla_w_uq, mla_kv_a_g, mla_w_ukv,
           mla_q_g, mla_k_g, diff_q_g, diff_k_g, diff_lq1, diff_lk1, diff_lq2, diff_lk2, diff_subln_g,
           na_q_g, na_k_g, na_rpb, gqa_q_g, gqa_k_g, g_ffn, w_up, conv_w, conv_b, w_down):
    n_b = x.shape[0]
    n_layer = w_mod.shape[0]
    assert x.shape[1:] == (SEQ, D) and ctx.shape[1:] == (CTX, D)

    xs = jnp.concatenate([x, ctx], axis=1)
    mod_rows = -(-(n_b + 1) // 8) * 8
    cc = jnp.concatenate([c, c_ctx[None], jnp.zeros((mod_rows - n_b - 1, D), F32)], axis=0)
    mods_all = _modulation(cc, w_mod, b_mod).reshape(n_layer, mod_rows, 1, 6 * D)

    cs32, cs64 = _rope_table(32), _rope_table(64)
    bias_all = _na_bias_tiles(na_rpb)

    s = np.cumsum([0, 256, 128, 32, 256, 256, 256, 256, 256, 256, 256, 128, 128])
    gq_perm = np.concatenate([np.arange(GQA_D) + (2 * g + r) * GQA_D for r in range(2) for g in range(2)])

    n_l = n_layer
    gq_cols = w_in[:, :, s[9]:s[10]][:, :, gq_perm]
    wqk = jnp.concatenate([w_in[:, :, s[0]:s[3]], w_in[:, :, s[3]:s[5]], w_in[:, :, s[6]:s[8]], gq_cols,
                           w_in[:, :, s[10]:s[11]]], axis=2).swapaxes(1, 2).astype(BF16)
    wv = jnp.concatenate([w_in[:, :, s[5]:s[6]], w_in[:, :, s[8]:s[9]], w_in[:, :, s[11]:s[12]]],
                         axis=2).astype(BF16)
    wuq = mla_w_uq.swapaxes(1, 2).astype(BF16)
    wukv = mla_w_ukv.reshape(n_l, MLA_KVR, MLA_H, MLA_NOPE + MLA_V)
    wukn = wukv[..., :MLA_NOPE].reshape(n_l, MLA_KVR, MLA_H * MLA_NOPE).swapaxes(1, 2).astype(BF16)
    wuv = wukv[..., MLA_NOPE:].reshape(n_l, MLA_KVR, MLA_H * MLA_V).swapaxes(1, 2).astype(BF16)
    gcol = jnp.concatenate([mla_q_a_g, mla_kv_a_g, mla_q_g, mla_k_g, diff_q_g, diff_k_g, na_q_g, na_k_g,
                            gqa_q_g, gqa_k_g], axis=1).astype(F32)[..., None]
    lam_rows = jnp.stack([diff_lq1, diff_lk1, diff_lq2, diff_lk2], axis=1).astype(F32)
    lam_rows = jnp.concatenate([lam_rows, jnp.zeros_like(lam_rows)], axis=1)
    gsub = jnp.tile(diff_subln_g.astype(F32), (1, 2)).reshape(n_l, 1, 128)
    bounds = {
        "mla": _logit_bound(MLA_D, mla_q_g, mla_k_g),
        "diff": _logit_bound(DIFF_D, diff_q_g, diff_k_g),
        "na": _logit_bound(NA_D, na_q_g, na_k_g, LOG2E * jnp.maximum(jnp.max(na_rpb, axis=(1, 2, 3)), 0.0)),
        "gqa": _logit_bound(GQA_D, gqa_q_g, gqa_k_g),
    }
    w_out_b, w_down_b = w_out.astype(BF16), w_down.astype(BF16)
    w_up_b = w_up.astype(BF16)
    conv_wb = _chunk_interleave(jnp.concatenate([conv_w, conv_b[:, None, :]], axis=1).astype(F32))
    gmix, gffn = g_mix.reshape(n_l, 1, D), g_ffn.reshape(n_l, 1, D)

    for l in range(n_layer):
        with_ctx = l < n_layer - 1
        lambda_init = 0.8 - 0.6 * math.exp(-0.3 * l)
        (q_mla, k_mla, v_mla, q_diff, k_diff, v_diff, q_na, k_na, v_na, q_gqa, k_gqa, v_gqa) = _inproj(
            l, xs, mods_all, gmix, wqk, wv, wuq, wukn, wuv, gcol, cs32, cs64)

        n_rb = NRB if with_ctx else NLAT
        mix_a = _attention(_mla_kernel, "attn_mla", q_mla, k_mla, v_mla, bounds["mla"][l], [], n_rb)
        mix_b = _attention(
            functools.partial(_diff_kernel, lambda_init=lambda_init), "attn_diff",
            q_diff, k_diff, v_diff, bounds["diff"][l],
            [(lam_rows, _layer_spec(lam_rows, l)), (gsub, _layer_spec(gsub, l))], n_rb)
        mix_c = _attention(_na_kernel, "attn_na", q_na, k_na, v_na, bounds["na"][l],
                           [(bias_all, _layer_spec(bias_all, l))], n_rb)
        mix_d = _attention(_gqa_kernel, "attn_gqa", q_gqa, k_gqa, v_gqa, bounds["gqa"][l], [], n_rb)

        rows_per_sample = T if with_ctx else SEQ
        x1 = _outproj(l, xs, mods_all, (mix_a, mix_b, mix_c, mix_d), w_out_b, rows_per_sample,
                      OUT_TM if with_ctx else OUT_TM_LAST)
        xs = _ffn(l, x1, mods_all, gffn, w_up_b, conv_wb, w_down_b, rows_per_sample, rows_per_sample // 2)
    return xs
```

```python
import functools
import math

import numpy as np
import jax
import jax.numpy as jnp
from jax import lax
from jax.experimental import pallas as pl
from jax.experimental.pallas import tpu as pltpu

F32 = jnp.float32
BF16 = jnp.bfloat16

D = 1024
SEQ = 2048
GRID_W = 64
CTX = 256
T = SEQ + CTX
RB = 256
NRB = T // RB
NLAT = SEQ // RB
IN_SUB = 3
EPS = 1e-6
NEG_INF = -1e30
ROPE_THETA = 10000.0
LOG2E = 1.4426950408889634

MLA_H, MLA_NOPE, MLA_ROPE, MLA_V, MLA_QR, MLA_KVR = 4, 64, 32, 64, 256, 128
MLA_D = MLA_NOPE + MLA_ROPE
DIFF_H, DIFF_D = 4, 32
NA_H, NA_D, NA_ROWS, NA_COLS = 4, 64, 8, 16
GQA_H, GQA_KV, GQA_D = 4, 2, 64
D_FF = 2816
FF_CHUNK = 256
N_CHUNK = D_FF // FF_CHUNK
HALO = 16
FFN_TILES = 4
CONV_STRIP = 256

R_CQ, R_CKV, R_KR, R_DQ, R_DK, R_NQ, R_NK, R_GQ, R_GK = 0, 256, 384, 416, 672, 928, 1184, 1440, 1696
QK_ROWS = 1824
V_COLS = 640

G_QA, G_KVA, G_MQ, G_MK, G_DQ, G_DK, G_NQ, G_NK, G_GQ, G_GK = 0, 256, 384, 480, 576, 608, 640, 704, 768, 832
G_ROWS = 896

VMEM_LIMIT = 56 * 1024 * 1024


def _params(sem):
    return pltpu.CompilerParams(dimension_semantics=sem, vmem_limit_bytes=VMEM_LIMIT)


def _dot(a, b):
    return jnp.dot(a, b, preferred_element_type=F32)


def _dot_nt(a, b):
    return lax.dot_general(a, b, (((1,), (1,)), ((), ())), preferred_element_type=F32)


def _mod_kernel(c_ref, w_ref, b_ref, o_ref):
    c = c_ref[...]
    a = (c * jax.nn.sigmoid(c)).astype(BF16)
    o_ref[0] = _dot(a, w_ref[0].astype(BF16)) + b_ref[0]


def _modulation(cc, w_mod, b_mod):
    n_layer = w_mod.shape[0]
    rows = cc.shape[0]
    return pl.pallas_call(
        _mod_kernel,
        grid=(n_layer, 6),
        in_specs=[
            pl.BlockSpec((rows, D), lambda l, j: (0, 0)),
            pl.BlockSpec((1, D, D), lambda l, j: (l, 0, j)),
            pl.BlockSpec((1, 1, D), lambda l, j: (l, 0, j)),
        ],
        out_specs=pl.BlockSpec((1, rows, D), lambda l, j: (l, 0, j)),
        out_shape=jax.ShapeDtypeStruct((n_layer, rows, 6 * D), F32),
        compiler_params=_params(("parallel", "parallel")),
        name="modulation",
    )(cc, w_mod, b_mod.reshape(n_layer, 1, 6 * D))


def _rms_rows(x, g, n):
    ss = jnp.sum(x * x, axis=0, keepdims=True) * (1.0 / n)
    return x * lax.rsqrt(ss + EPS) * g


def _rope_rows(x, cos, sin_signed, w):
    rot = jnp.concatenate([x[w:2 * w], x[0:w], x[3 * w:4 * w], x[2 * w:3 * w]], axis=0)
    return x * cos + rot * sin_signed


def _inproj_kernel(x_ref, modb_ref, modc_ref, gmix_ref, wqk_ref, wv_ref, wuq_ref, wukn_ref, wuv_ref, gcol_ref,
                   cs32_ref, cs64_ref,
                   qmla_ref, kmla_ref, vmla_ref, qdiff_ref, kdiff_ref, vdiff_ref,
                   qna_ref, kna_ref, vna_ref, qgqa_ref, kgqa_ref, vgqa_ref):
    step = pl.program_id(0)

    def modulated(sb):
        r0 = sb * RB
        x = x_ref[0, r0:r0 + RB, :]
        mod = modb_ref[0]
        if sb == IN_SUB - 1:
            mod = jnp.where(step == NRB // IN_SUB - 1, modc_ref[0], mod)
        shift, scale = mod[:, 0:D], mod[:, D:2 * D]
        ms = jnp.mean(x * x, axis=-1, keepdims=True)
        h = x * lax.rsqrt(ms + EPS) * gmix_ref[...]
        return (h * (1.0 + scale) + shift).astype(BF16)

    def gain(off, n):
        return gcol_ref[off:off + n, :]

    def block(sb, hb, after_first_projection):
        rows = slice(sb * RB, (sb + 1) * RB)
        _inproj_block(rows, hb, after_first_projection, gain, wqk_ref, wv_ref, wuq_ref, wukn_ref, wuv_ref,
                      cs32_ref, cs64_ref, qmla_ref, kmla_ref, vmla_ref, qdiff_ref, kdiff_ref, vdiff_ref,
                      qna_ref, kna_ref, vna_ref, qgqa_ref, kgqa_ref, vgqa_ref)

    hbs = [modulated(0)]
    for sb in range(IN_SUB):
        nxt = (lambda sb=sb: hbs.append(modulated(sb + 1))) if sb + 1 < IN_SUB else (lambda: None)
        block(sb, hbs[sb], nxt)


def _inproj_block(rows, hb, after_first_projection, gain, wqk_ref, wv_ref, wuq_ref, wukn_ref, wuv_ref,
                  cs32_ref, cs64_ref, qmla_ref, kmla_ref, vmla_ref, qdiff_ref, kdiff_ref, vdiff_ref,
                  qna_ref, kna_ref, vna_ref, qgqa_ref, kgqa_ref, vgqa_ref):
    def project(lo, hi):
        return _dot_nt(wqk_ref[lo:hi, :], hb)

    cos32, sin32 = cs32_ref[0:32, rows], cs32_ref[32:64, rows]
    cos64, sin64 = cs64_ref[0:64, rows], cs64_ref[64:128, rows]

    def values():
        pv = _dot(hb, wv_ref[...])
        vdiff_ref[0, rows, :] = pv[:, 0:256].astype(BF16)
        vna_ref[0, rows, :] = pv[:, 256:512].astype(BF16)
        vgqa_ref[0, rows, :] = pv[:, 512:640].astype(BF16)

    def mla(pt):
        cq = _rms_rows(pt[0:MLA_QR], gain(G_QA, MLA_QR), MLA_QR).astype(BF16)
        qt = _dot(wuq_ref[...], cq)
        ckv = _rms_rows(pt[R_CKV:R_CKV + MLA_KVR], gain(G_KVA, MLA_KVR), MLA_KVR).astype(BF16)
        knt = _dot(wukn_ref[...], ckv)
        vt = _dot(wuv_ref[...], ckv)
        vmla_ref[0, rows, :] = vt.T.astype(BF16)
        kr = pt[R_KR:R_KR + MLA_ROPE]
        kr_ss = jnp.sum(kr * kr, axis=0, keepdims=True)
        g_mq, g_mk = gain(G_MQ, MLA_D), gain(G_MK, MLA_D)
        zpad = jnp.zeros((128 - MLA_D, RB), F32)
        q_parts = []
        for hd in range(MLA_H):
            qh = _rms_rows(qt[hd * MLA_D:(hd + 1) * MLA_D], g_mq, MLA_D)
            q_rope = _rope_rows(qh[MLA_NOPE:], cos32, sin32, MLA_ROPE // 4)
            q_parts += [qh[:MLA_NOPE], q_rope, zpad]
            kn = knt[hd * MLA_NOPE:(hd + 1) * MLA_NOPE]
            ss = (jnp.sum(kn * kn, axis=0, keepdims=True) + kr_ss) * (1.0 / MLA_D)
            r = lax.rsqrt(ss + EPS)
            k_rope = _rope_rows(kr * r * g_mk[MLA_NOPE:], cos32, sin32, MLA_ROPE // 4)
            kmla_ref[0, hd * 128:(hd + 1) * 128, rows] = jnp.concatenate(
                [kn * r * g_mk[:MLA_NOPE], k_rope, zpad], axis=0).astype(BF16)
        q_all = jnp.concatenate(q_parts, axis=0) * (MLA_D ** -0.5 * LOG2E)
        qmla_ref[0, rows, :] = q_all.T.astype(BF16)

    def diff(pt):
        g_dq, g_dk = gain(G_DQ, DIFF_D), gain(G_DK, DIFF_D)
        q_parts, k_parts = [], []
        for gi in range(2 * DIFF_H):
            qg = _rms_rows(pt[gi * DIFF_D:(gi + 1) * DIFF_D], g_dq, DIFF_D)
            q_parts.append(_rope_rows(qg, cos32, sin32, DIFF_D // 4))
            kg = _rms_rows(pt[256 + gi * DIFF_D:256 + (gi + 1) * DIFF_D], g_dk, DIFF_D)
            k_parts.append(_rope_rows(kg, cos32, sin32, DIFF_D // 4))
        qdiff_ref[0, rows, :] = (jnp.concatenate(q_parts, axis=0) * (DIFF_D ** -0.5 * LOG2E)).T.astype(BF16)
        kdiff_ref[0, :, rows] = jnp.concatenate(k_parts, axis=0).astype(BF16)

    def na(pt):
        g_nq, g_nk = gain(G_NQ, NA_D), gain(G_NK, NA_D)
        q_parts, k_parts = [], []
        for hd in range(NA_H):
            q_parts.append(_rms_rows(pt[hd * NA_D:(hd + 1) * NA_D], g_nq, NA_D))
            k_parts.append(_rms_rows(pt[256 + hd * NA_D:256 + (hd + 1) * NA_D], g_nk, NA_D))
        qna_ref[0, rows, :] = (jnp.concatenate(q_parts, axis=0) * (NA_D ** -0.5 * LOG2E)).T.astype(BF16)
        kna_ref[0, :, rows] = jnp.concatenate(k_parts, axis=0).astype(BF16)

    def gqa(pt):
        g_gq, g_gk = gain(G_GQ, GQA_D), gain(G_GK, GQA_D)
        q_parts, k_parts = [], []
        for hd in range(GQA_H):
            qg = _rms_rows(pt[hd * GQA_D:(hd + 1) * GQA_D], g_gq, GQA_D)
            q_parts.append(_rope_rows(qg, cos64, sin64, GQA_D // 4))
        for hd in range(GQA_KV):
            kg = _rms_rows(pt[256 + hd * GQA_D:256 + (hd + 1) * GQA_D], g_gk, GQA_D)
            k_parts.append(_rope_rows(kg, cos64, sin64, GQA_D // 4))
        qgqa_ref[0, rows, :] = (jnp.concatenate(q_parts, axis=0) * (GQA_D ** -0.5 * LOG2E)).T.astype(BF16)
        kgqa_ref[0, :, rows] = jnp.concatenate(k_parts, axis=0).astype(BF16)

    pt_mla = project(R_CQ, R_DQ)
    after_first_projection()
    pt_diff = project(R_DQ, R_NQ)
    mla(pt_mla)
    pt_na = project(R_NQ, R_GQ)
    diff(pt_diff)
    pt_gqa = project(R_GQ, QK_ROWS)
    na(pt_na)
    values()
    gqa(pt_gqa)


def _inproj(l, xs, mods, gmix, wqk, wv, wuq, wukn, wuv, gcol, cs32, cs64):
    n_b = xs.shape[0]

    def full(a):
        return _layer_spec(a, l)

    rows_per_step = IN_SUB * RB

    def tok(width):
        return pl.BlockSpec((1, rows_per_step, width), lambda r, b: (b, r, 0))

    def chan(rows):
        return pl.BlockSpec((1, rows, rows_per_step), lambda r, b: (b, 0, r))

    def tshape(width):
        return jax.ShapeDtypeStruct((n_b, T, width), BF16)

    def cshape(rows):
        return jax.ShapeDtypeStruct((n_b, rows, T), BF16)

    return pl.pallas_call(
        _inproj_kernel,
        grid=(NRB // IN_SUB, n_b),
        in_specs=[
            pl.BlockSpec((1, rows_per_step, D), lambda r, b: (b, r, 0)),
            pl.BlockSpec((None, 1, 1, 6 * D), lambda r, b: (l, b, 0, 0)),
            pl.BlockSpec((None, 1, 1, 6 * D), lambda r, b: (l, n_b, 0, 0)),
            full(gmix), full(wqk), full(wv), full(wuq), full(wukn), full(wuv), full(gcol),
            pl.BlockSpec((64, rows_per_step), lambda r, b: (0, r)),
            pl.BlockSpec((128, rows_per_step), lambda r, b: (0, r)),
        ],
        out_specs=[tok(512), chan(512), tok(256), tok(256), chan(256), tok(256),
                   tok(256), chan(256), tok(256), tok(256), chan(128), tok(128)],
        out_shape=[tshape(512), cshape(512), tshape(256), tshape(256), cshape(256), tshape(256),
                   tshape(256), cshape(256), tshape(256), tshape(256), cshape(128), tshape(128)],
        compiler_params=_params(("parallel", "parallel")),
        name="inproj_prep",
    )(xs, mods, mods, gmix, wqk, wv, wuq, wukn, wuv, gcol, cs32, cs64)


def _lane_id(shape):
    return lax.broadcasted_iota(jnp.int32, shape, 1)


KEY_CHUNK = 768
LAT_CHUNKS = tuple((lo, KEY_CHUNK) for lo in range(0, T, KEY_CHUNK))
CTX_CHUNKS = ((SEQ, CTX),)


def _k_slab(k_ref, idx):
    return lambda lo, n: k_ref[0, idx * 128:(idx + 1) * 128, pl.ds(lo, n)]


def _v_slab(v_ref, idx):
    return lambda lo, n: v_ref[0, pl.ds(lo, n), idx * 128:(idx + 1) * 128]


def _attend(units, chunks, shift=None):
    def scores(i, c):
        q, keys, _, bias = units[i]
        lo, n = chunks[c]
        s = _dot(q, keys(lo, n))
        b = None if bias is None else bias(c)
        return s if b is None else s + b

    if shift is not None:
        items = [(i, c) for i in range(len(units)) for c in range(len(chunks))]
        acc, l = [None] * len(units), [None] * len(units)
        nxt = scores(*items[0])
        for idx, (i, c) in enumerate(items):
            cur = nxt
            if idx + 1 < len(items):
                nxt = scores(*items[idx + 1])
            e = jnp.exp2(cur - shift)
            lc = jnp.sum(e, axis=-1, keepdims=True)
            oc = _dot(e.astype(BF16), units[i][2](*chunks[c]))
            acc[i] = oc if acc[i] is None else acc[i] + oc
            l[i] = lc if l[i] is None else l[i] + lc
        return list(zip(acc, l))

    cur = [scores(0, c) for c in range(len(chunks))]
    out = []
    for i in range(len(units)):
        m = functools.reduce(jnp.maximum, [jnp.max(s, axis=-1, keepdims=True) for s in cur])
        nxt, acc, l = [], None, None
        for c, (lo, n) in enumerate(chunks):
            if i + 1 < len(units):
                nxt.append(scores(i + 1, c))
            e = jnp.exp2(cur[c] - m)
            lc = jnp.sum(e, axis=-1, keepdims=True)
            oc = _dot(e.astype(BF16), units[i][2](lo, n))
            acc = oc if acc is None else acc + oc
            l = lc if l is None else l + lc
        out.append((acc, l))
        cur = nxt
    return out


def _with_shift(bnd_ref, run):
    @pl.when(bnd_ref[1] > 0.5)
    def _():
        run(bnd_ref[0])

    @pl.when(bnd_ref[1] <= 0.5)
    def _():
        run(None)


def _query_blocks(n_rb, bnd_ref, run):
    def all_blocks(shift):
        def body(rb, carry):
            run(rb, pl.ds(pl.multiple_of(rb * RB, RB), RB), True, shift)
            return carry

        lax.fori_loop(0, NLAT, body, 0)
        if n_rb > NLAT:
            run(NLAT, pl.ds(SEQ, CTX), False, shift)

    _with_shift(bnd_ref, all_blocks)


def _mla_kernel(q_ref, k_ref, v_ref, bnd_ref, o_ref, *, n_rb):
    lane = _lane_id((RB, 128))

    def run(rb, rows, latent, shift):
        units = [(q_ref[0, rows,hd * 128:(hd + 1) * 128], _k_slab(k_ref, hd), _v_slab(v_ref, hd // 2), None)
                 for hd in range(MLA_H)]
        o = [acc * (1.0 / l) for acc, l in _attend(units, LAT_CHUNKS if latent else CTX_CHUNKS, shift)]
        outs = [jnp.where(lane < 64, o[0], o[1]), jnp.where(lane < 64, o[2], o[3])]
        o_ref[0, rows, :] = jnp.concatenate(outs, axis=1).astype(o_ref.dtype)

    _query_blocks(n_rb, bnd_ref, run)


def _gqa_kernel(q_ref, k_ref, v_ref, bnd_ref, o_ref, *, n_rb):
    lane = _lane_id((RB, 128))

    def run(rb, rows, latent, shift):
        order = [(rep, grp) for rep in range(2) for grp in range(2)]
        units = []
        for rep, grp in order:
            qs = q_ref[0, rows,rep * 128:(rep + 1) * 128]
            qm = jnp.where((lane >= 64) == (grp == 1), qs, jnp.zeros_like(qs))
            units.append((qm, _k_slab(k_ref, 0), _v_slab(v_ref, 0), None))
        res = {}
        for (rep, grp), (acc, l) in zip(order, _attend(units, LAT_CHUNKS if latent else CTX_CHUNKS, shift)):
            o = acc * (1.0 / l)
            res[(grp, rep)] = o if grp == rep else pltpu.roll(o, 64, axis=1)
        outs = [jnp.where(lane < 64, res[(grp, 0)], res[(grp, 1)]) for grp in range(2)]
        o_ref[0, rows, :] = jnp.concatenate(outs, axis=1).astype(o_ref.dtype)

    _query_blocks(n_rb, bnd_ref, run)


def _diff_kernel(q_ref, k_ref, v_ref, bnd_ref, lam_ref, g_ref, o_ref, *, n_rb, lambda_init):
    lane = _lane_id((RB, 128))
    lq1, lk1, lq2, lk2 = lam_ref[0:1, :], lam_ref[1:2, :], lam_ref[2:3, :], lam_ref[3:4, :]
    lam = (jnp.exp(jnp.sum(lq1 * lk1, axis=-1, keepdims=True))
           - jnp.exp(jnp.sum(lq2 * lk2, axis=-1, keepdims=True)) + lambda_init)
    gsub = g_ref[...]
    grp = lane // DIFF_D

    def run(rb, rows, latent, shift):
        units = []
        for hd in range(DIFF_H):
            pair, sub = divmod(hd, 2)
            qs = q_ref[0, rows,pair * 128:(pair + 1) * 128]
            for which in range(2):
                qm = jnp.where(grp == 2 * sub + which, qs, jnp.zeros_like(qs))
                units.append((qm, _k_slab(k_ref, pair), _v_slab(v_ref, pair), None))
        res = _attend(units, LAT_CHUNKS if latent else CTX_CHUNKS, shift)
        outs = []
        for pair in range(2):
            halves = []
            for sub in range(2):
                (a1, l1), (a2, l2) = res[2 * (2 * pair + sub)], res[2 * (2 * pair + sub) + 1]
                halves.append(a1 * (1.0 / l1) - a2 * (lam / l2))
            o = jnp.where(lane < 64, halves[0], halves[1])
            o2 = o * o
            ss0 = jnp.sum(jnp.where(lane < 64, o2, 0.0), axis=-1, keepdims=True)
            ss1 = jnp.sum(jnp.where(lane < 64, 0.0, o2), axis=-1, keepdims=True)
            ss = jnp.where(lane < 64, ss0, ss1) * (1.0 / (2 * DIFF_D))
            outs.append(o * lax.rsqrt(ss + EPS) * gsub * (1.0 - lambda_init))
        o_ref[0, rows, :] = jnp.concatenate(outs, axis=1).astype(o_ref.dtype)

    _query_blocks(n_rb, bnd_ref, run)


NA_WIN = 768
NA_CHUNK = 768
NA_TILES = 49
GRID_ROWS = SEQ // GRID_W


def _na_kernel(q_ref, k_ref, v_ref, bnd_ref, bias_ref, o_ref, *, n_rb):
    lane = _lane_id((RB, 128))

    def run(rb, rows, latent, shift):
        na_chunks, tile_idx = CTX_CHUNKS, None
        if latent:
            win_row = jnp.clip(4 * rb - 4, 0, GRID_ROWS - NA_WIN // GRID_W)
            off = pl.multiple_of(win_row * GRID_W, RB)
            local = [(pl.multiple_of(off + c * NA_CHUNK, RB), NA_CHUNK) for c in range(NA_WIN // NA_CHUNK)]
            na_chunks = local + list(CTX_CHUNKS)
            tile_idx = []
            for c in range(NA_WIN // NA_CHUNK):
                per_row = []
                for ri in range(RB // GRID_W):
                    r = 4 * rb + ri
                    row_start = jnp.clip(r - NA_ROWS // 2, 0, GRID_ROWS - NA_ROWS)
                    idxs = []
                    for p in range(NA_CHUNK // 128):
                        kr = win_row + c * (NA_CHUNK // GRID_W) + 2 * p
                        ok1 = (kr >= row_start) & (kr < row_start + NA_ROWS)
                        ok2 = (kr + 1 >= row_start) & (kr + 1 < row_start + NA_ROWS)
                        e = jnp.clip(kr - r + NA_ROWS, 0, 15)
                        idxs.append(jnp.where(ok1 & ok2, e, jnp.where(ok1, 16 + e, jnp.where(ok2, 32 + e, 48))))
                    per_row.append(idxs)
                tile_idx.append(per_row)
        units = []
        for hd in range(NA_H):
            pair, sub = divmod(hd, 2)
            qs = q_ref[0, rows,pair * 128:(pair + 1) * 128]
            qm = jnp.where((lane >= 64) == (sub == 1), qs, jnp.zeros_like(qs))

            def bias(c, hd=hd):
                if tile_idx is None or c >= NA_WIN // NA_CHUNK:
                    return None
                return jnp.concatenate(
                    [jnp.concatenate([bias_ref[hd, idx] for idx in row], axis=1) for row in tile_idx[c]], axis=0)

            units.append((qm, _k_slab(k_ref, pair), _v_slab(v_ref, pair), bias))
        o = [acc * (1.0 / l) for acc, l in _attend(units, na_chunks, shift)]
        outs = [jnp.where(lane < 64, o[0], o[1]), jnp.where(lane < 64, o[2], o[3])]
        o_ref[0, rows, :] = jnp.concatenate(outs, axis=1).astype(o_ref.dtype)

    _query_blocks(n_rb, bnd_ref, run)


SHIFT_MAX = 40.0


def _logit_bound(d, q_gain, k_gain, bias_max=0.0):
    bound = (1.01 * math.sqrt(d) * LOG2E * jnp.max(jnp.abs(q_gain), axis=-1) * jnp.max(jnp.abs(k_gain), axis=-1)
             + bias_max)
    return jnp.stack([bound, (bound <= SHIFT_MAX).astype(F32)], axis=-1).astype(F32)


def _layer_spec(a, l):
    return pl.BlockSpec((None,) + a.shape[1:], lambda *_, _n=a.ndim - 1: (l,) + (0,) * _n)


def _attention(body, name, q, k, v, bound, extra, n_rb, out_width=256):
    n_b = q.shape[0]
    extra = [(bound, pl.BlockSpec(memory_space=pltpu.SMEM))] + list(extra)
    in_specs = [
        pl.BlockSpec((1, T, q.shape[2]), lambda b: (b, 0, 0)),
        pl.BlockSpec((1, k.shape[1], T), lambda b: (b, 0, 0)),
        pl.BlockSpec((1, T, v.shape[2]), lambda b: (b, 0, 0)),
    ] + [spec for _, spec in extra]
    return pl.pallas_call(
        functools.partial(body, n_rb=n_rb),
        grid=(n_b,),
        in_specs=in_specs,
        out_specs=pl.BlockSpec((1, n_rb * RB, out_width), lambda b: (b, 0, 0)),
        out_shape=jax.ShapeDtypeStruct((n_b, n_rb * RB, out_width), BF16),
        compiler_params=_params(("parallel",)),
        name=name,
    )(q, k, v, *[a for a, _ in extra])


def _ffn_kernel(x_ref, xb_ref, xa_ref, m0_ref, m1_ref, m2_ref, m3_ref, m0b_ref, m1b_ref, m2b_ref, m3b_ref,
                m0a_ref, m1a_ref, m2a_ref, m3a_ref, modb_ref, modc_ref, g_ref, wo_ref, wa_ref, wg_ref, cw_ref,
                wd_ref, o_ref, h_scr, x1_scr, u0_scr, u1_scr, act_scr, *, tm, rows_per_sample):
    rb = pl.program_id(1)
    j = pl.program_id(2)
    row0 = rb * tm
    split = SEQ % tm if rows_per_sample > SEQ else 0
    band_lo, band_hi = split - HALO, split + HALO

    def per_row(rows, col):
        lat = modb_ref[0][:, col * D:(col + 1) * D]
        if rows_per_sample == SEQ:
            return lat
        return jnp.where(rows >= SEQ, modc_ref[0][:, col * D:(col + 1) * D], lat)

    def modulated(x, rows):
        ms = jnp.mean(x * x, axis=-1, keepdims=True)
        y = x * lax.rsqrt(ms + EPS) * g_ref[...]
        return (y * (1.0 + per_row(rows, 4)) + per_row(rows, 3)).astype(BF16)

    def prologue():
        rows = row0 - HALO + lax.broadcasted_iota(jnp.int32, (tm + 2 * HALO, 1), 0)
        x = jnp.concatenate([xb_ref[0], x_ref[0], xa_ref[0]], axis=0)
        mixes = [jnp.concatenate([b[0], m[0], a[0]], axis=0) for b, m, a in (
            (m0b_ref, m0_ref, m0a_ref), (m1b_ref, m1_ref, m1a_ref),
            (m2b_ref, m2_ref, m2a_ref), (m3b_ref, m3_ref, m3a_ref))]
        acc = _dot(mixes[0], wo_ref[0:256, :])
        for i in range(1, 4):
            acc += _dot(mixes[i], wo_ref[i * 256:(i + 1) * 256, :])
        x1 = x + per_row(rows, 2) * acc
        x1_scr[...] = x1[HALO:HALO + tm]
        h = modulated(x1, rows)
        has_before = (row0 != 0) & (row0 != SEQ)
        has_after = (row0 + tm != SEQ) & (row0 + tm != rows_per_sample)
        zero = jnp.zeros((HALO, D), BF16)
        h_scr[0:HALO, :] = jnp.where(has_before, h[0:HALO], zero)
        h_scr[HALO + tm:, :] = jnp.where(has_after, h[HALO + tm:], zero)
        h_scr[HALO:HALO + tm, :] = h[HALO:HALO + tm]
        o_ref[0] = jnp.zeros((tm, D), F32)

    slots = (u0_scr, u1_scr)

    def tiles(total):
        units = total // HALO
        sizes = [(units // FFN_TILES + (1 if t < units % FFN_TILES else 0)) * HALO for t in range(FFN_TILES)]
        edges = np.cumsum([0] + sizes)
        return [(int(edges[t]), int(edges[t + 1])) for t in range(FFN_TILES)]

    up_tiles, down_tiles = tiles(tm + 2 * HALO), tiles(tm)

    def up(slot, t):
        lo, hi = up_tiles[t]
        slots[slot][lo:hi, :FF_CHUNK] = _dot(h_scr[lo:hi, :], wa_ref[...])
        slots[slot][lo:hi, FF_CHUNK:] = _dot(h_scr[lo:hi, :], wg_ref[...])

    def conv(u_scr, lo, hi, c_lo, c_hi, masked):
        prev = u_scr[HALO - 1 + lo:HALO - 1 + hi, c_lo:c_hi]
        nxt = u_scr[HALO + 1 + lo:HALO + 1 + hi, c_lo:c_hi]
        if masked:
            rows = row0 + lo + lax.broadcasted_iota(jnp.int32, (hi - lo, 1), 0)
            prev = jnp.where(rows != SEQ, prev, 0.0)
            nxt = jnp.where(rows != SEQ - 1, nxt, 0.0)
        return (prev * cw_ref[0:1, c_lo:c_hi] + u_scr[HALO + lo:HALO + hi, c_lo:c_hi] * cw_ref[1:2, c_lo:c_hi]
                + nxt * cw_ref[2:3, c_lo:c_hi] + cw_ref[3:4, c_lo:c_hi])

    def conv_down(slot, t):
        t_lo, t_hi = down_tiles[t]
        cuts = sorted({t_lo, t_hi} | ({c for c in (band_lo, band_hi) if t_lo < c < t_hi} if split else set()))
        for lo, hi in zip(cuts[:-1], cuts[1:]):
            masked = bool(split) and band_lo <= lo < band_hi
            for c_lo in range(0, FF_CHUNK, CONV_STRIP):
                a = conv(slots[slot], lo, hi, c_lo, c_lo + CONV_STRIP, masked)
                g = conv(slots[slot], lo, hi, FF_CHUNK + c_lo, FF_CHUNK + c_lo + CONV_STRIP, masked)
                act_scr[lo:hi, c_lo:c_lo + CONV_STRIP] = (g * jax.nn.sigmoid(g) * a).astype(BF16)
        o_ref[0, t_lo:t_hi, :] += _dot(act_scr[t_lo:t_hi, :], wd_ref[...])

    @pl.when(j == 0)
    def _():
        prologue()
        for t in range(FFN_TILES):
            up(0, t)

    for parity in range(2):
        @pl.when((j > 0) & (j < N_CHUNK) & (j % 2 == parity))
        def _():
            for t in range(FFN_TILES):
                up(parity, t)
                conv_down(1 - parity, t)

    @pl.when(j == N_CHUNK)
    def _():
        for t in range(FFN_TILES):
            conv_down((N_CHUNK - 1) % 2, t)
        rows = row0 + lax.broadcasted_iota(jnp.int32, (tm, 1), 0)
        o_ref[0] = x1_scr[...] + per_row(rows, 5) * o_ref[0]


def _chunk_interleave(a):
    parts = []
    for j in range(N_CHUNK):
        parts += [a[..., j * FF_CHUNK:(j + 1) * FF_CHUNK], a[..., D_FF + j * FF_CHUNK:D_FF + (j + 1) * FF_CHUNK]]
    return jnp.concatenate(parts, axis=-1)


def _ffn(l, xs, mixes, mods, g_ffn, w_out, w_up, conv_wb, w_down, rows_per_sample, tm):
    n_b = xs.shape[0]
    n_rb = rows_per_sample // tm
    halo_per_block = tm // HALO
    body = functools.partial(_ffn_kernel, tm=tm, rows_per_sample=rows_per_sample)

    def up_idx(j):
        return jnp.minimum(j, N_CHUNK - 1)

    def down_idx(j):
        return jnp.maximum(j - 1, 0)

    def block_and_halos(a):
        width, last = a.shape[2], a.shape[1] // HALO - 1
        return (pl.BlockSpec((1, tm, width), lambda b, r, j: (b, r, 0)),
                pl.BlockSpec((1, HALO, width), lambda b, r, j: (b, jnp.maximum(r * halo_per_block - 1, 0), 0)),
                pl.BlockSpec((1, HALO, width), lambda b, r, j: (b, jnp.minimum((r + 1) * halo_per_block, last), 0)))

    x_specs = block_and_halos(xs)
    mix_specs = [block_and_halos(m) for m in mixes]
    return pl.pallas_call(
        body,
        grid=(n_b, n_rb, N_CHUNK + 1),
        in_specs=[
            *x_specs,
            *[s[0] for s in mix_specs], *[s[1] for s in mix_specs], *[s[2] for s in mix_specs],
            pl.BlockSpec((None, 1, 1, 6 * D), lambda b, r, j: (l, b, 0, 0)),
            pl.BlockSpec((None, 1, 1, 6 * D), lambda b, r, j: (l, n_b, 0, 0)),
            _layer_spec(g_ffn, l),
            _layer_spec(w_out, l),
            pl.BlockSpec((None, D, FF_CHUNK), lambda b, r, j: (l, 0, up_idx(j))),
            pl.BlockSpec((None, D, FF_CHUNK), lambda b, r, j: (l, 0, N_CHUNK + up_idx(j))),
            pl.BlockSpec((None, 4, 2 * FF_CHUNK), lambda b, r, j: (l, 0, down_idx(j))),
            pl.BlockSpec((None, FF_CHUNK, D), lambda b, r, j: (l, down_idx(j), 0)),
        ],
        out_specs=pl.BlockSpec((1, tm, D), lambda b, r, j: (b, r, 0)),
        out_shape=jax.ShapeDtypeStruct((n_b, rows_per_sample, D), F32),
        scratch_shapes=[
            pltpu.VMEM((tm + 2 * HALO, D), BF16),
            pltpu.VMEM((tm, D), F32),
            pltpu.VMEM((tm + 2 * HALO, 2 * FF_CHUNK), F32),
            pltpu.VMEM((tm + 2 * HALO, 2 * FF_CHUNK), F32),
            pltpu.VMEM((tm, FF_CHUNK), BF16),
        ],
        compiler_params=_params(("parallel", "parallel", "arbitrary")),
        name="outproj_conv_ffn",
    )(xs, xs, xs, *mixes, *mixes, *mixes, mods, mods, g_ffn, w_out, w_up, w_up, conv_wb, w_down)


def _rope_table(rot_dim):
    n_freq = rot_dim // 4
    inv = jnp.power(ROPE_THETA, -jnp.arange(n_freq, dtype=F32) / n_freq)
    t = jnp.arange(SEQ)
    row = (t // GRID_W).astype(F32)
    col = (t % GRID_W).astype(F32)
    ar, ac = row[:, None] * inv, col[:, None] * inv
    ang = jnp.concatenate([ar, ar, ac, ac], axis=-1)
    sign = jnp.concatenate([-jnp.ones(n_freq), jnp.ones(n_freq), -jnp.ones(n_freq), jnp.ones(n_freq)]).astype(F32)
    cos = jnp.concatenate([jnp.cos(ang), jnp.ones((CTX, rot_dim), F32)], axis=0)
    sin = jnp.concatenate([jnp.sin(ang) * sign, jnp.zeros((CTX, rot_dim), F32)], axis=0)
    return jnp.concatenate([cos.T, sin.T], axis=0)


def _na_bias_tiles(rpb):
    n_l, n_h = rpb.shape[:2]
    n_off = 2 * NA_ROWS - 1
    p = jnp.pad(rpb.astype(F32) * LOG2E, ((0, 0), (0, 0), (0, 0), (48, 49)))
    sk = jnp.broadcast_to(p[..., None, :], (n_l, n_h, n_off, GRID_W, 128)).reshape(n_l, n_h, n_off, GRID_W * 128)
    sk = sk[..., :GRID_W * 127].reshape(n_l, n_h, n_off, GRID_W, 127)[..., 63:127]
    c = np.arange(GRID_W)
    win_start = np.clip(c - NA_COLS // 2, 0, GRID_W - NA_COLS)
    v_col = (c[None, :] >= win_start[:, None]) & (c[None, :] < win_start[:, None] + NA_COLS)
    tz = jnp.where(v_col, sk, NEG_INF)
    neg1 = jnp.full((n_l, n_h, 1, GRID_W, GRID_W), NEG_INF, F32)
    tzx = jnp.concatenate([neg1, tz, neg1], axis=2)
    first, second = tzx[:, :, 0:16], tzx[:, :, 1:17]
    neg16 = jnp.full_like(first, NEG_INF)
    return jnp.concatenate([
        jnp.concatenate([first, second], axis=-1), jnp.concatenate([first, neg16], axis=-1),
        jnp.concatenate([neg16, second], axis=-1), jnp.concatenate([neg1, neg1], axis=-1)], axis=2)


def kernel(x, c, ctx, c_ctx, w_mod, b_mod, g_mix, w_in, w_out, mla_q_a_g, mla_w_uq, mla_kv_a_g, mla_w_ukv,
           mla_q_g, mla_k_g, diff_q_g, diff_k_g, diff_lq1, diff_lk1, diff_lq2, diff_lk2, diff_subln_g,
           na_q_g, na_k_g, na_rpb, gqa_q_g, gqa_k_g, g_ffn, w_up, conv_w, conv_b, w_down):
    n_b = x.shape[0]
    n_layer = w_mod.shape[0]
    assert x.shape[1:] == (SEQ, D) and ctx.shape[1:] == (CTX, D)

    xs = jnp.concatenate([x, ctx], axis=1)
    mod_rows = -(-(n_b + 1) // 8) * 8
    cc = jnp.concatenate([c, c_ctx[None], jnp.zeros((mod_rows - n_b - 1, D), F32)], axis=0)
    mods_all = _modulation(cc, w_mod, b_mod).reshape(n_layer, mod_rows, 1, 6 * D)

    cs32, cs64 = _rope_table(32), _rope_table(64)
    bias_all = _na_bias_tiles(na_rpb)

    s = np.cumsum([0, 256, 128, 32, 256, 256, 256, 256, 256, 256, 256, 128, 128])
    gq_perm = np.concatenate([np.arange(GQA_D) + (2 * g + r) * GQA_D for r in range(2) for g in range(2)])

    n_l = n_layer
    gq_cols = w_in[:, :, s[9]:s[10]][:, :, gq_perm]
    wqk = jnp.concatenate([w_in[:, :, s[0]:s[3]], w_in[:, :, s[3]:s[5]], w_in[:, :, s[6]:s[8]], gq_cols,
                           w_in[:, :, s[10]:s[11]]], axis=2).swapaxes(1, 2).astype(BF16)
    wv = jnp.concatenate([w_in[:, :, s[5]:s[6]], w_in[:, :, s[8]:s[9]], w_in[:, :, s[11]:s[12]]],
                         axis=2).astype(BF16)
    wuq = mla_w_uq.swapaxes(1, 2).astype(BF16)
    wukv = mla_w_ukv.reshape(n_l, MLA_KVR, MLA_H, MLA_NOPE + MLA_V)
    wukn = wukv[..., :MLA_NOPE].reshape(n_l, MLA_KVR, MLA_H * MLA_NOPE).swapaxes(1, 2).astype(BF16)
    wuv = wukv[..., MLA_NOPE:].reshape(n_l, MLA_KVR, MLA_H * MLA_V).swapaxes(1, 2).astype(BF16)
    gcol = jnp.concatenate([mla_q_a_g, mla_kv_a_g, mla_q_g, mla_k_g, diff_q_g, diff_k_g, na_q_g, na_k_g,
                            gqa_q_g, gqa_k_g], axis=1).astype(F32)[..., None]
    lam_rows = jnp.stack([diff_lq1, diff_lk1, diff_lq2, diff_lk2], axis=1).astype(F32)
    lam_rows = jnp.concatenate([lam_rows, jnp.zeros_like(lam_rows)], axis=1)
    gsub = jnp.tile(diff_subln_g.astype(F32), (1, 2)).reshape(n_l, 1, 128)
    bounds = {
        "mla": _logit_bound(MLA_D, mla_q_g, mla_k_g),
        "diff": _logit_bound(DIFF_D, diff_q_g, diff_k_g),
        "na": _logit_bound(NA_D, na_q_g, na_k_g, LOG2E * jnp.maximum(jnp.max(na_rpb, axis=(1, 2, 3)), 0.0)),
        "gqa": _logit_bound(GQA_D, gqa_q_g, gqa_k_g),
    }
    w_out_b, w_down_b = w_out.astype(BF16), w_down.astype(BF16)
    w_up_b = w_up.astype(BF16)
    conv_wb = _chunk_interleave(jnp.concatenate([conv_w, conv_b[:, None, :]], axis=1).astype(F32))
    gmix, gffn = g_mix.reshape(n_l, 1, D), g_ffn.reshape(n_l, 1, D)

    for l in range(n_layer):
        with_ctx = l < n_layer - 1
        lambda_init = 0.8 - 0.6 * math.exp(-0.3 * l)
        (q_mla, k_mla, v_mla, q_diff, k_diff, v_diff, q_na, k_na, v_na, q_gqa, k_gqa, v_gqa) = _inproj(
            l, xs, mods_all, gmix, wqk, wv, wuq, wukn, wuv, gcol, cs32, cs64)

        n_rb = NRB if with_ctx else NLAT
        mix_a = _attention(_mla_kernel, "attn_mla", q_mla, k_mla, v_mla, bounds["mla"][l], [], n_rb)
        mix_b = _attention(
            functools.partial(_diff_kernel, lambda_init=lambda_init), "attn_diff",
            q_diff, k_diff, v_diff, bounds["diff"][l],
            [(lam_rows, _layer_spec(lam_rows, l)), (gsub, _layer_spec(gsub, l))], n_rb)
        mix_c = _attention(_na_kernel, "attn_na", q_na, k_na, v_na, bounds["na"][l],
                           [(bias_all, _layer_spec(bias_all, l))], n_rb)
        mix_d = _attention(_gqa_kernel, "attn_gqa", q_gqa, k_gqa, v_gqa, bounds["gqa"][l], [], n_rb)

        rows_per_sample = T if with_ctx else SEQ
        xs = _ffn(l, xs, (mix_a, mix_b, mix_c, mix_d), mods_all, gffn, w_out_b, w_up_b, conv_wb, w_down_b,
                  rows_per_sample, rows_per_sample // 2)
    return xs
```

```python
import functools
import math

import numpy as np
import jax
import jax.numpy as jnp
from jax import lax
from jax.experimental import pallas as pl
from jax.experimental.pallas import tpu as pltpu

F32 = jnp.float32
BF16 = jnp.bfloat16

D = 1024
SEQ = 2048
GRID_W = 64
CTX = 256
T = SEQ + CTX
RB = 256
NRB = T // RB
NLAT = SEQ // RB
IN_SUB = 3
EPS = 1e-6
NEG_INF = -1e30
ROPE_THETA = 10000.0
LOG2E = 1.4426950408889634

MLA_H, MLA_NOPE, MLA_ROPE, MLA_V, MLA_QR, MLA_KVR = 4, 64, 32, 64, 256, 128
MLA_D = MLA_NOPE + MLA_ROPE
DIFF_H, DIFF_D = 4, 32
NA_H, NA_D, NA_ROWS, NA_COLS = 4, 64, 8, 16
GQA_H, GQA_KV, GQA_D = 4, 2, 64
D_FF = 2816
FF_CHUNK = 512
N_CHUNK = -(-D_FF // FF_CHUNK)
FF_WIDTHS = tuple(min(FF_CHUNK, D_FF - c * FF_CHUNK) for c in range(N_CHUNK))
OUT_TM, OUT_TM_LAST = 768, 1024
HALO = 16
FFN_TILES = 4
CONV_STRIP = 256

R_CQ, R_CKV, R_KR, R_DQ, R_DK, R_NQ, R_NK, R_GQ, R_GK = 0, 256, 384, 416, 672, 928, 1184, 1440, 1696
QK_ROWS = 1824
V_COLS = 640

G_QA, G_KVA, G_MQ, G_MK, G_DQ, G_DK, G_NQ, G_NK, G_GQ, G_GK = 0, 256, 384, 480, 576, 608, 640, 704, 768, 832
G_ROWS = 896

VMEM_LIMIT = 56 * 1024 * 1024


def _params(sem):
    return pltpu.CompilerParams(dimension_semantics=sem, vmem_limit_bytes=VMEM_LIMIT)


def _dot(a, b):
    return jnp.dot(a, b, preferred_element_type=F32)


def _dot_nt(a, b):
    return lax.dot_general(a, b, (((1,), (1,)), ((), ())), preferred_element_type=F32)


def _mod_kernel(c_ref, w_ref, b_ref, o_ref):
    c = c_ref[...]
    a = (c * jax.nn.sigmoid(c)).astype(BF16)
    o_ref[0] = _dot(a, w_ref[0].astype(BF16)) + b_ref[0]


def _modulation(cc, w_mod, b_mod):
    n_layer = w_mod.shape[0]
    rows = cc.shape[0]
    return pl.pallas_call(
        _mod_kernel,
        grid=(n_layer, 6),
        in_specs=[
            pl.BlockSpec((rows, D), lambda l, j: (0, 0)),
            pl.BlockSpec((1, D, D), lambda l, j: (l, 0, j)),
            pl.BlockSpec((1, 1, D), lambda l, j: (l, 0, j)),
        ],
        out_specs=pl.BlockSpec((1, rows, D), lambda l, j: (l, 0, j)),
        out_shape=jax.ShapeDtypeStruct((n_layer, rows, 6 * D), F32),
        compiler_params=_params(("parallel", "parallel")),
        name="modulation",
    )(cc, w_mod, b_mod.reshape(n_layer, 1, 6 * D))


def _rms_rows(x, g, n):
    ss = jnp.sum(x * x, axis=0, keepdims=True) * (1.0 / n)
    return x * lax.rsqrt(ss + EPS) * g


def _rope_rows(x, cos, sin_signed, w):
    rot = jnp.concatenate([x[w:2 * w], x[0:w], x[3 * w:4 * w], x[2 * w:3 * w]], axis=0)
    return x * cos + rot * sin_signed


def _inproj_kernel(x_ref, modb_ref, modc_ref, gmix_ref, wqk_ref, wv_ref, wuq_ref, wukn_ref, wuv_ref, gcol_ref,
                   cs32_ref, cs64_ref,
                   qmla_ref, kmla_ref, vmla_ref, qdiff_ref, kdiff_ref, vdiff_ref,
                   qna_ref, kna_ref, vna_ref, qgqa_ref, kgqa_ref, vgqa_ref):
    step = pl.program_id(0)

    def modulated(sb):
        r0 = sb * RB
        x = x_ref[0, r0:r0 + RB, :]
        mod = modb_ref[0]
        if sb == IN_SUB - 1:
            mod = jnp.where(step == NRB // IN_SUB - 1, modc_ref[0], mod)
        shift, scale = mod[:, 0:D], mod[:, D:2 * D]
        ms = jnp.mean(x * x, axis=-1, keepdims=True)
        h = x * lax.rsqrt(ms + EPS) * gmix_ref[...]
        return (h * (1.0 + scale) + shift).astype(BF16)

    def gain(off, n):
        return gcol_ref[off:off + n, :]

    def block(sb, hb, after_first_projection):
        rows = slice(sb * RB, (sb + 1) * RB)
        _inproj_block(rows, hb, after_first_projection, gain, wqk_ref, wv_ref, wuq_ref, wukn_ref, wuv_ref,
                      cs32_ref, cs64_ref, qmla_ref, kmla_ref, vmla_ref, qdiff_ref, kdiff_ref, vdiff_ref,
                      qna_ref, kna_ref, vna_ref, qgqa_ref, kgqa_ref, vgqa_ref)

    hbs = [modulated(0)]
    for sb in range(IN_SUB):
        nxt = (lambda sb=sb: hbs.append(modulated(sb + 1))) if sb + 1 < IN_SUB else (lambda: None)
        block(sb, hbs[sb], nxt)


def _inproj_block(rows, hb, after_first_projection, gain, wqk_ref, wv_ref, wuq_ref, wukn_ref, wuv_ref,
                  cs32_ref, cs64_ref, qmla_ref, kmla_ref, vmla_ref, qdiff_ref, kdiff_ref, vdiff_ref,
                  qna_ref, kna_ref, vna_ref, qgqa_ref, kgqa_ref, vgqa_ref):
    def project(lo, hi):
        return _dot_nt(wqk_ref[lo:hi, :], hb)

    cos32, sin32 = cs32_ref[0:32, rows], cs32_ref[32:64, rows]
    cos64, sin64 = cs64_ref[0:64, rows], cs64_ref[64:128, rows]

    def values():
        pv = _dot(hb, wv_ref[...])
        vdiff_ref[0, rows, :] = pv[:, 0:256].astype(BF16)
        vna_ref[0, rows, :] = pv[:, 256:512].astype(BF16)
        vgqa_ref[0, rows, :] = pv[:, 512:640].astype(BF16)

    def mla(pt):
        cq = _rms_rows(pt[0:MLA_QR], gain(G_QA, MLA_QR), MLA_QR).astype(BF16)
        qt = _dot(wuq_ref[...], cq)
        ckv = _rms_rows(pt[R_CKV:R_CKV + MLA_KVR], gain(G_KVA, MLA_KVR), MLA_KVR).astype(BF16)
        knt = _dot(wukn_ref[...], ckv)
        vt = _dot(wuv_ref[...], ckv)
        vmla_ref[0, rows, :] = vt.T.astype(BF16)
        kr = pt[R_KR:R_KR + MLA_ROPE]
        kr_ss = jnp.sum(kr * kr, axis=0, keepdims=True)
        g_mq, g_mk = gain(G_MQ, MLA_D), gain(G_MK, MLA_D)
        zpad = jnp.zeros((128 - MLA_D, RB), F32)
        q_parts = []
        for hd in range(MLA_H):
            qh = _rms_rows(qt[hd * MLA_D:(hd + 1) * MLA_D], g_mq, MLA_D)
            q_rope = _rope_rows(qh[MLA_NOPE:], cos32, sin32, MLA_ROPE // 4)
            q_parts += [qh[:MLA_NOPE], q_rope, zpad]
            kn = knt[hd * MLA_NOPE:(hd + 1) * MLA_NOPE]
            ss = (jnp.sum(kn * kn, axis=0, keepdims=True) + kr_ss) * (1.0 / MLA_D)
            r = lax.rsqrt(ss + EPS)
            k_rope = _rope_rows(kr * r * g_mk[MLA_NOPE:], cos32, sin32, MLA_ROPE // 4)
            kmla_ref[0, hd * 128:(hd + 1) * 128, rows] = jnp.concatenate(
                [kn * r * g_mk[:MLA_NOPE], k_rope, zpad], axis=0).astype(BF16)
        q_all = jnp.concatenate(q_parts, axis=0) * (MLA_D ** -0.5 * LOG2E)
        qmla_ref[0, rows, :] = q_all.T.astype(BF16)

    def diff(pt):
        g_dq, g_dk = gain(G_DQ, DIFF_D), gain(G_DK, DIFF_D)
        q_parts, k_parts = [], []
        for gi in range(2 * DIFF_H):
            qg = _rms_rows(pt[gi * DIFF_D:(gi + 1) * DIFF_D], g_dq, DIFF_D)
            q_parts.append(_rope_rows(qg, cos32, sin32, DIFF_D // 4))
            kg = _rms_rows(pt[256 + gi * DIFF_D:256 + (gi + 1) * DIFF_D], g_dk, DIFF_D)
            k_parts.append(_rope_rows(kg, cos32, sin32, DIFF_D // 4))
        qdiff_ref[0, rows, :] = (jnp.concatenate(q_parts, axis=0) * (DIFF_D ** -0.5 * LOG2E)).T.astype(BF16)
        kdiff_ref[0, :, rows] = jnp.concatenate(k_parts, axis=0).astype(BF16)

    def na(pt):
        g_nq, g_nk = gain(G_NQ, NA_D), gain(G_NK, NA_D)
        q_parts, k_parts = [], []
        for hd in range(NA_H):
            q_parts.append(_rms_rows(pt[hd * NA_D:(hd + 1) * NA_D], g_nq, NA_D))
            k_parts.append(_rms_rows(pt[256 + hd * NA_D:256 + (hd + 1) * NA_D], g_nk, NA_D))
        qna_ref[0, rows, :] = (jnp.concatenate(q_parts, axis=0) * (NA_D ** -0.5 * LOG2E)).T.astype(BF16)
        kna_ref[0, :, rows] = jnp.concatenate(k_parts, axis=0).astype(BF16)

    def gqa(pt):
        g_gq, g_gk = gain(G_GQ, GQA_D), gain(G_GK, GQA_D)
        q_parts, k_parts = [], []
        for hd in range(GQA_H):
            qg = _rms_rows(pt[hd * GQA_D:(hd + 1) * GQA_D], g_gq, GQA_D)
            q_parts.append(_rope_rows(qg, cos64, sin64, GQA_D // 4))
        for hd in range(GQA_KV):
            kg = _rms_rows(pt[256 + hd * GQA_D:256 + (hd + 1) * GQA_D], g_gk, GQA_D)
            k_parts.append(_rope_rows(kg, cos64, sin64, GQA_D // 4))
        qgqa_ref[0, rows, :] = (jnp.concatenate(q_parts, axis=0) * (GQA_D ** -0.5 * LOG2E)).T.astype(BF16)
        kgqa_ref[0, :, rows] = jnp.concatenate(k_parts, axis=0).astype(BF16)

    pt_mla = project(R_CQ, R_DQ)
    after_first_projection()
    pt_diff = project(R_DQ, R_NQ)
    mla(pt_mla)
    pt_na = project(R_NQ, R_GQ)
    diff(pt_diff)
    pt_gqa = project(R_GQ, QK_ROWS)
    na(pt_na)
    values()
    gqa(pt_gqa)


def _inproj(l, xs, mods, gmix, wqk, wv, wuq, wukn, wuv, gcol, cs32, cs64):
    n_b = xs.shape[0]

    def full(a):
        return _layer_spec(a, l)

    rows_per_step = IN_SUB * RB

    def tok(width):
        return pl.BlockSpec((1, rows_per_step, width), lambda r, b: (b, r, 0))

    def chan(rows):
        return pl.BlockSpec((1, rows, rows_per_step), lambda r, b: (b, 0, r))

    def tshape(width):
        return jax.ShapeDtypeStruct((n_b, T, width), BF16)

    def cshape(rows):
        return jax.ShapeDtypeStruct((n_b, rows, T), BF16)

    return pl.pallas_call(
        _inproj_kernel,
        grid=(NRB // IN_SUB, n_b),
        in_specs=[
            pl.BlockSpec((1, rows_per_step, D), lambda r, b: (b, r, 0)),
            pl.BlockSpec((None, 1, 1, 6 * D), lambda r, b: (l, b, 0, 0)),
            pl.BlockSpec((None, 1, 1, 6 * D), lambda r, b: (l, n_b, 0, 0)),
            full(gmix), full(wqk), full(wv), full(wuq), full(wukn), full(wuv), full(gcol),
            pl.BlockSpec((64, rows_per_step), lambda r, b: (0, r)),
            pl.BlockSpec((128, rows_per_step), lambda r, b: (0, r)),
        ],
        out_specs=[tok(512), chan(512), tok(256), tok(256), chan(256), tok(256),
                   tok(256), chan(256), tok(256), tok(256), chan(128), tok(128)],
        out_shape=[tshape(512), cshape(512), tshape(256), tshape(256), cshape(256), tshape(256),
                   tshape(256), cshape(256), tshape(256), tshape(256), cshape(128), tshape(128)],
        compiler_params=_params(("parallel", "parallel")),
        name="inproj_prep",
    )(xs, mods, mods, gmix, wqk, wv, wuq, wukn, wuv, gcol, cs32, cs64)


def _lane_id(shape):
    return lax.broadcasted_iota(jnp.int32, shape, 1)


KEY_CHUNK = 768
LAT_CHUNKS = tuple((lo, KEY_CHUNK) for lo in range(0, T, KEY_CHUNK))
CTX_CHUNKS = ((SEQ, CTX),)


def _k_slab(k_ref, idx):
    return lambda lo, n: k_ref[0, idx * 128:(idx + 1) * 128, pl.ds(lo, n)]


def _v_slab(v_ref, idx):
    return lambda lo, n: v_ref[0, pl.ds(lo, n), idx * 128:(idx + 1) * 128]


def _attend(units, chunks, shift=None):
    def scores(i, c):
        q, keys, _, bias = units[i]
        lo, n = chunks[c]
        s = _dot(q, keys(lo, n))
        b = None if bias is None else bias(c)
        return s if b is None else s + b

    if shift is not None:
        items = [(i, c) for i in range(len(units)) for c in range(len(chunks))]
        acc, l = [None] * len(units), [None] * len(units)
        nxt = scores(*items[0])
        for idx, (i, c) in enumerate(items):
            cur = nxt
            if idx + 1 < len(items):
                nxt = scores(*items[idx + 1])
            e = jnp.exp2(cur - shift)
            lc = jnp.sum(e, axis=-1, keepdims=True)
            oc = _dot(e.astype(BF16), units[i][2](*chunks[c]))
            acc[i] = oc if acc[i] is None else acc[i] + oc
            l[i] = lc if l[i] is None else l[i] + lc
        return list(zip(acc, l))

    cur = [scores(0, c) for c in range(len(chunks))]
    out = []
    for i in range(len(units)):
        m = functools.reduce(jnp.maximum, [jnp.max(s, axis=-1, keepdims=True) for s in cur])
        nxt, acc, l = [], None, None
        for c, (lo, n) in enumerate(chunks):
            if i + 1 < len(units):
                nxt.append(scores(i + 1, c))
            e = jnp.exp2(cur[c] - m)
            lc = jnp.sum(e, axis=-1, keepdims=True)
            oc = _dot(e.astype(BF16), units[i][2](lo, n))
            acc = oc if acc is None else acc + oc
            l = lc if l is None else l + lc
        out.append((acc, l))
        cur = nxt
    return out


def _with_shift(bnd_ref, run):
    @pl.when(bnd_ref[1] > 0.5)
    def _():
        run(bnd_ref[0])

    @pl.when(bnd_ref[1] <= 0.5)
    def _():
        run(None)


def _query_blocks(n_rb, bnd_ref, run):
    def all_blocks(shift):
        def body(rb, carry):
            run(rb, pl.ds(pl.multiple_of(rb * RB, RB), RB), True, shift)
            return carry

        lax.fori_loop(0, NLAT, body, 0)
        if n_rb > NLAT:
            run(NLAT, pl.ds(SEQ, CTX), False, shift)

    _with_shift(bnd_ref, all_blocks)


def _mla_kernel(q_ref, k_ref, v_ref, bnd_ref, o_ref, *, n_rb):
    lane = _lane_id((RB, 128))

    def run(rb, rows, latent, shift):
        units = [(q_ref[0, rows,hd * 128:(hd + 1) * 128], _k_slab(k_ref, hd), _v_slab(v_ref, hd // 2), None)
                 for hd in range(MLA_H)]
        o = [acc * (1.0 / l) for acc, l in _attend(units, LAT_CHUNKS if latent else CTX_CHUNKS, shift)]
        outs = [jnp.where(lane < 64, o[0], o[1]), jnp.where(lane < 64, o[2], o[3])]
        o_ref[0, rows, :] = jnp.concatenate(outs, axis=1).astype(o_ref.dtype)

    _query_blocks(n_rb, bnd_ref, run)


def _gqa_kernel(q_ref, k_ref, v_ref, bnd_ref, o_ref, *, n_rb):
    lane = _lane_id((RB, 128))

    def run(rb, rows, latent, shift):
        order = [(rep, grp) for rep in range(2) for grp in range(2)]
        units = []
        for rep, grp in order:
            qs = q_ref[0, rows,rep * 128:(rep + 1) * 128]
            qm = jnp.where((lane >= 64) == (grp == 1), qs, jnp.zeros_like(qs))
            units.append((qm, _k_slab(k_ref, 0), _v_slab(v_ref, 0), None))
        res = {}
        for (rep, grp), (acc, l) in zip(order, _attend(units, LAT_CHUNKS if latent else CTX_CHUNKS, shift)):
            o = acc * (1.0 / l)
            res[(grp, rep)] = o if grp == rep else pltpu.roll(o, 64, axis=1)
        outs = [jnp.where(lane < 64, res[(grp, 0)], res[(grp, 1)]) for grp in range(2)]
        o_ref[0, rows, :] = jnp.concatenate(outs, axis=1).astype(o_ref.dtype)

    _query_blocks(n_rb, bnd_ref, run)


def _diff_kernel(q_ref, k_ref, v_ref, bnd_ref, lam_ref, g_ref, o_ref, *, n_rb, lambda_init):
    lane = _lane_id((RB, 128))
    lq1, lk1, lq2, lk2 = lam_ref[0:1, :], lam_ref[1:2, :], lam_ref[2:3, :], lam_ref[3:4, :]
    lam = (jnp.exp(jnp.sum(lq1 * lk1, axis=-1, keepdims=True))
           - jnp.exp(jnp.sum(lq2 * lk2, axis=-1, keepdims=True)) + lambda_init)
    gsub = g_ref[...]
    grp = lane // DIFF_D

    def run(rb, rows, latent, shift):
        units = []
        for hd in range(DIFF_H):
            pair, sub = divmod(hd, 2)
            qs = q_ref[0, rows,pair * 128:(pair + 1) * 128]
            for which in range(2):
                qm = jnp.where(grp == 2 * sub + which, qs, jnp.zeros_like(qs))
                units.append((qm, _k_slab(k_ref, pair), _v_slab(v_ref, pair), None))
        res = _attend(units, LAT_CHUNKS if latent else CTX_CHUNKS, shift)
        outs = []
        for pair in range(2):
            halves = []
            for sub in range(2):
                (a1, l1), (a2, l2) = res[2 * (2 * pair + sub)], res[2 * (2 * pair + sub) + 1]
                halves.append(a1 * (1.0 / l1) - a2 * (lam / l2))
            o = jnp.where(lane < 64, halves[0], halves[1])
            o2 = o * o
            ss0 = jnp.sum(jnp.where(lane < 64, o2, 0.0), axis=-1, keepdims=True)
            ss1 = jnp.sum(jnp.where(lane < 64, 0.0, o2), axis=-1, keepdims=True)
            ss = jnp.where(lane < 64, ss0, ss1) * (1.0 / (2 * DIFF_D))
            outs.append(o * lax.rsqrt(ss + EPS) * gsub * (1.0 - lambda_init))
        o_ref[0, rows, :] = jnp.concatenate(outs, axis=1).astype(o_ref.dtype)

    _query_blocks(n_rb, bnd_ref, run)


NA_WIN = 768
NA_CHUNK = 768
NA_TILES = 49
GRID_ROWS = SEQ // GRID_W


def _na_kernel(q_ref, k_ref, v_ref, bnd_ref, bias_ref, o_ref, *, n_rb):
    lane = _lane_id((RB, 128))

    def run(rb, rows, latent, shift):
        na_chunks, tile_idx = CTX_CHUNKS, None
        if latent:
            win_row = jnp.clip(4 * rb - 4, 0, GRID_ROWS - NA_WIN // GRID_W)
            off = pl.multiple_of(win_row * GRID_W, RB)
            local = [(pl.multiple_of(off + c * NA_CHUNK, RB), NA_CHUNK) for c in range(NA_WIN // NA_CHUNK)]
            na_chunks = local + list(CTX_CHUNKS)
            tile_idx = []
            for c in range(NA_WIN // NA_CHUNK):
                per_row = []
                for ri in range(RB // GRID_W):
                    r = 4 * rb + ri
                    row_start = jnp.clip(r - NA_ROWS // 2, 0, GRID_ROWS - NA_ROWS)
                    idxs = []
                    for p in range(NA_CHUNK // 128):
                        kr = win_row + c * (NA_CHUNK // GRID_W) + 2 * p
                        ok1 = (kr >= row_start) & (kr < row_start + NA_ROWS)
                        ok2 = (kr + 1 >= row_start) & (kr + 1 < row_start + NA_ROWS)
                        e = jnp.clip(kr - r + NA_ROWS, 0, 15)
                        idxs.append(jnp.where(ok1 & ok2, e, jnp.where(ok1, 16 + e, jnp.where(ok2, 32 + e, 48))))
                    per_row.append(idxs)
                tile_idx.append(per_row)
        units = []
        for hd in range(NA_H):
            pair, sub = divmod(hd, 2)
            qs = q_ref[0, rows,pair * 128:(pair + 1) * 128]
            qm = jnp.where((lane >= 64) == (sub == 1), qs, jnp.zeros_like(qs))

            def bias(c, hd=hd):
                if tile_idx is None or c >= NA_WIN // NA_CHUNK:
                    return None
                return jnp.concatenate(
                    [jnp.concatenate([bias_ref[hd, idx] for idx in row], axis=1) for row in tile_idx[c]], axis=0)

            units.append((qm, _k_slab(k_ref, pair), _v_slab(v_ref, pair), bias))
        o = [acc * (1.0 / l) for acc, l in _attend(units, na_chunks, shift)]
        outs = [jnp.where(lane < 64, o[0], o[1]), jnp.where(lane < 64, o[2], o[3])]
        o_ref[0, rows, :] = jnp.concatenate(outs, axis=1).astype(o_ref.dtype)

    _query_blocks(n_rb, bnd_ref, run)


SHIFT_MAX = 40.0


def _logit_bound(d, q_gain, k_gain, bias_max=0.0):
    bound = (1.01 * math.sqrt(d) * LOG2E * jnp.max(jnp.abs(q_gain), axis=-1) * jnp.max(jnp.abs(k_gain), axis=-1)
             + bias_max)
    return jnp.stack([bound, (bound <= SHIFT_MAX).astype(F32)], axis=-1).astype(F32)


def _layer_spec(a, l):
    return pl.BlockSpec((None,) + a.shape[1:], lambda *_, _n=a.ndim - 1: (l,) + (0,) * _n)


def _attention(body, name, q, k, v, bound, extra, n_rb, out_width=256):
    n_b = q.shape[0]
    extra = [(bound, pl.BlockSpec(memory_space=pltpu.SMEM))] + list(extra)
    in_specs = [
        pl.BlockSpec((1, T, q.shape[2]), lambda b: (b, 0, 0)),
        pl.BlockSpec((1, k.shape[1], T), lambda b: (b, 0, 0)),
        pl.BlockSpec((1, T, v.shape[2]), lambda b: (b, 0, 0)),
    ] + [spec for _, spec in extra]
    return pl.pallas_call(
        functools.partial(body, n_rb=n_rb),
        grid=(n_b,),
        in_specs=in_specs,
        out_specs=pl.BlockSpec((1, n_rb * RB, out_width), lambda b: (b, 0, 0)),
        out_shape=jax.ShapeDtypeStruct((n_b, n_rb * RB, out_width), BF16),
        compiler_params=_params(("parallel",)),
        name=name,
    )(q, k, v, *[a for a, _ in extra])


def _outproj_kernel(x_ref, modb_ref, modc_ref, ma_ref, mb_ref, mc_ref, md_ref, w_ref, o_ref, *, tm):
    rows = pl.program_id(1) * tm + lax.broadcasted_iota(jnp.int32, (tm, 1), 0)
    g1 = jnp.where(rows >= SEQ, modc_ref[0][:, 2 * D:3 * D], modb_ref[0][:, 2 * D:3 * D])
    acc = _dot(ma_ref[0], w_ref[0:256, :])
    acc += _dot(mb_ref[0], w_ref[256:512, :])
    acc += _dot(mc_ref[0], w_ref[512:768, :])
    acc += _dot(md_ref[0], w_ref[768:1024, :])
    o_ref[0] = x_ref[0] + g1 * acc


def _outproj(l, xs, mods, mixes, w_out, rows_per_sample, tm):
    n_b = xs.shape[0]
    mix_spec = pl.BlockSpec((1, tm, 256), lambda b, r: (b, r, 0))
    return pl.pallas_call(
        functools.partial(_outproj_kernel, tm=tm),
        grid=(n_b, rows_per_sample // tm),
        in_specs=[
            pl.BlockSpec((1, tm, D), lambda b, r: (b, r, 0)),
            pl.BlockSpec((None, 1, 1, 6 * D), lambda b, r: (l, b, 0, 0)),
            pl.BlockSpec((None, 1, 1, 6 * D), lambda b, r: (l, n_b, 0, 0)),
            mix_spec, mix_spec, mix_spec, mix_spec,
            _layer_spec(w_out, l),
        ],
        out_specs=pl.BlockSpec((1, tm, D), lambda b, r: (b, r, 0)),
        out_shape=jax.ShapeDtypeStruct((n_b, rows_per_sample, D), F32),
        compiler_params=_params(("parallel", "parallel")),
        name="outproj",
    )(xs, mods, mods, *mixes, w_out)


def _ffn_kernel(x_ref, hb_ref, ha_ref, modb_ref, modc_ref, g_ref, wa_ref, wg_ref, cwa_ref, cwg_ref, wd_ref, o_ref,
                h_scr, u0_scr, u1_scr, act_scr, *, tm, rows_per_sample):
    rb = pl.program_id(1)
    j = pl.program_id(2)
    row0 = rb * tm
    split = SEQ % tm if rows_per_sample > SEQ else 0
    band_lo, band_hi = split - HALO, split + HALO

    def per_row(rows, col):
        lat = modb_ref[0][:, col * D:(col + 1) * D]
        if rows_per_sample == SEQ:
            return lat
        return jnp.where(rows >= SEQ, modc_ref[0][:, col * D:(col + 1) * D], lat)

    def modulated(x, rows):
        ms = jnp.mean(x * x, axis=-1, keepdims=True)
        y = x * lax.rsqrt(ms + EPS) * g_ref[...]
        return (y * (1.0 + per_row(rows, 4)) + per_row(rows, 3)).astype(BF16)

    def prologue():
        halo_rows = lax.broadcasted_iota(jnp.int32, (HALO, 1), 0)
        has_before = (row0 != 0) & (row0 != SEQ)
        has_after = (row0 + tm != SEQ) & (row0 + tm != rows_per_sample)
        zero = jnp.zeros((HALO, D), BF16)
        h_scr[0:HALO, :] = jnp.where(has_before, modulated(hb_ref[0], row0 - HALO + halo_rows), zero)
        h_scr[HALO + tm:, :] = jnp.where(has_after, modulated(ha_ref[0], row0 + tm + halo_rows), zero)
        main_rows = row0 + lax.broadcasted_iota(jnp.int32, (tm, 1), 0)
        h_scr[HALO:HALO + tm, :] = modulated(x_ref[0], main_rows)
        o_ref[0] = jnp.zeros((tm, D), F32)

    slots = (u0_scr, u1_scr)

    def tiles(total):
        units = total // HALO
        sizes = [(units // FFN_TILES + (1 if t < units % FFN_TILES else 0)) * HALO for t in range(FFN_TILES)]
        edges = np.cumsum([0] + sizes)
        return [(int(edges[t]), int(edges[t + 1])) for t in range(FFN_TILES)]

    up_tiles, down_tiles = tiles(tm + 2 * HALO), tiles(tm)

    def up(c, t):
        lo, hi = up_tiles[t]
        w = FF_WIDTHS[c]
        slots[c % 2][lo:hi, :w] = _dot(h_scr[lo:hi, :], wa_ref[:, :w])
        slots[c % 2][lo:hi, FF_CHUNK:FF_CHUNK + w] = _dot(h_scr[lo:hi, :], wg_ref[:, :w])

    def conv(u_scr, cw_ref, lo, hi, u_lo, c_lo, width, masked):
        prev = u_scr[HALO - 1 + lo:HALO - 1 + hi, u_lo:u_lo + width]
        nxt = u_scr[HALO + 1 + lo:HALO + 1 + hi, u_lo:u_lo + width]
        if masked:
            rows = row0 + lo + lax.broadcasted_iota(jnp.int32, (hi - lo, 1), 0)
            prev = jnp.where(rows != SEQ, prev, 0.0)
            nxt = jnp.where(rows != SEQ - 1, nxt, 0.0)
        cols = slice(c_lo, c_lo + width)
        return (prev * cw_ref[0:1, cols] + u_scr[HALO + lo:HALO + hi, u_lo:u_lo + width] * cw_ref[1:2, cols]
                + nxt * cw_ref[2:3, cols] + cw_ref[3:4, cols])

    def conv_down(c, t):
        u_scr, w = slots[c % 2], FF_WIDTHS[c]
        t_lo, t_hi = down_tiles[t]
        cuts = sorted({t_lo, t_hi} | ({e for e in (band_lo, band_hi) if t_lo < e < t_hi} if split else set()))
        for lo, hi in zip(cuts[:-1], cuts[1:]):
            masked = bool(split) and band_lo <= lo < band_hi
            for c_lo in range(0, w, CONV_STRIP):
                a = conv(u_scr, cwa_ref, lo, hi, c_lo, c_lo, CONV_STRIP, masked)
                g = conv(u_scr, cwg_ref, lo, hi, FF_CHUNK + c_lo, c_lo, CONV_STRIP, masked)
                act_scr[lo:hi, c_lo:c_lo + CONV_STRIP] = (g * jax.nn.sigmoid(g) * a).astype(BF16)
        o_ref[0, t_lo:t_hi, :] += _dot(act_scr[t_lo:t_hi, :w], wd_ref[:w, :])

    for step in range(N_CHUNK + 1):
        @pl.when(j == step)
        def _(step=step):
            if step == 0:
                prologue()
            for t in range(FFN_TILES):
                if step < N_CHUNK:
                    up(step, t)
                if step > 0:
                    conv_down(step - 1, t)
            if step == N_CHUNK:
                rows = row0 + lax.broadcasted_iota(jnp.int32, (tm, 1), 0)
                o_ref[0] = x_ref[0] + per_row(rows, 5) * o_ref[0]


def _ffn(l, x1, mods, g_ffn, w_up_a, w_up_g, conv_a, conv_g, w_down, rows_per_sample, tm):
    n_b = x1.shape[0]
    n_rb = rows_per_sample // tm
    halo_per_block = tm // HALO
    n_halo = x1.shape[1] // HALO
    body = functools.partial(_ffn_kernel, tm=tm, rows_per_sample=rows_per_sample)

    def up_idx(j):
        return jnp.minimum(j, N_CHUNK - 1)

    def down_idx(j):
        return jnp.maximum(j - 1, 0)

    return pl.pallas_call(
        body,
        grid=(n_b, n_rb, N_CHUNK + 1),
        in_specs=[
            pl.BlockSpec((1, tm, D), lambda b, r, j: (b, r, 0)),
            pl.BlockSpec((1, HALO, D), lambda b, r, j: (b, jnp.maximum(r * halo_per_block - 1, 0), 0)),
            pl.BlockSpec((1, HALO, D), lambda b, r, j: (b, jnp.minimum((r + 1) * halo_per_block, n_halo - 1), 0)),
            pl.BlockSpec((None, 1, 1, 6 * D), lambda b, r, j: (l, b, 0, 0)),
            pl.BlockSpec((None, 1, 1, 6 * D), lambda b, r, j: (l, n_b, 0, 0)),
            _layer_spec(g_ffn, l),
            pl.BlockSpec((None, D, FF_CHUNK), lambda b, r, j: (l, 0, up_idx(j))),
            pl.BlockSpec((None, D, FF_CHUNK), lambda b, r, j: (l, 0, up_idx(j))),
            pl.BlockSpec((None, 4, FF_CHUNK), lambda b, r, j: (l, 0, down_idx(j))),
            pl.BlockSpec((None, 4, FF_CHUNK), lambda b, r, j: (l, 0, down_idx(j))),
            pl.BlockSpec((None, FF_CHUNK, D), lambda b, r, j: (l, down_idx(j), 0)),
        ],
        out_specs=pl.BlockSpec((1, tm, D), lambda b, r, j: (b, r, 0)),
        out_shape=jax.ShapeDtypeStruct((n_b, rows_per_sample, D), F32),
        scratch_shapes=[
            pltpu.VMEM((tm + 2 * HALO, D), BF16),
            pltpu.VMEM((tm + 2 * HALO, 2 * FF_CHUNK), F32),
            pltpu.VMEM((tm + 2 * HALO, 2 * FF_CHUNK), F32),
            pltpu.VMEM((tm, FF_CHUNK), BF16),
        ],
        compiler_params=_params(("parallel", "parallel", "arbitrary")),
        name="conv_ffn",
    )(x1, x1, x1, mods, mods, g_ffn, w_up_a, w_up_g, conv_a, conv_g, w_down)


def _rope_table(rot_dim):
    n_freq = rot_dim // 4
    inv = jnp.power(ROPE_THETA, -jnp.arange(n_freq, dtype=F32) / n_freq)
    t = jnp.arange(SEQ)
    row = (t // GRID_W).astype(F32)
    col = (t % GRID_W).astype(F32)
    ar, ac = row[:, None] * inv, col[:, None] * inv
    ang = jnp.concatenate([ar, ar, ac, ac], axis=-1)
    sign = jnp.concatenate([-jnp.ones(n_freq), jnp.ones(n_freq), -jnp.ones(n_freq), jnp.ones(n_freq)]).astype(F32)
    cos = jnp.concatenate([jnp.cos(ang), jnp.ones((CTX, rot_dim), F32)], axis=0)
    sin = jnp.concatenate([jnp.sin(ang) * sign, jnp.zeros((CTX, rot_dim), F32)], axis=0)
    return jnp.concatenate([cos.T, sin.T], axis=0)


def _na_bias_tiles(rpb):
    n_l, n_h = rpb.shape[:2]
    n_off = 2 * NA_ROWS - 1
    p = jnp.pad(rpb.astype(F32) * LOG2E, ((0, 0), (0, 0), (0, 0), (48, 49)))
    sk = jnp.broadcast_to(p[..., None, :], (n_l, n_h, n_off, GRID_W, 128)).reshape(n_l, n_h, n_off, GRID_W * 128)
    sk = sk[..., :GRID_W * 127].reshape(n_l, n_h, n_off, GRID_W, 127)[..., 63:127]
    c = np.arange(GRID_W)
    win_start = np.clip(c - NA_COLS // 2, 0, GRID_W - NA_COLS)
    v_col = (c[None, :] >= win_start[:, None]) & (c[None, :] < win_start[:, None] + NA_COLS)
    tz = jnp.where(v_col, sk, NEG_INF)
    neg1 = jnp.full((n_l, n_h, 1, GRID_W, GRID_W), NEG_INF, F32)
    tzx = jnp.concatenate([neg1, tz, neg1], axis=2)
    first, second = tzx[:, :, 0:16], tzx[:, :, 1:17]
    neg16 = jnp.full_like(first, NEG_INF)
    return jnp.concatenate([
        jnp.concatenate([first, second], axis=-1), jnp.concatenate([first, neg16], axis=-1),
        jnp.concatenate([neg16, second], axis=-1), jnp.concatenate([neg1, neg1], axis=-1)], axis=2)


def kernel(x, c, ctx, c_ctx, w_mod, b_mod, g_mix, w_in, w_out, mla_q_a_g, mla_w_uq, mla_kv_a_g, mla_w_ukv,
           mla_q_g, mla_k_g, diff_q_g, diff_k_g, diff_lq1, diff_lk1, diff_lq2, diff_lk2, diff_subln_g,
           na_q_g, na_k_g, na_rpb, gqa_q_g, gqa_k_g, g_ffn, w_up, conv_w, conv_b, w_down):
    n_b = x.shape[0]
    n_layer = w_mod.shape[0]
    assert x.shape[1:] == (SEQ, D) and ctx.shape[1:] == (CTX, D)

    xs = jnp.concatenate([x, ctx], axis=1)
    mod_rows = -(-(n_b + 1) // 8) * 8
    cc = jnp.concatenate([c, c_ctx[None], jnp.zeros((mod_rows - n_b - 1, D), F32)], axis=0)
    mods_all = _modulation(cc, w_mod, b_mod).reshape(n_layer, mod_rows, 1, 6 * D)

    cs32, cs64 = _rope_table(32), _rope_table(64)
    bias_all = _na_bias_tiles(na_rpb)

    s = np.cumsum([0, 256, 128, 32, 256, 256, 256, 256, 256, 256, 256, 128, 128])
    gq_perm = np.concatenate([np.arange(GQA_D) + (2 * g + r) * GQA_D for r in range(2) for g in range(2)])

    n_l = n_layer
    gq_cols = w_in[:, :, s[9]:s[10]][:, :, gq_perm]
    wqk = jnp.concatenate([w_in[:, :, s[0]:s[3]], w_in[:, :, s[3]:s[5]], w_in[:, :, s[6]:s[8]], gq_cols,
                           w_in[:, :, s[10]:s[11]]], axis=2).swapaxes(1, 2).astype(BF16)
    wv = jnp.concatenate([w_in[:, :, s[5]:s[6]], w_in[:, :, s[8]:s[9]], w_in[:, :, s[11]:s[12]]],
                         axis=2).astype(BF16)
    wuq = mla_w_uq.swapaxes(1, 2).astype(BF16)
    wukv = mla_w_ukv.reshape(n_l, MLA_KVR, MLA_H, MLA_NOPE + MLA_V)
    wukn = wukv[..., :MLA_NOPE].reshape(n_l, MLA_KVR, MLA_H * MLA_NOPE).swapaxes(1, 2).astype(BF16)
    wuv = wukv[..., MLA_NOPE:].reshape(n_l, MLA_KVR, MLA_H * MLA_V).swapaxes(1, 2).astype(BF16)
    gcol = jnp.concatenate([mla_q_a_g, mla_kv_a_g, mla_q_g, mla_k_g, diff_q_g, diff_k_g, na_q_g, na_k_g,
                            gqa_q_g, gqa_k_g], axis=1).astype(F32)[..., None]
    lam_rows = jnp.stack([diff_lq1, diff_lk1, diff_lq2, diff_lk2], axis=1).astype(F32)
    lam_rows = jnp.concatenate([lam_rows, jnp.zeros_like(lam_rows)], axis=1)
    gsub = jnp.tile(diff_subln_g.astype(F32), (1, 2)).reshape(n_l, 1, 128)
    bounds = {
        "mla": _logit_bound(MLA_D, mla_q_g, mla_k_g),
        "diff": _logit_bound(DIFF_D, diff_q_g, diff_k_g),
        "na": _logit_bound(NA_D, na_q_g, na_k_g, LOG2E * jnp.maximum(jnp.max(na_rpb, axis=(1, 2, 3)), 0.0)),
        "gqa": _logit_bound(GQA_D, gqa_q_g, gqa_k_g),
    }
    w_out_b, w_down_b = w_out.astype(BF16), w_down.astype(BF16)
    w_up_a, w_up_g = w_up[:, :, :D_FF].astype(BF16), w_up[:, :, D_FF:].astype(BF16)
    conv_wb = jnp.concatenate([conv_w, conv_b[:, None, :]], axis=1).astype(F32)
    conv_a, conv_g = conv_wb[:, :, :D_FF], conv_wb[:, :, D_FF:]
    gmix, gffn = g_mix.reshape(n_l, 1, D), g_ffn.reshape(n_l, 1, D)

    for l in range(n_layer):
        with_ctx = l < n_layer - 1
        lambda_init = 0.8 - 0.6 * math.exp(-0.3 * l)
        (q_mla, k_mla, v_mla, q_diff, k_diff, v_diff, q_na, k_na, v_na, q_gqa, k_gqa, v_gqa) = _inproj(
            l, xs, mods_all, gmix, wqk, wv, wuq, wukn, wuv, gcol, cs32, cs64)

        n_rb = NRB if with_ctx else NLAT
        mix_a = _attention(_mla_kernel, "attn_mla", q_mla, k_mla, v_mla, bounds["mla"][l], [], n_rb)
        mix_b = _attention(
            functools.partial(_diff_kernel, lambda_init=lambda_init), "attn_diff",
            q_diff, k_diff, v_diff, bounds["diff"][l],
            [(lam_rows, _layer_spec(lam_rows, l)), (gsub, _layer_spec(gsub, l))], n_rb)
        mix_c = _attention(_na_kernel, "attn_na", q_na, k_na, v_na, bounds["na"][l],
                           [(bias_all, _layer_spec(bias_all, l))], n_rb)
        mix_d = _attention(_gqa_kernel, "attn_gqa", q_gqa, k_gqa, v_gqa, bounds["gqa"][l], [], n_rb)

        rows_per_sample = T if with_ctx else SEQ
        x1 = _outproj(l, xs, mods_all, (mix_a, mix_b, mix_c, mix_d), w_out_b, rows_per_sample,
                      OUT_TM if with_ctx else OUT_TM_LAST)
        xs = _ffn(l, x1, mods_all, gffn, w_up_a, w_up_g, conv_a, conv_g, w_down_b, rows_per_sample,
                  rows_per_sample // 2)
    return xs
```

```python
import functools
import math

import numpy as np
import jax
import jax.numpy as jnp
from jax import lax
from jax.experimental import pallas as pl
from jax.experimental.pallas import tpu as pltpu

F32 = jnp.float32
BF16 = jnp.bfloat16

D = 1024
SEQ = 2048
GRID_W = 64
CTX = 256
T = SEQ + CTX
RB = 256
NRB = T // RB
NLAT = SEQ // RB
IN_SUB = 3
EPS = 1e-6
NEG_INF = -1e30
ROPE_THETA = 10000.0
LOG2E = 1.4426950408889634

MLA_H, MLA_NOPE, MLA_ROPE, MLA_V, MLA_QR, MLA_KVR = 4, 64, 32, 64, 256, 128
MLA_D = MLA_NOPE + MLA_ROPE
DIFF_H, DIFF_D = 4, 32
NA_H, NA_D, NA_ROWS, NA_COLS = 4, 64, 8, 16
GQA_H, GQA_KV, GQA_D = 4, 2, 64
D_FF = 2816
FF_CHUNK = 512
N_CHUNK = -(-D_FF // FF_CHUNK)
FF_WIDTHS = tuple(min(FF_CHUNK, D_FF - c * FF_CHUNK) for c in range(N_CHUNK))
OUT_TM, OUT_TM_LAST = 768, 1024
HALO = 16
FFN_TILES = 4
CONV_STRIP = 256

R_CQ, R_CKV, R_KR, R_DQ, R_DK, R_NQ, R_NK, R_GQ, R_GK = 0, 256, 384, 416, 672, 928, 1184, 1440, 1696
QK_ROWS = 1824
V_COLS = 640

G_QA, G_KVA, G_MQ, G_MK, G_DQ, G_DK, G_NQ, G_NK, G_GQ, G_GK = 0, 256, 384, 480, 576, 608, 640, 704, 768, 832
G_ROWS = 896

VMEM_LIMIT = 56 * 1024 * 1024


def _params(sem):
    return pltpu.CompilerParams(dimension_semantics=sem, vmem_limit_bytes=VMEM_LIMIT)


def _dot(a, b):
    return jnp.dot(a, b, preferred_element_type=F32)


def _dot_nt(a, b):
    return lax.dot_general(a, b, (((1,), (1,)), ((), ())), preferred_element_type=F32)


def _mod_kernel(c_ref, w_ref, b_ref, o_ref):
    c = c_ref[...]
    a = (c * jax.nn.sigmoid(c)).astype(BF16)
    o_ref[0] = _dot(a, w_ref[0].astype(BF16)) + b_ref[0]


def _modulation(cc, w_mod, b_mod):
    n_layer = w_mod.shape[0]
    rows = cc.shape[0]
    return pl.pallas_call(
        _mod_kernel,
        grid=(n_layer, 6),
        in_specs=[
            pl.BlockSpec((rows, D), lambda l, j: (0, 0)),
            pl.BlockSpec((1, D, D), lambda l, j: (l, 0, j)),
            pl.BlockSpec((1, 1, D), lambda l, j: (l, 0, j)),
        ],
        out_specs=pl.BlockSpec((1, rows, D), lambda l, j: (l, 0, j)),
        out_shape=jax.ShapeDtypeStruct((n_layer, rows, 6 * D), F32),
        compiler_params=_params(("parallel", "parallel")),
        name="modulation",
    )(cc, w_mod, b_mod.reshape(n_layer, 1, 6 * D))


def _rms_rows(x, g, n):
    ss = jnp.sum(x * x, axis=0, keepdims=True) * (1.0 / n)
    return x * lax.rsqrt(ss + EPS) * g


def _rope_rows(x, cos, sin_signed, w):
    rot = jnp.concatenate([x[w:2 * w], x[0:w], x[3 * w:4 * w], x[2 * w:3 * w]], axis=0)
    return x * cos + rot * sin_signed


def _inproj_kernel(x_ref, modb_ref, modc_ref, gmix_ref, wqk_ref, wv_ref, wuq_ref, wukn_ref, wuv_ref, gcol_ref,
                   cs32_ref, cs64_ref,
                   qmla_ref, kmla_ref, vmla_ref, qdiff_ref, kdiff_ref, vdiff_ref,
                   qna_ref, kna_ref, vna_ref, qgqa_ref, kgqa_ref, vgqa_ref):
    step = pl.program_id(0)

    def modulated(sb):
        r0 = sb * RB
        x = x_ref[0, r0:r0 + RB, :]
        mod = modb_ref[0]
        if sb == IN_SUB - 1:
            mod = jnp.where(step == NRB // IN_SUB - 1, modc_ref[0], mod)
        shift, scale = mod[:, 0:D], mod[:, D:2 * D]
        ms = jnp.mean(x * x, axis=-1, keepdims=True)
        h = x * lax.rsqrt(ms + EPS) * gmix_ref[...]
        return (h * (1.0 + scale) + shift).astype(BF16)

    def gain(off, n):
        return gcol_ref[off:off + n, :]

    def block(sb, hb, after_first_projection):
        rows = slice(sb * RB, (sb + 1) * RB)
        _inproj_block(rows, hb, after_first_projection, gain, wqk_ref, wv_ref, wuq_ref, wukn_ref, wuv_ref,
                      cs32_ref, cs64_ref, qmla_ref, kmla_ref, vmla_ref, qdiff_ref, kdiff_ref, vdiff_ref,
                      qna_ref, kna_ref, vna_ref, qgqa_ref, kgqa_ref, vgqa_ref)

    hbs = [modulated(0)]
    for sb in range(IN_SUB):
        nxt = (lambda sb=sb: hbs.append(modulated(sb + 1))) if sb + 1 < IN_SUB else (lambda: None)
        block(sb, hbs[sb], nxt)


def _inproj_block(rows, hb, after_first_projection, gain, wqk_ref, wv_ref, wuq_ref, wukn_ref, wuv_ref,
                  cs32_ref, cs64_ref, qmla_ref, kmla_ref, vmla_ref, qdiff_ref, kdiff_ref, vdiff_ref,
                  qna_ref, kna_ref, vna_ref, qgqa_ref, kgqa_ref, vgqa_ref):
    def project(lo, hi):
        return _dot_nt(wqk_ref[lo:hi, :], hb)

    cos32, sin32 = cs32_ref[0:32, rows], cs32_ref[32:64, rows]
    cos64, sin64 = cs64_ref[0:64, rows], cs64_ref[64:128, rows]

    def values():
        pv = _dot(hb, wv_ref[...])
        vdiff_ref[0, rows, :] = pv[:, 0:256].astype(BF16)
        vna_ref[0, rows, :] = pv[:, 256:512].astype(BF16)
        vgqa_ref[0, rows, :] = pv[:, 512:640].astype(BF16)

    def mla(pt):
        cq = _rms_rows(pt[0:MLA_QR], gain(G_QA, MLA_QR), MLA_QR).astype(BF16)
        qt = _dot(wuq_ref[...], cq)
        ckv = _rms_rows(pt[R_CKV:R_CKV + MLA_KVR], gain(G_KVA, MLA_KVR), MLA_KVR).astype(BF16)
        knt = _dot(wukn_ref[...], ckv)
        vt = _dot(wuv_ref[...], ckv)
        vmla_ref[0, rows, :] = vt.T.astype(BF16)
        kr = pt[R_KR:R_KR + MLA_ROPE]
        kr_ss = jnp.sum(kr * kr, axis=0, keepdims=True)
        g_mq, g_mk = gain(G_MQ, MLA_D), gain(G_MK, MLA_D)
        zpad = jnp.zeros((128 - MLA_D, RB), F32)
        q_parts = []
        for hd in range(MLA_H):
            qh = _rms_rows(qt[hd * MLA_D:(hd + 1) * MLA_D], g_mq, MLA_D)
            q_rope = _rope_rows(qh[MLA_NOPE:], cos32, sin32, MLA_ROPE // 4)
            q_parts += [qh[:MLA_NOPE], q_rope, zpad]
            kn = knt[hd * MLA_NOPE:(hd + 1) * MLA_NOPE]
            ss = (jnp.sum(kn * kn, axis=0, keepdims=True) + kr_ss) * (1.0 / MLA_D)
            r = lax.rsqrt(ss + EPS)
            k_rope = _rope_rows(kr * r * g_mk[MLA_NOPE:], cos32, sin32, MLA_ROPE // 4)
            kmla_ref[0, hd * 128:(hd + 1) * 128, rows] = jnp.concatenate(
                [kn * r * g_mk[:MLA_NOPE], k_rope, zpad], axis=0).astype(BF16)
        q_all = jnp.concatenate(q_parts, axis=0) * (MLA_D ** -0.5 * LOG2E)
        qmla_ref[0, rows, :] = q_all.T.astype(BF16)

    def diff(pt):
        g_dq, g_dk = gain(G_DQ, DIFF_D), gain(G_DK, DIFF_D)
        q_parts, k_parts = [], []
        for gi in range(2 * DIFF_H):
            qg = _rms_rows(pt[gi * DIFF_D:(gi + 1) * DIFF_D], g_dq, DIFF_D)
            q_parts.append(_rope_rows(qg, cos32, sin32, DIFF_D // 4))
            kg = _rms_rows(pt[256 + gi * DIFF_D:256 + (gi + 1) * DIFF_D], g_dk, DIFF_D)
            k_parts.append(_rope_rows(kg, cos32, sin32, DIFF_D // 4))
        qdiff_ref[0, rows, :] = (jnp.concatenate(q_parts, axis=0) * (DIFF_D ** -0.5 * LOG2E)).T.astype(BF16)
        kdiff_ref[0, :, rows] = jnp.concatenate(k_parts, axis=0).astype(BF16)

    def na(pt):
        g_nq, g_nk = gain(G_NQ, NA_D), gain(G_NK, NA_D)
        q_parts, k_parts = [], []
        for hd in range(NA_H):
            q_parts.append(_rms_rows(pt[hd * NA_D:(hd + 1) * NA_D], g_nq, NA_D))
            k_parts.append(_rms_rows(pt[256 + hd * NA_D:256 + (hd + 1) * NA_D], g_nk, NA_D))
        qna_ref[0, rows, :] = (jnp.concatenate(q_parts, axis=0) * (NA_D ** -0.5 * LOG2E)).T.astype(BF16)
        kna_ref[0, :, rows] = jnp.concatenate(k_parts, axis=0).astype(BF16)

    def gqa(pt):
        g_gq, g_gk = gain(G_GQ, GQA_D), gain(G_GK, GQA_D)
        q_parts, k_parts = [], []
        for hd in range(GQA_H):
            qg = _rms_rows(pt[hd * GQA_D:(hd + 1) * GQA_D], g_gq, GQA_D)
            q_parts.append(_rope_rows(qg, cos64, sin64, GQA_D // 4))
        for hd in range(GQA_KV):
            kg = _rms_rows(pt[256 + hd * GQA_D:256 + (hd + 1) * GQA_D], g_gk, GQA_D)
            k_parts.append(_rope_rows(kg, cos64, sin64, GQA_D // 4))
        qgqa_ref[0, rows, :] = (jnp.concatenate(q_parts, axis=0) * (GQA_D ** -0.5 * LOG2E)).T.astype(BF16)
        kgqa_ref[0, :, rows] = jnp.concatenate(k_parts, axis=0).astype(BF16)

    pt_mla = project(R_CQ, R_DQ)
    after_first_projection()
    pt_diff = project(R_DQ, R_NQ)
    mla(pt_mla)
    pt_na = project(R_NQ, R_GQ)
    diff(pt_diff)
    pt_gqa = project(R_GQ, QK_ROWS)
    na(pt_na)
    values()
    gqa(pt_gqa)


def _inproj(l, xs, mods, gmix, wqk, wv, wuq, wukn, wuv, gcol, cs32, cs64):
    n_b = xs.shape[0]

    def full(a):
        return _layer_spec(a, l)

    rows_per_step = IN_SUB * RB

    def tok(width):
        return pl.BlockSpec((1, rows_per_step, width), lambda r, b: (b, r, 0))

    def chan(rows):
        return pl.BlockSpec((1, rows, rows_per_step), lambda r, b: (b, 0, r))

    def tshape(width):
        return jax.ShapeDtypeStruct((n_b, T, width), BF16)

    def cshape(rows):
        return jax.ShapeDtypeStruct((n_b, rows, T), BF16)

    return pl.pallas_call(
        _inproj_kernel,
        grid=(NRB // IN_SUB, n_b),
        in_specs=[
            pl.BlockSpec((1, rows_per_step, D), lambda r, b: (b, r, 0)),
            pl.BlockSpec((None, 1, 1, 6 * D), lambda r, b: (l, b, 0, 0)),
            pl.BlockSpec((None, 1, 1, 6 * D), lambda r, b: (l, n_b, 0, 0)),
            full(gmix), full(wqk), full(wv), full(wuq), full(wukn), full(wuv), full(gcol),
            pl.BlockSpec((64, rows_per_step), lambda r, b: (0, r)),
            pl.BlockSpec((128, rows_per_step), lambda r, b: (0, r)),
        ],
        out_specs=[tok(512), chan(512), tok(256), tok(256), chan(256), tok(256),
                   tok(256), chan(256), tok(256), tok(256), chan(128), tok(128)],
        out_shape=[tshape(512), cshape(512), tshape(256), tshape(256), cshape(256), tshape(256),
                   tshape(256), cshape(256), tshape(256), tshape(256), cshape(128), tshape(128)],
        compiler_params=_params(("parallel", "parallel")),
        name="inproj_prep",
    )(xs, mods, mods, gmix, wqk, wv, wuq, wukn, wuv, gcol, cs32, cs64)


def _lane_id(shape):
    return lax.broadcasted_iota(jnp.int32, shape, 1)


KEY_CHUNK = 768
LAT_CHUNKS = tuple((lo, KEY_CHUNK) for lo in range(0, T, KEY_CHUNK))
CTX_CHUNKS = ((SEQ, CTX),)


def _k_slab(k_ref, idx):
    return lambda lo, n: k_ref[0, idx * 128:(idx + 1) * 128, pl.ds(lo, n)]


def _v_slab(v_ref, idx):
    return lambda lo, n: v_ref[0, pl.ds(lo, n), idx * 128:(idx + 1) * 128]


def _attend(units, chunks, shift=None):
    def scores(i, c):
        q, keys, _, bias = units[i]
        lo, n = chunks[c]
        s = _dot(q, keys(lo, n))
        b = None if bias is None else bias(c)
        return s if b is None else s + b

    if shift is not None:
        items = [(i, c) for i in range(len(units)) for c in range(len(chunks))]
        acc, l = [None] * len(units), [None] * len(units)
        nxt = scores(*items[0])
        for idx, (i, c) in enumerate(items):
            cur = nxt
            if idx + 1 < len(items):
                nxt = scores(*items[idx + 1])
            e = jnp.exp2(cur - shift)
            lc = jnp.sum(e, axis=-1, keepdims=True)
            oc = _dot(e.astype(BF16), units[i][2](*chunks[c]))
            acc[i] = oc if acc[i] is None else acc[i] + oc
            l[i] = lc if l[i] is None else l[i] + lc
        return list(zip(acc, l))

    cur = [scores(0, c) for c in range(len(chunks))]
    out = []
    for i in range(len(units)):
        m = functools.reduce(jnp.maximum, [jnp.max(s, axis=-1, keepdims=True) for s in cur])
        nxt, acc, l = [], None, None
        for c, (lo, n) in enumerate(chunks):
            if i + 1 < len(units):
                nxt.append(scores(i + 1, c))
            e = jnp.exp2(cur[c] - m)
            lc = jnp.sum(e, axis=-1, keepdims=True)
            oc = _dot(e.astype(BF16), units[i][2](lo, n))
            acc = oc if acc is None else acc + oc
            l = lc if l is None else l + lc
        out.append((acc, l))
        cur = nxt
    return out


def _with_shift(bnd_ref, run):
    @pl.when(bnd_ref[1] > 0.5)
    def _():
        run(bnd_ref[0])

    @pl.when(bnd_ref[1] <= 0.5)
    def _():
        run(None)


def _query_blocks(n_rb, bnd_ref, run):
    def all_blocks(shift):
        def body(rb, carry):
            run(rb, pl.ds(pl.multiple_of(rb * RB, RB), RB), True, shift)
            return carry

        lax.fori_loop(0, NLAT, body, 0)
        if n_rb > NLAT:
            run(NLAT, pl.ds(SEQ, CTX), False, shift)

    _with_shift(bnd_ref, all_blocks)


def _mla_kernel(q_ref, k_ref, v_ref, bnd_ref, o_ref, *, n_rb):
    lane = _lane_id((RB, 128))

    def run(rb, rows, latent, shift):
        units = [(q_ref[0, rows,hd * 128:(hd + 1) * 128], _k_slab(k_ref, hd), _v_slab(v_ref, hd // 2), None)
                 for hd in range(MLA_H)]
        o = [acc * (1.0 / l) for acc, l in _attend(units, LAT_CHUNKS if latent else CTX_CHUNKS, shift)]
        outs = [jnp.where(lane < 64, o[0], o[1]), jnp.where(lane < 64, o[2], o[3])]
        o_ref[0, rows, :] = jnp.concatenate(outs, axis=1).astype(o_ref.dtype)

    _query_blocks(n_rb, bnd_ref, run)


def _gqa_kernel(q_ref, k_ref, v_ref, bnd_ref, o_ref, *, n_rb):
    lane = _lane_id((RB, 128))

    def run(rb, rows, latent, shift):
        order = [(rep, grp) for rep in range(2) for grp in range(2)]
        units = []
        for rep, grp in order:
            qs = q_ref[0, rows,rep * 128:(rep + 1) * 128]
            qm = jnp.where((lane >= 64) == (grp == 1), qs, jnp.zeros_like(qs))
            units.append((qm, _k_slab(k_ref, 0), _v_slab(v_ref, 0), None))
        res = {}
        for (rep, grp), (acc, l) in zip(order, _attend(units, LAT_CHUNKS if latent else CTX_CHUNKS, shift)):
            o = acc * (1.0 / l)
            res[(grp, rep)] = o if grp == rep else pltpu.roll(o, 64, axis=1)
        outs = [jnp.where(lane < 64, res[(grp, 0)], res[(grp, 1)]) for grp in range(2)]
        o_ref[0, rows, :] = jnp.concatenate(outs, axis=1).astype(o_ref.dtype)

    _query_blocks(n_rb, bnd_ref, run)


def _diff_kernel(q_ref, k_ref, v_ref, bnd_ref, lam_ref, g_ref, o_ref, *, n_rb, lambda_init):
    lane = _lane_id((RB, 128))
    lq1, lk1, lq2, lk2 = lam_ref[0:1, :], lam_ref[1:2, :], lam_ref[2:3, :], lam_ref[3:4, :]
    lam = (jnp.exp(jnp.sum(lq1 * lk1, axis=-1, keepdims=True))
           - jnp.exp(jnp.sum(lq2 * lk2, axis=-1, keepdims=True)) + lambda_init)
    gsub = g_ref[...]
    grp = lane // DIFF_D

    def run(rb, rows, latent, shift):
        units = []
        for hd in range(DIFF_H):
            pair, sub = divmod(hd, 2)
            qs = q_ref[0, rows,pair * 128:(pair + 1) * 128]
            for which in range(2):
                qm = jnp.where(grp == 2 * sub + which, qs, jnp.zeros_like(qs))
                units.append((qm, _k_slab(k_ref, pair), _v_slab(v_ref, pair), None))
        res = _attend(units, LAT_CHUNKS if latent else CTX_CHUNKS, shift)
        outs = []
        for pair in range(2):
            halves = []
            for sub in range(2):
                (a1, l1), (a2, l2) = res[2 * (2 * pair + sub)], res[2 * (2 * pair + sub) + 1]
                halves.append(a1 * (1.0 / l1) - a2 * (lam / l2))
            o = jnp.where(lane < 64, halves[0], halves[1])
            o2 = o * o
            ss0 = jnp.sum(jnp.where(lane < 64, o2, 0.0), axis=-1, keepdims=True)
            ss1 = jnp.sum(jnp.where(lane < 64, 0.0, o2), axis=-1, keepdims=True)
            ss = jnp.where(lane < 64, ss0, ss1) * (1.0 / (2 * DIFF_D))
            outs.append(o * lax.rsqrt(ss + EPS) * gsub * (1.0 - lambda_init))
        o_ref[0, rows, :] = jnp.concatenate(outs, axis=1).astype(o_ref.dtype)

    _query_blocks(n_rb, bnd_ref, run)


NA_WIN = 768
NA_CHUNK = 768
NA_TILES = 49
GRID_ROWS = SEQ // GRID_W


def _na_kernel(q_ref, k_ref, v_ref, bnd_ref, bias_ref, o_ref, *, n_rb):
    lane = _lane_id((RB, 128))

    def run(rb, rows, latent, shift):
        na_chunks, tile_idx = CTX_CHUNKS, None
        if latent:
            win_row = jnp.clip(4 * rb - 4, 0, GRID_ROWS - NA_WIN // GRID_W)
            off = pl.multiple_of(win_row * GRID_W, RB)
            local = [(pl.multiple_of(off + c * NA_CHUNK, RB), NA_CHUNK) for c in range(NA_WIN // NA_CHUNK)]
            na_chunks = local + list(CTX_CHUNKS)
            tile_idx = []
            for c in range(NA_WIN // NA_CHUNK):
                per_row = []
                for ri in range(RB // GRID_W):
                    r = 4 * rb + ri
                    row_start = jnp.clip(r - NA_ROWS // 2, 0, GRID_ROWS - NA_ROWS)
                    idxs = []
                    for p in range(NA_CHUNK // 128):
                        kr = win_row + c * (NA_CHUNK // GRID_W) + 2 * p
                        ok1 = (kr >= row_start) & (kr < row_start + NA_ROWS)
                        ok2 = (kr + 1 >= row_start) & (kr + 1 < row_start + NA_ROWS)
                        e = jnp.clip(kr - r + NA_ROWS, 0, 15)
                        idxs.append(jnp.where(ok1 & ok2, e, jnp.where(ok1, 16 + e, jnp.where(ok2, 32 + e, 48))))
                    per_row.append(idxs)
                tile_idx.append(per_row)
        units = []
        for hd in range(NA_H):
            pair, sub = divmod(hd, 2)
            qs = q_ref[0, rows,pair * 128:(pair + 1) * 128]
            qm = jnp.where((lane >= 64) == (sub == 1), qs, jnp.zeros_like(qs))

            def bias(c, hd=hd):
                if tile_idx is None or c >= NA_WIN // NA_CHUNK:
                    return None
                return jnp.concatenate(
                    [jnp.concatenate([bias_ref[hd, idx] for idx in row], axis=1) for row in tile_idx[c]], axis=0)

            units.append((qm, _k_slab(k_ref, pair), _v_slab(v_ref, pair), bias))
        o = [acc * (1.0 / l) for acc, l in _attend(units, na_chunks, shift)]
        outs = [jnp.where(lane < 64, o[0], o[1]), jnp.where(lane < 64, o[2], o[3])]
        o_ref[0, rows, :] = jnp.concatenate(outs, axis=1).astype(o_ref.dtype)

    _query_blocks(n_rb, bnd_ref, run)


SHIFT_MAX = 40.0


def _logit_bound(d, q_gain, k_gain, bias_max=0.0):
    bound = (1.01 * math.sqrt(d) * LOG2E * jnp.max(jnp.abs(q_gain), axis=-1) * jnp.max(jnp.abs(k_gain), axis=-1)
             + bias_max)
    return jnp.stack([bound, (bound <= SHIFT_MAX).astype(F32)], axis=-1).astype(F32)


def _layer_spec(a, l):
    return pl.BlockSpec((None,) + a.shape[1:], lambda *_, _n=a.ndim - 1: (l,) + (0,) * _n)


def _attention(body, name, q, k, v, bound, extra, n_rb, out_width=256):
    n_b = q.shape[0]
    extra = [(bound, pl.BlockSpec(memory_space=pltpu.SMEM))] + list(extra)
    in_specs = [
        pl.BlockSpec((1, T, q.shape[2]), lambda b: (b, 0, 0)),
        pl.BlockSpec((1, k.shape[1], T), lambda b: (b, 0, 0)),
        pl.BlockSpec((1, T, v.shape[2]), lambda b: (b, 0, 0)),
    ] + [spec for _, spec in extra]
    return pl.pallas_call(
        functools.partial(body, n_rb=n_rb),
        grid=(n_b,),
        in_specs=in_specs,
        out_specs=pl.BlockSpec((1, n_rb * RB, out_width), lambda b: (b, 0, 0)),
        out_shape=jax.ShapeDtypeStruct((n_b, n_rb * RB, out_width), BF16),
        compiler_params=_params(("parallel",)),
        name=name,
    )(q, k, v, *[a for a, _ in extra])


def _outproj_kernel(x_ref, modb_ref, modc_ref, ma_ref, mb_ref, mc_ref, md_ref, w_ref, o_ref, *, tm):
    rows = pl.program_id(1) * tm + lax.broadcasted_iota(jnp.int32, (tm, 1), 0)
    g1 = jnp.where(rows >= SEQ, modc_ref[0][:, 2 * D:3 * D], modb_ref[0][:, 2 * D:3 * D])
    acc = _dot(ma_ref[0], w_ref[0:256, :])
    acc += _dot(mb_ref[0], w_ref[256:512, :])
    acc += _dot(mc_ref[0], w_ref[512:768, :])
    acc += _dot(md_ref[0], w_ref[768:1024, :])
    o_ref[0] = x_ref[0] + g1 * acc


def _outproj(l, xs, mods, mixes, w_out, rows_per_sample, tm):
    n_b = xs.shape[0]
    mix_spec = pl.BlockSpec((1, tm, 256), lambda b, r: (b, r, 0))
    return pl.pallas_call(
        functools.partial(_outproj_kernel, tm=tm),
        grid=(n_b, rows_per_sample // tm),
        in_specs=[
            pl.BlockSpec((1, tm, D), lambda b, r: (b, r, 0)),
            pl.BlockSpec((None, 1, 1, 6 * D), lambda b, r: (l, b, 0, 0)),
            pl.BlockSpec((None, 1, 1, 6 * D), lambda b, r: (l, n_b, 0, 0)),
            mix_spec, mix_spec, mix_spec, mix_spec,
            _layer_spec(w_out, l),
        ],
        out_specs=pl.BlockSpec((1, tm, D), lambda b, r: (b, r, 0)),
        out_shape=jax.ShapeDtypeStruct((n_b, rows_per_sample, D), F32),
        compiler_params=_params(("parallel", "parallel")),
        name="outproj",
    )(xs, mods, mods, *mixes, w_out)


def _ffn_kernel(x_ref, hb_ref, ha_ref, modb_ref, modc_ref, g_ref, wa_ref, wg_ref, cwa_ref, cwg_ref, wd_ref, o_ref,
                h_scr, u0_scr, u1_scr, act_scr, *, tm, rows_per_sample):
    rb = pl.program_id(1)
    j = pl.program_id(2)
    row0 = rb * tm
    split = SEQ % tm if rows_per_sample > SEQ else 0
    band_lo, band_hi = split - HALO, split + HALO

    def per_row(rows, col):
        lat = modb_ref[0][:, col * D:(col + 1) * D]
        if rows_per_sample == SEQ:
            return lat
        return jnp.where(rows >= SEQ, modc_ref[0][:, col * D:(col + 1) * D], lat)

    def modulated(x, rows):
        ms = jnp.mean(x * x, axis=-1, keepdims=True)
        y = x * lax.rsqrt(ms + EPS) * g_ref[...]
        return (y * (1.0 + per_row(rows, 4)) + per_row(rows, 3)).astype(BF16)

    def prologue():
        halo_rows = lax.broadcasted_iota(jnp.int32, (HALO, 1), 0)
        has_before = (row0 != 0) & (row0 != SEQ)
        has_after = (row0 + tm != SEQ) & (row0 + tm != rows_per_sample)
        zero = jnp.zeros((HALO, D), BF16)
        h_scr[0:HALO, :] = jnp.where(has_before, modulated(hb_ref[0], row0 - HALO + halo_rows), zero)
        h_scr[HALO + tm:, :] = jnp.where(has_after, modulated(ha_ref[0], row0 + tm + halo_rows), zero)
        main_rows = row0 + lax.broadcasted_iota(jnp.int32, (tm, 1), 0)
        h_scr[HALO:HALO + tm, :] = modulated(x_ref[0], main_rows)
        o_ref[0] = jnp.zeros((tm, D), F32)

    slots = (u0_scr, u1_scr)

    def tiles(total):
        units = total // HALO
        sizes = [(units // FFN_TILES + (1 if t < units % FFN_TILES else 0)) * HALO for t in range(FFN_TILES)]
        edges = np.cumsum([0] + sizes)
        return [(int(edges[t]), int(edges[t + 1])) for t in range(FFN_TILES)]

    up_tiles, down_tiles = tiles(tm + 2 * HALO), tiles(tm)

    def up(c, t):
        lo, hi = up_tiles[t]
        w = FF_WIDTHS[c]
        slots[c % 2][lo:hi, :w] = _dot(h_scr[lo:hi, :], wa_ref[:, :w])
        slots[c % 2][lo:hi, FF_CHUNK:FF_CHUNK + w] = _dot(h_scr[lo:hi, :], wg_ref[:, FF_CHUNK - w:])

    def conv(u_scr, cw_ref, lo, hi, u_lo, c_lo, width, masked):
        prev = u_scr[HALO - 1 + lo:HALO - 1 + hi, u_lo:u_lo + width]
        nxt = u_scr[HALO + 1 + lo:HALO + 1 + hi, u_lo:u_lo + width]
        if masked:
            rows = row0 + lo + lax.broadcasted_iota(jnp.int32, (hi - lo, 1), 0)
            prev = jnp.where(rows != SEQ, prev, 0.0)
            nxt = jnp.where(rows != SEQ - 1, nxt, 0.0)
        cols = slice(c_lo, c_lo + width)
        return (prev * cw_ref[0:1, cols] + u_scr[HALO + lo:HALO + hi, u_lo:u_lo + width] * cw_ref[1:2, cols]
                + nxt * cw_ref[2:3, cols] + cw_ref[3:4, cols])

    def conv_down(c, t):
        u_scr, w = slots[c % 2], FF_WIDTHS[c]
        t_lo, t_hi = down_tiles[t]
        cuts = sorted({t_lo, t_hi} | ({e for e in (band_lo, band_hi) if t_lo < e < t_hi} if split else set()))
        for lo, hi in zip(cuts[:-1], cuts[1:]):
            masked = bool(split) and band_lo <= lo < band_hi
            for c_lo in range(0, w, CONV_STRIP):
                a = conv(u_scr, cwa_ref, lo, hi, c_lo, c_lo, CONV_STRIP, masked)
                g = conv(u_scr, cwg_ref, lo, hi, FF_CHUNK + c_lo, FF_CHUNK - w + c_lo, CONV_STRIP, masked)
                act_scr[lo:hi, c_lo:c_lo + CONV_STRIP] = (g * jax.nn.sigmoid(g) * a).astype(BF16)
        o_ref[0, t_lo:t_hi, :] += _dot(act_scr[t_lo:t_hi, :w], wd_ref[FF_CHUNK - w:, :])

    for step in range(N_CHUNK + 1):
        @pl.when(j == step)
        def _(step=step):
            if step == 0:
                prologue()
            for t in range(FFN_TILES):
                if step < N_CHUNK:
                    up(step, t)
                if step > 0:
                    conv_down(step - 1, t)
            if step == N_CHUNK:
                rows = row0 + lax.broadcasted_iota(jnp.int32, (tm, 1), 0)
                o_ref[0] = x_ref[0] + per_row(rows, 5) * o_ref[0]


def _ffn(l, x1, mods, g_ffn, w_up, conv_wb, w_down, rows_per_sample, tm):
    n_b = x1.shape[0]
    n_rb = rows_per_sample // tm
    halo_per_block = tm // HALO
    n_halo = x1.shape[1] // HALO
    body = functools.partial(_ffn_kernel, tm=tm, rows_per_sample=rows_per_sample)

    def up_idx(j):
        return jnp.minimum(j, N_CHUNK - 1)

    def down_idx(j):
        return jnp.maximum(j - 1, 0)

    def tail_clamped(c):
        return pl.multiple_of(jnp.minimum(c * FF_CHUNK, D_FF - FF_CHUNK), 256)

    return pl.pallas_call(
        body,
        grid=(n_b, n_rb, N_CHUNK + 1),
        in_specs=[
            pl.BlockSpec((1, tm, D), lambda b, r, j: (b, r, 0)),
            pl.BlockSpec((1, HALO, D), lambda b, r, j: (b, jnp.maximum(r * halo_per_block - 1, 0), 0)),
            pl.BlockSpec((1, HALO, D), lambda b, r, j: (b, jnp.minimum((r + 1) * halo_per_block, n_halo - 1), 0)),
            pl.BlockSpec((None, 1, 1, 6 * D), lambda b, r, j: (l, b, 0, 0)),
            pl.BlockSpec((None, 1, 1, 6 * D), lambda b, r, j: (l, n_b, 0, 0)),
            _layer_spec(g_ffn, l),
            pl.BlockSpec((None, D, FF_CHUNK), lambda b, r, j: (l, 0, up_idx(j))),
            pl.BlockSpec((None, pl.Element(D), pl.Element(FF_CHUNK)),
                         lambda b, r, j: (l, 0, pl.multiple_of(D_FF + tail_clamped(up_idx(j)), 256))),
            pl.BlockSpec((None, 4, FF_CHUNK), lambda b, r, j: (l, 0, down_idx(j))),
            pl.BlockSpec((None, pl.Element(4), pl.Element(FF_CHUNK)),
                         lambda b, r, j: (l, 0, pl.multiple_of(D_FF + tail_clamped(down_idx(j)), 256))),
            pl.BlockSpec((None, pl.Element(FF_CHUNK), pl.Element(D)),
                         lambda b, r, j: (l, tail_clamped(down_idx(j)), 0)),
        ],
        out_specs=pl.BlockSpec((1, tm, D), lambda b, r, j: (b, r, 0)),
        out_shape=jax.ShapeDtypeStruct((n_b, rows_per_sample, D), F32),
        scratch_shapes=[
            pltpu.VMEM((tm + 2 * HALO, D), BF16),
            pltpu.VMEM((tm + 2 * HALO, 2 * FF_CHUNK), F32),
            pltpu.VMEM((tm + 2 * HALO, 2 * FF_CHUNK), F32),
            pltpu.VMEM((tm, FF_CHUNK), BF16),
        ],
        compiler_params=_params(("parallel", "parallel", "arbitrary")),
        name="conv_ffn",
    )(x1, x1, x1, mods, mods, g_ffn, w_up, w_up, conv_wb, conv_wb, w_down)


def _rope_table(rot_dim):
    n_freq = rot_dim // 4
    inv = jnp.power(ROPE_THETA, -jnp.arange(n_freq, dtype=F32) / n_freq)
    t = jnp.arange(SEQ)
    row = (t // GRID_W).astype(F32)
    col = (t % GRID_W).astype(F32)
    ar, ac = row[:, None] * inv, col[:, None] * inv
    ang = jnp.concatenate([ar, ar, ac, ac], axis=-1)
    sign = jnp.concatenate([-jnp.ones(n_freq), jnp.ones(n_freq), -jnp.ones(n_freq), jnp.ones(n_freq)]).astype(F32)
    cos = jnp.concatenate([jnp.cos(ang), jnp.ones((CTX, rot_dim), F32)], axis=0)
    sin = jnp.concatenate([jnp.sin(ang) * sign, jnp.zeros((CTX, rot_dim), F32)], axis=0)
    return jnp.concatenate([cos.T, sin.T], axis=0)


def _na_bias_tiles(rpb):
    n_l, n_h = rpb.shape[:2]
    n_off = 2 * NA_ROWS - 1
    p = jnp.pad(rpb.astype(F32) * LOG2E, ((0, 0), (0, 0), (0, 0), (48, 49)))
    sk = jnp.broadcast_to(p[..., None, :], (n_l, n_h, n_off, GRID_W, 128)).reshape(n_l, n_h, n_off, GRID_W * 128)
    sk = sk[..., :GRID_W * 127].reshape(n_l, n_h, n_off, GRID_W, 127)[..., 63:127]
    c = np.arange(GRID_W)
    win_start = np.clip(c - NA_COLS // 2, 0, GRID_W - NA_COLS)
    v_col = (c[None, :] >= win_start[:, None]) & (c[None, :] < win_start[:, None] + NA_COLS)
    tz = jnp.where(v_col, sk, NEG_INF)
    neg1 = jnp.full((n_l, n_h, 1, GRID_W, GRID_W), NEG_INF, F32)
    tzx = jnp.concatenate([neg1, tz, neg1], axis=2)
    first, second = tzx[:, :, 0:16], tzx[:, :, 1:17]
    neg16 = jnp.full_like(first, NEG_INF)
    return jnp.concatenate([
        jnp.concatenate([first, second], axis=-1), jnp.concatenate([first, neg16], axis=-1),
        jnp.concatenate([neg16, second], axis=-1), jnp.concatenate([neg1, neg1], axis=-1)], axis=2)


def kernel(x, c, ctx, c_ctx, w_mod, b_mod, g_mix, w_in, w_out, mla_q_a_g, mla_w_uq, mla_kv_a_g, mla_w_ukv,
           mla_q_g, mla_k_g, diff_q_g, diff_k_g, diff_lq1, diff_lk1, diff_lq2, diff_lk2, diff_subln_g,
           na_q_g, na_k_g, na_rpb, gqa_q_g, gqa_k_g, g_ffn, w_up, conv_w, conv_b, w_down):
    n_b = x.shape[0]
    n_layer = w_mod.shape[0]
    assert x.shape[1:] == (SEQ, D) and ctx.shape[1:] == (CTX, D)

    xs = jnp.concatenate([x, ctx], axis=1)
    mod_rows = -(-(n_b + 1) // 8) * 8
    cc = jnp.concatenate([c, c_ctx[None], jnp.zeros((mod_rows - n_b - 1, D), F32)], axis=0)
    mods_all = _modulation(cc, w_mod, b_mod).reshape(n_layer, mod_rows, 1, 6 * D)

    cs32, cs64 = _rope_table(32), _rope_table(64)
    bias_all = _na_bias_tiles(na_rpb)

    s = np.cumsum([0, 256, 128, 32, 256, 256, 256, 256, 256, 256, 256, 128, 128])
    gq_perm = np.concatenate([np.arange(GQA_D) + (2 * g + r) * GQA_D for r in range(2) for g in range(2)])

    n_l = n_layer
    gq_cols = w_in[:, :, s[9]:s[10]][:, :, gq_perm]
    wqk = jnp.concatenate([w_in[:, :, s[0]:s[3]], w_in[:, :, s[3]:s[5]], w_in[:, :, s[6]:s[8]], gq_cols,
                           w_in[:, :, s[10]:s[11]]], axis=2).swapaxes(1, 2).astype(BF16)
    wv = jnp.concatenate([w_in[:, :, s[5]:s[6]], w_in[:, :, s[8]:s[9]], w_in[:, :, s[11]:s[12]]],
                         axis=2).astype(BF16)
    wuq = mla_w_uq.swapaxes(1, 2).astype(BF16)
    wukv = mla_w_ukv.reshape(n_l, MLA_KVR, MLA_H, MLA_NOPE + MLA_V)
    wukn = wukv[..., :MLA_NOPE].reshape(n_l, MLA_KVR, MLA_H * MLA_NOPE).swapaxes(1, 2).astype(BF16)
    wuv = wukv[..., MLA_NOPE:].reshape(n_l, MLA_KVR, MLA_H * MLA_V).swapaxes(1, 2).astype(BF16)
    gcol = jnp.concatenate([mla_q_a_g, mla_kv_a_g, mla_q_g, mla_k_g, diff_q_g, diff_k_g, na_q_g, na_k_g,
                            gqa_q_g, gqa_k_g], axis=1).astype(F32)[..., None]
    lam_rows = jnp.stack([diff_lq1, diff_lk1, diff_lq2, diff_lk2], axis=1).astype(F32)
    lam_rows = jnp.concatenate([lam_rows, jnp.zeros_like(lam_rows)], axis=1)
    gsub = jnp.tile(diff_subln_g.astype(F32), (1, 2)).reshape(n_l, 1, 128)
    bounds = {
        "mla": _logit_bound(MLA_D, mla_q_g, mla_k_g),
        "diff": _logit_bound(DIFF_D, diff_q_g, diff_k_g),
        "na": _logit_bound(NA_D, na_q_g, na_k_g, LOG2E * jnp.maximum(jnp.max(na_rpb, axis=(1, 2, 3)), 0.0)),
        "gqa": _logit_bound(GQA_D, gqa_q_g, gqa_k_g),
    }
    w_out_b, w_down_b = w_out.astype(BF16), w_down.astype(BF16)
    w_up_b = w_up.astype(BF16)
    conv_wb = jnp.concatenate([conv_w, conv_b[:, None, :]], axis=1).astype(F32)
    gmix, gffn = g_mix.reshape(n_l, 1, D), g_ffn.reshape(n_l, 1, D)

    for l in range(n_layer):
        with_ctx = l < n_layer - 1
        lambda_init = 0.8 - 0.6 * math.exp(-0.3 * l)
        (q_mla, k_mla, v_mla, q_diff, k_diff, v_diff, q_na, k_na, v_na, q_gqa, k_gqa, v_gqa) = _inproj(
            l, xs, mods_all, gmix, wqk, wv, wuq, wukn, wuv, gcol, cs32, cs64)

        n_rb = NRB if with_ctx else NLAT
        mix_a = _attention(_mla_kernel, "attn_mla", q_mla, k_mla, v_mla, bounds["mla"][l], [], n_rb)
        mix_b = _attention(
            functools.partial(_diff_kernel, lambda_init=lambda_init), "attn_diff",
            q_diff, k_diff, v_diff, bounds["diff"][l],
            [(lam_rows, _layer_spec(lam_rows, l)), (gsub, _layer_spec(gsub, l))], n_rb)
        mix_c = _attention(_na_kernel, "attn_na", q_na, k_na, v_na, bounds["na"][l],
                           [(bias_all, _layer_spec(bias_all, l))], n_rb)
        mix_d = _attention(_gqa_kernel, "attn_gqa", q_gqa, k_gqa, v_gqa, bounds["gqa"][l], [], n_rb)

        rows_per_sample = T if with_ctx else SEQ
        x1 = _outproj(l, xs, mods_all, (mix_a, mix_b, mix_c, mix_d), w_out_b, rows_per_sample,
                      OUT_TM if with_ctx else OUT_TM_LAST)
        xs = _ffn(l, x1, mods_all, gffn, w_up_b, conv_wb, w_down_b, rows_per_sample, rows_per_sample // 2)
    return xs
```

```python
import functools
import math

import numpy as np
import jax
import jax.numpy as jnp
from jax import lax
from jax.experimental import pallas as pl
from jax.experimental.pallas import tpu as pltpu

F32 = jnp.float32
BF16 = jnp.bfloat16

D = 1024
SEQ = 2048
GRID_W = 64
CTX = 256
T = SEQ + CTX
RB = 256
NRB = T // RB
NLAT = SEQ // RB
IN_SUB = 3
EPS = 1e-6
NEG_INF = -1e30
ROPE_THETA = 10000.0
LOG2E = 1.4426950408889634

MLA_H, MLA_NOPE, MLA_ROPE, MLA_V, MLA_QR, MLA_KVR = 4, 64, 32, 64, 256, 128
MLA_D = MLA_NOPE + MLA_ROPE
DIFF_H, DIFF_D = 4, 32
NA_H, NA_D, NA_ROWS, NA_COLS = 4, 64, 8, 16
GQA_H, GQA_KV, GQA_D = 4, 2, 64
D_FF = 2816
FF_CHUNK = 768
N_CHUNK = -(-D_FF // FF_CHUNK)
FF_WIDTHS = tuple(min(FF_CHUNK, D_FF - c * FF_CHUNK) for c in range(N_CHUNK))
OUT_TM, OUT_TM_LAST = 768, 1024
HALO = 16
FFN_TILES = 4
CONV_STRIP = 256

R_CQ, R_CKV, R_KR, R_DQ, R_DK, R_NQ, R_NK, R_GQ, R_GK = 0, 256, 384, 416, 672, 928, 1184, 1440, 1696
QK_ROWS = 1824
V_COLS = 640

G_QA, G_KVA, G_MQ, G_MK, G_DQ, G_DK, G_NQ, G_NK, G_GQ, G_GK = 0, 256, 384, 480, 576, 608, 640, 704, 768, 832
G_ROWS = 896

VMEM_LIMIT = 56 * 1024 * 1024


def _params(sem):
    return pltpu.CompilerParams(dimension_semantics=sem, vmem_limit_bytes=VMEM_LIMIT)


def _dot(a, b):
    return jnp.dot(a, b, preferred_element_type=F32)


def _dot_nt(a, b):
    return lax.dot_general(a, b, (((1,), (1,)), ((), ())), preferred_element_type=F32)


def _mod_kernel(c_ref, w_ref, b_ref, o_ref):
    c = c_ref[...]
    a = (c * jax.nn.sigmoid(c)).astype(BF16)
    o_ref[0] = _dot(a, w_ref[0].astype(BF16)) + b_ref[0]


def _modulation(cc, w_mod, b_mod):
    n_layer = w_mod.shape[0]
    rows = cc.shape[0]
    return pl.pallas_call(
        _mod_kernel,
        grid=(n_layer, 6),
        in_specs=[
            pl.BlockSpec((rows, D), lambda l, j: (0, 0)),
            pl.BlockSpec((1, D, D), lambda l, j: (l, 0, j)),
            pl.BlockSpec((1, 1, D), lambda l, j: (l, 0, j)),
        ],
        out_specs=pl.BlockSpec((1, rows, D), lambda l, j: (l, 0, j)),
        out_shape=jax.ShapeDtypeStruct((n_layer, rows, 6 * D), F32),
        compiler_params=_params(("parallel", "parallel")),
        name="modulation",
    )(cc, w_mod, b_mod.reshape(n_layer, 1, 6 * D))


def _rms_rows(x, g, n):
    ss = jnp.sum(x * x, axis=0, keepdims=True) * (1.0 / n)
    return x * lax.rsqrt(ss + EPS) * g


def _rope_rows(x, cos, sin_signed, w):
    rot = jnp.concatenate([x[w:2 * w], x[0:w], x[3 * w:4 * w], x[2 * w:3 * w]], axis=0)
    return x * cos + rot * sin_signed


def _inproj_kernel(x_ref, modb_ref, modc_ref, gmix_ref, wqk_ref, wv_ref, wuq_ref, wukn_ref, wuv_ref, gcol_ref,
                   cs32_ref, cs64_ref,
                   qmla_ref, kmla_ref, vmla_ref, qdiff_ref, kdiff_ref, vdiff_ref,
                   qna_ref, kna_ref, vna_ref, qgqa_ref, kgqa_ref, vgqa_ref):
    step = pl.program_id(0)

    def modulated(sb):
        r0 = sb * RB
        x = x_ref[0, r0:r0 + RB, :]
        mod = modb_ref[0]
        if sb == IN_SUB - 1:
            mod = jnp.where(step == NRB // IN_SUB - 1, modc_ref[0], mod)
        shift, scale = mod[:, 0:D], mod[:, D:2 * D]
        ms = jnp.mean(x * x, axis=-1, keepdims=True)
        h = x * lax.rsqrt(ms + EPS) * gmix_ref[...]
        return (h * (1.0 + scale) + shift).astype(BF16)

    def gain(off, n):
        return gcol_ref[off:off + n, :]

    def block(sb, hb, after_first_projection):
        rows = slice(sb * RB, (sb + 1) * RB)
        _inproj_block(rows, hb, after_first_projection, gain, wqk_ref, wv_ref, wuq_ref, wukn_ref, wuv_ref,
                      cs32_ref, cs64_ref, qmla_ref, kmla_ref, vmla_ref, qdiff_ref, kdiff_ref, vdiff_ref,
                      qna_ref, kna_ref, vna_ref, qgqa_ref, kgqa_ref, vgqa_ref)

    hbs = [modulated(0)]
    for sb in range(IN_SUB):
        nxt = (lambda sb=sb: hbs.append(modulated(sb + 1))) if sb + 1 < IN_SUB else (lambda: None)
        block(sb, hbs[sb], nxt)


def _inproj_block(rows, hb, after_first_projection, gain, wqk_ref, wv_ref, wuq_ref, wukn_ref, wuv_ref,
                  cs32_ref, cs64_ref, qmla_ref, kmla_ref, vmla_ref, qdiff_ref, kdiff_ref, vdiff_ref,
                  qna_ref, kna_ref, vna_ref, qgqa_ref, kgqa_ref, vgqa_ref):
    def project(lo, hi):
        return _dot_nt(wqk_ref[lo:hi, :], hb)

    cos32, sin32 = cs32_ref[0:32, rows], cs32_ref[32:64, rows]
    cos64, sin64 = cs64_ref[0:64, rows], cs64_ref[64:128, rows]

    def values():
        pv = _dot(hb, wv_ref[...])
        vdiff_ref[0, rows, :] = pv[:, 0:256].astype(BF16)
        vna_ref[0, rows, :] = pv[:, 256:512].astype(BF16)
        vgqa_ref[0, rows, :] = pv[:, 512:640].astype(BF16)

    def mla(pt):
        cq = _rms_rows(pt[0:MLA_QR], gain(G_QA, MLA_QR), MLA_QR).astype(BF16)
        qt = _dot(wuq_ref[...], cq)
        ckv = _rms_rows(pt[R_CKV:R_CKV + MLA_KVR], gain(G_KVA, MLA_KVR), MLA_KVR).astype(BF16)
        knt = _dot(wukn_ref[...], ckv)
        vt = _dot(wuv_ref[...], ckv)
        vmla_ref[0, rows, :] = vt.T.astype(BF16)
        kr = pt[R_KR:R_KR + MLA_ROPE]
        kr_ss = jnp.sum(kr * kr, axis=0, keepdims=True)
        g_mq, g_mk = gain(G_MQ, MLA_D), gain(G_MK, MLA_D)
        zpad = jnp.zeros((128 - MLA_D, RB), F32)
        q_parts = []
        for hd in range(MLA_H):
            qh = _rms_rows(qt[hd * MLA_D:(hd + 1) * MLA_D], g_mq, MLA_D)
            q_rope = _rope_rows(qh[MLA_NOPE:], cos32, sin32, MLA_ROPE // 4)
            q_parts += [qh[:MLA_NOPE], q_rope, zpad]
            kn = knt[hd * MLA_NOPE:(hd + 1) * MLA_NOPE]
            ss = (jnp.sum(kn * kn, axis=0, keepdims=True) + kr_ss) * (1.0 / MLA_D)
            r = lax.rsqrt(ss + EPS)
            k_rope = _rope_rows(kr * r * g_mk[MLA_NOPE:], cos32, sin32, MLA_ROPE // 4)
            kmla_ref[0, hd * 128:(hd + 1) * 128, rows] = jnp.concatenate(
                [kn * r * g_mk[:MLA_NOPE], k_rope, zpad], axis=0).astype(BF16)
        q_all = jnp.concatenate(q_parts, axis=0) * (MLA_D ** -0.5 * LOG2E)
        qmla_ref[0, rows, :] = q_all.T.astype(BF16)

    def diff(pt):
        g_dq, g_dk = gain(G_DQ, DIFF_D), gain(G_DK, DIFF_D)
        q_parts, k_parts = [], []
        for gi in range(2 * DIFF_H):
            qg = _rms_rows(pt[gi * DIFF_D:(gi + 1) * DIFF_D], g_dq, DIFF_D)
            q_parts.append(_rope_rows(qg, cos32, sin32, DIFF_D // 4))
            kg = _rms_rows(pt[256 + gi * DIFF_D:256 + (gi + 1) * DIFF_D], g_dk, DIFF_D)
            k_parts.append(_rope_rows(kg, cos32, sin32, DIFF_D // 4))
        qdiff_ref[0, rows, :] = (jnp.concatenate(q_parts, axis=0) * (DIFF_D ** -0.5 * LOG2E)).T.astype(BF16)
        kdiff_ref[0, :, rows] = jnp.concatenate(k_parts, axis=0).astype(BF16)

    def na(pt):
        g_nq, g_nk = gain(G_NQ, NA_D), gain(G_NK, NA_D)
        q_parts, k_parts = [], []
        for hd in range(NA_H):
            q_parts.append(_rms_rows(pt[hd * NA_D:(hd + 1) * NA_D], g_nq, NA_D))
            k_parts.append(_rms_rows(pt[256 + hd * NA_D:256 + (hd + 1) * NA_D], g_nk, NA_D))
        qna_ref[0, rows, :] = (jnp.concatenate(q_parts, axis=0) * (NA_D ** -0.5 * LOG2E)).T.astype(BF16)
        kna_ref[0, :, rows] = jnp.concatenate(k_parts, axis=0).astype(BF16)

    def gqa(pt):
        g_gq, g_gk = gain(G_GQ, GQA_D), gain(G_GK, GQA_D)
        q_parts, k_parts = [], []
        for hd in range(GQA_H):
            qg = _rms_rows(pt[hd * GQA_D:(hd + 1) * GQA_D], g_gq, GQA_D)
            q_parts.append(_rope_rows(qg, cos64, sin64, GQA_D // 4))
        for hd in range(GQA_KV):
            kg = _rms_rows(pt[256 + hd * GQA_D:256 + (hd + 1) * GQA_D], g_gk, GQA_D)
            k_parts.append(_rope_rows(kg, cos64, sin64, GQA_D // 4))
        qgqa_ref[0, rows, :] = (jnp.concatenate(q_parts, axis=0) * (GQA_D ** -0.5 * LOG2E)).T.astype(BF16)
        kgqa_ref[0, :, rows] = jnp.concatenate(k_parts, axis=0).astype(BF16)

    pt_mla = project(R_CQ, R_DQ)
    after_first_projection()
    pt_diff = project(R_DQ, R_NQ)
    mla(pt_mla)
    pt_na = project(R_NQ, R_GQ)
    diff(pt_diff)
    pt_gqa = project(R_GQ, QK_ROWS)
    na(pt_na)
    values()
    gqa(pt_gqa)


def _inproj(l, xs, mods, gmix, wqk, wv, wuq, wukn, wuv, gcol, cs32, cs64):
    n_b = xs.shape[0]

    def full(a):
        return _layer_spec(a, l)

    rows_per_step = IN_SUB * RB

    def tok(width):
        return pl.BlockSpec((1, rows_per_step, width), lambda r, b: (b, r, 0))

    def chan(rows):
        return pl.BlockSpec((1, rows, rows_per_step), lambda r, b: (b, 0, r))

    def tshape(width):
        return jax.ShapeDtypeStruct((n_b, T, width), BF16)

    def cshape(rows):
        return jax.ShapeDtypeStruct((n_b, rows, T), BF16)

    return pl.pallas_call(
        _inproj_kernel,
        grid=(NRB // IN_SUB, n_b),
        in_specs=[
            pl.BlockSpec((1, rows_per_step, D), lambda r, b: (b, r, 0)),
            pl.BlockSpec((None, 1, 1, 6 * D), lambda r, b: (l, b, 0, 0)),
            pl.BlockSpec((None, 1, 1, 6 * D), lambda r, b: (l, n_b, 0, 0)),
            full(gmix), full(wqk), full(wv), full(wuq), full(wukn), full(wuv), full(gcol),
            pl.BlockSpec((64, rows_per_step), lambda r, b: (0, r)),
            pl.BlockSpec((128, rows_per_step), lambda r, b: (0, r)),
        ],
        out_specs=[tok(512), chan(512), tok(256), tok(256), chan(256), tok(256),
                   tok(256), chan(256), tok(256), tok(256), chan(128), tok(128)],
        out_shape=[tshape(512), cshape(512), tshape(256), tshape(256), cshape(256), tshape(256),
                   tshape(256), cshape(256), tshape(256), tshape(256), cshape(128), tshape(128)],
        compiler_params=_params(("parallel", "parallel")),
        name="inproj_prep",
    )(xs, mods, mods, gmix, wqk, wv, wuq, wukn, wuv, gcol, cs32, cs64)


def _lane_id(shape):
    return lax.broadcasted_iota(jnp.int32, shape, 1)


KEY_CHUNK = 768
LAT_CHUNKS = tuple((lo, KEY_CHUNK) for lo in range(0, T, KEY_CHUNK))
CTX_CHUNKS = ((SEQ, CTX),)


def _k_slab(k_ref, idx):
    return lambda lo, n: k_ref[0, idx * 128:(idx + 1) * 128, pl.ds(lo, n)]


def _v_slab(v_ref, idx):
    return lambda lo, n: v_ref[0, pl.ds(lo, n), idx * 128:(idx + 1) * 128]


def _attend(units, chunks, shift=None):
    def scores(i, c):
        q, keys, _, bias = units[i]
        lo, n = chunks[c]
        s = _dot(q, keys(lo, n))
        b = None if bias is None else bias(c)
        return s if b is None else s + b

    if shift is not None:
        items = [(i, c) for i in range(len(units)) for c in range(len(chunks))]
        acc, l = [None] * len(units), [None] * len(units)
        nxt = scores(*items[0])
        for idx, (i, c) in enumerate(items):
            cur = nxt
            if idx + 1 < len(items):
                nxt = scores(*items[idx + 1])
            e = jnp.exp2(cur - shift)
            lc = jnp.sum(e, axis=-1, keepdims=True)
            oc = _dot(e.astype(BF16), units[i][2](*chunks[c]))
            acc[i] = oc if acc[i] is None else acc[i] + oc
            l[i] = lc if l[i] is None else l[i] + lc
        return list(zip(acc, l))

    cur = [scores(0, c) for c in range(len(chunks))]
    out = []
    for i in range(len(units)):
        m = functools.reduce(jnp.maximum, [jnp.max(s, axis=-1, keepdims=True) for s in cur])
        nxt, acc, l = [], None, None
        for c, (lo, n) in enumerate(chunks):
            if i + 1 < len(units):
                nxt.append(scores(i + 1, c))
            e = jnp.exp2(cur[c] - m)
            lc = jnp.sum(e, axis=-1, keepdims=True)
            oc = _dot(e.astype(BF16), units[i][2](lo, n))
            acc = oc if acc is None else acc + oc
            l = lc if l is None else l + lc
        out.append((acc, l))
        cur = nxt
    return out


def _with_shift(bnd_ref, run):
    @pl.when(bnd_ref[1] > 0.5)
    def _():
        run(bnd_ref[0])

    @pl.when(bnd_ref[1] <= 0.5)
    def _():
        run(None)


def _query_blocks(n_rb, bnd_ref, run):
    def all_blocks(shift):
        def body(rb, carry):
            run(rb, pl.ds(pl.multiple_of(rb * RB, RB), RB), True, shift)
            return carry

        lax.fori_loop(0, NLAT, body, 0)
        if n_rb > NLAT:
            run(NLAT, pl.ds(SEQ, CTX), False, shift)

    _with_shift(bnd_ref, all_blocks)


def _mla_kernel(q_ref, k_ref, v_ref, bnd_ref, o_ref, *, n_rb):
    lane = _lane_id((RB, 128))

    def run(rb, rows, latent, shift):
        units = [(q_ref[0, rows,hd * 128:(hd + 1) * 128], _k_slab(k_ref, hd), _v_slab(v_ref, hd // 2), None)
                 for hd in range(MLA_H)]
        o = [acc * (1.0 / l) for acc, l in _attend(units, LAT_CHUNKS if latent else CTX_CHUNKS, shift)]
        outs = [jnp.where(lane < 64, o[0], o[1]), jnp.where(lane < 64, o[2], o[3])]
        o_ref[0, rows, :] = jnp.concatenate(outs, axis=1).astype(o_ref.dtype)

    _query_blocks(n_rb, bnd_ref, run)


def _gqa_kernel(q_ref, k_ref, v_ref, bnd_ref, o_ref, *, n_rb):
    lane = _lane_id((RB, 128))

    def run(rb, rows, latent, shift):
        order = [(rep, grp) for rep in range(2) for grp in range(2)]
        units = []
        for rep, grp in order:
            qs = q_ref[0, rows,rep * 128:(rep + 1) * 128]
            qm = jnp.where((lane >= 64) == (grp == 1), qs, jnp.zeros_like(qs))
            units.append((qm, _k_slab(k_ref, 0), _v_slab(v_ref, 0), None))
        res = {}
        for (rep, grp), (acc, l) in zip(order, _attend(units, LAT_CHUNKS if latent else CTX_CHUNKS, shift)):
            o = acc * (1.0 / l)
            res[(grp, rep)] = o if grp == rep else pltpu.roll(o, 64, axis=1)
        outs = [jnp.where(lane < 64, res[(grp, 0)], res[(grp, 1)]) for grp in range(2)]
        o_ref[0, rows, :] = jnp.concatenate(outs, axis=1).astype(o_ref.dtype)

    _query_blocks(n_rb, bnd_ref, run)


def _diff_kernel(q_ref, k_ref, v_ref, bnd_ref, lam_ref, g_ref, o_ref, *, n_rb, lambda_init):
    lane = _lane_id((RB, 128))
    lq1, lk1, lq2, lk2 = lam_ref[0:1, :], lam_ref[1:2, :], lam_ref[2:3, :], lam_ref[3:4, :]
    lam = (jnp.exp(jnp.sum(lq1 * lk1, axis=-1, keepdims=True))
           - jnp.exp(jnp.sum(lq2 * lk2, axis=-1, keepdims=True)) + lambda_init)
    gsub = g_ref[...]
    grp = lane // DIFF_D

    def run(rb, rows, latent, shift):
        units = []
        for hd in range(DIFF_H):
            pair, sub = divmod(hd, 2)
            qs = q_ref[0, rows,pair * 128:(pair + 1) * 128]
            for which in range(2):
                qm = jnp.where(grp == 2 * sub + which, qs, jnp.zeros_like(qs))
                units.append((qm, _k_slab(k_ref, pair), _v_slab(v_ref, pair), None))
        res = _attend(units, LAT_CHUNKS if latent else CTX_CHUNKS, shift)
        outs = []
        for pair in range(2):
            halves = []
            for sub in range(2):
                (a1, l1), (a2, l2) = res[2 * (2 * pair + sub)], res[2 * (2 * pair + sub) + 1]
                halves.append(a1 * (1.0 / l1) - a2 * (lam / l2))
            o = jnp.where(lane < 64, halves[0], halves[1])
            o2 = o * o
            ss0 = jnp.sum(jnp.where(lane < 64, o2, 0.0), axis=-1, keepdims=True)
            ss1 = jnp.sum(jnp.where(lane < 64, 0.0, o2), axis=-1, keepdims=True)
            ss = jnp.where(lane < 64, ss0, ss1) * (1.0 / (2 * DIFF_D))
            outs.append(o * lax.rsqrt(ss + EPS) * gsub * (1.0 - lambda_init))
        o_ref[0, rows, :] = jnp.concatenate(outs, axis=1).astype(o_ref.dtype)

    _query_blocks(n_rb, bnd_ref, run)


NA_WIN = 768
NA_CHUNK = 768
NA_TILES = 49
GRID_ROWS = SEQ // GRID_W


def _na_kernel(q_ref, k_ref, v_ref, bnd_ref, bias_ref, o_ref, *, n_rb):
    lane = _lane_id((RB, 128))

    def run(rb, rows, latent, shift):
        na_chunks, tile_idx = CTX_CHUNKS, None
        if latent:
            win_row = jnp.clip(4 * rb - 4, 0, GRID_ROWS - NA_WIN // GRID_W)
            off = pl.multiple_of(win_row * GRID_W, RB)
            local = [(pl.multiple_of(off + c * NA_CHUNK, RB), NA_CHUNK) for c in range(NA_WIN // NA_CHUNK)]
            na_chunks = local + list(CTX_CHUNKS)
            tile_idx = []
            for c in range(NA_WIN // NA_CHUNK):
                per_row = []
                for ri in range(RB // GRID_W):
                    r = 4 * rb + ri
                    row_start = jnp.clip(r - NA_ROWS // 2, 0, GRID_ROWS - NA_ROWS)
                    idxs = []
                    for p in range(NA_CHUNK // 128):
                        kr = win_row + c * (NA_CHUNK // GRID_W) + 2 * p
                        ok1 = (kr >= row_start) & (kr < row_start + NA_ROWS)
                        ok2 = (kr + 1 >= row_start) & (kr + 1 < row_start + NA_ROWS)
                        e = jnp.clip(kr - r + NA_ROWS, 0, 15)
                        idxs.append(jnp.where(ok1 & ok2, e, jnp.where(ok1, 16 + e, jnp.where(ok2, 32 + e, 48))))
                    per_row.append(idxs)
                tile_idx.append(per_row)
        units = []
        for hd in range(NA_H):
            pair, sub = divmod(hd, 2)
            qs = q_ref[0, rows,pair * 128:(pair + 1) * 128]
            qm = jnp.where((lane >= 64) == (sub == 1), qs, jnp.zeros_like(qs))

            def bias(c, hd=hd):
                if tile_idx is None or c >= NA_WIN // NA_CHUNK:
                    return None
                return jnp.concatenate(
                    [jnp.concatenate([bias_ref[hd, idx] for idx in row], axis=1) for row in tile_idx[c]], axis=0)

            units.append((qm, _k_slab(k_ref, pair), _v_slab(v_ref, pair), bias))
        o = [acc * (1.0 / l) for acc, l in _attend(units, na_chunks, shift)]
        outs = [jnp.where(lane < 64, o[0], o[1]), jnp.where(lane < 64, o[2], o[3])]
        o_ref[0, rows, :] = jnp.concatenate(outs, axis=1).astype(o_ref.dtype)

    _query_blocks(n_rb, bnd_ref, run)


SHIFT_MAX = 40.0


def _logit_bound(d, q_gain, k_gain, bias_max=0.0):
    bound = (1.01 * math.sqrt(d) * LOG2E * jnp.max(jnp.abs(q_gain), axis=-1) * jnp.max(jnp.abs(k_gain), axis=-1)
             + bias_max)
    return jnp.stack([bound, (bound <= SHIFT_MAX).astype(F32)], axis=-1).astype(F32)


def _layer_spec(a, l):
    return pl.BlockSpec((None,) + a.shape[1:], lambda *_, _n=a.ndim - 1: (l,) + (0,) * _n)


def _attention(body, name, q, k, v, bound, extra, n_rb, out_width=256):
    n_b = q.shape[0]
    extra = [(bound, pl.BlockSpec(memory_space=pltpu.SMEM))] + list(extra)
    in_specs = [
        pl.BlockSpec((1, T, q.shape[2]), lambda b: (b, 0, 0)),
        pl.BlockSpec((1, k.shape[1], T), lambda b: (b, 0, 0)),
        pl.BlockSpec((1, T, v.shape[2]), lambda b: (b, 0, 0)),
    ] + [spec for _, spec in extra]
    return pl.pallas_call(
        functools.partial(body, n_rb=n_rb),
        grid=(n_b,),
        in_specs=in_specs,
        out_specs=pl.BlockSpec((1, n_rb * RB, out_width), lambda b: (b, 0, 0)),
        out_shape=jax.ShapeDtypeStruct((n_b, n_rb * RB, out_width), BF16),
        compiler_params=_params(("parallel",)),
        name=name,
    )(q, k, v, *[a for a, _ in extra])


def _outproj_kernel(x_ref, modb_ref, modc_ref, ma_ref, mb_ref, mc_ref, md_ref, w_ref, o_ref, *, tm):
    rows = pl.program_id(1) * tm + lax.broadcasted_iota(jnp.int32, (tm, 1), 0)
    g1 = jnp.where(rows >= SEQ, modc_ref[0][:, 2 * D:3 * D], modb_ref[0][:, 2 * D:3 * D])
    acc = _dot(ma_ref[0], w_ref[0:256, :])
    acc += _dot(mb_ref[0], w_ref[256:512, :])
    acc += _dot(mc_ref[0], w_ref[512:768, :])
    acc += _dot(md_ref[0], w_ref[768:1024, :])
    o_ref[0] = x_ref[0] + g1 * acc


def _outproj(l, xs, mods, mixes, w_out, rows_per_sample, tm):
    n_b = xs.shape[0]
    mix_spec = pl.BlockSpec((1, tm, 256), lambda b, r: (b, r, 0))
    return pl.pallas_call(
        functools.partial(_outproj_kernel, tm=tm),
        grid=(n_b, rows_per_sample // tm),
        in_specs=[
            pl.BlockSpec((1, tm, D), lambda b, r: (b, r, 0)),
            pl.BlockSpec((None, 1, 1, 6 * D), lambda b, r: (l, b, 0, 0)),
            pl.BlockSpec((None, 1, 1, 6 * D), lambda b, r: (l, n_b, 0, 0)),
            mix_spec, mix_spec, mix_spec, mix_spec,
            _layer_spec(w_out, l),
        ],
        out_specs=pl.BlockSpec((1, tm, D), lambda b, r: (b, r, 0)),
        out_shape=jax.ShapeDtypeStruct((n_b, rows_per_sample, D), F32),
        compiler_params=_params(("parallel", "parallel")),
        name="outproj",
    )(xs, mods, mods, *mixes, w_out)


def _ffn_kernel(x_ref, hb_ref, ha_ref, modb_ref, modc_ref, g_ref, wa_ref, wg_ref, cwa_ref, cwg_ref, wd_ref, o_ref,
                h_scr, u0_scr, u1_scr, act_scr, *, tm, rows_per_sample):
    rb = pl.program_id(1)
    j = pl.program_id(2)
    row0 = rb * tm
    split = SEQ % tm if rows_per_sample > SEQ else 0
    band_lo, band_hi = split - HALO, split + HALO

    def per_row(rows, col):
        lat = modb_ref[0][:, col * D:(col + 1) * D]
        if rows_per_sample == SEQ:
            return lat
        return jnp.where(rows >= SEQ, modc_ref[0][:, col * D:(col + 1) * D], lat)

    def modulated(x, rows):
        ms = jnp.mean(x * x, axis=-1, keepdims=True)
        y = x * lax.rsqrt(ms + EPS) * g_ref[...]
        return (y * (1.0 + per_row(rows, 4)) + per_row(rows, 3)).astype(BF16)

    def prologue():
        halo_rows = lax.broadcasted_iota(jnp.int32, (HALO, 1), 0)
        has_before = (row0 != 0) & (row0 != SEQ)
        has_after = (row0 + tm != SEQ) & (row0 + tm != rows_per_sample)
        zero = jnp.zeros((HALO, D), BF16)
        h_scr[0:HALO, :] = jnp.where(has_before, modulated(hb_ref[0], row0 - HALO + halo_rows), zero)
        h_scr[HALO + tm:, :] = jnp.where(has_after, modulated(ha_ref[0], row0 + tm + halo_rows), zero)
        main_rows = row0 + lax.broadcasted_iota(jnp.int32, (tm, 1), 0)
        h_scr[HALO:HALO + tm, :] = modulated(x_ref[0], main_rows)
        o_ref[0] = jnp.zeros((tm, D), F32)

    slots = (u0_scr, u1_scr)

    def tiles(total):
        units = total // HALO
        sizes = [(units // FFN_TILES + (1 if t < units % FFN_TILES else 0)) * HALO for t in range(FFN_TILES)]
        edges = np.cumsum([0] + sizes)
        return [(int(edges[t]), int(edges[t + 1])) for t in range(FFN_TILES)]

    up_tiles, down_tiles = tiles(tm + 2 * HALO), tiles(tm)

    def up(c, t):
        lo, hi = up_tiles[t]
        w = FF_WIDTHS[c]
        slots[c % 2][lo:hi, :w] = _dot(h_scr[lo:hi, :], wa_ref[:, :w])
        slots[c % 2][lo:hi, FF_CHUNK:FF_CHUNK + w] = _dot(h_scr[lo:hi, :], wg_ref[:, FF_CHUNK - w:])

    def conv(u_scr, cw_ref, lo, hi, u_lo, c_lo, width, masked):
        prev = u_scr[HALO - 1 + lo:HALO - 1 + hi, u_lo:u_lo + width]
        nxt = u_scr[HALO + 1 + lo:HALO + 1 + hi, u_lo:u_lo + width]
        if masked:
            rows = row0 + lo + lax.broadcasted_iota(jnp.int32, (hi - lo, 1), 0)
            prev = jnp.where(rows != SEQ, prev, 0.0)
            nxt = jnp.where(rows != SEQ - 1, nxt, 0.0)
        cols = slice(c_lo, c_lo + width)
        return (prev * cw_ref[0:1, cols] + u_scr[HALO + lo:HALO + hi, u_lo:u_lo + width] * cw_ref[1:2, cols]
                + nxt * cw_ref[2:3, cols] + cw_ref[3:4, cols])

    def conv_down(c, t):
        u_scr, w = slots[c % 2], FF_WIDTHS[c]
        t_lo, t_hi = down_tiles[t]
        cuts = sorted({t_lo, t_hi} | ({e for e in (band_lo, band_hi) if t_lo < e < t_hi} if split else set()))
        for lo, hi in zip(cuts[:-1], cuts[1:]):
            masked = bool(split) and band_lo <= lo < band_hi
            for c_lo in range(0, w, CONV_STRIP):
                a = conv(u_scr, cwa_ref, lo, hi, c_lo, c_lo, CONV_STRIP, masked)
                g = conv(u_scr, cwg_ref, lo, hi, FF_CHUNK + c_lo, FF_CHUNK - w + c_lo, CONV_STRIP, masked)
                act_scr[lo:hi, c_lo:c_lo + CONV_STRIP] = (g * jax.nn.sigmoid(g) * a).astype(BF16)
        o_ref[0, t_lo:t_hi, :] += _dot(act_scr[t_lo:t_hi, :w], wd_ref[FF_CHUNK - w:, :])

    for step in range(N_CHUNK + 1):
        @pl.when(j == step)
        def _(step=step):
            if step == 0:
                prologue()
            for t in range(FFN_TILES):
                if step < N_CHUNK:
                    up(step, t)
                if step > 0:
                    conv_down(step - 1, t)
            if step == N_CHUNK:
                rows = row0 + lax.broadcasted_iota(jnp.int32, (tm, 1), 0)
                o_ref[0] = x_ref[0] + per_row(rows, 5) * o_ref[0]


def _ffn(l, x1, mods, g_ffn, w_up, conv_wb, w_down, rows_per_sample, tm):
    n_b = x1.shape[0]
    n_rb = rows_per_sample // tm
    halo_per_block = tm // HALO
    n_halo = x1.shape[1] // HALO
    body = functools.partial(_ffn_kernel, tm=tm, rows_per_sample=rows_per_sample)

    def up_idx(j):
        return jnp.minimum(j, N_CHUNK - 1)

    def down_idx(j):
        return jnp.maximum(j - 1, 0)

    def tail_clamped(c):
        return pl.multiple_of(jnp.minimum(c * FF_CHUNK, D_FF - FF_CHUNK), 256)

    return pl.pallas_call(
        body,
        grid=(n_b, n_rb, N_CHUNK + 1),
        in_specs=[
            pl.BlockSpec((1, tm, D), lambda b, r, j: (b, r, 0)),
            pl.BlockSpec((1, HALO, D), lambda b, r, j: (b, jnp.maximum(r * halo_per_block - 1, 0), 0)),
            pl.BlockSpec((1, HALO, D), lambda b, r, j: (b, jnp.minimum((r + 1) * halo_per_block, n_halo - 1), 0)),
            pl.BlockSpec((None, 1, 1, 6 * D), lambda b, r, j: (l, b, 0, 0)),
            pl.BlockSpec((None, 1, 1, 6 * D), lambda b, r, j: (l, n_b, 0, 0)),
            _layer_spec(g_ffn, l),
            pl.BlockSpec((None, D, FF_CHUNK), lambda b, r, j: (l, 0, up_idx(j))),
            pl.BlockSpec((None, pl.Element(D), pl.Element(FF_CHUNK)),
                         lambda b, r, j: (l, 0, pl.multiple_of(D_FF + tail_clamped(up_idx(j)), 256))),
            pl.BlockSpec((None, 4, FF_CHUNK), lambda b, r, j: (l, 0, down_idx(j))),
            pl.BlockSpec((None, pl.Element(4), pl.Element(FF_CHUNK)),
                         lambda b, r, j: (l, 0, pl.multiple_of(D_FF + tail_clamped(down_idx(j)), 256))),
            pl.BlockSpec((None, pl.Element(FF_CHUNK), pl.Element(D)),
                         lambda b, r, j: (l, tail_clamped(down_idx(j)), 0)),
        ],
        out_specs=pl.BlockSpec((1, tm, D), lambda b, r, j: (b, r, 0)),
        out_shape=jax.ShapeDtypeStruct((n_b, rows_per_sample, D), F32),
        scratch_shapes=[
            pltpu.VMEM((tm + 2 * HALO, D), BF16),
            pltpu.VMEM((tm + 2 * HALO, 2 * FF_CHUNK), F32),
            pltpu.VMEM((tm + 2 * HALO, 2 * FF_CHUNK), F32),
            pltpu.VMEM((tm, FF_CHUNK), BF16),
        ],
        compiler_params=_params(("parallel", "parallel", "arbitrary")),
        name="conv_ffn",
    )(x1, x1, x1, mods, mods, g_ffn, w_up, w_up, conv_wb, conv_wb, w_down)


def _rope_table(rot_dim):
    n_freq = rot_dim // 4
    inv = jnp.power(ROPE_THETA, -jnp.arange(n_freq, dtype=F32) / n_freq)
    t = jnp.arange(SEQ)
    row = (t // GRID_W).astype(F32)
    col = (t % GRID_W).astype(F32)
    ar, ac = row[:, None] * inv, col[:, None] * inv
    ang = jnp.concatenate([ar, ar, ac, ac], axis=-1)
    sign = jnp.concatenate([-jnp.ones(n_freq), jnp.ones(n_freq), -jnp.ones(n_freq), jnp.ones(n_freq)]).astype(F32)
    cos = jnp.concatenate([jnp.cos(ang), jnp.ones((CTX, rot_dim), F32)], axis=0)
    sin = jnp.concatenate([jnp.sin(ang) * sign, jnp.zeros((CTX, rot_dim), F32)], axis=0)
    return jnp.concatenate([cos.T, sin.T], axis=0)


def _na_bias_tiles(rpb):
    n_l, n_h = rpb.shape[:2]
    n_off = 2 * NA_ROWS - 1
    p = jnp.pad(rpb.astype(F32) * LOG2E, ((0, 0), (0, 0), (0, 0), (48, 49)))
    sk = jnp.broadcast_to(p[..., None, :], (n_l, n_h, n_off, GRID_W, 128)).reshape(n_l, n_h, n_off, GRID_W * 128)
    sk = sk[..., :GRID_W * 127].reshape(n_l, n_h, n_off, GRID_W, 127)[..., 63:127]
    c = np.arange(GRID_W)
    win_start = np.clip(c - NA_COLS // 2, 0, GRID_W - NA_COLS)
    v_col = (c[None, :] >= win_start[:, None]) & (c[None, :] < win_start[:, None] + NA_COLS)
    tz = jnp.where(v_col, sk, NEG_INF)
    neg1 = jnp.full((n_l, n_h, 1, GRID_W, GRID_W), NEG_INF, F32)
    tzx = jnp.concatenate([neg1, tz, neg1], axis=2)
    first, second = tzx[:, :, 0:16], tzx[:, :, 1:17]
    neg16 = jnp.full_like(first, NEG_INF)
    return jnp.concatenate([
        jnp.concatenate([first, second], axis=-1), jnp.concatenate([first, neg16], axis=-1),
        jnp.concatenate([neg16, second], axis=-1), jnp.concatenate([neg1, neg1], axis=-1)], axis=2)


def kernel(x, c, ctx, c_ctx, w_mod, b_mod, g_mix, w_in, w_out, mla_q_a_g, mla_w_uq, mla_kv_a_g, mla_w_ukv,
           mla_q_g, mla_k_g, diff_q_g, diff_k_g, diff_lq1, diff_lk1, diff_lq2, diff_lk2, diff_subln_g,
           na_q_g, na_k_g, na_rpb, gqa_q_g, gqa_k_g, g_ffn, w_up, conv_w, conv_b, w_down):
    n_b = x.shape[0]
    n_layer = w_mod.shape[0]
    assert x.shape[1:] == (SEQ, D) and ctx.shape[1:] == (CTX, D)

    xs = jnp.concatenate([x, ctx], axis=1)
    mod_rows = -(-(n_b + 1) // 8) * 8
    cc = jnp.concatenate([c, c_ctx[None], jnp.zeros((mod_rows - n_b - 1, D), F32)], axis=0)
    mods_all = _modulation(cc, w_mod, b_mod).reshape(n_layer, mod_rows, 1, 6 * D)

    cs32, cs64 = _rope_table(32), _rope_table(64)
    bias_all = _na_bias_tiles(na_rpb)

    s = np.cumsum([0, 256, 128, 32, 256, 256, 256, 256, 256, 256, 256, 128, 128])
    gq_perm = np.concatenate([np.arange(GQA_D) + (2 * g + r) * GQA_D for r in range(2) for g in range(2)])

    n_l = n_layer
    gq_cols = w_in[:, :, s[9]:s[10]][:, :, gq_perm]
    wqk = jnp.concatenate([w_in[:, :, s[0]:s[3]], w_in[:, :, s[3]:s[5]], w_in[:, :, s[6]:s[8]], gq_cols,
                           w_in[:, :, s[10]:s[11]]], axis=2).swapaxes(1, 2).astype(BF16)
    wv = jnp.concatenate([w_in[:, :, s[5]:s[6]], w_in[:, :, s[8]:s[9]], w_in[:, :, s[11]:s[12]]],
                         axis=2).astype(BF16)
    wuq = mla_w_uq.swapaxes(1, 2).astype(BF16)
    wukv = mla_w_ukv.reshape(n_l, MLA_KVR, MLA_H, MLA_NOPE + MLA_V)
    wukn = wukv[..., :MLA_NOPE].reshape(n_l, MLA_KVR, MLA_H * MLA_NOPE).swapaxes(1, 2).astype(BF16)
    wuv = wukv[..., MLA_NOPE:].reshape(n_l, MLA_KVR, MLA_H * MLA_V).swapaxes(1, 2).astype(BF16)
    gcol = jnp.concatenate([mla_q_a_g, mla_kv_a_g, mla_q_g, mla_k_g, diff_q_g, diff_k_g, na_q_g, na_k_g,
                            gqa_q_g, gqa_k_g], axis=1).astype(F32)[..., None]
    lam_rows = jnp.stack([diff_lq1, diff_lk1, diff_lq2, diff_lk2], axis=1).astype(F32)
    lam_rows = jnp.concatenate([lam_rows, jnp.zeros_like(lam_rows)], axis=1)
    gsub = jnp.tile(diff_subln_g.astype(F32), (1, 2)).reshape(n_l, 1, 128)
    bounds = {
        "mla": _logit_bound(MLA_D, mla_q_g, mla_k_g),
        "diff": _logit_bound(DIFF_D, diff_q_g, diff_k_g),
        "na": _logit_bound(NA_D, na_q_g, na_k_g, LOG2E * jnp.maximum(jnp.max(na_rpb, axis=(1, 2, 3)), 0.0)),
        "gqa": _logit_bound(GQA_D, gqa_q_g, gqa_k_g),
    }
    w_out_b, w_down_b = w_out.astype(BF16), w_down.astype(BF16)
    w_up_b = w_up.astype(BF16)
    conv_wb = jnp.concatenate([conv_w, conv_b[:, None, :]], axis=1).astype(F32)
    gmix, gffn = g_mix.reshape(n_l, 1, D), g_ffn.reshape(n_l, 1, D)

    for l in range(n_layer):
        with_ctx = l < n_layer - 1
        lambda_init = 0.8 - 0.6 * math.exp(-0.3 * l)
        (q_mla, k_mla, v_mla, q_diff, k_diff, v_diff, q_na, k_na, v_na, q_gqa, k_gqa, v_gqa) = _inproj(
            l, xs, mods_all, gmix, wqk, wv, wuq, wukn, wuv, gcol, cs32, cs64)

        n_rb = NRB if with_ctx else NLAT
        mix_a = _attention(_mla_kernel, "attn_mla", q_mla, k_mla, v_mla, bounds["mla"][l], [], n_rb)
        mix_b = _attention(
            functools.partial(_diff_kernel, lambda_init=lambda_init), "attn_diff",
            q_diff, k_diff, v_diff, bounds["diff"][l],
            [(lam_rows, _layer_spec(lam_rows, l)), (gsub, _layer_spec(gsub, l))], n_rb)
        mix_c = _attention(_na_kernel, "attn_na", q_na, k_na, v_na, bounds["na"][l],
                           [(bias_all, _layer_spec(bias_all, l))], n_rb)
        mix_d = _attention(_gqa_kernel, "attn_gqa", q_gqa, k_gqa, v_gqa, bounds["gqa"][l], [], n_rb)

        rows_per_sample = T if with_ctx else SEQ
        x1 = _outproj(l, xs, mods_all, (mix_a, mix_b, mix_c, mix_d), w_out_b, rows_per_sample,
                      OUT_TM if with_ctx else OUT_TM_LAST)
        xs = _ffn(l, x1, mods_all, gffn, w_up_b, conv_wb, w_down_b, rows_per_sample, rows_per_sample // 2)
    return xs
```

```python
import functools
import math

import numpy as np
import jax
import jax.numpy as jnp
from jax import lax
from jax.experimental import pallas as pl
from jax.experimental.pallas import tpu as pltpu

F32 = jnp.float32
BF16 = jnp.bfloat16

D = 1024
SEQ = 2048
GRID_W = 64
CTX = 256
T = SEQ + CTX
RB = 256
NRB = T // RB
NLAT = SEQ // RB
IN_SUB = 3
EPS = 1e-6
NEG_INF = -1e30
ROPE_THETA = 10000.0
LOG2E = 1.4426950408889634

MLA_H, MLA_NOPE, MLA_ROPE, MLA_V, MLA_QR, MLA_KVR = 4, 64, 32, 64, 256, 128
MLA_D = MLA_NOPE + MLA_ROPE
DIFF_H, DIFF_D = 4, 32
NA_H, NA_D, NA_ROWS, NA_COLS = 4, 64, 8, 16
GQA_H, GQA_KV, GQA_D = 4, 2, 64
D_FF = 2816
FF_CHUNK = 1024
N_CHUNK = -(-D_FF // FF_CHUNK)
FF_WIDTHS = tuple(min(FF_CHUNK, D_FF - c * FF_CHUNK) for c in range(N_CHUNK))
OUT_TM, OUT_TM_LAST = 768, 1024
HALO = 16
FFN_TILES = 4
CONV_STRIP = 256

R_CQ, R_CKV, R_KR, R_DQ, R_DK, R_NQ, R_NK, R_GQ, R_GK = 0, 256, 384, 416, 672, 928, 1184, 1440, 1696
QK_ROWS = 1824
V_COLS = 640

G_QA, G_KVA, G_MQ, G_MK, G_DQ, G_DK, G_NQ, G_NK, G_GQ, G_GK = 0, 256, 384, 480, 576, 608, 640, 704, 768, 832
G_ROWS = 896

VMEM_LIMIT = 56 * 1024 * 1024


def _params(sem):
    return pltpu.CompilerParams(dimension_semantics=sem, vmem_limit_bytes=VMEM_LIMIT)


def _dot(a, b):
    return jnp.dot(a, b, preferred_element_type=F32)


def _dot_nt(a, b):
    return lax.dot_general(a, b, (((1,), (1,)), ((), ())), preferred_element_type=F32)


def _mod_kernel(c_ref, w_ref, b_ref, o_ref):
    c = c_ref[...]
    a = (c * jax.nn.sigmoid(c)).astype(BF16)
    o_ref[0] = _dot(a, w_ref[0].astype(BF16)) + b_ref[0]


def _modulation(cc, w_mod, b_mod):
    n_layer = w_mod.shape[0]
    rows = cc.shape[0]
    return pl.pallas_call(
        _mod_kernel,
        grid=(n_layer, 6),
        in_specs=[
            pl.BlockSpec((rows, D), lambda l, j: (0, 0)),
            pl.BlockSpec((1, D, D), lambda l, j: (l, 0, j)),
            pl.BlockSpec((1, 1, D), lambda l, j: (l, 0, j)),
        ],
        out_specs=pl.BlockSpec((1, rows, D), lambda l, j: (l, 0, j)),
        out_shape=jax.ShapeDtypeStruct((n_layer, rows, 6 * D), F32),
        compiler_params=_params(("parallel", "parallel")),
        name="modulation",
    )(cc, w_mod, b_mod.reshape(n_layer, 1, 6 * D))


def _rms_rows(x, g, n):
    ss = jnp.sum(x * x, axis=0, keepdims=True) * (1.0 / n)
    return x * lax.rsqrt(ss + EPS) * g


def _rope_rows(x, cos, sin_signed, w):
    rot = jnp.concatenate([x[w:2 * w], x[0:w], x[3 * w:4 * w], x[2 * w:3 * w]], axis=0)
    return x * cos + rot * sin_signed


def _inproj_kernel(x_ref, modb_ref, modc_ref, gmix_ref, wqk_ref, wv_ref, wuq_ref, wukn_ref, wuv_ref, gcol_ref,
                   cs32_ref, cs64_ref,
                   qmla_ref, kmla_ref, vmla_ref, qdiff_ref, kdiff_ref, vdiff_ref,
                   qna_ref, kna_ref, vna_ref, qgqa_ref, kgqa_ref, vgqa_ref):
    step = pl.program_id(0)

    def modulated(sb):
        r0 = sb * RB
        x = x_ref[0, r0:r0 + RB, :]
        mod = modb_ref[0]
        if sb == IN_SUB - 1:
            mod = jnp.where(step == NRB // IN_SUB - 1, modc_ref[0], mod)
        shift, scale = mod[:, 0:D], mod[:, D:2 * D]
        ms = jnp.mean(x * x, axis=-1, keepdims=True)
        h = x * lax.rsqrt(ms + EPS) * gmix_ref[...]
        return (h * (1.0 + scale) + shift).astype(BF16)

    def gain(off, n):
        return gcol_ref[off:off + n, :]

    def block(sb, hb, after_first_projection):
        rows = slice(sb * RB, (sb + 1) * RB)
        _inproj_block(rows, hb, after_first_projection, gain, wqk_ref, wv_ref, wuq_ref, wukn_ref, wuv_ref,
                      cs32_ref, cs64_ref, qmla_ref, kmla_ref, vmla_ref, qdiff_ref, kdiff_ref, vdiff_ref,
                      qna_ref, kna_ref, vna_ref, qgqa_ref, kgqa_ref, vgqa_ref)

    hbs = [modulated(0)]
    for sb in range(IN_SUB):
        nxt = (lambda sb=sb: hbs.append(modulated(sb + 1))) if sb + 1 < IN_SUB else (lambda: None)
        block(sb, hbs[sb], nxt)


def _inproj_block(rows, hb, after_first_projection, gain, wqk_ref, wv_ref, wuq_ref, wukn_ref, wuv_ref,
                  cs32_ref, cs64_ref, qmla_ref, kmla_ref, vmla_ref, qdiff_ref, kdiff_ref, vdiff_ref,
                  qna_ref, kna_ref, vna_ref, qgqa_ref, kgqa_ref, vgqa_ref):
    def project(lo, hi):
        return _dot_nt(wqk_ref[lo:hi, :], hb)

    cos32, sin32 = cs32_ref[0:32, rows], cs32_ref[32:64, rows]
    cos64, sin64 = cs64_ref[0:64, rows], cs64_ref[64:128, rows]

    def values():
        pv = _dot(hb, wv_ref[...])
        vdiff_ref[0, rows, :] = pv[:, 0:256].astype(BF16)
        vna_ref[0, rows, :] = pv[:, 256:512].astype(BF16)
        vgqa_ref[0, rows, :] = pv[:, 512:640].astype(BF16)

    def mla(pt):
        cq = _rms_rows(pt[0:MLA_QR], gain(G_QA, MLA_QR), MLA_QR).astype(BF16)
        qt = _dot(wuq_ref[...], cq)
        ckv = _rms_rows(pt[R_CKV:R_CKV + MLA_KVR], gain(G_KVA, MLA_KVR), MLA_KVR).astype(BF16)
        knt = _dot(wukn_ref[...], ckv)
        vt = _dot(wuv_ref[...], ckv)
        vmla_ref[0, rows, :] = vt.T.astype(BF16)
        kr = pt[R_KR:R_KR + MLA_ROPE]
        kr_ss = jnp.sum(kr * kr, axis=0, keepdims=True)
        g_mq, g_mk = gain(G_MQ, MLA_D), gain(G_MK, MLA_D)
        zpad = jnp.zeros((128 - MLA_D, RB), F32)
        q_parts = []
        for hd in range(MLA_H):
            qh = _rms_rows(qt[hd * MLA_D:(hd + 1) * MLA_D], g_mq, MLA_D)
            q_rope = _rope_rows(qh[MLA_NOPE:], cos32, sin32, MLA_ROPE // 4)
            q_parts += [qh[:MLA_NOPE], q_rope, zpad]
            kn = knt[hd * MLA_NOPE:(hd + 1) * MLA_NOPE]
            ss = (jnp.sum(kn * kn, axis=0, keepdims=True) + kr_ss) * (1.0 / MLA_D)
            r = lax.rsqrt(ss + EPS)
            k_rope = _rope_rows(kr * r * g_mk[MLA_NOPE:], cos32, sin32, MLA_ROPE // 4)
            kmla_ref[0, hd * 128:(hd + 1) * 128, rows] = jnp.concatenate(
                [kn * r * g_mk[:MLA_NOPE], k_rope, zpad], axis=0).astype(BF16)
        q_all = jnp.concatenate(q_parts, axis=0) * (MLA_D ** -0.5 * LOG2E)
        qmla_ref[0, rows, :] = q_all.T.astype(BF16)

    def diff(pt):
        g_dq, g_dk = gain(G_DQ, DIFF_D), gain(G_DK, DIFF_D)
        q_parts, k_parts = [], []
        for gi in range(2 * DIFF_H):
            qg = _rms_rows(pt[gi * DIFF_D:(gi + 1) * DIFF_D], g_dq, DIFF_D)
            q_parts.append(_rope_rows(qg, cos32, sin32, DIFF_D // 4))
            kg = _rms_rows(pt[256 + gi * DIFF_D:256 + (gi + 1) * DIFF_D], g_dk, DIFF_D)
            k_parts.append(_rope_rows(kg, cos32, sin32, DIFF_D // 4))
        qdiff_ref[0, rows, :] = (jnp.concatenate(q_parts, axis=0) * (DIFF_D ** -0.5 * LOG2E)).T.astype(BF16)
        kdiff_ref[0, :, rows] = jnp.concatenate(k_parts, axis=0).astype(BF16)

    def na(pt):
        g_nq, g_nk = gain(G_NQ, NA_D), gain(G_NK, NA_D)
        q_parts, k_parts = [], []
        for hd in range(NA_H):
            q_parts.append(_rms_rows(pt[hd * NA_D:(hd + 1) * NA_D], g_nq, NA_D))
            k_parts.append(_rms_rows(pt[256 + hd * NA_D:256 + (hd + 1) * NA_D], g_nk, NA_D))
        qna_ref[0, rows, :] = (jnp.concatenate(q_parts, axis=0) * (NA_D ** -0.5 * LOG2E)).T.astype(BF16)
        kna_ref[0, :, rows] = jnp.concatenate(k_parts, axis=0).astype(BF16)

    def gqa(pt):
        g_gq, g_gk = gain(G_GQ, GQA_D), gain(G_GK, GQA_D)
        q_parts, k_parts = [], []
        for hd in range(GQA_H):
            qg = _rms_rows(pt[hd * GQA_D:(hd + 1) * GQA_D], g_gq, GQA_D)
            q_parts.append(_rope_rows(qg, cos64, sin64, GQA_D // 4))
        for hd in range(GQA_KV):
            kg = _rms_rows(pt[256 + hd * GQA_D:256 + (hd + 1) * GQA_D], g_gk, GQA_D)
            k_parts.append(_rope_rows(kg, cos64, sin64, GQA_D // 4))
        qgqa_ref[0, rows, :] = (jnp.concatenate(q_parts, axis=0) * (GQA_D ** -0.5 * LOG2E)).T.astype(BF16)
        kgqa_ref[0, :, rows] = jnp.concatenate(k_parts, axis=0).astype(BF16)

    pt_mla = project(R_CQ, R_DQ)
    after_first_projection()
    pt_diff = project(R_DQ, R_NQ)
    mla(pt_mla)
    pt_na = project(R_NQ, R_GQ)
    diff(pt_diff)
    pt_gqa = project(R_GQ, QK_ROWS)
    na(pt_na)
    values()
    gqa(pt_gqa)


def _inproj(l, xs, mods, gmix, wqk, wv, wuq, wukn, wuv, gcol, cs32, cs64):
    n_b = xs.shape[0]

    def full(a):
        return _layer_spec(a, l)

    rows_per_step = IN_SUB * RB

    def tok(width):
        return pl.BlockSpec((1, rows_per_step, width), lambda r, b: (b, r, 0))

    def chan(rows):
        return pl.BlockSpec((1, rows, rows_per_step), lambda r, b: (b, 0, r))

    def tshape(width):
        return jax.ShapeDtypeStruct((n_b, T, width), BF16)

    def cshape(rows):
        return jax.ShapeDtypeStruct((n_b, rows, T), BF16)

    return pl.pallas_call(
        _inproj_kernel,
        grid=(NRB // IN_SUB, n_b),
        in_specs=[
            pl.BlockSpec((1, rows_per_step, D), lambda r, b: (b, r, 0)),
            pl.BlockSpec((None, 1, 1, 6 * D), lambda r, b: (l, b, 0, 0)),
            pl.BlockSpec((None, 1, 1, 6 * D), lambda r, b: (l, n_b, 0, 0)),
            full(gmix), full(wqk), full(wv), full(wuq), full(wukn), full(wuv), full(gcol),
            pl.BlockSpec((64, rows_per_step), lambda r, b: (0, r)),
            pl.BlockSpec((128, rows_per_step), lambda r, b: (0, r)),
        ],
        out_specs=[tok(512), chan(512), tok(256), tok(256), chan(256), tok(256),
                   tok(256), chan(256), tok(256), tok(256), chan(128), tok(128)],
        out_shape=[tshape(512), cshape(512), tshape(256), tshape(256), cshape(256), tshape(256),
                   tshape(256), cshape(256), tshape(256), tshape(256), cshape(128), tshape(128)],
        compiler_params=_params(("parallel", "parallel")),
        name="inproj_prep",
    )(xs, mods, mods, gmix, wqk, wv, wuq, wukn, wuv, gcol, cs32, cs64)


def _lane_id(shape):
    return lax.broadcasted_iota(jnp.int32, shape, 1)


KEY_CHUNK = 768
LAT_CHUNKS = tuple((lo, KEY_CHUNK) for lo in range(0, T, KEY_CHUNK))
CTX_CHUNKS = ((SEQ, CTX),)


def _k_slab(k_ref, idx):
    return lambda lo, n: k_ref[0, idx * 128:(idx + 1) * 128, pl.ds(lo, n)]


def _v_slab(v_ref, idx):
    return lambda lo, n: v_ref[0, pl.ds(lo, n), idx * 128:(idx + 1) * 128]


def _attend(units, chunks, shift=None):
    def scores(i, c):
        q, keys, _, bias = units[i]
        lo, n = chunks[c]
        s = _dot(q, keys(lo, n))
        b = None if bias is None else bias(c)
        return s if b is None else s + b

    if shift is not None:
        items = [(i, c) for i in range(len(units)) for c in range(len(chunks))]
        acc, l = [None] * len(units), [None] * len(units)
        nxt = scores(*items[0])
        for idx, (i, c) in enumerate(items):
            cur = nxt
            if idx + 1 < len(items):
                nxt = scores(*items[idx + 1])
            e = jnp.exp2(cur - shift)
            lc = jnp.sum(e, axis=-1, keepdims=True)
            oc = _dot(e.astype(BF16), units[i][2](*chunks[c]))
            acc[i] = oc if acc[i] is None else acc[i] + oc
            l[i] = lc if l[i] is None else l[i] + lc
        return list(zip(acc, l))

    cur = [scores(0, c) for c in range(len(chunks))]
    out = []
    for i in range(len(units)):
        m = functools.reduce(jnp.maximum, [jnp.max(s, axis=-1, keepdims=True) for s in cur])
        nxt, acc, l = [], None, None
        for c, (lo, n) in enumerate(chunks):
            if i + 1 < len(units):
                nxt.append(scores(i + 1, c))
            e = jnp.exp2(cur[c] - m)
            lc = jnp.sum(e, axis=-1, keepdims=True)
            oc = _dot(e.astype(BF16), units[i][2](lo, n))
            acc = oc if acc is None else acc + oc
            l = lc if l is None else l + lc
        out.append((acc, l))
        cur = nxt
    return out


def _with_shift(bnd_ref, run):
    @pl.when(bnd_ref[1] > 0.5)
    def _():
        run(bnd_ref[0])

    @pl.when(bnd_ref[1] <= 0.5)
    def _():
        run(None)


def _query_blocks(n_rb, bnd_ref, run):
    def all_blocks(shift):
        def body(rb, carry):
            run(rb, pl.ds(pl.multiple_of(rb * RB, RB), RB), True, shift)
            return carry

        lax.fori_loop(0, NLAT, body, 0)
        if n_rb > NLAT:
            run(NLAT, pl.ds(SEQ, CTX), False, shift)

    _with_shift(bnd_ref, all_blocks)


def _mla_kernel(q_ref, k_ref, v_ref, bnd_ref, o_ref, *, n_rb):
    lane = _lane_id((RB, 128))

    def run(rb, rows, latent, shift):
        units = [(q_ref[0, rows,hd * 128:(hd + 1) * 128], _k_slab(k_ref, hd), _v_slab(v_ref, hd // 2), None)
                 for hd in range(MLA_H)]
        o = [acc * (1.0 / l) for acc, l in _attend(units, LAT_CHUNKS if latent else CTX_CHUNKS, shift)]
        outs = [jnp.where(lane < 64, o[0], o[1]), jnp.where(lane < 64, o[2], o[3])]
        o_ref[0, rows, :] = jnp.concatenate(outs, axis=1).astype(o_ref.dtype)

    _query_blocks(n_rb, bnd_ref, run)


def _gqa_kernel(q_ref, k_ref, v_ref, bnd_ref, o_ref, *, n_rb):
    lane = _lane_id((RB, 128))

    def run(rb, rows, latent, shift):
        order = [(rep, grp) for rep in range(2) for grp in range(2)]
        units = []
        for rep, grp in order:
            qs = q_ref[0, rows,rep * 128:(rep + 1) * 128]
            qm = jnp.where((lane >= 64) == (grp == 1), qs, jnp.zeros_like(qs))
            units.append((qm, _k_slab(k_ref, 0), _v_slab(v_ref, 0), None))
        res = {}
        for (rep, grp), (acc, l) in zip(order, _attend(units, LAT_CHUNKS if latent else CTX_CHUNKS, shift)):
            o = acc * (1.0 / l)
            res[(grp, rep)] = o if grp == rep else pltpu.roll(o, 64, axis=1)
        outs = [jnp.where(lane < 64, res[(grp, 0)], res[(grp, 1)]) for grp in range(2)]
        o_ref[0, rows, :] = jnp.concatenate(outs, axis=1).astype(o_ref.dtype)

    _query_blocks(n_rb, bnd_ref, run)


def _diff_kernel(q_ref, k_ref, v_ref, bnd_ref, lam_ref, g_ref, o_ref, *, n_rb, lambda_init):
    lane = _lane_id((RB, 128))
    lq1, lk1, lq2, lk2 = lam_ref[0:1, :], lam_ref[1:2, :], lam_ref[2:3, :], lam_ref[3:4, :]
    lam = (jnp.exp(jnp.sum(lq1 * lk1, axis=-1, keepdims=True))
           - jnp.exp(jnp.sum(lq2 * lk2, axis=-1, keepdims=True)) + lambda_init)
    gsub = g_ref[...]
    grp = lane // DIFF_D

    def run(rb, rows, latent, shift):
        units = []
        for hd in range(DIFF_H):
            pair, sub = divmod(hd, 2)
            qs = q_ref[0, rows,pair * 128:(pair + 1) * 128]
            for which in range(2):
                qm = jnp.where(grp == 2 * sub + which, qs, jnp.zeros_like(qs))
                units.append((qm, _k_slab(k_ref, pair), _v_slab(v_ref, pair), None))
        res = _attend(units, LAT_CHUNKS if latent else CTX_CHUNKS, shift)
        outs = []
        for pair in range(2):
            halves = []
            for sub in range(2):
                (a1, l1), (a2, l2) = res[2 * (2 * pair + sub)], res[2 * (2 * pair + sub) + 1]
                halves.append(a1 * (1.0 / l1) - a2 * (lam / l2))
            o = jnp.where(lane < 64, halves[0], halves[1])
            o2 = o * o
            ss0 = jnp.sum(jnp.where(lane < 64, o2, 0.0), axis=-1, keepdims=True)
            ss1 = jnp.sum(jnp.where(lane < 64, 0.0, o2), axis=-1, keepdims=True)
            ss = jnp.where(lane < 64, ss0, ss1) * (1.0 / (2 * DIFF_D))
            outs.append(o * lax.rsqrt(ss + EPS) * gsub * (1.0 - lambda_init))
        o_ref[0, rows, :] = jnp.concatenate(outs, axis=1).astype(o_ref.dtype)

    _query_blocks(n_rb, bnd_ref, run)


NA_WIN = 768
NA_CHUNK = 768
NA_TILES = 49
GRID_ROWS = SEQ // GRID_W


def _na_kernel(q_ref, k_ref, v_ref, bnd_ref, bias_ref, o_ref, *, n_rb):
    lane = _lane_id((RB, 128))

    def run(rb, rows, latent, shift):
        na_chunks, tile_idx = CTX_CHUNKS, None
        if latent:
            win_row = jnp.clip(4 * rb - 4, 0, GRID_ROWS - NA_WIN // GRID_W)
            off = pl.multiple_of(win_row * GRID_W, RB)
            local = [(pl.multiple_of(off + c * NA_CHUNK, RB), NA_CHUNK) for c in range(NA_WIN // NA_CHUNK)]
            na_chunks = local + list(CTX_CHUNKS)
            tile_idx = []
            for c in range(NA_WIN // NA_CHUNK):
                per_row = []
                for ri in range(RB // GRID_W):
                    r = 4 * rb + ri
                    row_start = jnp.clip(r - NA_ROWS // 2, 0, GRID_ROWS - NA_ROWS)
                    idxs = []
                    for p in range(NA_CHUNK // 128):
                        kr = win_row + c * (NA_CHUNK // GRID_W) + 2 * p
                        ok1 = (kr >= row_start) & (kr < row_start + NA_ROWS)
                        ok2 = (kr + 1 >= row_start) & (kr + 1 < row_start + NA_ROWS)
                        e = jnp.clip(kr - r + NA_ROWS, 0, 15)
                        idxs.append(jnp.where(ok1 & ok2, e, jnp.where(ok1, 16 + e, jnp.where(ok2, 32 + e, 48))))
                    per_row.append(idxs)
                tile_idx.append(per_row)
        units = []
        for hd in range(NA_H):
            pair, sub = divmod(hd, 2)
            qs = q_ref[0, rows,pair * 128:(pair + 1) * 128]
            qm = jnp.where((lane >= 64) == (sub == 1), qs, jnp.zeros_like(qs))

            def bias(c, hd=hd):
                if tile_idx is None or c >= NA_WIN // NA_CHUNK:
                    return None
                return jnp.concatenate(
                    [jnp.concatenate([bias_ref[hd, idx] for idx in row], axis=1) for row in tile_idx[c]], axis=0)

            units.append((qm, _k_slab(k_ref, pair), _v_slab(v_ref, pair), bias))
        o = [acc * (1.0 / l) for acc, l in _attend(units, na_chunks, shift)]
        outs = [jnp.where(lane < 64, o[0], o[1]), jnp.where(lane < 64, o[2], o[3])]
        o_ref[0, rows, :] = jnp.concatenate(outs, axis=1).astype(o_ref.dtype)

    _query_blocks(n_rb, bnd_ref, run)


SHIFT_MAX = 40.0


def _logit_bound(d, q_gain, k_gain, bias_max=0.0):
    bound = (1.01 * math.sqrt(d) * LOG2E * jnp.max(jnp.abs(q_gain), axis=-1) * jnp.max(jnp.abs(k_gain), axis=-1)
             + bias_max)
    return jnp.stack([bound, (bound <= SHIFT_MAX).astype(F32)], axis=-1).astype(F32)


def _layer_spec(a, l):
    return pl.BlockSpec((None,) + a.shape[1:], lambda *_, _n=a.ndim - 1: (l,) + (0,) * _n)


def _attention(body, name, q, k, v, bound, extra, n_rb, out_width=256):
    n_b = q.shape[0]
    extra = [(bound, pl.BlockSpec(memory_space=pltpu.SMEM))] + list(extra)
    in_specs = [
        pl.BlockSpec((1, T, q.shape[2]), lambda b: (b, 0, 0)),
        pl.BlockSpec((1, k.shape[1], T), lambda b: (b, 0, 0)),
        pl.BlockSpec((1, T, v.shape[2]), lambda b: (b, 0, 0)),
    ] + [spec for _, spec in extra]
    return pl.pallas_call(
        functools.partial(body, n_rb=n_rb),
        grid=(n_b,),
        in_specs=in_specs,
        out_specs=pl.BlockSpec((1, n_rb * RB, out_width), lambda b: (b, 0, 0)),
        out_shape=jax.ShapeDtypeStruct((n_b, n_rb * RB, out_width), BF16),
        compiler_params=_params(("parallel",)),
        name=name,
    )(q, k, v, *[a for a, _ in extra])


def _outproj_kernel(x_ref, modb_ref, modc_ref, ma_ref, mb_ref, mc_ref, md_ref, w_ref, o_ref, *, tm):
    rows = pl.program_id(1) * tm + lax.broadcasted_iota(jnp.int32, (tm, 1), 0)
    g1 = jnp.where(rows >= SEQ, modc_ref[0][:, 2 * D:3 * D], modb_ref[0][:, 2 * D:3 * D])
    acc = _dot(ma_ref[0], w_ref[0:256, :])
    acc += _dot(mb_ref[0], w_ref[256:512, :])
    acc += _dot(mc_ref[0], w_ref[512:768, :])
    acc += _dot(md_ref[0], w_ref[768:1024, :])
    o_ref[0] = x_ref[0] + g1 * acc


def _outproj(l, xs, mods, mixes, w_out, rows_per_sample, tm):
    n_b = xs.shape[0]
    mix_spec = pl.BlockSpec((1, tm, 256), lambda b, r: (b, r, 0))
    return pl.pallas_call(
        functools.partial(_outproj_kernel, tm=tm),
        grid=(n_b, rows_per_sample // tm),
        in_specs=[
            pl.BlockSpec((1, tm, D), lambda b, r: (b, r, 0)),
            pl.BlockSpec((None, 1, 1, 6 * D), lambda b, r: (l, b, 0, 0)),
            pl.BlockSpec((None, 1, 1, 6 * D), lambda b, r: (l, n_b, 0, 0)),
            mix_spec, mix_spec, mix_spec, mix_spec,
            _layer_spec(w_out, l),
        ],
        out_specs=pl.BlockSpec((1, tm, D), lambda b, r: (b, r, 0)),
        out_shape=jax.ShapeDtypeStruct((n_b, rows_per_sample, D), F32),
        compiler_params=_params(("parallel", "parallel")),
        name="outproj",
    )(xs, mods, mods, *mixes, w_out)


def _ffn_kernel(x_ref, hb_ref, ha_ref, modb_ref, modc_ref, g_ref, wa_ref, wg_ref, cwa_ref, cwg_ref, wd_ref, o_ref,
                h_scr, u0_scr, u1_scr, act_scr, *, tm, rows_per_sample):
    rb = pl.program_id(1)
    j = pl.program_id(2)
    row0 = rb * tm
    split = SEQ % tm if rows_per_sample > SEQ else 0
    band_lo, band_hi = split - HALO, split + HALO

    def per_row(rows, col):
        lat = modb_ref[0][:, col * D:(col + 1) * D]
        if rows_per_sample == SEQ:
            return lat
        return jnp.where(rows >= SEQ, modc_ref[0][:, col * D:(col + 1) * D], lat)

    def modulated(x, rows):
        ms = jnp.mean(x * x, axis=-1, keepdims=True)
        y = x * lax.rsqrt(ms + EPS) * g_ref[...]
        return (y * (1.0 + per_row(rows, 4)) + per_row(rows, 3)).astype(BF16)

    def prologue():
        halo_rows = lax.broadcasted_iota(jnp.int32, (HALO, 1), 0)
        has_before = (row0 != 0) & (row0 != SEQ)
        has_after = (row0 + tm != SEQ) & (row0 + tm != rows_per_sample)
        zero = jnp.zeros((HALO, D), BF16)
        h_scr[0:HALO, :] = jnp.where(has_before, modulated(hb_ref[0], row0 - HALO + halo_rows), zero)
        h_scr[HALO + tm:, :] = jnp.where(has_after, modulated(ha_ref[0], row0 + tm + halo_rows), zero)
        main_rows = row0 + lax.broadcasted_iota(jnp.int32, (tm, 1), 0)
        h_scr[HALO:HALO + tm, :] = modulated(x_ref[0], main_rows)
        o_ref[0] = jnp.zeros((tm, D), F32)

    slots = (u0_scr, u1_scr)

    def tiles(total):
        units = total // HALO
        sizes = [(units // FFN_TILES + (1 if t < units % FFN_TILES else 0)) * HALO for t in range(FFN_TILES)]
        edges = np.cumsum([0] + sizes)
        return [(int(edges[t]), int(edges[t + 1])) for t in range(FFN_TILES)]

    up_tiles, down_tiles = tiles(tm + 2 * HALO), tiles(tm)

    def up(c, t):
        lo, hi = up_tiles[t]
        w = FF_WIDTHS[c]
        slots[c % 2][lo:hi, :w] = _dot(h_scr[lo:hi, :], wa_ref[:, :w])
        slots[c % 2][lo:hi, FF_CHUNK:FF_CHUNK + w] = _dot(h_scr[lo:hi, :], wg_ref[:, FF_CHUNK - w:])

    def conv(u_scr, cw_ref, lo, hi, u_lo, c_lo, width, masked):
        prev = u_scr[HALO - 1 + lo:HALO - 1 + hi, u_lo:u_lo + width]
        nxt = u_scr[HALO + 1 + lo:HALO + 1 + hi, u_lo:u_lo + width]
        if masked:
            rows = row0 + lo + lax.broadcasted_iota(jnp.int32, (hi - lo, 1), 0)
            prev = jnp.where(rows != SEQ, prev, 0.0)
            nxt = jnp.where(rows != SEQ - 1, nxt, 0.0)
        cols = slice(c_lo, c_lo + width)
        return (prev * cw_ref[0:1, cols] + u_scr[HALO + lo:HALO + hi, u_lo:u_lo + width] * cw_ref[1:2, cols]
                + nxt * cw_ref[2:3, cols] + cw_ref[3:4, cols])

    def conv_down(c, t):
        u_scr, w = slots[c % 2], FF_WIDTHS[c]
        t_lo, t_hi = down_tiles[t]
        cuts = sorted({t_lo, t_hi} | ({e for e in (band_lo, band_hi) if t_lo < e < t_hi} if split else set()))
        for lo, hi in zip(cuts[:-1], cuts[1:]):
            masked = bool(split) and band_lo <= lo < band_hi
            for c_lo in range(0, w, CONV_STRIP):
                a = conv(u_scr, cwa_ref, lo, hi, c_lo, c_lo, CONV_STRIP, masked)
                g = conv(u_scr, cwg_ref, lo, hi, FF_CHUNK + c_lo, FF_CHUNK - w + c_lo, CONV_STRIP, masked)
                act_scr[lo:hi, c_lo:c_lo + CONV_STRIP] = (g * jax.nn.sigmoid(g) * a).astype(BF16)
        o_ref[0, t_lo:t_hi, :] += _dot(act_scr[t_lo:t_hi, :w], wd_ref[FF_CHUNK - w:, :])

    for step in range(N_CHUNK + 1):
        @pl.when(j == step)
        def _(step=step):
            if step == 0:
                prologue()
            for t in range(FFN_TILES):
                if step < N_CHUNK:
                    up(step, t)
                if step > 0:
                    conv_down(step - 1, t)
            if step == N_CHUNK:
                rows = row0 + lax.broadcasted_iota(jnp.int32, (tm, 1), 0)
                o_ref[0] = x_ref[0] + per_row(rows, 5) * o_ref[0]


def _ffn(l, x1, mods, g_ffn, w_up, conv_wb, w_down, rows_per_sample, tm):
    n_b = x1.shape[0]
    n_rb = rows_per_sample // tm
    halo_per_block = tm // HALO
    n_halo = x1.shape[1] // HALO
    body = functools.partial(_ffn_kernel, tm=tm, rows_per_sample=rows_per_sample)

    def up_idx(j):
        return jnp.minimum(j, N_CHUNK - 1)

    def down_idx(j):
        return jnp.maximum(j - 1, 0)

    def tail_clamped(c):
        return pl.multiple_of(jnp.minimum(c * FF_CHUNK, D_FF - FF_CHUNK), 256)

    return pl.pallas_call(
        body,
        grid=(n_b, n_rb, N_CHUNK + 1),
        in_specs=[
            pl.BlockSpec((1, tm, D), lambda b, r, j: (b, r, 0)),
            pl.BlockSpec((1, HALO, D), lambda b, r, j: (b, jnp.maximum(r * halo_per_block - 1, 0), 0)),
            pl.BlockSpec((1, HALO, D), lambda b, r, j: (b, jnp.minimum((r + 1) * halo_per_block, n_halo - 1), 0)),
            pl.BlockSpec((None, 1, 1, 6 * D), lambda b, r, j: (l, b, 0, 0)),
            pl.BlockSpec((None, 1, 1, 6 * D), lambda b, r, j: (l, n_b, 0, 0)),
            _layer_spec(g_ffn, l),
            pl.BlockSpec((None, D, FF_CHUNK), lambda b, r, j: (l, 0, up_idx(j))),
            pl.BlockSpec((None, pl.Element(D), pl.Element(FF_CHUNK)),
                         lambda b, r, j: (l, 0, pl.multiple_of(D_FF + tail_clamped(up_idx(j)), 256))),
            pl.BlockSpec((None, 4, FF_CHUNK), lambda b, r, j: (l, 0, down_idx(j))),
            pl.BlockSpec((None, pl.Element(4), pl.Element(FF_CHUNK)),
                         lambda b, r, j: (l, 0, pl.multiple_of(D_FF + tail_clamped(down_idx(j)), 256))),
            pl.BlockSpec((None, pl.Element(FF_CHUNK), pl.Element(D)),
                         lambda b, r, j: (l, tail_clamped(down_idx(j)), 0)),
        ],
        out_specs=pl.BlockSpec((1, tm, D), lambda b, r, j: (b, r, 0)),
        out_shape=jax.ShapeDtypeStruct((n_b, rows_per_sample, D), F32),
        scratch_shapes=[
            pltpu.VMEM((tm + 2 * HALO, D), BF16),
            pltpu.VMEM((tm + 2 * HALO, 2 * FF_CHUNK), F32),
            pltpu.VMEM((tm + 2 * HALO, 2 * FF_CHUNK), F32),
            pltpu.VMEM((tm, FF_CHUNK), BF16),
        ],
        compiler_params=_params(("parallel", "parallel", "arbitrary")),
        name="conv_ffn",
    )(x1, x1, x1, mods, mods, g_ffn, w_up, w_up, conv_wb, conv_wb, w_down)


def _rope_table(rot_dim):
    n_freq = rot_dim // 4
    inv = jnp.power(ROPE_THETA, -jnp.arange(n_freq, dtype=F32) / n_freq)
    t = jnp.arange(SEQ)
    row = (t // GRID_W).astype(F32)
    col = (t % GRID_W).astype(F32)
    ar, ac = row[:, None] * inv, col[:, None] * inv
    ang = jnp.concatenate([ar, ar, ac, ac], axis=-1)
    sign = jnp.concatenate([-jnp.ones(n_freq), jnp.ones(n_freq), -jnp.ones(n_freq), jnp.ones(n_freq)]).astype(F32)
    cos = jnp.concatenate([jnp.cos(ang), jnp.ones((CTX, rot_dim), F32)], axis=0)
    sin = jnp.concatenate([jnp.sin(ang) * sign, jnp.zeros((CTX, rot_dim), F32)], axis=0)
    return jnp.concatenate([cos.T, sin.T], axis=0)


def _na_bias_tiles(rpb):
    n_l, n_h = rpb.shape[:2]
    n_off = 2 * NA_ROWS - 1
    p = jnp.pad(rpb.astype(F32) * LOG2E, ((0, 0), (0, 0), (0, 0), (48, 49)))
    sk = jnp.broadcast_to(p[..., None, :], (n_l, n_h, n_off, GRID_W, 128)).reshape(n_l, n_h, n_off, GRID_W * 128)
    sk = sk[..., :GRID_W * 127].reshape(n_l, n_h, n_off, GRID_W, 127)[..., 63:127]
    c = np.arange(GRID_W)
    win_start = np.clip(c - NA_COLS // 2, 0, GRID_W - NA_COLS)
    v_col = (c[None, :] >= win_start[:, None]) & (c[None, :] < win_start[:, None] + NA_COLS)
    tz = jnp.where(v_col, sk, NEG_INF)
    neg1 = jnp.full((n_l, n_h, 1, GRID_W, GRID_W), NEG_INF, F32)
    tzx = jnp.concatenate([neg1, tz, neg1], axis=2)
    first, second = tzx[:, :, 0:16], tzx[:, :, 1:17]
    neg16 = jnp.full_like(first, NEG_INF)
    return jnp.concatenate([
        jnp.concatenate([first, second], axis=-1), jnp.concatenate([first, neg16], axis=-1),
        jnp.concatenate([neg16, second], axis=-1), jnp.concatenate([neg1, neg1], axis=-1)], axis=2)


def kernel(x, c, ctx, c_ctx, w_mod, b_mod, g_mix, w_in, w_out, mla_q_a_g, mla_w_uq, mla_kv_a_g, mla_w_ukv,
           mla_q_g, mla_k_g, diff_q_g, diff_k_g, diff_lq1, diff_lk1, diff_lq2, diff_lk2, diff_subln_g,
           na_q_g, na_k_g, na_rpb, gqa_q_g, gqa_k_g, g_ffn, w_up, conv_w, conv_b, w_down):
    n_b = x.shape[0]
    n_layer = w_mod.shape[0]
    assert x.shape[1:] == (SEQ, D) and ctx.shape[1:] == (CTX, D)

    xs = jnp.concatenate([x, ctx], axis=1)
    mod_rows = -(-(n_b + 1) // 8) * 8
    cc = jnp.concatenate([c, c_ctx[None], jnp.zeros((mod_rows - n_b - 1, D), F32)], axis=0)
    mods_all = _modulation(cc, w_mod, b_mod).reshape(n_layer, mod_rows, 1, 6 * D)

    cs32, cs64 = _rope_table(32), _rope_table(64)
    bias_all = _na_bias_tiles(na_rpb)

    s = np.cumsum([0, 256, 128, 32, 256, 256, 256, 256, 256, 256, 256, 128, 128])
    gq_perm = np.concatenate([np.arange(GQA_D) + (2 * g + r) * GQA_D for r in range(2) for g in range(2)])

    n_l = n_layer
    gq_cols = w_in[:, :, s[9]:s[10]][:, :, gq_perm]
    wqk = jnp.concatenate([w_in[:, :, s[0]:s[3]], w_in[:, :, s[3]:s[5]], w_in[:, :, s[6]:s[8]], gq_cols,
                           w_in[:, :, s[10]:s[11]]], axis=2).swapaxes(1, 2).astype(BF16)
    wv = jnp.concatenate([w_in[:, :, s[5]:s[6]], w_in[:, :, s[8]:s[9]], w_in[:, :, s[11]:s[12]]],
                         axis=2).astype(BF16)
    wuq = mla_w_uq.swapaxes(1, 2).astype(BF16)
    wukv = mla_w_ukv.reshape(n_l, MLA_KVR, MLA_H, MLA_NOPE + MLA_V)
    wukn = wukv[..., :MLA_NOPE].reshape(n_l, MLA_KVR, MLA_H * MLA_NOPE).swapaxes(1, 2).astype(BF16)
    wuv = wukv[..., MLA_NOPE:].reshape(n_l, MLA_KVR, MLA_H * MLA_V).swapaxes(1, 2).astype(BF16)
    gcol = jnp.concatenate([mla_q_a_g, mla_kv_a_g, mla_q_g, mla_k_g, diff_q_g, diff_k_g, na_q_g, na_k_g,
                            gqa_q_g, gqa_k_g], axis=1).astype(F32)[..., None]
    lam_rows = jnp.stack([diff_lq1, diff_lk1, diff_lq2, diff_lk2], axis=1).astype(F32)
    lam_rows = jnp.concatenate([lam_rows, jnp.zeros_like(lam_rows)], axis=1)
    gsub = jnp.tile(diff_subln_g.astype(F32), (1, 2)).reshape(n_l, 1, 128)
    bounds = {
        "mla": _logit_bound(MLA_D, mla_q_g, mla_k_g),
        "diff": _logit_bound(DIFF_D, diff_q_g, diff_k_g),
        "na": _logit_bound(NA_D, na_q_g, na_k_g, LOG2E * jnp.maximum(jnp.max(na_rpb, axis=(1, 2, 3)), 0.0)),
        "gqa": _logit_bound(GQA_D, gqa_q_g, gqa_k_g),
    }
    w_out_b, w_down_b = w_out.astype(BF16), w_down.astype(BF16)
    w_up_b = w_up.astype(BF16)
    conv_wb = jnp.concatenate([conv_w, conv_b[:, None, :]], axis=1).astype(F32)
    gmix, gffn = g_mix.reshape(n_l, 1, D), g_ffn.reshape(n_l, 1, D)

    for l in range(n_layer):
        with_ctx = l < n_layer - 1
        lambda_init = 0.8 - 0.6 * math.exp(-0.3 * l)
        (q_mla, k_mla, v_mla, q_diff, k_diff, v_diff, q_na, k_na, v_na, q_gqa, k_gqa, v_gqa) = _inproj(
            l, xs, mods_all, gmix, wqk, wv, wuq, wukn, wuv, gcol, cs32, cs64)

        n_rb = NRB if with_ctx else NLAT
        mix_a = _attention(_mla_kernel, "attn_mla", q_mla, k_mla, v_mla, bounds["mla"][l], [], n_rb)
        mix_b = _attention(
            functools.partial(_diff_kernel, lambda_init=lambda_init), "attn_diff",
            q_diff, k_diff, v_diff, bounds["diff"][l],
            [(lam_rows, _layer_spec(lam_rows, l)), (gsub, _layer_spec(gsub, l))], n_rb)
        mix_c = _attention(_na_kernel, "attn_na", q_na, k_na, v_na, bounds["na"][l],
                           [(bias_all, _layer_spec(bias_all, l))], n_rb)
        mix_d = _attention(_gqa_kernel, "attn_gqa", q_gqa, k_gqa, v_gqa, bounds["gqa"][l], [], n_rb)

        rows_per_sample = T if with_ctx else SEQ
        x1 = _outproj(l, xs, mods_all, (mix_a, mix_b, mix_c, mix_d), w_out_b, rows_per_sample,
                      OUT_TM if with_ctx else OUT_TM_LAST)
        xs = _ffn(l, x1, mods_all, gffn, w_up_b, conv_wb, w_down_b, rows_per_sample, rows_per_sample // 2)
    return xs
```

```python
import functools
import math

import numpy as np
import jax
import jax.numpy as jnp
from jax import lax
from jax.experimental import pallas as pl
from jax.experimental.pallas import tpu as pltpu

F32 = jnp.float32
BF16 = jnp.bfloat16

D = 1024
SEQ = 2048
GRID_W = 64
CTX = 256
T = SEQ + CTX
RB = 256
NRB = T // RB
NLAT = SEQ // RB
IN_SUB = 3
EPS = 1e-6
NEG_INF = -1e30
ROPE_THETA = 10000.0
LOG2E = 1.4426950408889634

MLA_H, MLA_NOPE, MLA_ROPE, MLA_V, MLA_QR, MLA_KVR = 4, 64, 32, 64, 256, 128
MLA_D = MLA_NOPE + MLA_ROPE
DIFF_H, DIFF_D = 4, 32
NA_H, NA_D, NA_ROWS, NA_COLS = 4, 64, 8, 16
GQA_H, GQA_KV, GQA_D = 4, 2, 64
D_FF = 2816
FF_CHUNK = 1024
N_CHUNK = -(-D_FF // FF_CHUNK)
FF_WIDTHS = tuple(min(FF_CHUNK, D_FF - c * FF_CHUNK) for c in range(N_CHUNK))
OUT_TM, OUT_TM_LAST = 1152, 1024
HALO = 16
FFN_TILES = 4
CONV_STRIP = 256

R_CQ, R_CKV, R_KR, R_DQ, R_DK, R_NQ, R_NK, R_GQ, R_GK = 0, 256, 384, 416, 672, 928, 1184, 1440, 1696
QK_ROWS = 1824
V_COLS = 640

G_QA, G_KVA, G_MQ, G_MK, G_DQ, G_DK, G_NQ, G_NK, G_GQ, G_GK = 0, 256, 384, 480, 576, 608, 640, 704, 768, 832
G_ROWS = 896

VMEM_LIMIT = 56 * 1024 * 1024


def _params(sem):
    return pltpu.CompilerParams(dimension_semantics=sem, vmem_limit_bytes=VMEM_LIMIT)


def _dot(a, b):
    return jnp.dot(a, b, preferred_element_type=F32)


def _dot_nt(a, b):
    return lax.dot_general(a, b, (((1,), (1,)), ((), ())), preferred_element_type=F32)


def _mod_kernel(c_ref, w_ref, b_ref, o_ref):
    c = c_ref[...]
    a = (c * jax.nn.sigmoid(c)).astype(BF16)
    o_ref[0] = _dot(a, w_ref[0].astype(BF16)) + b_ref[0]


def _modulation(cc, w_mod, b_mod):
    n_layer = w_mod.shape[0]
    rows = cc.shape[0]
    return pl.pallas_call(
        _mod_kernel,
        grid=(n_layer, 6),
        in_specs=[
            pl.BlockSpec((rows, D), lambda l, j: (0, 0)),
            pl.BlockSpec((1, D, D), lambda l, j: (l, 0, j)),
            pl.BlockSpec((1, 1, D), lambda l, j: (l, 0, j)),
        ],
        out_specs=pl.BlockSpec((1, rows, D), lambda l, j: (l, 0, j)),
        out_shape=jax.ShapeDtypeStruct((n_layer, rows, 6 * D), F32),
        compiler_params=_params(("parallel", "parallel")),
        name="modulation",
    )(cc, w_mod, b_mod.reshape(n_layer, 1, 6 * D))


def _rms_rows(x, g, n):
    ss = jnp.sum(x * x, axis=0, keepdims=True) * (1.0 / n)
    return x * lax.rsqrt(ss + EPS) * g


def _rope_rows(x, cos, sin_signed, w):
    rot = jnp.concatenate([x[w:2 * w], x[0:w], x[3 * w:4 * w], x[2 * w:3 * w]], axis=0)
    return x * cos + rot * sin_signed


def _inproj_kernel(x_ref, modb_ref, modc_ref, gmix_ref, wqk_ref, wv_ref, wuq_ref, wukn_ref, wuv_ref, gcol_ref,
                   cs32_ref, cs64_ref,
                   qmla_ref, kmla_ref, vmla_ref, qdiff_ref, kdiff_ref, vdiff_ref,
                   qna_ref, kna_ref, vna_ref, qgqa_ref, kgqa_ref, vgqa_ref):
    step = pl.program_id(0)

    def modulated(sb):
        r0 = sb * RB
        x = x_ref[0, r0:r0 + RB, :]
        mod = modb_ref[0]
        if sb == IN_SUB - 1:
            mod = jnp.where(step == NRB // IN_SUB - 1, modc_ref[0], mod)
        shift, scale = mod[:, 0:D], mod[:, D:2 * D]
        ms = jnp.mean(x * x, axis=-1, keepdims=True)
        h = x * lax.rsqrt(ms + EPS) * gmix_ref[...]
        return (h * (1.0 + scale) + shift).astype(BF16)

    def gain(off, n):
        return gcol_ref[off:off + n, :]

    def block(sb, hb, after_first_projection):
        rows = slice(sb * RB, (sb + 1) * RB)
        _inproj_block(rows, hb, after_first_projection, gain, wqk_ref, wv_ref, wuq_ref, wukn_ref, wuv_ref,
                      cs32_ref, cs64_ref, qmla_ref, kmla_ref, vmla_ref, qdiff_ref, kdiff_ref, vdiff_ref,
                      qna_ref, kna_ref, vna_ref, qgqa_ref, kgqa_ref, vgqa_ref)

    hbs = [modulated(0)]
    for sb in range(IN_SUB):
        nxt = (lambda sb=sb: hbs.append(modulated(sb + 1))) if sb + 1 < IN_SUB else (lambda: None)
        block(sb, hbs[sb], nxt)


def _inproj_block(rows, hb, after_first_projection, gain, wqk_ref, wv_ref, wuq_ref, wukn_ref, wuv_ref,
                  cs32_ref, cs64_ref, qmla_ref, kmla_ref, vmla_ref, qdiff_ref, kdiff_ref, vdiff_ref,
                  qna_ref, kna_ref, vna_ref, qgqa_ref, kgqa_ref, vgqa_ref):
    def project(lo, hi):
        return _dot_nt(wqk_ref[lo:hi, :], hb)

    cos32, sin32 = cs32_ref[0:32, rows], cs32_ref[32:64, rows]
    cos64, sin64 = cs64_ref[0:64, rows], cs64_ref[64:128, rows]

    def values():
        pv = _dot(hb, wv_ref[...])
        vdiff_ref[0, rows, :] = pv[:, 0:256].astype(BF16)
        vna_ref[0, rows, :] = pv[:, 256:512].astype(BF16)
        vgqa_ref[0, rows, :] = pv[:, 512:640].astype(BF16)

    def mla(pt):
        cq = _rms_rows(pt[0:MLA_QR], gain(G_QA, MLA_QR), MLA_QR).astype(BF16)
        qt = _dot(wuq_ref[...], cq)
        ckv = _rms_rows(pt[R_CKV:R_CKV + MLA_KVR], gain(G_KVA, MLA_KVR), MLA_KVR).astype(BF16)
        knt = _dot(wukn_ref[...], ckv)
        vt = _dot(wuv_ref[...], ckv)
        vmla_ref[0, rows, :] = vt.T.astype(BF16)
        kr = pt[R_KR:R_KR + MLA_ROPE]
        kr_ss = jnp.sum(kr * kr, axis=0, keepdims=True)
        g_mq, g_mk = gain(G_MQ, MLA_D), gain(G_MK, MLA_D)
        zpad = jnp.zeros((128 - MLA_D, RB), F32)
        q_parts = []
        for hd in range(MLA_H):
            qh = _rms_rows(qt[hd * MLA_D:(hd + 1) * MLA_D], g_mq, MLA_D)
            q_rope = _rope_rows(qh[MLA_NOPE:], cos32, sin32, MLA_ROPE // 4)
            q_parts += [qh[:MLA_NOPE], q_rope, zpad]
            kn = knt[hd * MLA_NOPE:(hd + 1) * MLA_NOPE]
            ss = (jnp.sum(kn * kn, axis=0, keepdims=True) + kr_ss) * (1.0 / MLA_D)
            r = lax.rsqrt(ss + EPS)
            k_rope = _rope_rows(kr * r * g_mk[MLA_NOPE:], cos32, sin32, MLA_ROPE // 4)
            kmla_ref[0, hd * 128:(hd + 1) * 128, rows] = jnp.concatenate(
                [kn * r * g_mk[:MLA_NOPE], k_rope, zpad], axis=0).astype(BF16)
        q_all = jnp.concatenate(q_parts, axis=0) * (MLA_D ** -0.5 * LOG2E)
        qmla_ref[0, rows, :] = q_all.T.astype(BF16)

    def diff(pt):
        g_dq, g_dk = gain(G_DQ, DIFF_D), gain(G_DK, DIFF_D)
        q_parts, k_parts = [], []
        for gi in range(2 * DIFF_H):
            qg = _rms_rows(pt[gi * DIFF_D:(gi + 1) * DIFF_D], g_dq, DIFF_D)
            q_parts.append(_rope_rows(qg, cos32, sin32, DIFF_D // 4))
            kg = _rms_rows(pt[256 + gi * DIFF_D:256 + (gi + 1) * DIFF_D], g_dk, DIFF_D)
            k_parts.append(_rope_rows(kg, cos32, sin32, DIFF_D // 4))
        qdiff_ref[0, rows, :] = (jnp.concatenate(q_parts, axis=0) * (DIFF_D ** -0.5 * LOG2E)).T.astype(BF16)
        kdiff_ref[0, :, rows] = jnp.concatenate(k_parts, axis=0).astype(BF16)

    def na(pt):
        g_nq, g_nk = gain(G_NQ, NA_D), gain(G_NK, NA_D)
        q_parts, k_parts = [], []
        for hd in range(NA_H):
            q_parts.append(_rms_rows(pt[hd * NA_D:(hd + 1) * NA_D], g_nq, NA_D))
            k_parts.append(_rms_rows(pt[256 + hd * NA_D:256 + (hd + 1) * NA_D], g_nk, NA_D))
        qna_ref[0, rows, :] = (jnp.concatenate(q_parts, axis=0) * (NA_D ** -0.5 * LOG2E)).T.astype(BF16)
        kna_ref[0, :, rows] = jnp.concatenate(k_parts, axis=0).astype(BF16)

    def gqa(pt):
        g_gq, g_gk = gain(G_GQ, GQA_D), gain(G_GK, GQA_D)
        q_parts, k_parts = [], []
        for hd in range(GQA_H):
            qg = _rms_rows(pt[hd * GQA_D:(hd + 1) * GQA_D], g_gq, GQA_D)
            q_parts.append(_rope_rows(qg, cos64, sin64, GQA_D // 4))
        for hd in range(GQA_KV):
            kg = _rms_rows(pt[256 + hd * GQA_D:256 + (hd + 1) * GQA_D], g_gk, GQA_D)
            k_parts.append(_rope_rows(kg, cos64, sin64, GQA_D // 4))
        qgqa_ref[0, rows, :] = (jnp.concatenate(q_parts, axis=0) * (GQA_D ** -0.5 * LOG2E)).T.astype(BF16)
        kgqa_ref[0, :, rows] = jnp.concatenate(k_parts, axis=0).astype(BF16)

    pt_mla = project(R_CQ, R_DQ)
    after_first_projection()
    pt_diff = project(R_DQ, R_NQ)
    mla(pt_mla)
    pt_na = project(R_NQ, R_GQ)
    diff(pt_diff)
    pt_gqa = project(R_GQ, QK_ROWS)
    na(pt_na)
    values()
    gqa(pt_gqa)


def _inproj(l, xs, mods, gmix, wqk, wv, wuq, wukn, wuv, gcol, cs32, cs64):
    n_b = xs.shape[0]

    def full(a):
        return _layer_spec(a, l)

    rows_per_step = IN_SUB * RB

    def tok(width):
        return pl.BlockSpec((1, rows_per_step, width), lambda r, b: (b, r, 0))

    def chan(rows):
        return pl.BlockSpec((1, rows, rows_per_step), lambda r, b: (b, 0, r))

    def tshape(width):
        return jax.ShapeDtypeStruct((n_b, T, width), BF16)

    def cshape(rows):
        return jax.ShapeDtypeStruct((n_b, rows, T), BF16)

    return pl.pallas_call(
        _inproj_kernel,
        grid=(NRB // IN_SUB, n_b),
        in_specs=[
            pl.BlockSpec((1, rows_per_step, D), lambda r, b: (b, r, 0)),
            pl.BlockSpec((None, 1, 1, 6 * D), lambda r, b: (l, b, 0, 0)),
            pl.BlockSpec((None, 1, 1, 6 * D), lambda r, b: (l, n_b, 0, 0)),
            full(gmix), full(wqk), full(wv), full(wuq), full(wukn), full(wuv), full(gcol),
            pl.BlockSpec((64, rows_per_step), lambda r, b: (0, r)),
            pl.BlockSpec((128, rows_per_step), lambda r, b: (0, r)),
        ],
        out_specs=[tok(512), chan(512), tok(256), tok(256), chan(256), tok(256),
                   tok(256), chan(256), tok(256), tok(256), chan(128), tok(128)],
        out_shape=[tshape(512), cshape(512), tshape(256), tshape(256), cshape(256), tshape(256),
                   tshape(256), cshape(256), tshape(256), tshape(256), cshape(128), tshape(128)],
        compiler_params=_params(("parallel", "parallel")),
        name="inproj_prep",
    )(xs, mods, mods, gmix, wqk, wv, wuq, wukn, wuv, gcol, cs32, cs64)


def _lane_id(shape):
    return lax.broadcasted_iota(jnp.int32, shape, 1)


KEY_CHUNK = 768
LAT_CHUNKS = tuple((lo, KEY_CHUNK) for lo in range(0, T, KEY_CHUNK))
CTX_CHUNKS = ((SEQ, CTX),)


def _k_slab(k_ref, idx):
    return lambda lo, n: k_ref[0, idx * 128:(idx + 1) * 128, pl.ds(lo, n)]


def _v_slab(v_ref, idx):
    return lambda lo, n: v_ref[0, pl.ds(lo, n), idx * 128:(idx + 1) * 128]


def _attend(units, chunks, shift=None):
    def scores(i, c):
        q, keys, _, bias = units[i]
        lo, n = chunks[c]
        s = _dot(q, keys(lo, n))
        b = None if bias is None else bias(c)
        return s if b is None else s + b

    if shift is not None:
        items = [(i, c) for i in range(len(units)) for c in range(len(chunks))]
        acc, l = [None] * len(units), [None] * len(units)
        nxt = scores(*items[0])
        for idx, (i, c) in enumerate(items):
            cur = nxt
            if idx + 1 < len(items):
                nxt = scores(*items[idx + 1])
            e = jnp.exp2(cur - shift)
            lc = jnp.sum(e, axis=-1, keepdims=True)
            oc = _dot(e.astype(BF16), units[i][2](*chunks[c]))
            acc[i] = oc if acc[i] is None else acc[i] + oc
            l[i] = lc if l[i] is None else l[i] + lc
        return list(zip(acc, l))

    cur = [scores(0, c) for c in range(len(chunks))]
    out = []
    for i in range(len(units)):
        m = functools.reduce(jnp.maximum, [jnp.max(s, axis=-1, keepdims=True) for s in cur])
        nxt, acc, l = [], None, None
        for c, (lo, n) in enumerate(chunks):
            if i + 1 < len(units):
                nxt.append(scores(i + 1, c))
            e = jnp.exp2(cur[c] - m)
            lc = jnp.sum(e, axis=-1, keepdims=True)
            oc = _dot(e.astype(BF16), units[i][2](lo, n))
            acc = oc if acc is None else acc + oc
            l = lc if l is None else l + lc
        out.append((acc, l))
        cur = nxt
    return out


def _with_shift(bnd_ref, run):
    @pl.when(bnd_ref[1] > 0.5)
    def _():
        run(bnd_ref[0])

    @pl.when(bnd_ref[1] <= 0.5)
    def _():
        run(None)


def _query_blocks(n_rb, bnd_ref, run):
    def all_blocks(shift):
        def body(rb, carry):
            run(rb, pl.ds(pl.multiple_of(rb * RB, RB), RB), True, shift)
            return carry

        lax.fori_loop(0, NLAT, body, 0)
        if n_rb > NLAT:
            run(NLAT, pl.ds(SEQ, CTX), False, shift)

    _with_shift(bnd_ref, all_blocks)


def _mla_kernel(q_ref, k_ref, v_ref, bnd_ref, o_ref, *, n_rb):
    lane = _lane_id((RB, 128))

    def run(rb, rows, latent, shift):
        units = [(q_ref[0, rows,hd * 128:(hd + 1) * 128], _k_slab(k_ref, hd), _v_slab(v_ref, hd // 2), None)
                 for hd in range(MLA_H)]
        o = [acc * (1.0 / l) for acc, l in _attend(units, LAT_CHUNKS if latent else CTX_CHUNKS, shift)]
        outs = [jnp.where(lane < 64, o[0], o[1]), jnp.where(lane < 64, o[2], o[3])]
        o_ref[0, rows, :] = jnp.concatenate(outs, axis=1).astype(o_ref.dtype)

    _query_blocks(n_rb, bnd_ref, run)


def _gqa_kernel(q_ref, k_ref, v_ref, bnd_ref, o_ref, *, n_rb):
    lane = _lane_id((RB, 128))

    def run(rb, rows, latent, shift):
        order = [(rep, grp) for rep in range(2) for grp in range(2)]
        units = []
        for rep, grp in order:
            qs = q_ref[0, rows,rep * 128:(rep + 1) * 128]
            qm = jnp.where((lane >= 64) == (grp == 1), qs, jnp.zeros_like(qs))
            units.append((qm, _k_slab(k_ref, 0), _v_slab(v_ref, 0), None))
        res = {}
        for (rep, grp), (acc, l) in zip(order, _attend(units, LAT_CHUNKS if latent else CTX_CHUNKS, shift)):
            o = acc * (1.0 / l)
            res[(grp, rep)] = o if grp == rep else pltpu.roll(o, 64, axis=1)
        outs = [jnp.where(lane < 64, res[(grp, 0)], res[(grp, 1)]) for grp in range(2)]
        o_ref[0, rows, :] = jnp.concatenate(outs, axis=1).astype(o_ref.dtype)

    _query_blocks(n_rb, bnd_ref, run)


def _diff_kernel(q_ref, k_ref, v_ref, bnd_ref, lam_ref, g_ref, o_ref, *, n_rb, lambda_init):
    lane = _lane_id((RB, 128))
    lq1, lk1, lq2, lk2 = lam_ref[0:1, :], lam_ref[1:2, :], lam_ref[2:3, :], lam_ref[3:4, :]
    lam = (jnp.exp(jnp.sum(lq1 * lk1, axis=-1, keepdims=True))
           - jnp.exp(jnp.sum(lq2 * lk2, axis=-1, keepdims=True)) + lambda_init)
    gsub = g_ref[...]
    grp = lane // DIFF_D

    def run(rb, rows, latent, shift):
        units = []
        for hd in range(DIFF_H):
            pair, sub = divmod(hd, 2)
            qs = q_ref[0, rows,pair * 128:(pair + 1) * 128]
            for which in range(2):
                qm = jnp.where(grp == 2 * sub + which, qs, jnp.zeros_like(qs))
                units.append((qm, _k_slab(k_ref, pair), _v_slab(v_ref, pair), None))
        res = _attend(units, LAT_CHUNKS if latent else CTX_CHUNKS, shift)
        outs = []
        for pair in range(2):
            halves = []
            for sub in range(2):
                (a1, l1), (a2, l2) = res[2 * (2 * pair + sub)], res[2 * (2 * pair + sub) + 1]
                halves.append(a1 * (1.0 / l1) - a2 * (lam / l2))
            o = jnp.where(lane < 64, halves[0], halves[1])
            o2 = o * o
            ss0 = jnp.sum(jnp.where(lane < 64, o2, 0.0), axis=-1, keepdims=True)
            ss1 = jnp.sum(jnp.where(lane < 64, 0.0, o2), axis=-1, keepdims=True)
            ss = jnp.where(lane < 64, ss0, ss1) * (1.0 / (2 * DIFF_D))
            outs.append(o * lax.rsqrt(ss + EPS) * gsub * (1.0 - lambda_init))
        o_ref[0, rows, :] = jnp.concatenate(outs, axis=1).astype(o_ref.dtype)

    _query_blocks(n_rb, bnd_ref, run)


NA_WIN = 768
NA_CHUNK = 768
NA_TILES = 49
GRID_ROWS = SEQ // GRID_W


def _na_kernel(q_ref, k_ref, v_ref, bnd_ref, bias_ref, o_ref, *, n_rb):
    lane = _lane_id((RB, 128))

    def run(rb, rows, latent, shift):
        na_chunks, tile_idx = CTX_CHUNKS, None
        if latent:
            win_row = jnp.clip(4 * rb - 4, 0, GRID_ROWS - NA_WIN // GRID_W)
            off = pl.multiple_of(win_row * GRID_W, RB)
            local = [(pl.multiple_of(off + c * NA_CHUNK, RB), NA_CHUNK) for c in range(NA_WIN // NA_CHUNK)]
            na_chunks = local + list(CTX_CHUNKS)
            tile_idx = []
            for c in range(NA_WIN // NA_CHUNK):
                per_row = []
                for ri in range(RB // GRID_W):
                    r = 4 * rb + ri
                    row_start = jnp.clip(r - NA_ROWS // 2, 0, GRID_ROWS - NA_ROWS)
                    idxs = []
                    for p in range(NA_CHUNK // 128):
                        kr = win_row + c * (NA_CHUNK // GRID_W) + 2 * p
                        ok1 = (kr >= row_start) & (kr < row_start + NA_ROWS)
                        ok2 = (kr + 1 >= row_start) & (kr + 1 < row_start + NA_ROWS)
                        e = jnp.clip(kr - r + NA_ROWS, 0, 15)
                        idxs.append(jnp.where(ok1 & ok2, e, jnp.where(ok1, 16 + e, jnp.where(ok2, 32 + e, 48))))
                    per_row.append(idxs)
                tile_idx.append(per_row)
        units = []
        for hd in range(NA_H):
            pair, sub = divmod(hd, 2)
            qs = q_ref[0, rows,pair * 128:(pair + 1) * 128]
            qm = jnp.where((lane >= 64) == (sub == 1), qs, jnp.zeros_like(qs))

            def bias(c, hd=hd):
                if tile_idx is None or c >= NA_WIN // NA_CHUNK:
                    return None
                return jnp.concatenate(
                    [jnp.concatenate([bias_ref[hd, idx] for idx in row], axis=1) for row in tile_idx[c]], axis=0)

            units.append((qm, _k_slab(k_ref, pair), _v_slab(v_ref, pair), bias))
        o = [acc * (1.0 / l) for acc, l in _attend(units, na_chunks, shift)]
        outs = [jnp.where(lane < 64, o[0], o[1]), jnp.where(lane < 64, o[2], o[3])]
        o_ref[0, rows, :] = jnp.concatenate(outs, axis=1).astype(o_ref.dtype)

    _query_blocks(n_rb, bnd_ref, run)


SHIFT_MAX = 40.0


def _logit_bound(d, q_gain, k_gain, bias_max=0.0):
    bound = (1.01 * math.sqrt(d) * LOG2E * jnp.max(jnp.abs(q_gain), axis=-1) * jnp.max(jnp.abs(k_gain), axis=-1)
             + bias_max)
    return jnp.stack([bound, (bound <= SHIFT_MAX).astype(F32)], axis=-1).astype(F32)


def _layer_spec(a, l):
    return pl.BlockSpec((None,) + a.shape[1:], lambda *_, _n=a.ndim - 1: (l,) + (0,) * _n)


def _attention(body, name, q, k, v, bound, extra, n_rb, out_width=256):
    n_b = q.shape[0]
    extra = [(bound, pl.BlockSpec(memory_space=pltpu.SMEM))] + list(extra)
    in_specs = [
        pl.BlockSpec((1, T, q.shape[2]), lambda b: (b, 0, 0)),
        pl.BlockSpec((1, k.shape[1], T), lambda b: (b, 0, 0)),
        pl.BlockSpec((1, T, v.shape[2]), lambda b: (b, 0, 0)),
    ] + [spec for _, spec in extra]
    return pl.pallas_call(
        functools.partial(body, n_rb=n_rb),
        grid=(n_b,),
        in_specs=in_specs,
        out_specs=pl.BlockSpec((1, n_rb * RB, out_width), lambda b: (b, 0, 0)),
        out_shape=jax.ShapeDtypeStruct((n_b, n_rb * RB, out_width), BF16),
        compiler_params=_params(("parallel",)),
        name=name,
    )(q, k, v, *[a for a, _ in extra])


def _outproj_kernel(x_ref, modb_ref, modc_ref, ma_ref, mb_ref, mc_ref, md_ref, w_ref, o_ref, *, tm):
    rows = pl.program_id(1) * tm + lax.broadcasted_iota(jnp.int32, (tm, 1), 0)
    g1 = jnp.where(rows >= SEQ, modc_ref[0][:, 2 * D:3 * D], modb_ref[0][:, 2 * D:3 * D])
    acc = _dot(ma_ref[0], w_ref[0:256, :])
    acc += _dot(mb_ref[0], w_ref[256:512, :])
    acc += _dot(mc_ref[0], w_ref[512:768, :])
    acc += _dot(md_ref[0], w_ref[768:1024, :])
    o_ref[0] = x_ref[0] + g1 * acc


def _outproj(l, xs, mods, mixes, w_out, rows_per_sample, tm):
    n_b = xs.shape[0]
    mix_spec = pl.BlockSpec((1, tm, 256), lambda b, r: (b, r, 0))
    return pl.pallas_call(
        functools.partial(_outproj_kernel, tm=tm),
        grid=(n_b, rows_per_sample // tm),
        in_specs=[
            pl.BlockSpec((1, tm, D), lambda b, r: (b, r, 0)),
            pl.BlockSpec((None, 1, 1, 6 * D), lambda b, r: (l, b, 0, 0)),
            pl.BlockSpec((None, 1, 1, 6 * D), lambda b, r: (l, n_b, 0, 0)),
            mix_spec, mix_spec, mix_spec, mix_spec,
            _layer_spec(w_out, l),
        ],
        out_specs=pl.BlockSpec((1, tm, D), lambda b, r: (b, r, 0)),
        out_shape=jax.ShapeDtypeStruct((n_b, rows_per_sample, D), F32),
        compiler_params=_params(("parallel", "parallel")),
        name="outproj",
    )(xs, mods, mods, *mixes, w_out)


def _ffn_kernel(x_ref, hb_ref, ha_ref, modb_ref, modc_ref, g_ref, wa_ref, wg_ref, cwa_ref, cwg_ref, wd_ref, o_ref,
                h_scr, u0_scr, u1_scr, act_scr, *, tm, rows_per_sample):
    rb = pl.program_id(1)
    j = pl.program_id(2)
    row0 = rb * tm
    split = SEQ % tm if rows_per_sample > SEQ else 0
    band_lo, band_hi = split - HALO, split + HALO

    def per_row(rows, col):
        lat = modb_ref[0][:, col * D:(col + 1) * D]
        if rows_per_sample == SEQ:
            return lat
        return jnp.where(rows >= SEQ, modc_ref[0][:, col * D:(col + 1) * D], lat)

    def modulated(x, rows):
        ms = jnp.mean(x * x, axis=-1, keepdims=True)
        y = x * lax.rsqrt(ms + EPS) * g_ref[...]
        return (y * (1.0 + per_row(rows, 4)) + per_row(rows, 3)).astype(BF16)

    def prologue():
        halo_rows = lax.broadcasted_iota(jnp.int32, (HALO, 1), 0)
        has_before = (row0 != 0) & (row0 != SEQ)
        has_after = (row0 + tm != SEQ) & (row0 + tm != rows_per_sample)
        zero = jnp.zeros((HALO, D), BF16)
        h_scr[0:HALO, :] = jnp.where(has_before, modulated(hb_ref[0], row0 - HALO + halo_rows), zero)
        h_scr[HALO + tm:, :] = jnp.where(has_after, modulated(ha_ref[0], row0 + tm + halo_rows), zero)
        main_rows = row0 + lax.broadcasted_iota(jnp.int32, (tm, 1), 0)
        h_scr[HALO:HALO + tm, :] = modulated(x_ref[0], main_rows)
        o_ref[0] = jnp.zeros((tm, D), F32)

    slots = (u0_scr, u1_scr)

    def tiles(total):
        units = total // HALO
        sizes = [(units // FFN_TILES + (1 if t < units % FFN_TILES else 0)) * HALO for t in range(FFN_TILES)]
        edges = np.cumsum([0] + sizes)
        return [(int(edges[t]), int(edges[t + 1])) for t in range(FFN_TILES)]

    up_tiles, down_tiles = tiles(tm + 2 * HALO), tiles(tm)

    def up(c, t):
        lo, hi = up_tiles[t]
        w = FF_WIDTHS[c]
        slots[c % 2][lo:hi, :w] = _dot(h_scr[lo:hi, :], wa_ref[:, :w])
        slots[c % 2][lo:hi, FF_CHUNK:FF_CHUNK + w] = _dot(h_scr[lo:hi, :], wg_ref[:, FF_CHUNK - w:])

    def conv(u_scr, cw_ref, lo, hi, u_lo, c_lo, width, masked):
        prev = u_scr[HALO - 1 + lo:HALO - 1 + hi, u_lo:u_lo + width]
        nxt = u_scr[HALO + 1 + lo:HALO + 1 + hi, u_lo:u_lo + width]
        if masked:
            rows = row0 + lo + lax.broadcasted_iota(jnp.int32, (hi - lo, 1), 0)
            prev = jnp.where(rows != SEQ, prev, 0.0)
            nxt = jnp.where(rows != SEQ - 1, nxt, 0.0)
        cols = slice(c_lo, c_lo + width)
        return (prev * cw_ref[0:1, cols] + u_scr[HALO + lo:HALO + hi, u_lo:u_lo + width] * cw_ref[1:2, cols]
                + nxt * cw_ref[2:3, cols] + cw_ref[3:4, cols])

    def conv_down(c, t):
        u_scr, w = slots[c % 2], FF_WIDTHS[c]
        t_lo, t_hi = down_tiles[t]
        cuts = sorted({t_lo, t_hi} | ({e for e in (band_lo, band_hi) if t_lo < e < t_hi} if split else set()))
        for lo, hi in zip(cuts[:-1], cuts[1:]):
            masked = bool(split) and band_lo <= lo < band_hi
            for c_lo in range(0, w, CONV_STRIP):
                a = conv(u_scr, cwa_ref, lo, hi, c_lo, c_lo, CONV_STRIP, masked)
                g = conv(u_scr, cwg_ref, lo, hi, FF_CHUNK + c_lo, FF_CHUNK - w + c_lo, CONV_STRIP, masked)
                act_scr[lo:hi, c_lo:c_lo + CONV_STRIP] = (g * jax.nn.sigmoid(g) * a).astype(BF16)
        o_ref[0, t_lo:t_hi, :] += _dot(act_scr[t_lo:t_hi, :w], wd_ref[FF_CHUNK - w:, :])

    for step in range(N_CHUNK + 1):
        @pl.when(j == step)
        def _(step=step):
            if step == 0:
                prologue()
            for t in range(FFN_TILES):
                if step < N_CHUNK:
                    up(step, t)
                if step > 0:
                    conv_down(step - 1, t)
            if step == N_CHUNK:
                rows = row0 + lax.broadcasted_iota(jnp.int32, (tm, 1), 0)
                o_ref[0] = x_ref[0] + per_row(rows, 5) * o_ref[0]


def _ffn(l, x1, mods, g_ffn, w_up, conv_wb, w_down, rows_per_sample, tm):
    n_b = x1.shape[0]
    n_rb = rows_per_sample // tm
    halo_per_block = tm // HALO
    n_halo = x1.shape[1] // HALO
    body = functools.partial(_ffn_kernel, tm=tm, rows_per_sample=rows_per_sample)

    def up_idx(j):
        return jnp.minimum(j, N_CHUNK - 1)

    def down_idx(j):
        return jnp.maximum(j - 1, 0)

    def tail_clamped(c):
        return pl.multiple_of(jnp.minimum(c * FF_CHUNK, D_FF - FF_CHUNK), 256)

    return pl.pallas_call(
        body,
        grid=(n_b, n_rb, N_CHUNK + 1),
        in_specs=[
            pl.BlockSpec((1, tm, D), lambda b, r, j: (b, r, 0)),
            pl.BlockSpec((1, HALO, D), lambda b, r, j: (b, jnp.maximum(r * halo_per_block - 1, 0), 0)),
            pl.BlockSpec((1, HALO, D), lambda b, r, j: (b, jnp.minimum((r + 1) * halo_per_block, n_halo - 1), 0)),
            pl.BlockSpec((None, 1, 1, 6 * D), lambda b, r, j: (l, b, 0, 0)),
            pl.BlockSpec((None, 1, 1, 6 * D), lambda b, r, j: (l, n_b, 0, 0)),
            _layer_spec(g_ffn, l),
            pl.BlockSpec((None, D, FF_CHUNK), lambda b, r, j: (l, 0, up_idx(j))),
            pl.BlockSpec((None, pl.Element(D), pl.Element(FF_CHUNK)),
                         lambda b, r, j: (l, 0, pl.multiple_of(D_FF + tail_clamped(up_idx(j)), 256))),
            pl.BlockSpec((None, 4, FF_CHUNK), lambda b, r, j: (l, 0, down_idx(j))),
            pl.BlockSpec((None, pl.Element(4), pl.Element(FF_CHUNK)),
                         lambda b, r, j: (l, 0, pl.multiple_of(D_FF + tail_clamped(down_idx(j)), 256))),
            pl.BlockSpec((None, pl.Element(FF_CHUNK), pl.Element(D)),
                         lambda b, r, j: (l, tail_clamped(down_idx(j)), 0)),
        ],
        out_specs=pl.BlockSpec((1, tm, D), lambda b, r, j: (b, r, 0)),
        out_shape=jax.ShapeDtypeStruct((n_b, rows_per_sample, D), F32),
        scratch_shapes=[
            pltpu.VMEM((tm + 2 * HALO, D), BF16),
            pltpu.VMEM((tm + 2 * HALO, 2 * FF_CHUNK), F32),
            pltpu.VMEM((tm + 2 * HALO, 2 * FF_CHUNK), F32),
            pltpu.VMEM((tm, FF_CHUNK), BF16),
        ],
        compiler_params=_params(("parallel", "parallel", "arbitrary")),
        name="conv_ffn",
    )(x1, x1, x1, mods, mods, g_ffn, w_up, w_up, conv_wb, conv_wb, w_down)


def _rope_table(rot_dim):
    n_freq = rot_dim // 4
    inv = jnp.power(ROPE_THETA, -jnp.arange(n_freq, dtype=F32) / n_freq)
    t = jnp.arange(SEQ)
    row = (t // GRID_W).astype(F32)
    col = (t % GRID_W).astype(F32)
    ar, ac = row[:, None] * inv, col[:, None] * inv
    ang = jnp.concatenate([ar, ar, ac, ac], axis=-1)
    sign = jnp.concatenate([-jnp.ones(n_freq), jnp.ones(n_freq), -jnp.ones(n_freq), jnp.ones(n_freq)]).astype(F32)
    cos = jnp.concatenate([jnp.cos(ang), jnp.ones((CTX, rot_dim), F32)], axis=0)
    sin = jnp.concatenate([jnp.sin(ang) * sign, jnp.zeros((CTX, rot_dim), F32)], axis=0)
    return jnp.concatenate([cos.T, sin.T], axis=0)


def _na_bias_tiles(rpb):
    n_l, n_h = rpb.shape[:2]
    n_off = 2 * NA_ROWS - 1
    p = jnp.pad(rpb.astype(F32) * LOG2E, ((0, 0), (0, 0), (0, 0), (48, 49)))
    sk = jnp.broadcast_to(p[..., None, :], (n_l, n_h, n_off, GRID_W, 128)).reshape(n_l, n_h, n_off, GRID_W * 128)
    sk = sk[..., :GRID_W * 127].reshape(n_l, n_h, n_off, GRID_W, 127)[..., 63:127]
    c = np.arange(GRID_W)
    win_start = np.clip(c - NA_COLS // 2, 0, GRID_W - NA_COLS)
    v_col = (c[None, :] >= win_start[:, None]) & (c[None, :] < win_start[:, None] + NA_COLS)
    tz = jnp.where(v_col, sk, NEG_INF)
    neg1 = jnp.full((n_l, n_h, 1, GRID_W, GRID_W), NEG_INF, F32)
    tzx = jnp.concatenate([neg1, tz, neg1], axis=2)
    first, second = tzx[:, :, 0:16], tzx[:, :, 1:17]
    neg16 = jnp.full_like(first, NEG_INF)
    return jnp.concatenate([
        jnp.concatenate([first, second], axis=-1), jnp.concatenate([first, neg16], axis=-1),
        jnp.concatenate([neg16, second], axis=-1), jnp.concatenate([neg1, neg1], axis=-1)], axis=2)


def kernel(x, c, ctx, c_ctx, w_mod, b_mod, g_mix, w_in, w_out, mla_q_a_g, mla_w_uq, mla_kv_a_g, mla_w_ukv,
           mla_q_g, mla_k_g, diff_q_g, diff_k_g, diff_lq1, diff_lk1, diff_lq2, diff_lk2, diff_subln_g,
           na_q_g, na_k_g, na_rpb, gqa_q_g, gqa_k_g, g_ffn, w_up, conv_w, conv_b, w_down):
    n_b = x.shape[0]
    n_layer = w_mod.shape[0]
    assert x.shape[1:] == (SEQ, D) and ctx.shape[1:] == (CTX, D)

    xs = jnp.concatenate([x, ctx], axis=1)
    mod_rows = -(-(n_b + 1) // 8) * 8
    cc = jnp.concatenate([c, c_ctx[None], jnp.zeros((mod_rows - n_b - 1, D), F32)], axis=0)
    mods_all = _modulation(cc, w_mod, b_mod).reshape(n_layer, mod_rows, 1, 6 * D)

    cs32, cs64 = _rope_table(32), _rope_table(64)
    bias_all = _na_bias_tiles(na_rpb)

    s = np.cumsum([0, 256, 128, 32, 256, 256, 256, 256, 256, 256, 256, 128, 128])
    gq_perm = np.concatenate([np.arange(GQA_D) + (2 * g + r) * GQA_D for r in range(2) for g in range(2)])

    n_l = n_layer
    w_in_b = w_in.astype(BF16)
    gq_cols = w_in_b[:, :, s[9]:s[10]][:, :, gq_perm]
    wqk = jnp.concatenate([w_in_b[:, :, s[0]:s[3]], w_in_b[:, :, s[3]:s[5]], w_in_b[:, :, s[6]:s[8]], gq_cols,
                           w_in_b[:, :, s[10]:s[11]]], axis=2).swapaxes(1, 2)
    wv = jnp.concatenate([w_in_b[:, :, s[5]:s[6]], w_in_b[:, :, s[8]:s[9]], w_in_b[:, :, s[11]:s[12]]],
                         axis=2)
    wuq = mla_w_uq.swapaxes(1, 2).astype(BF16)
    wukv = mla_w_ukv.reshape(n_l, MLA_KVR, MLA_H, MLA_NOPE + MLA_V)
    wukn = wukv[..., :MLA_NOPE].reshape(n_l, MLA_KVR, MLA_H * MLA_NOPE).swapaxes(1, 2).astype(BF16)
    wuv = wukv[..., MLA_NOPE:].reshape(n_l, MLA_KVR, MLA_H * MLA_V).swapaxes(1, 2).astype(BF16)
    gcol = jnp.concatenate([mla_q_a_g, mla_kv_a_g, mla_q_g, mla_k_g, diff_q_g, diff_k_g, na_q_g, na_k_g,
                            gqa_q_g, gqa_k_g], axis=1).astype(F32)[..., None]
    lam_rows = jnp.stack([diff_lq1, diff_lk1, diff_lq2, diff_lk2], axis=1).astype(F32)
    lam_rows = jnp.concatenate([lam_rows, jnp.zeros_like(lam_rows)], axis=1)
    gsub = jnp.tile(diff_subln_g.astype(F32), (1, 2)).reshape(n_l, 1, 128)
    bounds = {
        "mla": _logit_bound(MLA_D, mla_q_g, mla_k_g),
        "diff": _logit_bound(DIFF_D, diff_q_g, diff_k_g),
        "na": _logit_bound(NA_D, na_q_g, na_k_g, LOG2E * jnp.maximum(jnp.max(na_rpb, axis=(1, 2, 3)), 0.0)),
        "gqa": _logit_bound(GQA_D, gqa_q_g, gqa_k_g),
    }
    w_out_b, w_down_b = w_out.astype(BF16), w_down.astype(BF16)
    w_up_b = w_up.astype(BF16)
    conv_wb = jnp.concatenate([conv_w, conv_b[:, None, :]], axis=1).astype(F32)
    gmix, gffn = g_mix.reshape(n_l, 1, D), g_ffn.reshape(n_l, 1, D)

    for l in range(n_layer):
        with_ctx = l < n_layer - 1
        lambda_init = 0.8 - 0.6 * math.exp(-0.3 * l)
        (q_mla, k_mla, v_mla, q_diff, k_diff, v_diff, q_na, k_na, v_na, q_gqa, k_gqa, v_gqa) = _inproj(
            l, xs, mods_all, gmix, wqk, wv, wuq, wukn, wuv, gcol, cs32, cs64)

        n_rb = NRB if with_ctx else NLAT
        mix_a = _attention(_mla_kernel, "attn_mla", q_mla, k_mla, v_mla, bounds["mla"][l], [], n_rb)
        mix_b = _attention(
            functools.partial(_diff_kernel, lambda_init=lambda_init), "attn_diff",
            q_diff, k_diff, v_diff, bounds["diff"][l],
            [(lam_rows, _layer_spec(lam_rows, l)), (gsub, _layer_spec(gsub, l))], n_rb)
        mix_c = _attention(_na_kernel, "attn_na", q_na, k_na, v_na, bounds["na"][l],
                           [(bias_all, _layer_spec(bias_all, l))], n_rb)
        mix_d = _attention(_gqa_kernel, "attn_gqa", q_gqa, k_gqa, v_gqa, bounds["gqa"][l], [], n_rb)

        rows_per_sample = T if with_ctx else SEQ
        x1 = _outproj(l, xs, mods_all, (mix_a, mix_b, mix_c, mix_d), w_out_b, rows_per_sample,
                      OUT_TM if with_ctx else OUT_TM_LAST)
        xs = _ffn(l, x1, mods_all, gffn, w_up_b, conv_wb, w_down_b, rows_per_sample, rows_per_sample // 2)
    return xs
```

```python
import functools
import math

import numpy as np
import jax
import jax.numpy as jnp
from jax import lax
from jax.experimental import pallas as pl
from jax.experimental.pallas import tpu as pltpu

F32 = jnp.float32
BF16 = jnp.bfloat16

D = 1024
SEQ = 2048
GRID_W = 64
CTX = 256
T = SEQ + CTX
RB = 256
NRB = T // RB
NLAT = SEQ // RB
IN_SUB = 3
EPS = 1e-6
NEG_INF = -1e30
ROPE_THETA = 10000.0
LOG2E = 1.4426950408889634

MLA_H, MLA_NOPE, MLA_ROPE, MLA_V, MLA_QR, MLA_KVR = 4, 64, 32, 64, 256, 128
MLA_D = MLA_NOPE + MLA_ROPE
DIFF_H, DIFF_D = 4, 32
NA_H, NA_D, NA_ROWS, NA_COLS = 4, 64, 8, 16
GQA_H, GQA_KV, GQA_D = 4, 2, 64
D_FF = 2816
FF_CHUNK = 1024
N_CHUNK = -(-D_FF // FF_CHUNK)
FF_WIDTHS = tuple(min(FF_CHUNK, D_FF - c * FF_CHUNK) for c in range(N_CHUNK))
OUT_TM, OUT_TM_LAST = 768, 1024
HALO = 16
FFN_TILES = 4
CONV_STRIP = 256

R_CQ, R_CKV, R_KR, R_DQ, R_DK, R_NQ, R_NK, R_GQ, R_GK = 0, 256, 384, 416, 672, 928, 1184, 1440, 1696
QK_ROWS = 1824
V_COLS = 640

G_QA, G_KVA, G_MQ, G_MK, G_DQ, G_DK, G_NQ, G_NK, G_GQ, G_GK = 0, 256, 384, 480, 576, 608, 640, 704, 768, 832
G_ROWS = 896

VMEM_LIMIT = 56 * 1024 * 1024


def _params(sem):
    return pltpu.CompilerParams(dimension_semantics=sem, vmem_limit_bytes=VMEM_LIMIT)


def _dot(a, b):
    return jnp.dot(a, b, preferred_element_type=F32)


def _dot_nt(a, b):
    return lax.dot_general(a, b, (((1,), (1,)), ((), ())), preferred_element_type=F32)


def _mod_kernel(c_ref, w_ref, b_ref, o_ref):
    c = c_ref[...]
    a = (c * jax.nn.sigmoid(c)).astype(BF16)
    o_ref[0] = _dot(a, w_ref[0].astype(BF16)) + b_ref[0]


def _modulation(cc, w_mod, b_mod):
    n_layer = w_mod.shape[0]
    rows = cc.shape[0]
    return pl.pallas_call(
        _mod_kernel,
        grid=(n_layer, 6),
        in_specs=[
            pl.BlockSpec((rows, D), lambda l, j: (0, 0)),
            pl.BlockSpec((1, D, D), lambda l, j: (l, 0, j)),
            pl.BlockSpec((1, 1, D), lambda l, j: (l, 0, j)),
        ],
        out_specs=pl.BlockSpec((1, rows, D), lambda l, j: (l, 0, j)),
        out_shape=jax.ShapeDtypeStruct((n_layer, rows, 6 * D), F32),
        compiler_params=_params(("parallel", "parallel")),
        name="modulation",
    )(cc, w_mod, b_mod.reshape(n_layer, 1, 6 * D))


def _rms_rows(x, g, n):
    ss = jnp.sum(x * x, axis=0, keepdims=True) * (1.0 / n)
    return x * lax.rsqrt(ss + EPS) * g


def _rope_rows(x, cos, sin_signed, w):
    rot = jnp.concatenate([x[w:2 * w], x[0:w], x[3 * w:4 * w], x[2 * w:3 * w]], axis=0)
    return x * cos + rot * sin_signed


def _inproj_kernel(x_ref, modb_ref, modc_ref, gmix_ref, wqk_ref, wv_ref, wuq_ref, wukn_ref, wuv_ref, gcol_ref,
                   cs32_ref, cs64_ref,
                   qmla_ref, kmla_ref, vmla_ref, qdiff_ref, kdiff_ref, vdiff_ref,
                   qna_ref, kna_ref, vna_ref, qgqa_ref, kgqa_ref, vgqa_ref):
    step = pl.program_id(0)

    def modulated(sb):
        r0 = sb * RB
        x = x_ref[0, r0:r0 + RB, :]
        mod = modb_ref[0]
        if sb == IN_SUB - 1:
            mod = jnp.where(step == NRB // IN_SUB - 1, modc_ref[0], mod)
        shift, scale = mod[:, 0:D], mod[:, D:2 * D]
        ms = jnp.mean(x * x, axis=-1, keepdims=True)
        h = x * lax.rsqrt(ms + EPS) * gmix_ref[...]
        return (h * (1.0 + scale) + shift).astype(BF16)

    def gain(off, n):
        return gcol_ref[off:off + n, :]

    def block(sb, hb, after_first_projection):
        rows = slice(sb * RB, (sb + 1) * RB)
        _inproj_block(rows, hb, after_first_projection, gain, wqk_ref, wv_ref, wuq_ref, wukn_ref, wuv_ref,
                      cs32_ref, cs64_ref, qmla_ref, kmla_ref, vmla_ref, qdiff_ref, kdiff_ref, vdiff_ref,
                      qna_ref, kna_ref, vna_ref, qgqa_ref, kgqa_ref, vgqa_ref)

    hbs = [modulated(0)]
    for sb in range(IN_SUB):
        nxt = (lambda sb=sb: hbs.append(modulated(sb + 1))) if sb + 1 < IN_SUB else (lambda: None)
        block(sb, hbs[sb], nxt)


def _inproj_block(rows, hb, after_first_projection, gain, wqk_ref, wv_ref, wuq_ref, wukn_ref, wuv_ref,
                  cs32_ref, cs64_ref, qmla_ref, kmla_ref, vmla_ref, qdiff_ref, kdiff_ref, vdiff_ref,
                  qna_ref, kna_ref, vna_ref, qgqa_ref, kgqa_ref, vgqa_ref):
    def project(lo, hi):
        return _dot_nt(wqk_ref[lo:hi, :], hb)

    cos32, sin32 = cs32_ref[0:32, rows], cs32_ref[32:64, rows]
    cos64, sin64 = cs64_ref[0:64, rows], cs64_ref[64:128, rows]

    def values():
        pv = _dot(hb, wv_ref[...])
        vdiff_ref[0, rows, :] = pv[:, 0:256].astype(BF16)
        vna_ref[0, rows, :] = pv[:, 256:512].astype(BF16)
        vgqa_ref[0, rows, :] = pv[:, 512:640].astype(BF16)

    def mla(pt):
        cq = _rms_rows(pt[0:MLA_QR], gain(G_QA, MLA_QR), MLA_QR).astype(BF16)
        qt = _dot(wuq_ref[...], cq)
        ckv = _rms_rows(pt[R_CKV:R_CKV + MLA_KVR], gain(G_KVA, MLA_KVR), MLA_KVR).astype(BF16)
        knt = _dot(wukn_ref[...], ckv)
        vt = _dot(wuv_ref[...], ckv)
        vmla_ref[0, rows, :] = vt.T.astype(BF16)
        kr = pt[R_KR:R_KR + MLA_ROPE]
        kr_ss = jnp.sum(kr * kr, axis=0, keepdims=True)
        g_mq, g_mk = gain(G_MQ, MLA_D), gain(G_MK, MLA_D)
        zpad = jnp.zeros((128 - MLA_D, RB), F32)
        q_parts = []
        for hd in range(MLA_H):
            qh = _rms_rows(qt[hd * MLA_D:(hd + 1) * MLA_D], g_mq, MLA_D)
            q_rope = _rope_rows(qh[MLA_NOPE:], cos32, sin32, MLA_ROPE // 4)
            q_parts += [qh[:MLA_NOPE], q_rope, zpad]
            kn = knt[hd * MLA_NOPE:(hd + 1) * MLA_NOPE]
            ss = (jnp.sum(kn * kn, axis=0, keepdims=True) + kr_ss) * (1.0 / MLA_D)
            r = lax.rsqrt(ss + EPS)
            k_rope = _rope_rows(kr * r * g_mk[MLA_NOPE:], cos32, sin32, MLA_ROPE // 4)
            kmla_ref[0, hd * 128:(hd + 1) * 128, rows] = jnp.concatenate(
                [kn * r * g_mk[:MLA_NOPE], k_rope, zpad], axis=0).astype(BF16)
        q_all = jnp.concatenate(q_parts, axis=0) * (MLA_D ** -0.5 * LOG2E)
        qmla_ref[0, rows, :] = q_all.T.astype(BF16)

    def diff(pt):
        g_dq, g_dk = gain(G_DQ, DIFF_D), gain(G_DK, DIFF_D)
        q_parts, k_parts = [], []
        for gi in range(2 * DIFF_H):
            qg = _rms_rows(pt[gi * DIFF_D:(gi + 1) * DIFF_D], g_dq, DIFF_D)
            q_parts.append(_rope_rows(qg, cos32, sin32, DIFF_D // 4))
            kg = _rms_rows(pt[256 + gi * DIFF_D:256 + (gi + 1) * DIFF_D], g_dk, DIFF_D)
            k_parts.append(_rope_rows(kg, cos32, sin32, DIFF_D // 4))
        qdiff_ref[0, rows, :] = (jnp.concatenate(q_parts, axis=0) * (DIFF_D ** -0.5 * LOG2E)).T.astype(BF16)
        kdiff_ref[0, :, rows] = jnp.concatenate(k_parts, axis=0).astype(BF16)

    def na(pt):
        g_nq, g_nk = gain(G_NQ, NA_D), gain(G_NK, NA_D)
        q_parts, k_parts = [], []
        for hd in range(NA_H):
            q_parts.append(_rms_rows(pt[hd * NA_D:(hd + 1) * NA_D], g_nq, NA_D))
            k_parts.append(_rms_rows(pt[256 + hd * NA_D:256 + (hd + 1) * NA_D], g_nk, NA_D))
        qna_ref[0, rows, :] = (jnp.concatenate(q_parts, axis=0) * (NA_D ** -0.5 * LOG2E)).T.astype(BF16)
        kna_ref[0, :, rows] = jnp.concatenate(k_parts, axis=0).astype(BF16)

    def gqa(pt):
        g_gq, g_gk = gain(G_GQ, GQA_D), gain(G_GK, GQA_D)
        q_parts, k_parts = [], []
        for hd in range(GQA_H):
            qg = _rms_rows(pt[hd * GQA_D:(hd + 1) * GQA_D], g_gq, GQA_D)
            q_parts.append(_rope_rows(qg, cos64, sin64, GQA_D // 4))
        for hd in range(GQA_KV):
            kg = _rms_rows(pt[256 + hd * GQA_D:256 + (hd + 1) * GQA_D], g_gk, GQA_D)
            k_parts.append(_rope_rows(kg, cos64, sin64, GQA_D // 4))
        qgqa_ref[0, rows, :] = (jnp.concatenate(q_parts, axis=0) * (GQA_D ** -0.5 * LOG2E)).T.astype(BF16)
        kgqa_ref[0, :, rows] = jnp.concatenate(k_parts, axis=0).astype(BF16)

    pt_mla = project(R_CQ, R_DQ)
    after_first_projection()
    pt_diff = project(R_DQ, R_NQ)
    mla(pt_mla)
    pt_na = project(R_NQ, R_GQ)
    diff(pt_diff)
    pt_gqa = project(R_GQ, QK_ROWS)
    na(pt_na)
    values()
    gqa(pt_gqa)


def _inproj(l, xs, mods, gmix, wqk, wv, wuq, wukn, wuv, gcol, cs32, cs64):
    n_b = xs.shape[0]

    def full(a):
        return _layer_spec(a, l)

    rows_per_step = IN_SUB * RB

    def tok(width):
        return pl.BlockSpec((1, rows_per_step, width), lambda r, b: (b, r, 0))

    def chan(rows):
        return pl.BlockSpec((1, rows, rows_per_step), lambda r, b: (b, 0, r))

    def tshape(width):
        return jax.ShapeDtypeStruct((n_b, T, width), BF16)

    def cshape(rows):
        return jax.ShapeDtypeStruct((n_b, rows, T), BF16)

    return pl.pallas_call(
        _inproj_kernel,
        grid=(NRB // IN_SUB, n_b),
        in_specs=[
            pl.BlockSpec((1, rows_per_step, D), lambda r, b: (b, r, 0)),
            pl.BlockSpec((None, 1, 1, 6 * D), lambda r, b: (l, b, 0, 0)),
            pl.BlockSpec((None, 1, 1, 6 * D), lambda r, b: (l, n_b, 0, 0)),
            full(gmix), full(wqk), full(wv), full(wuq), full(wukn), full(wuv), full(gcol),
            pl.BlockSpec((64, rows_per_step), lambda r, b: (0, r)),
            pl.BlockSpec((128, rows_per_step), lambda r, b: (0, r)),
        ],
        out_specs=[tok(512), chan(512), tok(256), tok(256), chan(256), tok(256),
                   tok(256), chan(256), tok(256), tok(256), chan(128), tok(128)],
        out_shape=[tshape(512), cshape(512), tshape(256), tshape(256), cshape(256), tshape(256),
                   tshape(256), cshape(256), tshape(256), tshape(256), cshape(128), tshape(128)],
        compiler_params=_params(("parallel", "parallel")),
        name="inproj_prep",
    )(xs, mods, mods, gmix, wqk, wv, wuq, wukn, wuv, gcol, cs32, cs64)


def _lane_id(shape):
    return lax.broadcasted_iota(jnp.int32, shape, 1)


QB_UNROLL = 2
KEY_CHUNK = 768
LAT_CHUNKS = tuple((lo, KEY_CHUNK) for lo in range(0, T, KEY_CHUNK))
CTX_CHUNKS = ((SEQ, CTX),)


def _k_slab(k_ref, idx):
    return lambda lo, n: k_ref[0, idx * 128:(idx + 1) * 128, pl.ds(lo, n)]


def _v_slab(v_ref, idx):
    return lambda lo, n: v_ref[0, pl.ds(lo, n), idx * 128:(idx + 1) * 128]


def _attend(units, chunks, shift=None):
    def scores(i, c):
        q, keys, _, bias = units[i]
        lo, n = chunks[c]
        s = _dot(q, keys(lo, n))
        b = None if bias is None else bias(c)
        return s if b is None else s + b

    if shift is not None:
        items = [(i, c) for i in range(len(units)) for c in range(len(chunks))]
        acc, l = [None] * len(units), [None] * len(units)
        nxt = scores(*items[0])
        for idx, (i, c) in enumerate(items):
            cur = nxt
            if idx + 1 < len(items):
                nxt = scores(*items[idx + 1])
            e = jnp.exp2(cur - shift)
            lc = jnp.sum(e, axis=-1, keepdims=True)
            oc = _dot(e.astype(BF16), units[i][2](*chunks[c]))
            acc[i] = oc if acc[i] is None else acc[i] + oc
            l[i] = lc if l[i] is None else l[i] + lc
        return list(zip(acc, l))

    cur = [scores(0, c) for c in range(len(chunks))]
    out = []
    for i in range(len(units)):
        m = functools.reduce(jnp.maximum, [jnp.max(s, axis=-1, keepdims=True) for s in cur])
        nxt, acc, l = [], None, None
        for c, (lo, n) in enumerate(chunks):
            if i + 1 < len(units):
                nxt.append(scores(i + 1, c))
            e = jnp.exp2(cur[c] - m)
            lc = jnp.sum(e, axis=-1, keepdims=True)
            oc = _dot(e.astype(BF16), units[i][2](lo, n))
            acc = oc if acc is None else acc + oc
            l = lc if l is None else l + lc
        out.append((acc, l))
        cur = nxt
    return out


def _with_shift(bnd_ref, run):
    @pl.when(bnd_ref[1] > 0.5)
    def _():
        run(bnd_ref[0])

    @pl.when(bnd_ref[1] <= 0.5)
    def _():
        run(None)


def _query_blocks(n_rb, bnd_ref, run):
    def all_blocks(shift):
        def body(i, carry):
            for u in range(QB_UNROLL):
                rb = i * QB_UNROLL + u
                run(rb, pl.ds(pl.multiple_of(rb * RB, RB), RB), True, shift)
            return carry

        lax.fori_loop(0, NLAT // QB_UNROLL, body, 0)
        if n_rb > NLAT:
            run(NLAT, pl.ds(SEQ, CTX), False, shift)

    _with_shift(bnd_ref, all_blocks)


def _mla_kernel(q_ref, k_ref, v_ref, bnd_ref, o_ref, *, n_rb):
    lane = _lane_id((RB, 128))

    def run(rb, rows, latent, shift):
        units = [(q_ref[0, rows,hd * 128:(hd + 1) * 128], _k_slab(k_ref, hd), _v_slab(v_ref, hd // 2), None)
                 for hd in range(MLA_H)]
        o = [acc * (1.0 / l) for acc, l in _attend(units, LAT_CHUNKS if latent else CTX_CHUNKS, shift)]
        outs = [jnp.where(lane < 64, o[0], o[1]), jnp.where(lane < 64, o[2], o[3])]
        o_ref[0, rows, :] = jnp.concatenate(outs, axis=1).astype(o_ref.dtype)

    _query_blocks(n_rb, bnd_ref, run)


def _gqa_kernel(q_ref, k_ref, v_ref, bnd_ref, o_ref, *, n_rb):
    lane = _lane_id((RB, 128))

    def run(rb, rows, latent, shift):
        order = [(rep, grp) for rep in range(2) for grp in range(2)]
        units = []
        for rep, grp in order:
            qs = q_ref[0, rows,rep * 128:(rep + 1) * 128]
            qm = jnp.where((lane >= 64) == (grp == 1), qs, jnp.zeros_like(qs))
            units.append((qm, _k_slab(k_ref, 0), _v_slab(v_ref, 0), None))
        res = {}
        for (rep, grp), (acc, l) in zip(order, _attend(units, LAT_CHUNKS if latent else CTX_CHUNKS, shift)):
            o = acc * (1.0 / l)
            res[(grp, rep)] = o if grp == rep else pltpu.roll(o, 64, axis=1)
        outs = [jnp.where(lane < 64, res[(grp, 0)], res[(grp, 1)]) for grp in range(2)]
        o_ref[0, rows, :] = jnp.concatenate(outs, axis=1).astype(o_ref.dtype)

    _query_blocks(n_rb, bnd_ref, run)


def _diff_kernel(q_ref, k_ref, v_ref, bnd_ref, lam_ref, g_ref, o_ref, *, n_rb, lambda_init):
    lane = _lane_id((RB, 128))
    lq1, lk1, lq2, lk2 = lam_ref[0:1, :], lam_ref[1:2, :], lam_ref[2:3, :], lam_ref[3:4, :]
    lam = (jnp.exp(jnp.sum(lq1 * lk1, axis=-1, keepdims=True))
           - jnp.exp(jnp.sum(lq2 * lk2, axis=-1, keepdims=True)) + lambda_init)
    gsub = g_ref[...]
    grp = lane // DIFF_D

    def run(rb, rows, latent, shift):
        units = []
        for hd in range(DIFF_H):
            pair, sub = divmod(hd, 2)
            qs = q_ref[0, rows,pair * 128:(pair + 1) * 128]
            for which in range(2):
                qm = jnp.where(grp == 2 * sub + which, qs, jnp.zeros_like(qs))
                units.append((qm, _k_slab(k_ref, pair), _v_slab(v_ref, pair), None))
        res = _attend(units, LAT_CHUNKS if latent else CTX_CHUNKS, shift)
        outs = []
        for pair in range(2):
            halves = []
            for sub in range(2):
                (a1, l1), (a2, l2) = res[2 * (2 * pair + sub)], res[2 * (2 * pair + sub) + 1]
                halves.append(a1 * (1.0 / l1) - a2 * (lam / l2))
            o = jnp.where(lane < 64, halves[0], halves[1])
            o2 = o * o
            ss0 = jnp.sum(jnp.where(lane < 64, o2, 0.0), axis=-1, keepdims=True)
            ss1 = jnp.sum(jnp.where(lane < 64, 0.0, o2), axis=-1, keepdims=True)
            ss = jnp.where(lane < 64, ss0, ss1) * (1.0 / (2 * DIFF_D))
            outs.append(o * lax.rsqrt(ss + EPS) * gsub * (1.0 - lambda_init))
        o_ref[0, rows, :] = jnp.concatenate(outs, axis=1).astype(o_ref.dtype)

    _query_blocks(n_rb, bnd_ref, run)


NA_WIN = 768
NA_CHUNK = 768
NA_TILES = 49
GRID_ROWS = SEQ // GRID_W


def _na_kernel(q_ref, k_ref, v_ref, bnd_ref, bias_ref, o_ref, *, n_rb):
    lane = _lane_id((RB, 128))

    def run(rb, rows, latent, shift):
        na_chunks, tile_idx = CTX_CHUNKS, None
        if latent:
            win_row = jnp.clip(4 * rb - 4, 0, GRID_ROWS - NA_WIN // GRID_W)
            off = pl.multiple_of(win_row * GRID_W, RB)
            local = [(pl.multiple_of(off + c * NA_CHUNK, RB), NA_CHUNK) for c in range(NA_WIN // NA_CHUNK)]
            na_chunks = local + list(CTX_CHUNKS)
            tile_idx = []
            for c in range(NA_WIN // NA_CHUNK):
                per_row = []
                for ri in range(RB // GRID_W):
                    r = 4 * rb + ri
                    row_start = jnp.clip(r - NA_ROWS // 2, 0, GRID_ROWS - NA_ROWS)
                    idxs = []
                    for p in range(NA_CHUNK // 128):
                        kr = win_row + c * (NA_CHUNK // GRID_W) + 2 * p
                        ok1 = (kr >= row_start) & (kr < row_start + NA_ROWS)
                        ok2 = (kr + 1 >= row_start) & (kr + 1 < row_start + NA_ROWS)
                        e = jnp.clip(kr - r + NA_ROWS, 0, 15)
                        idxs.append(jnp.where(ok1 & ok2, e, jnp.where(ok1, 16 + e, jnp.where(ok2, 32 + e, 48))))
                    per_row.append(idxs)
                tile_idx.append(per_row)
        units = []
        for hd in range(NA_H):
            pair, sub = divmod(hd, 2)
            qs = q_ref[0, rows,pair * 128:(pair + 1) * 128]
            qm = jnp.where((lane >= 64) == (sub == 1), qs, jnp.zeros_like(qs))

            def bias(c, hd=hd):
                if tile_idx is None or c >= NA_WIN // NA_CHUNK:
                    return None
                return jnp.concatenate(
                    [jnp.concatenate([bias_ref[hd, idx] for idx in row], axis=1) for row in tile_idx[c]], axis=0)

            units.append((qm, _k_slab(k_ref, pair), _v_slab(v_ref, pair), bias))
        o = [acc * (1.0 / l) for acc, l in _attend(units, na_chunks, shift)]
        outs = [jnp.where(lane < 64, o[0], o[1]), jnp.where(lane < 64, o[2], o[3])]
        o_ref[0, rows, :] = jnp.concatenate(outs, axis=1).astype(o_ref.dtype)

    _query_blocks(n_rb, bnd_ref, run)


SHIFT_MAX = 40.0


def _logit_bound(d, q_gain, k_gain, bias_max=0.0):
    bound = (1.01 * math.sqrt(d) * LOG2E * jnp.max(jnp.abs(q_gain), axis=-1) * jnp.max(jnp.abs(k_gain), axis=-1)
             + bias_max)
    return jnp.stack([bound, (bound <= SHIFT_MAX).astype(F32)], axis=-1).astype(F32)


def _layer_spec(a, l):
    return pl.BlockSpec((None,) + a.shape[1:], lambda *_, _n=a.ndim - 1: (l,) + (0,) * _n)


def _attention(body, name, q, k, v, bound, extra, n_rb, out_width=256):
    n_b = q.shape[0]
    extra = [(bound, pl.BlockSpec(memory_space=pltpu.SMEM))] + list(extra)
    in_specs = [
        pl.BlockSpec((1, T, q.shape[2]), lambda b: (b, 0, 0)),
        pl.BlockSpec((1, k.shape[1], T), lambda b: (b, 0, 0)),
        pl.BlockSpec((1, T, v.shape[2]), lambda b: (b, 0, 0)),
    ] + [spec for _, spec in extra]
    return pl.pallas_call(
        functools.partial(body, n_rb=n_rb),
        grid=(n_b,),
        in_specs=in_specs,
        out_specs=pl.BlockSpec((1, n_rb * RB, out_width), lambda b: (b, 0, 0)),
        out_shape=jax.ShapeDtypeStruct((n_b, n_rb * RB, out_width), BF16),
        compiler_params=_params(("parallel",)),
        name=name,
    )(q, k, v, *[a for a, _ in extra])


def _outproj_kernel(x_ref, modb_ref, modc_ref, ma_ref, mb_ref, mc_ref, md_ref, w_ref, o_ref, *, tm):
    rows = pl.program_id(1) * tm + lax.broadcasted_iota(jnp.int32, (tm, 1), 0)
    g1 = jnp.where(rows >= SEQ, modc_ref[0][:, 2 * D:3 * D], modb_ref[0][:, 2 * D:3 * D])
    acc = _dot(ma_ref[0], w_ref[0:256, :])
    acc += _dot(mb_ref[0], w_ref[256:512, :])
    acc += _dot(mc_ref[0], w_ref[512:768, :])
    acc += _dot(md_ref[0], w_ref[768:1024, :])
    o_ref[0] = x_ref[0] + g1 * acc


def _outproj(l, xs, mods, mixes, w_out, rows_per_sample, tm):
    n_b = xs.shape[0]
    mix_spec = pl.BlockSpec((1, tm, 256), lambda b, r: (b, r, 0))
    return pl.pallas_call(
        functools.partial(_outproj_kernel, tm=tm),
        grid=(n_b, rows_per_sample // tm),
        in_specs=[
            pl.BlockSpec((1, tm, D), lambda b, r: (b, r, 0)),
            pl.BlockSpec((None, 1, 1, 6 * D), lambda b, r: (l, b, 0, 0)),
            pl.BlockSpec((None, 1, 1, 6 * D), lambda b, r: (l, n_b, 0, 0)),
            mix_spec, mix_spec, mix_spec, mix_spec,
            _layer_spec(w_out, l),
        ],
        out_specs=pl.BlockSpec((1, tm, D), lambda b, r: (b, r, 0)),
        out_shape=jax.ShapeDtypeStruct((n_b, rows_per_sample, D), F32),
        compiler_params=_params(("parallel", "parallel")),
        name="outproj",
    )(xs, mods, mods, *mixes, w_out)


def _ffn_kernel(x_ref, hb_ref, ha_ref, modb_ref, modc_ref, g_ref, wa_ref, wg_ref, cwa_ref, cwg_ref, wd_ref, o_ref,
                h_scr, u0_scr, u1_scr, act_scr, *, tm, rows_per_sample):
    rb = pl.program_id(1)
    j = pl.program_id(2)
    row0 = rb * tm
    split = SEQ % tm if rows_per_sample > SEQ else 0
    band_lo, band_hi = split - HALO, split + HALO

    def per_row(rows, col):
        lat = modb_ref[0][:, col * D:(col + 1) * D]
        if rows_per_sample == SEQ:
            return lat
        return jnp.where(rows >= SEQ, modc_ref[0][:, col * D:(col + 1) * D], lat)

    def modulated(x, rows):
        ms = jnp.mean(x * x, axis=-1, keepdims=True)
        y = x * lax.rsqrt(ms + EPS) * g_ref[...]
        return (y * (1.0 + per_row(rows, 4)) + per_row(rows, 3)).astype(BF16)

    def prologue():
        halo_rows = lax.broadcasted_iota(jnp.int32, (HALO, 1), 0)
        has_before = (row0 != 0) & (row0 != SEQ)
        has_after = (row0 + tm != SEQ) & (row0 + tm != rows_per_sample)
        zero = jnp.zeros((HALO, D), BF16)
        h_scr[0:HALO, :] = jnp.where(has_before, modulated(hb_ref[0], row0 - HALO + halo_rows), zero)
        h_scr[HALO + tm:, :] = jnp.where(has_after, modulated(ha_ref[0], row0 + tm + halo_rows), zero)
        main_rows = row0 + lax.broadcasted_iota(jnp.int32, (tm, 1), 0)
        h_scr[HALO:HALO + tm, :] = modulated(x_ref[0], main_rows)
        o_ref[0] = jnp.zeros((tm, D), F32)

    slots = (u0_scr, u1_scr)

    def tiles(total):
        units = total // HALO
        sizes = [(units // FFN_TILES + (1 if t < units % FFN_TILES else 0)) * HALO for t in range(FFN_TILES)]
        edges = np.cumsum([0] + sizes)
        return [(int(edges[t]), int(edges[t + 1])) for t in range(FFN_TILES)]

    up_tiles, down_tiles = tiles(tm + 2 * HALO), tiles(tm)

    def up(c, t):
        lo, hi = up_tiles[t]
        w = FF_WIDTHS[c]
        slots[c % 2][lo:hi, :w] = _dot(h_scr[lo:hi, :], wa_ref[:, :w])
        slots[c % 2][lo:hi, FF_CHUNK:FF_CHUNK + w] = _dot(h_scr[lo:hi, :], wg_ref[:, FF_CHUNK - w:])

    def conv(u_scr, cw_ref, lo, hi, u_lo, c_lo, width, masked):
        prev = u_scr[HALO - 1 + lo:HALO - 1 + hi, u_lo:u_lo + width]
        nxt = u_scr[HALO + 1 + lo:HALO + 1 + hi, u_lo:u_lo + width]
        if masked:
            rows = row0 + lo + lax.broadcasted_iota(jnp.int32, (hi - lo, 1), 0)
            prev = jnp.where(rows != SEQ, prev, 0.0)
            nxt = jnp.where(rows != SEQ - 1, nxt, 0.0)
        cols = slice(c_lo, c_lo + width)
        return (prev * cw_ref[0:1, cols] + u_scr[HALO + lo:HALO + hi, u_lo:u_lo + width] * cw_ref[1:2, cols]
                + nxt * cw_ref[2:3, cols] + cw_ref[3:4, cols])

    def conv_down(c, t):
        u_scr, w = slots[c % 2], FF_WIDTHS[c]
        t_lo, t_hi = down_tiles[t]
        cuts = sorted({t_lo, t_hi} | ({e for e in (band_lo, band_hi) if t_lo < e < t_hi} if split else set()))
        for lo, hi in zip(cuts[:-1], cuts[1:]):
            masked = bool(split) and band_lo <= lo < band_hi
            for c_lo in range(0, w, CONV_STRIP):
                a = conv(u_scr, cwa_ref, lo, hi, c_lo, c_lo, CONV_STRIP, masked)
                g = conv(u_scr, cwg_ref, lo, hi, FF_CHUNK + c_lo, FF_CHUNK - w + c_lo, CONV_STRIP, masked)
                act_scr[lo:hi, c_lo:c_lo + CONV_STRIP] = (g * jax.nn.sigmoid(g) * a).astype(BF16)
        o_ref[0, t_lo:t_hi, :] += _dot(act_scr[t_lo:t_hi, :w], wd_ref[FF_CHUNK - w:, :])

    for step in range(N_CHUNK + 1):
        @pl.when(j == step)
        def _(step=step):
            if step == 0:
                prologue()
            for t in range(FFN_TILES):
                if step < N_CHUNK:
                    up(step, t)
                if step > 0:
                    conv_down(step - 1, t)
            if step == N_CHUNK:
                rows = row0 + lax.broadcasted_iota(jnp.int32, (tm, 1), 0)
                o_ref[0] = x_ref[0] + per_row(rows, 5) * o_ref[0]


def _ffn(l, x1, mods, g_ffn, w_up, conv_wb, w_down, rows_per_sample, tm):
    n_b = x1.shape[0]
    n_rb = rows_per_sample // tm
    halo_per_block = tm // HALO
    n_halo = x1.shape[1] // HALO
    body = functools.partial(_ffn_kernel, tm=tm, rows_per_sample=rows_per_sample)

    def up_idx(j):
        return jnp.minimum(j, N_CHUNK - 1)

    def down_idx(j):
        return jnp.maximum(j - 1, 0)

    def tail_clamped(c):
        return pl.multiple_of(jnp.minimum(c * FF_CHUNK, D_FF - FF_CHUNK), 256)

    return pl.pallas_call(
        body,
        grid=(n_b, n_rb, N_CHUNK + 1),
        in_specs=[
            pl.BlockSpec((1, tm, D), lambda b, r, j: (b, r, 0)),
            pl.BlockSpec((1, HALO, D), lambda b, r, j: (b, jnp.maximum(r * halo_per_block - 1, 0), 0)),
            pl.BlockSpec((1, HALO, D), lambda b, r, j: (b, jnp.minimum((r + 1) * halo_per_block, n_halo - 1), 0)),
            pl.BlockSpec((None, 1, 1, 6 * D), lambda b, r, j: (l, b, 0, 0)),
            pl.BlockSpec((None, 1, 1, 6 * D), lambda b, r, j: (l, n_b, 0, 0)),
            _layer_spec(g_ffn, l),
            pl.BlockSpec((None, D, FF_CHUNK), lambda b, r, j: (l, 0, up_idx(j))),
            pl.BlockSpec((None, pl.Element(D), pl.Element(FF_CHUNK)),
                         lambda b, r, j: (l, 0, pl.multiple_of(D_FF + tail_clamped(up_idx(j)), 256))),
            pl.BlockSpec((None, 4, FF_CHUNK), lambda b, r, j: (l, 0, down_idx(j))),
            pl.BlockSpec((None, pl.Element(4), pl.Element(FF_CHUNK)),
                         lambda b, r, j: (l, 0, pl.multiple_of(D_FF + tail_clamped(down_idx(j)), 256))),
            pl.BlockSpec((None, pl.Element(FF_CHUNK), pl.Element(D)),
                         lambda b, r, j: (l, tail_clamped(down_idx(j)), 0)),
        ],
        out_specs=pl.BlockSpec((1, tm, D), lambda b, r, j: (b, r, 0)),
        out_shape=jax.ShapeDtypeStruct((n_b, rows_per_sample, D), F32),
        scratch_shapes=[
            pltpu.VMEM((tm + 2 * HALO, D), BF16),
            pltpu.VMEM((tm + 2 * HALO, 2 * FF_CHUNK), F32),
            pltpu.VMEM((tm + 2 * HALO, 2 * FF_CHUNK), F32),
            pltpu.VMEM((tm, FF_CHUNK), BF16),
        ],
        compiler_params=_params(("parallel", "parallel", "arbitrary")),
        name="conv_ffn",
    )(x1, x1, x1, mods, mods, g_ffn, w_up, w_up, conv_wb, conv_wb, w_down)


def _rope_table(rot_dim):
    n_freq = rot_dim // 4
    inv = jnp.power(ROPE_THETA, -jnp.arange(n_freq, dtype=F32) / n_freq)
    t = jnp.arange(SEQ)
    row = (t // GRID_W).astype(F32)
    col = (t % GRID_W).astype(F32)
    ar, ac = row[:, None] * inv, col[:, None] * inv
    ang = jnp.concatenate([ar, ar, ac, ac], axis=-1)
    sign = jnp.concatenate([-jnp.ones(n_freq), jnp.ones(n_freq), -jnp.ones(n_freq), jnp.ones(n_freq)]).astype(F32)
    cos = jnp.concatenate([jnp.cos(ang), jnp.ones((CTX, rot_dim), F32)], axis=0)
    sin = jnp.concatenate([jnp.sin(ang) * sign, jnp.zeros((CTX, rot_dim), F32)], axis=0)
    return jnp.concatenate([cos.T, sin.T], axis=0)


def _na_bias_tiles(rpb):
    n_l, n_h = rpb.shape[:2]
    n_off = 2 * NA_ROWS - 1
    p = jnp.pad(rpb.astype(F32) * LOG2E, ((0, 0), (0, 0), (0, 0), (48, 49)))
    sk = jnp.broadcast_to(p[..., None, :], (n_l, n_h, n_off, GRID_W, 128)).reshape(n_l, n_h, n_off, GRID_W * 128)
    sk = sk[..., :GRID_W * 127].reshape(n_l, n_h, n_off, GRID_W, 127)[..., 63:127]
    c = np.arange(GRID_W)
    win_start = np.clip(c - NA_COLS // 2, 0, GRID_W - NA_COLS)
    v_col = (c[None, :] >= win_start[:, None]) & (c[None, :] < win_start[:, None] + NA_COLS)
    tz = jnp.where(v_col, sk, NEG_INF)
    neg1 = jnp.full((n_l, n_h, 1, GRID_W, GRID_W), NEG_INF, F32)
    tzx = jnp.concatenate([neg1, tz, neg1], axis=2)
    first, second = tzx[:, :, 0:16], tzx[:, :, 1:17]
    neg16 = jnp.full_like(first, NEG_INF)
    return jnp.concatenate([
        jnp.concatenate([first, second], axis=-1), jnp.concatenate([first, neg16], axis=-1),
        jnp.concatenate([neg16, second], axis=-1), jnp.concatenate([neg1, neg1], axis=-1)], axis=2)


def kernel(x, c, ctx, c_ctx, w_mod, b_mod, g_mix, w_in, w_out, mla_q_a_g, mla_w_uq, mla_kv_a_g, mla_w_ukv,
           mla_q_g, mla_k_g, diff_q_g, diff_k_g, diff_lq1, diff_lk1, diff_lq2, diff_lk2, diff_subln_g,
           na_q_g, na_k_g, na_rpb, gqa_q_g, gqa_k_g, g_ffn, w_up, conv_w, conv_b, w_down):
    n_b = x.shape[0]
    n_layer = w_mod.shape[0]
    assert x.shape[1:] == (SEQ, D) and ctx.shape[1:] == (CTX, D)

    xs = jnp.concatenate([x, ctx], axis=1)
    mod_rows = -(-(n_b + 1) // 8) * 8
    cc = jnp.concatenate([c, c_ctx[None], jnp.zeros((mod_rows - n_b - 1, D), F32)], axis=0)
    mods_all = _modulation(cc, w_mod, b_mod).reshape(n_layer, mod_rows, 1, 6 * D)

    cs32, cs64 = _rope_table(32), _rope_table(64)
    bias_all = _na_bias_tiles(na_rpb)

    s = np.cumsum([0, 256, 128, 32, 256, 256, 256, 256, 256, 256, 256, 128, 128])
    gq_perm = np.concatenate([np.arange(GQA_D) + (2 * g + r) * GQA_D for r in range(2) for g in range(2)])

    n_l = n_layer
    gq_cols = w_in[:, :, s[9]:s[10]][:, :, gq_perm]
    wqk = jnp.concatenate([w_in[:, :, s[0]:s[3]], w_in[:, :, s[3]:s[5]], w_in[:, :, s[6]:s[8]], gq_cols,
                           w_in[:, :, s[10]:s[11]]], axis=2).swapaxes(1, 2).astype(BF16)
    wv = jnp.concatenate([w_in[:, :, s[5]:s[6]], w_in[:, :, s[8]:s[9]], w_in[:, :, s[11]:s[12]]],
                         axis=2).astype(BF16)
    wuq = mla_w_uq.swapaxes(1, 2).astype(BF16)
    wukv = mla_w_ukv.reshape(n_l, MLA_KVR, MLA_H, MLA_NOPE + MLA_V)
    wukn = wukv[..., :MLA_NOPE].reshape(n_l, MLA_KVR, MLA_H * MLA_NOPE).swapaxes(1, 2).astype(BF16)
    wuv = wukv[..., MLA_NOPE:].reshape(n_l, MLA_KVR, MLA_H * MLA_V).swapaxes(1, 2).astype(BF16)
    gcol = jnp.concatenate([mla_q_a_g, mla_kv_a_g, mla_q_g, mla_k_g, diff_q_g, diff_k_g, na_q_g, na_k_g,
                            gqa_q_g, gqa_k_g], axis=1).astype(F32)[..., None]
    lam_rows = jnp.stack([diff_lq1, diff_lk1, diff_lq2, diff_lk2], axis=1).astype(F32)
    lam_rows = jnp.concatenate([lam_rows, jnp.zeros_like(lam_rows)], axis=1)
    gsub = jnp.tile(diff_subln_g.astype(F32), (1, 2)).reshape(n_l, 1, 128)
    bounds = {
        "mla": _logit_bound(MLA_D, mla_q_g, mla_k_g),
        "diff": _logit_bound(DIFF_D, diff_q_g, diff_k_g),
        "na": _logit_bound(NA_D, na_q_g, na_k_g, LOG2E * jnp.maximum(jnp.max(na_rpb, axis=(1, 2, 3)), 0.0)),
        "gqa": _logit_bound(GQA_D, gqa_q_g, gqa_k_g),
    }
    w_out_b, w_down_b = w_out.astype(BF16), w_down.astype(BF16)
    w_up_b = w_up.astype(BF16)
    conv_wb = jnp.concatenate([conv_w, conv_b[:, None, :]], axis=1).astype(F32)
    gmix, gffn = g_mix.reshape(n_l, 1, D), g_ffn.reshape(n_l, 1, D)

    for l in range(n_layer):
        with_ctx = l < n_layer - 1
        lambda_init = 0.8 - 0.6 * math.exp(-0.3 * l)
        (q_mla, k_mla, v_mla, q_diff, k_diff, v_diff, q_na, k_na, v_na, q_gqa, k_gqa, v_gqa) = _inproj(
            l, xs, mods_all, gmix, wqk, wv, wuq, wukn, wuv, gcol, cs32, cs64)

        n_rb = NRB if with_ctx else NLAT
        mix_a = _attention(_mla_kernel, "attn_mla", q_mla, k_mla, v_mla, bounds["mla"][l], [], n_rb)
        mix_b = _attention(
            functools.partial(_diff_kernel, lambda_init=lambda_init), "attn_diff",
            q_diff, k_diff, v_diff, bounds["diff"][l],
            [(lam_rows, _layer_spec(lam_rows, l)), (gsub, _layer_spec(gsub, l))], n_rb)
        mix_c = _attention(_na_kernel, "attn_na", q_na, k_na, v_na, bounds["na"][l],
                           [(bias_all, _layer_spec(bias_all, l))], n_rb)
        mix_d = _attention(_gqa_kernel, "attn_gqa", q_gqa, k_gqa, v_gqa, bounds["gqa"][l], [], n_rb)

        rows_per_sample = T if with_ctx else SEQ
        x1 = _outproj(l, xs, mods_all, (mix_a, mix_b, mix_c, mix_d), w_out_b, rows_per_sample,
                      OUT_TM if with_ctx else OUT_TM_LAST)
        xs = _ffn(l, x1, mods_all, gffn, w_up_b, conv_wb, w_down_b, rows_per_sample, rows_per_sample // 2)
    return xs
```

```python
import functools
import math

import numpy as np
import jax
import jax.numpy as jnp
from jax import lax
from jax.experimental import pallas as pl
from jax.experimental.pallas import tpu as pltpu

F32 = jnp.float32
BF16 = jnp.bfloat16

D = 1024
SEQ = 2048
GRID_W = 64
CTX = 256
T = SEQ + CTX
RB = 256
NRB = T // RB
NLAT = SEQ // RB
IN_SUB = 3
EPS = 1e-6
NEG_INF = -1e30
ROPE_THETA = 10000.0
LOG2E = 1.4426950408889634

MLA_H, MLA_NOPE, MLA_ROPE, MLA_V, MLA_QR, MLA_KVR = 4, 64, 32, 64, 256, 128
MLA_D = MLA_NOPE + MLA_ROPE
DIFF_H, DIFF_D = 4, 32
NA_H, NA_D, NA_ROWS, NA_COLS = 4, 64, 8, 16
GQA_H, GQA_KV, GQA_D = 4, 2, 64
D_FF = 2816
FF_CHUNK = 1024
N_CHUNK = -(-D_FF // FF_CHUNK)
FF_WIDTHS = tuple(min(FF_CHUNK, D_FF - c * FF_CHUNK) for c in range(N_CHUNK))
OUT_TM, OUT_TM_LAST = 768, 1024
HALO = 16
FFN_TILES = 4
CONV_STRIP = 256

R_CQ, R_CKV, R_KR, R_DQ, R_DK, R_NQ, R_NK, R_GQ, R_GK = 0, 256, 384, 416, 672, 928, 1184, 1440, 1696
QK_ROWS = 1824
V_COLS = 640

G_QA, G_KVA, G_MQ, G_MK, G_DQ, G_DK, G_NQ, G_NK, G_GQ, G_GK = 0, 256, 384, 480, 576, 608, 640, 704, 768, 832
G_ROWS = 896

VMEM_LIMIT = 56 * 1024 * 1024


def _params(sem):
    return pltpu.CompilerParams(dimension_semantics=sem, vmem_limit_bytes=VMEM_LIMIT)


def _dot(a, b):
    return jnp.dot(a, b, preferred_element_type=F32)


def _dot_nt(a, b):
    return lax.dot_general(a, b, (((1,), (1,)), ((), ())), preferred_element_type=F32)


def _mod_kernel(c_ref, w_ref, b_ref, o_ref):
    c = c_ref[...]
    a = (c * jax.nn.sigmoid(c)).astype(BF16)
    o_ref[0] = _dot(a, w_ref[0].astype(BF16)) + b_ref[0]


def _modulation(cc, w_mod, b_mod):
    n_layer = w_mod.shape[0]
    rows = cc.shape[0]
    return pl.pallas_call(
        _mod_kernel,
        grid=(n_layer, 6),
        in_specs=[
            pl.BlockSpec((rows, D), lambda l, j: (0, 0)),
            pl.BlockSpec((1, D, D), lambda l, j: (l, 0, j)),
            pl.BlockSpec((1, 1, D), lambda l, j: (l, 0, j)),
        ],
        out_specs=pl.BlockSpec((1, rows, D), lambda l, j: (l, 0, j)),
        out_shape=jax.ShapeDtypeStruct((n_layer, rows, 6 * D), F32),
        compiler_params=_params(("parallel", "parallel")),
        name="modulation",
    )(cc, w_mod, b_mod.reshape(n_layer, 1, 6 * D))


def _rms_rows(x, g, n):
    ss = jnp.sum(x * x, axis=0, keepdims=True) * (1.0 / n)
    return x * lax.rsqrt(ss + EPS) * g


def _rope_rows(x, cos, sin_signed, w):
    rot = jnp.concatenate([x[w:2 * w], x[0:w], x[3 * w:4 * w], x[2 * w:3 * w]], axis=0)
    return x * cos + rot * sin_signed


def _inproj_kernel(x_ref, modb_ref, modc_ref, gmix_ref, wqk_ref, wv_ref, wuq_ref, wukn_ref, wuv_ref, gcol_ref,
                   cs32_ref, cs64_ref,
                   qmla_ref, kmla_ref, vmla_ref, qdiff_ref, kdiff_ref, vdiff_ref,
                   qna_ref, kna_ref, vna_ref, qgqa_ref, kgqa_ref, vgqa_ref):
    step = pl.program_id(0)

    def modulated(sb):
        r0 = sb * RB
        x = x_ref[0, r0:r0 + RB, :]
        mod = modb_ref[0]
        if sb == IN_SUB - 1:
            mod = jnp.where(step == NRB // IN_SUB - 1, modc_ref[0], mod)
        shift, scale = mod[:, 0:D], mod[:, D:2 * D]
        ms = jnp.mean(x * x, axis=-1, keepdims=True)
        h = x * lax.rsqrt(ms + EPS) * gmix_ref[...]
        return (h * (1.0 + scale) + shift).astype(BF16)

    def gain(off, n):
        return gcol_ref[off:off + n, :]

    def block(sb, hb, after_first_projection):
        rows = slice(sb * RB, (sb + 1) * RB)
        _inproj_block(rows, hb, after_first_projection, gain, wqk_ref, wv_ref, wuq_ref, wukn_ref, wuv_ref,
                      cs32_ref, cs64_ref, qmla_ref, kmla_ref, vmla_ref, qdiff_ref, kdiff_ref, vdiff_ref,
                      qna_ref, kna_ref, vna_ref, qgqa_ref, kgqa_ref, vgqa_ref)

    hbs = [modulated(0)]
    for sb in range(IN_SUB):
        nxt = (lambda sb=sb: hbs.append(modulated(sb + 1))) if sb + 1 < IN_SUB else (lambda: None)
        block(sb, hbs[sb], nxt)


def _inproj_block(rows, hb, after_first_projection, gain, wqk_ref, wv_ref, wuq_ref, wukn_ref, wuv_ref,
                  cs32_ref, cs64_ref, qmla_ref, kmla_ref, vmla_ref, qdiff_ref, kdiff_ref, vdiff_ref,
                  qna_ref, kna_ref, vna_ref, qgqa_ref, kgqa_ref, vgqa_ref):
    def project(lo, hi):
        return _dot_nt(wqk_ref[lo:hi, :], hb)

    cos32, sin32 = cs32_ref[0:32, rows], cs32_ref[32:64, rows]
    cos64, sin64 = cs64_ref[0:64, rows], cs64_ref[64:128, rows]

    def values():
        pv = _dot(hb, wv_ref[...])
        vdiff_ref[0, rows, :] = pv[:, 0:256].astype(BF16)
        vna_ref[0, rows, :] = pv[:, 256:512].astype(BF16)
        vgqa_ref[0, rows, :] = pv[:, 512:640].astype(BF16)

    def mla(pt):
        cq = _rms_rows(pt[0:MLA_QR], gain(G_QA, MLA_QR), MLA_QR).astype(BF16)
        qt = _dot(wuq_ref[...], cq)
        ckv = _rms_rows(pt[R_CKV:R_CKV + MLA_KVR], gain(G_KVA, MLA_KVR), MLA_KVR).astype(BF16)
        knt = _dot(wukn_ref[...], ckv)
        vt = _dot(wuv_ref[...], ckv)
        vmla_ref[0, rows, :] = vt.T.astype(BF16)
        kr = pt[R_KR:R_KR + MLA_ROPE]
        kr_ss = jnp.sum(kr * kr, axis=0, keepdims=True)
        g_mq, g_mk = gain(G_MQ, MLA_D), gain(G_MK, MLA_D)
        zpad = jnp.zeros((128 - MLA_D, RB), F32)
        q_parts = []
        for hd in range(MLA_H):
            qh = _rms_rows(qt[hd * MLA_D:(hd + 1) * MLA_D], g_mq, MLA_D)
            q_rope = _rope_rows(qh[MLA_NOPE:], cos32, sin32, MLA_ROPE // 4)
            q_parts += [qh[:MLA_NOPE], q_rope, zpad]
            kn = knt[hd * MLA_NOPE:(hd + 1) * MLA_NOPE]
            ss = (jnp.sum(kn * kn, axis=0, keepdims=True) + kr_ss) * (1.0 / MLA_D)
            r = lax.rsqrt(ss + EPS)
            k_rope = _rope_rows(kr * r * g_mk[MLA_NOPE:], cos32, sin32, MLA_ROPE // 4)
            kmla_ref[0, hd * 128:(hd + 1) * 128, rows] = jnp.concatenate(
                [kn * r * g_mk[:MLA_NOPE], k_rope, zpad], axis=0).astype(BF16)
        q_all = jnp.concatenate(q_parts, axis=0) * (MLA_D ** -0.5 * LOG2E)
        qmla_ref[0, rows, :] = q_all.T.astype(BF16)

    def diff(pt):
        g_dq, g_dk = gain(G_DQ, DIFF_D), gain(G_DK, DIFF_D)
        q_parts, k_parts = [], []
        for gi in range(2 * DIFF_H):
            qg = _rms_rows(pt[gi * DIFF_D:(gi + 1) * DIFF_D], g_dq, DIFF_D)
            q_parts.append(_rope_rows(qg, cos32, sin32, DIFF_D // 4))
            kg = _rms_rows(pt[256 + gi * DIFF_D:256 + (gi + 1) * DIFF_D], g_dk, DIFF_D)
            k_parts.append(_rope_rows(kg, cos32, sin32, DIFF_D // 4))
        qdiff_ref[0, rows, :] = (jnp.concatenate(q_parts, axis=0) * (DIFF_D ** -0.5 * LOG2E)).T.astype(BF16)
        kdiff_ref[0, :, rows] = jnp.concatenate(k_parts, axis=0).astype(BF16)

    def na(pt):
        g_nq, g_nk = gain(G_NQ, NA_D), gain(G_NK, NA_D)
        q_parts, k_parts = [], []
        for hd in range(NA_H):
            q_parts.append(_rms_rows(pt[hd * NA_D:(hd + 1) * NA_D], g_nq, NA_D))
            k_parts.append(_rms_rows(pt[256 + hd * NA_D:256 + (hd + 1) * NA_D], g_nk, NA_D))
        qna_ref[0, rows, :] = (jnp.concatenate(q_parts, axis=0) * (NA_D ** -0.5 * LOG2E)).T.astype(BF16)
        kna_ref[0, :, rows] = jnp.concatenate(k_parts, axis=0).astype(BF16)

    def gqa(pt):
        g_gq, g_gk = gain(G_GQ, GQA_D), gain(G_GK, GQA_D)
        q_parts, k_parts = [], []
        for hd in range(GQA_H):
            qg = _rms_rows(pt[hd * GQA_D:(hd + 1) * GQA_D], g_gq, GQA_D)
            q_parts.append(_rope_rows(qg, cos64, sin64, GQA_D // 4))
        for hd in range(GQA_KV):
            kg = _rms_rows(pt[256 + hd * GQA_D:256 + (hd + 1) * GQA_D], g_gk, GQA_D)
            k_parts.append(_rope_rows(kg, cos64, sin64, GQA_D // 4))
        qgqa_ref[0, rows, :] = (jnp.concatenate(q_parts, axis=0) * (GQA_D ** -0.5 * LOG2E)).T.astype(BF16)
        kgqa_ref[0, :, rows] = jnp.concatenate(k_parts, axis=0).astype(BF16)

    pt_mla = project(R_CQ, R_DQ)
    after_first_projection()
    pt_diff = project(R_DQ, R_NQ)
    mla(pt_mla)
    pt_na = project(R_NQ, R_GQ)
    diff(pt_diff)
    pt_gqa = project(R_GQ, QK_ROWS)
    na(pt_na)
    values()
    gqa(pt_gqa)


def _inproj(l, xs, mods, gmix, wqk, wv, wuq, wukn, wuv, gcol, cs32, cs64):
    n_b = xs.shape[0]

    def full(a):
        return _layer_spec(a, l)

    rows_per_step = IN_SUB * RB

    def tok(width):
        return pl.BlockSpec((1, rows_per_step, width), lambda r, b: (b, r, 0))

    def chan(rows):
        return pl.BlockSpec((1, rows, rows_per_step), lambda r, b: (b, 0, r))

    def tshape(width):
        return jax.ShapeDtypeStruct((n_b, T, width), BF16)

    def cshape(rows):
        return jax.ShapeDtypeStruct((n_b, rows, T), BF16)

    return pl.pallas_call(
        _inproj_kernel,
        grid=(NRB // IN_SUB, n_b),
        in_specs=[
            pl.BlockSpec((1, rows_per_step, D), lambda r, b: (b, r, 0)),
            pl.BlockSpec((None, 1, 1, 6 * D), lambda r, b: (l, b, 0, 0)),
            pl.BlockSpec((None, 1, 1, 6 * D), lambda r, b: (l, n_b, 0, 0)),
            full(gmix), full(wqk), full(wv), full(wuq), full(wukn), full(wuv), full(gcol),
            pl.BlockSpec((64, rows_per_step), lambda r, b: (0, r)),
            pl.BlockSpec((128, rows_per_step), lambda r, b: (0, r)),
        ],
        out_specs=[tok(512), chan(512), tok(256), tok(256), chan(256), tok(256),
                   tok(256), chan(256), tok(256), tok(256), chan(128), tok(128)],
        out_shape=[tshape(512), cshape(512), tshape(256), tshape(256), cshape(256), tshape(256),
                   tshape(256), cshape(256), tshape(256), tshape(256), cshape(128), tshape(128)],
        compiler_params=_params(("parallel", "parallel")),
        name="inproj_prep",
    )(xs, mods, mods, gmix, wqk, wv, wuq, wukn, wuv, gcol, cs32, cs64)


def _lane_id(shape):
    return lax.broadcasted_iota(jnp.int32, shape, 1)


QB_UNROLL = 2
KEY_CHUNK = 768
LAT_CHUNKS = tuple((lo, KEY_CHUNK) for lo in range(0, T, KEY_CHUNK))
CTX_CHUNKS = ((SEQ, CTX),)


def _k_slab(k_ref, idx):
    return lambda lo, n: k_ref[0, idx * 128:(idx + 1) * 128, pl.ds(lo, n)]


def _v_slab(v_ref, idx):
    return lambda lo, n: v_ref[0, pl.ds(lo, n), idx * 128:(idx + 1) * 128]


def _attend(units, chunks, shift=None):
    def scores(i, c):
        q, keys, _, bias = units[i]
        lo, n = chunks[c]
        s = _dot(q, keys(lo, n))
        b = None if bias is None else bias(c)
        return s if b is None else s + b

    if shift is not None:
        items = [(i, c) for i in range(len(units)) for c in range(len(chunks))]
        acc, l = [None] * len(units), [None] * len(units)
        nxt = scores(*items[0])
        for idx, (i, c) in enumerate(items):
            cur = nxt
            if idx + 1 < len(items):
                nxt = scores(*items[idx + 1])
            e = jnp.exp2(cur - shift)
            lc = jnp.sum(e, axis=-1, keepdims=True)
            oc = _dot(e.astype(BF16), units[i][2](*chunks[c]))
            acc[i] = oc if acc[i] is None else acc[i] + oc
            l[i] = lc if l[i] is None else l[i] + lc
        return list(zip(acc, l))

    cur = [scores(0, c) for c in range(len(chunks))]
    out = []
    for i in range(len(units)):
        m = functools.reduce(jnp.maximum, [jnp.max(s, axis=-1, keepdims=True) for s in cur])
        nxt, acc, l = [], None, None
        for c, (lo, n) in enumerate(chunks):
            if i + 1 < len(units):
                nxt.append(scores(i + 1, c))
            e = jnp.exp2(cur[c] - m)
            lc = jnp.sum(e, axis=-1, keepdims=True)
            oc = _dot(e.astype(BF16), units[i][2](lo, n))
            acc = oc if acc is None else acc + oc
            l = lc if l is None else l + lc
        out.append((acc, l))
        cur = nxt
    return out


def _with_shift(bnd_ref, run):
    @pl.when(bnd_ref[1] > 0.5)
    def _():
        run(bnd_ref[0])

    @pl.when(bnd_ref[1] <= 0.5)
    def _():
        run(None)


def _query_blocks(n_rb, bnd_ref, run):
    def all_blocks(shift):
        def body(i, carry):
            for u in range(QB_UNROLL):
                rb = i * QB_UNROLL + u
                run(rb, pl.ds(pl.multiple_of(rb * RB, RB), RB), True, shift)
            return carry

        lax.fori_loop(0, NLAT // QB_UNROLL, body, 0)
        if n_rb > NLAT:
            run(NLAT, pl.ds(SEQ, CTX), False, shift)

    _with_shift(bnd_ref, all_blocks)


def _mla_kernel(q_ref, k_ref, v_ref, bnd_ref, o_ref, *, n_rb):
    lane = _lane_id((RB, 128))

    def run(rb, rows, latent, shift):
        units = [(q_ref[0, rows,hd * 128:(hd + 1) * 128], _k_slab(k_ref, hd), _v_slab(v_ref, hd // 2), None)
                 for hd in range(MLA_H)]
        o = [acc * (1.0 / l) for acc, l in _attend(units, LAT_CHUNKS if latent else CTX_CHUNKS, shift)]
        outs = [jnp.where(lane < 64, o[0], o[1]), jnp.where(lane < 64, o[2], o[3])]
        o_ref[0, rows, :] = jnp.concatenate(outs, axis=1).astype(o_ref.dtype)

    _query_blocks(n_rb, bnd_ref, run)


def _gqa_kernel(q_ref, k_ref, v_ref, bnd_ref, o_ref, *, n_rb):
    lane = _lane_id((RB, 128))

    def run(rb, rows, latent, shift):
        order = [(rep, grp) for rep in range(2) for grp in range(2)]
        units = []
        for rep, grp in order:
            qs = q_ref[0, rows,rep * 128:(rep + 1) * 128]
            qm = jnp.where((lane >= 64) == (grp == 1), qs, jnp.zeros_like(qs))
            units.append((qm, _k_slab(k_ref, 0), _v_slab(v_ref, 0), None))
        res = {}
        for (rep, grp), (acc, l) in zip(order, _attend(units, LAT_CHUNKS if latent else CTX_CHUNKS, shift)):
            o = acc * (1.0 / l)
            res[(grp, rep)] = o if grp == rep else pltpu.roll(o, 64, axis=1)
        outs = [jnp.where(lane < 64, res[(grp, 0)], res[(grp, 1)]) for grp in range(2)]
        o_ref[0, rows, :] = jnp.concatenate(outs, axis=1).astype(o_ref.dtype)

    _query_blocks(n_rb, bnd_ref, run)


def _diff_kernel(q_ref, k_ref, v_ref, bnd_ref, lam_ref, g_ref, o_ref, *, n_rb, lambda_init):
    lane = _lane_id((RB, 128))
    lq1, lk1, lq2, lk2 = lam_ref[0:1, :], lam_ref[1:2, :], lam_ref[2:3, :], lam_ref[3:4, :]
    lam = (jnp.exp(jnp.sum(lq1 * lk1, axis=-1, keepdims=True))
           - jnp.exp(jnp.sum(lq2 * lk2, axis=-1, keepdims=True)) + lambda_init)
    gsub = g_ref[...]
    grp = lane // DIFF_D

    def run(rb, rows, latent, shift):
        units = []
        for hd in range(DIFF_H):
            pair, sub = divmod(hd, 2)
            qs = q_ref[0, rows,pair * 128:(pair + 1) * 128]
            qm = jnp.concatenate([jnp.where(grp == 2 * sub + which, qs, jnp.zeros_like(qs)) for which in range(2)],
                                 axis=0)
            units.append((qm, _k_slab(k_ref, pair), _v_slab(v_ref, pair), None))
        res = _attend(units, LAT_CHUNKS if latent else CTX_CHUNKS, shift)
        outs = []
        for pair in range(2):
            halves = []
            for sub in range(2):
                acc, l = res[2 * pair + sub]
                (a1, l1), (a2, l2) = (acc[:RB], l[:RB]), (acc[RB:], l[RB:])
                halves.append(a1 * (1.0 / l1) - a2 * (lam / l2))
            o = jnp.where(lane < 64, halves[0], halves[1])
            o2 = o * o
            ss0 = jnp.sum(jnp.where(lane < 64, o2, 0.0), axis=-1, keepdims=True)
            ss1 = jnp.sum(jnp.where(lane < 64, 0.0, o2), axis=-1, keepdims=True)
            ss = jnp.where(lane < 64, ss0, ss1) * (1.0 / (2 * DIFF_D))
            outs.append(o * lax.rsqrt(ss + EPS) * gsub * (1.0 - lambda_init))
        o_ref[0, rows, :] = jnp.concatenate(outs, axis=1).astype(o_ref.dtype)

    _query_blocks(n_rb, bnd_ref, run)


NA_WIN = 768
NA_CHUNK = 768
NA_TILES = 49
GRID_ROWS = SEQ // GRID_W


def _na_kernel(q_ref, k_ref, v_ref, bnd_ref, bias_ref, o_ref, *, n_rb):
    lane = _lane_id((RB, 128))

    def run(rb, rows, latent, shift):
        na_chunks, tile_idx = CTX_CHUNKS, None
        if latent:
            win_row = jnp.clip(4 * rb - 4, 0, GRID_ROWS - NA_WIN // GRID_W)
            off = pl.multiple_of(win_row * GRID_W, RB)
            local = [(pl.multiple_of(off + c * NA_CHUNK, RB), NA_CHUNK) for c in range(NA_WIN // NA_CHUNK)]
            na_chunks = local + list(CTX_CHUNKS)
            tile_idx = []
            for c in range(NA_WIN // NA_CHUNK):
                per_row = []
                for ri in range(RB // GRID_W):
                    r = 4 * rb + ri
                    row_start = jnp.clip(r - NA_ROWS // 2, 0, GRID_ROWS - NA_ROWS)
                    idxs = []
                    for p in range(NA_CHUNK // 128):
                        kr = win_row + c * (NA_CHUNK // GRID_W) + 2 * p
                        ok1 = (kr >= row_start) & (kr < row_start + NA_ROWS)
                        ok2 = (kr + 1 >= row_start) & (kr + 1 < row_start + NA_ROWS)
                        e = jnp.clip(kr - r + NA_ROWS, 0, 15)
                        idxs.append(jnp.where(ok1 & ok2, e, jnp.where(ok1, 16 + e, jnp.where(ok2, 32 + e, 48))))
                    per_row.append(idxs)
                tile_idx.append(per_row)
        units = []
        for hd in range(NA_H):
            pair, sub = divmod(hd, 2)
            qs = q_ref[0, rows,pair * 128:(pair + 1) * 128]
            qm = jnp.where((lane >= 64) == (sub == 1), qs, jnp.zeros_like(qs))

            def bias(c, hd=hd):
                if tile_idx is None or c >= NA_WIN // NA_CHUNK:
                    return None
                return jnp.concatenate(
                    [jnp.concatenate([bias_ref[hd, idx] for idx in row], axis=1) for row in tile_idx[c]], axis=0)

            units.append((qm, _k_slab(k_ref, pair), _v_slab(v_ref, pair), bias))
        o = [acc * (1.0 / l) for acc, l in _attend(units, na_chunks, shift)]
        outs = [jnp.where(lane < 64, o[0], o[1]), jnp.where(lane < 64, o[2], o[3])]
        o_ref[0, rows, :] = jnp.concatenate(outs, axis=1).astype(o_ref.dtype)

    _query_blocks(n_rb, bnd_ref, run)


SHIFT_MAX = 40.0


def _logit_bound(d, q_gain, k_gain, bias_max=0.0):
    bound = (1.01 * math.sqrt(d) * LOG2E * jnp.max(jnp.abs(q_gain), axis=-1) * jnp.max(jnp.abs(k_gain), axis=-1)
             + bias_max)
    return jnp.stack([bound, (bound <= SHIFT_MAX).astype(F32)], axis=-1).astype(F32)


def _layer_spec(a, l):
    return pl.BlockSpec((None,) + a.shape[1:], lambda *_, _n=a.ndim - 1: (l,) + (0,) * _n)


def _attention(body, name, q, k, v, bound, extra, n_rb, out_width=256):
    n_b = q.shape[0]
    extra = [(bound, pl.BlockSpec(memory_space=pltpu.SMEM))] + list(extra)
    in_specs = [
        pl.BlockSpec((1, T, q.shape[2]), lambda b: (b, 0, 0)),
        pl.BlockSpec((1, k.shape[1], T), lambda b: (b, 0, 0)),
        pl.BlockSpec((1, T, v.shape[2]), lambda b: (b, 0, 0)),
    ] + [spec for _, spec in extra]
    return pl.pallas_call(
        functools.partial(body, n_rb=n_rb),
        grid=(n_b,),
        in_specs=in_specs,
        out_specs=pl.BlockSpec((1, n_rb * RB, out_width), lambda b: (b, 0, 0)),
        out_shape=jax.ShapeDtypeStruct((n_b, n_rb * RB, out_width), BF16),
        compiler_params=_params(("parallel",)),
        name=name,
    )(q, k, v, *[a for a, _ in extra])


def _outproj_kernel(x_ref, modb_ref, modc_ref, ma_ref, mb_ref, mc_ref, md_ref, w_ref, o_ref, *, tm):
    rows = pl.program_id(1) * tm + lax.broadcasted_iota(jnp.int32, (tm, 1), 0)
    g1 = jnp.where(rows >= SEQ, modc_ref[0][:, 2 * D:3 * D], modb_ref[0][:, 2 * D:3 * D])
    acc = _dot(ma_ref[0], w_ref[0:256, :])
    acc += _dot(mb_ref[0], w_ref[256:512, :])
    acc += _dot(mc_ref[0], w_ref[512:768, :])
    acc += _dot(md_ref[0], w_ref[768:1024, :])
    o_ref[0] = x_ref[0] + g1 * acc


def _outproj(l, xs, mods, mixes, w_out, rows_per_sample, tm):
    n_b = xs.shape[0]
    mix_spec = pl.BlockSpec((1, tm, 256), lambda b, r: (b, r, 0))
    return pl.pallas_call(
        functools.partial(_outproj_kernel, tm=tm),
        grid=(n_b, rows_per_sample // tm),
        in_specs=[
            pl.BlockSpec((1, tm, D), lambda b, r: (b, r, 0)),
            pl.BlockSpec((None, 1, 1, 6 * D), lambda b, r: (l, b, 0, 0)),
            pl.BlockSpec((None, 1, 1, 6 * D), lambda b, r: (l, n_b, 0, 0)),
            mix_spec, mix_spec, mix_spec, mix_spec,
            _layer_spec(w_out, l),
        ],
        out_specs=pl.BlockSpec((1, tm, D), lambda b, r: (b, r, 0)),
        out_shape=jax.ShapeDtypeStruct((n_b, rows_per_sample, D), F32),
        compiler_params=_params(("parallel", "parallel")),
        name="outproj",
    )(xs, mods, mods, *mixes, w_out)


def _ffn_kernel(x_ref, hb_ref, ha_ref, modb_ref, modc_ref, g_ref, wa_ref, wg_ref, cwa_ref, cwg_ref, wd_ref, o_ref,
                h_scr, u0_scr, u1_scr, act_scr, *, tm, rows_per_sample):
    rb = pl.program_id(1)
    j = pl.program_id(2)
    row0 = rb * tm
    split = SEQ % tm if rows_per_sample > SEQ else 0
    band_lo, band_hi = split - HALO, split + HALO

    def per_row(rows, col):
        lat = modb_ref[0][:, col * D:(col + 1) * D]
        if rows_per_sample == SEQ:
            return lat
        return jnp.where(rows >= SEQ, modc_ref[0][:, col * D:(col + 1) * D], lat)

    def modulated(x, rows):
        ms = jnp.mean(x * x, axis=-1, keepdims=True)
        y = x * lax.rsqrt(ms + EPS) * g_ref[...]
        return (y * (1.0 + per_row(rows, 4)) + per_row(rows, 3)).astype(BF16)

    def prologue():
        halo_rows = lax.broadcasted_iota(jnp.int32, (HALO, 1), 0)
        has_before = (row0 != 0) & (row0 != SEQ)
        has_after = (row0 + tm != SEQ) & (row0 + tm != rows_per_sample)
        zero = jnp.zeros((HALO, D), BF16)
        h_scr[0:HALO, :] = jnp.where(has_before, modulated(hb_ref[0], row0 - HALO + halo_rows), zero)
        h_scr[HALO + tm:, :] = jnp.where(has_after, modulated(ha_ref[0], row0 + tm + halo_rows), zero)
        main_rows = row0 + lax.broadcasted_iota(jnp.int32, (tm, 1), 0)
        h_scr[HALO:HALO + tm, :] = modulated(x_ref[0], main_rows)
        o_ref[0] = jnp.zeros((tm, D), F32)

    slots = (u0_scr, u1_scr)

    def tiles(total):
        units = total // HALO
        sizes = [(units // FFN_TILES + (1 if t < units % FFN_TILES else 0)) * HALO for t in range(FFN_TILES)]
        edges = np.cumsum([0] + sizes)
        return [(int(edges[t]), int(edges[t + 1])) for t in range(FFN_TILES)]

    up_tiles, down_tiles = tiles(tm + 2 * HALO), tiles(tm)

    def up(c, t):
        lo, hi = up_tiles[t]
        w = FF_WIDTHS[c]
        slots[c % 2][lo:hi, :w] = _dot(h_scr[lo:hi, :], wa_ref[:, :w])
        slots[c % 2][lo:hi, FF_CHUNK:FF_CHUNK + w] = _dot(h_scr[lo:hi, :], wg_ref[:, FF_CHUNK - w:])

    def conv(u_scr, cw_ref, lo, hi, u_lo, c_lo, width, masked):
        prev = u_scr[HALO - 1 + lo:HALO - 1 + hi, u_lo:u_lo + width]
        nxt = u_scr[HALO + 1 + lo:HALO + 1 + hi, u_lo:u_lo + width]
        if masked:
            rows = row0 + lo + lax.broadcasted_iota(jnp.int32, (hi - lo, 1), 0)
            prev = jnp.where(rows != SEQ, prev, 0.0)
            nxt = jnp.where(rows != SEQ - 1, nxt, 0.0)
        cols = slice(c_lo, c_lo + width)
        return (prev * cw_ref[0:1, cols] + u_scr[HALO + lo:HALO + hi, u_lo:u_lo + width] * cw_ref[1:2, cols]
                + nxt * cw_ref[2:3, cols] + cw_ref[3:4, cols])

    def conv_down(c, t):
        u_scr, w = slots[c % 2], FF_WIDTHS[c]
        t_lo, t_hi = down_tiles[t]
        cuts = sorted({t_lo, t_hi} | ({e for e in (band_lo, band_hi) if t_lo < e < t_hi} if split else set()))
        for lo, hi in zip(cuts[:-1], cuts[1:]):
            masked = bool(split) and band_lo <= lo < band_hi
            for c_lo in range(0, w, CONV_STRIP):
                a = conv(u_scr, cwa_ref, lo, hi, c_lo, c_lo, CONV_STRIP, masked)
                g = conv(u_scr, cwg_ref, lo, hi, FF_CHUNK + c_lo, FF_CHUNK - w + c_lo, CONV_STRIP, masked)
                act_scr[lo:hi, c_lo:c_lo + CONV_STRIP] = (g * jax.nn.sigmoid(g) * a).astype(BF16)
        o_ref[0, t_lo:t_hi, :] += _dot(act_scr[t_lo:t_hi, :w], wd_ref[FF_CHUNK - w:, :])

    for step in range(N_CHUNK + 1):
        @pl.when(j == step)
        def _(step=step):
            if step == 0:
                prologue()
            for t in range(FFN_TILES):
                if step < N_CHUNK:
                    up(step, t)
                if step > 0:
                    conv_down(step - 1, t)
            if step == N_CHUNK:
                rows = row0 + lax.broadcasted_iota(jnp.int32, (tm, 1), 0)
                o_ref[0] = x_ref[0] + per_row(rows, 5) * o_ref[0]


def _ffn(l, x1, mods, g_ffn, w_up, conv_wb, w_down, rows_per_sample, tm):
    n_b = x1.shape[0]
    n_rb = rows_per_sample // tm
    halo_per_block = tm // HALO
    n_halo = x1.shape[1] // HALO
    body = functools.partial(_ffn_kernel, tm=tm, rows_per_sample=rows_per_sample)

    def up_idx(j):
        return jnp.minimum(j, N_CHUNK - 1)

    def down_idx(j):
        return jnp.maximum(j - 1, 0)

    def tail_clamped(c):
        return pl.multiple_of(jnp.minimum(c * FF_CHUNK, D_FF - FF_CHUNK), 256)

    return pl.pallas_call(
        body,
        grid=(n_b, n_rb, N_CHUNK + 1),
        in_specs=[
            pl.BlockSpec((1, tm, D), lambda b, r, j: (b, r, 0)),
            pl.BlockSpec((1, HALO, D), lambda b, r, j: (b, jnp.maximum(r * halo_per_block - 1, 0), 0)),
            pl.BlockSpec((1, HALO, D), lambda b, r, j: (b, jnp.minimum((r + 1) * halo_per_block, n_halo - 1), 0)),
            pl.BlockSpec((None, 1, 1, 6 * D), lambda b, r, j: (l, b, 0, 0)),
            pl.BlockSpec((None, 1, 1, 6 * D), lambda b, r, j: (l, n_b, 0, 0)),
            _layer_spec(g_ffn, l),
            pl.BlockSpec((None, D, FF_CHUNK), lambda b, r, j: (l, 0, up_idx(j))),
            pl.BlockSpec((None, pl.Element(D), pl.Element(FF_CHUNK)),
                         lambda b, r, j: (l, 0, pl.multiple_of(D_FF + tail_clamped(up_idx(j)), 256))),
            pl.BlockSpec((None, 4, FF_CHUNK), lambda b, r, j: (l, 0, down_idx(j))),
            pl.BlockSpec((None, pl.Element(4), pl.Element(FF_CHUNK)),
                         lambda b, r, j: (l, 0, pl.multiple_of(D_FF + tail_clamped(down_idx(j)), 256))),
            pl.BlockSpec((None, pl.Element(FF_CHUNK), pl.Element(D)),
                         lambda b, r, j: (l, tail_clamped(down_idx(j)), 0)),
        ],
        out_specs=pl.BlockSpec((1, tm, D), lambda b, r, j: (b, r, 0)),
        out_shape=jax.ShapeDtypeStruct((n_b, rows_per_sample, D), F32),
        scratch_shapes=[
            pltpu.VMEM((tm + 2 * HALO, D), BF16),
            pltpu.VMEM((tm + 2 * HALO, 2 * FF_CHUNK), F32),
            pltpu.VMEM((tm + 2 * HALO, 2 * FF_CHUNK), F32),
            pltpu.VMEM((tm, FF_CHUNK), BF16),
        ],
        compiler_params=_params(("parallel", "parallel", "arbitrary")),
        name="conv_ffn",
    )(x1, x1, x1, mods, mods, g_ffn, w_up, w_up, conv_wb, conv_wb, w_down)


def _rope_table(rot_dim):
    n_freq = rot_dim // 4
    inv = jnp.power(ROPE_THETA, -jnp.arange(n_freq, dtype=F32) / n_freq)
    t = jnp.arange(SEQ)
    row = (t // GRID_W).astype(F32)
    col = (t % GRID_W).astype(F32)
    ar, ac = row[:, None] * inv, col[:, None] * inv
    ang = jnp.concatenate([ar, ar, ac, ac], axis=-1)
    sign = jnp.concatenate([-jnp.ones(n_freq), jnp.ones(n_freq), -jnp.ones(n_freq), jnp.ones(n_freq)]).astype(F32)
    cos = jnp.concatenate([jnp.cos(ang), jnp.ones((CTX, rot_dim), F32)], axis=0)
    sin = jnp.concatenate([jnp.sin(ang) * sign, jnp.zeros((CTX, rot_dim), F32)], axis=0)
    return jnp.concatenate([cos.T, sin.T], axis=0)


def _na_bias_tiles(rpb):
    n_l, n_h = rpb.shape[:2]
    n_off = 2 * NA_ROWS - 1
    p = jnp.pad(rpb.astype(F32) * LOG2E, ((0, 0), (0, 0), (0, 0), (48, 49)))
    sk = jnp.broadcast_to(p[..., None, :], (n_l, n_h, n_off, GRID_W, 128)).reshape(n_l, n_h, n_off, GRID_W * 128)
    sk = sk[..., :GRID_W * 127].reshape(n_l, n_h, n_off, GRID_W, 127)[..., 63:127]
    c = np.arange(GRID_W)
    win_start = np.clip(c - NA_COLS // 2, 0, GRID_W - NA_COLS)
    v_col = (c[None, :] >= win_start[:, None]) & (c[None, :] < win_start[:, None] + NA_COLS)
    tz = jnp.where(v_col, sk, NEG_INF)
    neg1 = jnp.full((n_l, n_h, 1, GRID_W, GRID_W), NEG_INF, F32)
    tzx = jnp.concatenate([neg1, tz, neg1], axis=2)
    first, second = tzx[:, :, 0:16], tzx[:, :, 1:17]
    neg16 = jnp.full_like(first, NEG_INF)
    return jnp.concatenate([
        jnp.concatenate([first, second], axis=-1), jnp.concatenate([first, neg16], axis=-1),
        jnp.concatenate([neg16, second], axis=-1), jnp.concatenate([neg1, neg1], axis=-1)], axis=2)


def kernel(x, c, ctx, c_ctx, w_mod, b_mod, g_mix, w_in, w_out, mla_q_a_g, mla_w_uq, mla_kv_a_g, mla_w_ukv,
           mla_q_g, mla_k_g, diff_q_g, diff_k_g, diff_lq1, diff_lk1, diff_lq2, diff_lk2, diff_subln_g,
           na_q_g, na_k_g, na_rpb, gqa_q_g, gqa_k_g, g_ffn, w_up, conv_w, conv_b, w_down):
    n_b = x.shape[0]
    n_layer = w_mod.shape[0]
    assert x.shape[1:] == (SEQ, D) and ctx.shape[1:] == (CTX, D)

    xs = jnp.concatenate([x, ctx], axis=1)
    mod_rows = -(-(n_b + 1) // 8) * 8
    cc = jnp.concatenate([c, c_ctx[None], jnp.zeros((mod_rows - n_b - 1, D), F32)], axis=0)
    mods_all = _modulation(cc, w_mod, b_mod).reshape(n_layer, mod_rows, 1, 6 * D)

    cs32, cs64 = _rope_table(32), _rope_table(64)
    bias_all = _na_bias_tiles(na_rpb)

    s = np.cumsum([0, 256, 128, 32, 256, 256, 256, 256, 256, 256, 256, 128, 128])
    gq_perm = np.concatenate([np.arange(GQA_D) + (2 * g + r) * GQA_D for r in range(2) for g in range(2)])

    n_l = n_layer
    gq_cols = w_in[:, :, s[9]:s[10]][:, :, gq_perm]
    wqk = jnp.concatenate([w_in[:, :, s[0]:s[3]], w_in[:, :, s[3]:s[5]], w_in[:, :, s[6]:s[8]], gq_cols,
                           w_in[:, :, s[10]:s[11]]], axis=2).swapaxes(1, 2).astype(BF16)
    wv = jnp.concatenate([w_in[:, :, s[5]:s[6]], w_in[:, :, s[8]:s[9]], w_in[:, :, s[11]:s[12]]],
                         axis=2).astype(BF16)
    wuq = mla_w_uq.swapaxes(1, 2).astype(BF16)
    wukv = mla_w_ukv.reshape(n_l, MLA_KVR, MLA_H, MLA_NOPE + MLA_V)
    wukn = wukv[..., :MLA_NOPE].reshape(n_l, MLA_KVR, MLA_H * MLA_NOPE).swapaxes(1, 2).astype(BF16)
    wuv = wukv[..., MLA_NOPE:].reshape(n_l, MLA_KVR, MLA_H * MLA_V).swapaxes(1, 2).astype(BF16)
    gcol = jnp.concatenate([mla_q_a_g, mla_kv_a_g, mla_q_g, mla_k_g, diff_q_g, diff_k_g, na_q_g, na_k_g,
                            gqa_q_g, gqa_k_g], axis=1).astype(F32)[..., None]
    lam_rows = jnp.stack([diff_lq1, diff_lk1, diff_lq2, diff_lk2], axis=1).astype(F32)
    lam_rows = jnp.concatenate([lam_rows, jnp.zeros_like(lam_rows)], axis=1)
    gsub = jnp.tile(diff_subln_g.astype(F32), (1, 2)).reshape(n_l, 1, 128)
    bounds = {
        "mla": _logit_bound(MLA_D, mla_q_g, mla_k_g),
        "diff": _logit_bound(DIFF_D, diff_q_g, diff_k_g),
        "na": _logit_bound(NA_D, na_q_g, na_k_g, LOG2E * jnp.maximum(jnp.max(na_rpb, axis=(1, 2, 3)), 0.0)),
        "gqa": _logit_bound(GQA_D, gqa_q_g, gqa_k_g),
    }
    w_out_b, w_down_b = w_out.astype(BF16), w_down.astype(BF16)
    w_up_b = w_up.astype(BF16)
    conv_wb = jnp.concatenate([conv_w, conv_b[:, None, :]], axis=1).astype(F32)
    gmix, gffn = g_mix.reshape(n_l, 1, D), g_ffn.reshape(n_l, 1, D)

    for l in range(n_layer):
        with_ctx = l < n_layer - 1
        lambda_init = 0.8 - 0.6 * math.exp(-0.3 * l)
        (q_mla, k_mla, v_mla, q_diff, k_diff, v_diff, q_na, k_na, v_na, q_gqa, k_gqa, v_gqa) = _inproj(
            l, xs, mods_all, gmix, wqk, wv, wuq, wukn, wuv, gcol, cs32, cs64)

        n_rb = NRB if with_ctx else NLAT
        mix_a = _attention(_mla_kernel, "attn_mla", q_mla, k_mla, v_mla, bounds["mla"][l], [], n_rb)
        mix_b = _attention(
            functools.partial(_diff_kernel, lambda_init=lambda_init), "attn_diff",
            q_diff, k_diff, v_diff, bounds["diff"][l],
            [(lam_rows, _layer_spec(lam_rows, l)), (gsub, _layer_spec(gsub, l))], n_rb)
        mix_c = _attention(_na_kernel, "attn_na", q_na, k_na, v_na, bounds["na"][l],
                           [(bias_all, _layer_spec(bias_all, l))], n_rb)
        mix_d = _attention(_gqa_kernel, "attn_gqa", q_gqa, k_gqa, v_gqa, bounds["gqa"][l], [], n_rb)

        rows_per_sample = T if with_ctx else SEQ
        x1 = _outproj(l, xs, mods_all, (mix_a, mix_b, mix_c, mix_d), w_out_b, rows_per_sample,
                      OUT_TM if with_ctx else OUT_TM_LAST)
        xs = _ffn(l, x1, mods_all, gffn, w_up_b, conv_wb, w_down_b, rows_per_sample, rows_per_sample // 2)
    return xs
```
